```python
import math
import jax, jax.numpy as jnp
from jax import lax
import numpy as np

D_MODEL = 1024
BATCH = 4
SEQ = 4096
DEPTH = 2

HEAD_DIM = 64
N_MIXERS = 4
N_HEADS_TOTAL = D_MODEL // HEAD_DIM
HPM = N_HEADS_TOTAL // N_MIXERS
GW = HPM * HEAD_DIM
MIX_WIDTH = GW * N_MIXERS
DIFF_QK_DIM = HEAD_DIM // 2
D_FF = 2816
CONV_WIDTH = 3
NUM_BUCKETS = 32
MAX_EXACT = NUM_BUCKETS // 2
MAX_DISTANCE = 128
N_BIAS_HEADS = 2 * HPM
Q_BLOCK = 128
CMP_LEN = 32
CMP_STRIDE = 16
SEL_LEN = 64
N_SEL = 16
WINDOW = 512
FORCED_SCORE = 1.0e4
NEG = -1.0e30
EPS = 1e-6

SPLIT_SIZES = (
    GW, GW, GW,
    GW, GW, GW, HPM,
    GW, GW, GW,
    GW, HEAD_DIM, HEAD_DIM, HEAD_DIM, HEAD_DIM, HEAD_DIM, HEAD_DIM, 3 * HPM,
)
P_IN = sum(SPLIT_SIZES)

kernel_name = "hymba_style_hybrid_diff_fox_stickbreak_nsa"


def _split_points(sizes):
    pts, acc = [], 0
    for s in sizes[:-1]:
        acc += s
        pts.append(acc)
    return pts


def rmsnorm(x, g):
    xf = x.astype(jnp.float32)
    y = xf * lax.rsqrt(jnp.mean(xf * xf, axis=-1, keepdims=True) + EPS)
    return (y * g).astype(x.dtype)


def _heads(t, n_heads):
    b, s, _ = t.shape
    return t.reshape(b, s, n_heads, -1).transpose(0, 2, 1, 3)


def _merge_blocks(o):
    nb, b, h, qb, dv = o.shape
    return o.transpose(1, 0, 3, 2, 4).reshape(b, nb * qb, h * dv)


def _t5_bucket(dist):
    n = jnp.maximum(dist, 0)
    large = MAX_EXACT + (jnp.log(jnp.maximum(n, 1).astype(jnp.float32) / MAX_EXACT)
                         / math.log(MAX_DISTANCE / MAX_EXACT) * (NUM_BUCKETS - MAX_EXACT)).astype(jnp.int32)
    return jnp.where(n < MAX_EXACT, n, jnp.minimum(large, NUM_BUCKETS - 1))


def _t5_bias(table, dist):
    return jnp.moveaxis(table[_t5_bucket(dist)], -1, -3).astype(jnp.float32)


def diff_attention(q, k, v, g_q, g_k, lam_vec, sub_g, table, lam_init):
    q = rmsnorm(q, g_q)
    k = rmsnorm(k, g_k)
    lam_f = lam_vec.astype(jnp.float32)
    lam = jnp.exp(jnp.sum(lam_f[0] * lam_f[1])) - jnp.exp(jnp.sum(lam_f[2] * lam_f[3])) + lam_init
    scale = DIFF_QK_DIM ** -0.5
    s_len = q.shape[2]
    k_pos = jnp.arange(s_len)

    def block(i):
        qb = lax.dynamic_slice_in_dim(q, i * Q_BLOCK, Q_BLOCK, axis=2)
        q_pos = i * Q_BLOCK + jnp.arange(Q_BLOCK)
        dist = q_pos[:, None] - k_pos[None, :]
        s = jnp.einsum('bhqmd,bhkmd->bhmqk', qb, k).astype(jnp.float32) * scale
        s = s + _t5_bias(table, dist)[:, None]
        s = jnp.where(dist >= 0, s, NEG)
        p = jax.nn.softmax(s, axis=-1)
        w = p[:, :, 0] - lam * p[:, :, 1]
        return jnp.einsum('bhqk,bhkd->bhqd', w.astype(v.dtype), v)

    o = lax.map(block, jnp.arange(s_len // Q_BLOCK))
    o = rmsnorm(o, sub_g) * (1.0 - lam_init)
    return _merge_blocks(o)


def forgetting_attention(q, k, v, f_logit, g_q, g_k):
    q = rmsnorm(q, g_q)
    k = rmsnorm(k, g_k)
    log_f = jax.nn.log_sigmoid(f_logit.astype(jnp.float32)).transpose(0, 2, 1)
    cum = jnp.cumsum(log_f, axis=-1)
    scale = HEAD_DIM ** -0.5
    s_len = q.shape[2]
    k_pos = jnp.arange(s_len)

    def block(i):
        qb = lax.dynamic_slice_in_dim(q, i * Q_BLOCK, Q_BLOCK, axis=2)
        cq = lax.dynamic_slice_in_dim(cum, i * Q_BLOCK, Q_BLOCK, axis=2)
        q_pos = i * Q_BLOCK + jnp.arange(Q_BLOCK)
        dist = q_pos[:, None] - k_pos[None, :]
        s = jnp.einsum('bhqd,bhkd->bhqk', qb, k).astype(jnp.float32) * scale
        s = s + (cq[..., :, None] - cum[..., None, :])
        s = jnp.where(dist >= 0, s, NEG)
        p = jax.nn.softmax(s, axis=-1)
        return jnp.einsum('bhqk,bhkd->bhqd', p.astype(v.dtype), v)

    return _merge_blocks(lax.map(block, jnp.arange(s_len // Q_BLOCK)))


def stick_breaking_attention(q, k, v):
    scale = HEAD_DIM ** -0.5
    s_len = q.shape[2]
    k_pos = jnp.arange(s_len)

    def block(i):
        qb = lax.dynamic_slice_in_dim(q, i * Q_BLOCK, Q_BLOCK, axis=2)
        q_pos = i * Q_BLOCK + jnp.arange(Q_BLOCK)
        past = (q_pos[:, None] - k_pos[None, :]) > 0
        z = jnp.einsum('bhqd,bhkd->bhqk', qb, k).astype(jnp.float32) * scale
        log_beta = jax.nn.log_sigmoid(z)
        log_keep = jnp.where(past, jax.nn.log_sigmoid(-z), 0.0)
        tail = lax.cumsum(log_keep, axis=3, reverse=True) - log_keep
        a = jnp.where(past, jnp.exp(log_beta + tail), 0.0)
        return jnp.einsum('bhqk,bhkd->bhqd', a.astype(v.dtype), v)

    return _merge_blocks(lax.map(block, jnp.arange(s_len // Q_BLOCK)))


def native_sparse_attention(q, k_cmp, v_cmp, k_slc, v_slc, k_win, v_win, gate_logit,
                            g_q, g_k, pe, phi_w, table):
    b, h, s_len, d = q.shape
    scale = d ** -0.5
    q = rmsnorm(q, g_q)
    n_cmp = (s_len - CMP_LEN) // CMP_STRIDE + 1
    cmp_idx = np.arange(n_cmp)[:, None] * CMP_STRIDE + np.arange(CMP_LEN)[None, :]
    kc = (k_cmp[:, cmp_idx] + pe[0]).reshape(b, n_cmp, CMP_LEN * d) @ phi_w[0]
    vc = (v_cmp[:, cmp_idx] + pe[1]).reshape(b, n_cmp, CMP_LEN * d) @ phi_w[1]
    kc = rmsnorm(kc, g_k[0])
    ks = rmsnorm(k_slc, g_k[1])
    kw = rmsnorm(k_win, g_k[2])
    cmp_end = jnp.asarray(cmp_idx[:, -1])
    n_sel_blocks = s_len // SEL_LEN
    n_sel = min(N_SEL, n_sel_blocks)
    c_start = np.arange(n_cmp) * CMP_STRIDE
    s_start = np.arange(n_sel_blocks) * SEL_LEN
    overlap = jnp.asarray(((c_start[:, None] < s_start[None, :] + SEL_LEN)
                           & (s_start[None, :] < c_start[:, None] + CMP_LEN)).astype(np.float32))
    sel_start = jnp.asarray(s_start)
    blk_ids = jnp.arange(n_sel_blocks)
    kw_pad = jnp.pad(kw, ((0, 0), (WINDOW, 0), (0, 0)))
    vw_pad = jnp.pad(v_win, ((0, 0), (WINDOW, 0), (0, 0)))
    gates = jax.nn.sigmoid(gate_logit.astype(jnp.float32)).reshape(b, s_len, 3, h)
    gather = jax.vmap(lambda arr, idx: arr[idx])

    def block(i):
        q_pos = i * Q_BLOCK + jnp.arange(Q_BLOCK)
        qb = lax.dynamic_slice_in_dim(q, i * Q_BLOCK, Q_BLOCK, axis=2)
        dist_c = q_pos[:, None] - cmp_end[None, :]
        valid_c = dist_c >= 0
        s_c = jnp.einsum('bhqd,bnd->bhqn', qb, kc).astype(jnp.float32) * scale + _t5_bias(table, dist_c)
        p_c = jnp.where(valid_c, jax.nn.softmax(jnp.where(valid_c, s_c, NEG), axis=-1), 0.0)
        o_c = jnp.einsum('bhqn,bnd->bhqd', p_c.astype(vc.dtype), vc)
        imp = jnp.einsum('bhqn,nj->bqj', p_c, overlap)
        forced = (blk_ids[None, :] == (q_pos // SEL_LEN)[:, None]) | (blk_ids[None, :] == 0)
        imp = jnp.where(forced, FORCED_SCORE, imp)
        imp = jnp.where(sel_start[None, :] <= q_pos[:, None], imp, -1.0)
        _, sel = lax.top_k(imp, n_sel)
        tok = (sel[..., None] * SEL_LEN + jnp.arange(SEL_LEN)).reshape(b, Q_BLOCK, n_sel * SEL_LEN)
        ks_g = gather(ks, tok.reshape(b, -1)).reshape(b, Q_BLOCK, n_sel * SEL_LEN, d)
        vs_g = gather(v_slc, tok.reshape(b, -1)).reshape(b, Q_BLOCK, n_sel * SEL_LEN, d)
        dist_s = q_pos[None, :, None] - tok
        s_s = jnp.einsum('bhqd,bqkd->bhqk', qb, ks_g).astype(jnp.float32) * scale + _t5_bias(table, dist_s)
        s_s = jnp.where((dist_s >= 0)[:, None], s_s, NEG)
        p_s = jax.nn.softmax(s_s, axis=-1)
        o_s = jnp.einsum('bhqk,bqkd->bhqd', p_s.astype(vs_g.dtype), vs_g)
        kwb = lax.dynamic_slice_in_dim(kw_pad, i * Q_BLOCK, Q_BLOCK + WINDOW, axis=1)
        vwb = lax.dynamic_slice_in_dim(vw_pad, i * Q_BLOCK, Q_BLOCK + WINDOW, axis=1)
        k_pos_w = i * Q_BLOCK - WINDOW + jnp.arange(Q_BLOCK + WINDOW)
        dist_w = q_pos[:, None] - k_pos_w[None, :]
        valid_w = (dist_w >= 0) & (dist_w < WINDOW) & (k_pos_w >= 0)[None, :]
        s_w = jnp.einsum('bhqd,bkd->bhqk', qb, kwb).astype(jnp.float32) * scale + _t5_bias(table, dist_w)
        p_w = jax.nn.softmax(jnp.where(valid_w, s_w, NEG), axis=-1)
        o_w = jnp.einsum('bhqk,bkd->bhqd', p_w.astype(vwb.dtype), vwb)
        g = lax.dynamic_slice_in_dim(gates, i * Q_BLOCK, Q_BLOCK, axis=1).transpose(0, 2, 3, 1)
        o = g[:, 0, :, :, None] * o_c + g[:, 1, :, :, None] * o_s + g[:, 2, :, :, None] * o_w
        return o.astype(q.dtype)

    return _merge_blocks(lax.map(block, jnp.arange(s_len // Q_BLOCK)))


def conv_ffn(h, w_up, conv_w, conv_b, w_down):
    u = h @ w_up
    gate, val = jnp.split(u, 2, axis=-1)
    gate = lax.conv_general_dilated(gate, conv_w, window_strides=(1,), padding=[(CONV_WIDTH - 1, 0)],
                                    dimension_numbers=('NWC', 'WIO', 'NWC'),
                                    feature_group_count=D_FF) + conv_b
    return (jax.nn.silu(gate) * val) @ w_down


def setup_inputs(seed: int = 0) -> dict:
    key = jax.random.key(seed)
    ks = jax.random.split(key, 26)

    def nrm(i, shape, s):
        return jax.random.normal(ks[i], shape, jnp.float32) * s

    return {
        "x": nrm(0, (BATCH, SEQ, D_MODEL), 1.0),
        "c": nrm(1, (BATCH, D_MODEL), 1.0),
        "rel_bias": nrm(2, (NUM_BUCKETS, N_BIAS_HEADS), 0.5),
        "ada_w": nrm(3, (DEPTH, D_MODEL, 6 * D_MODEL), 0.5 * D_MODEL ** -0.5),
        "ada_b": nrm(4, (DEPTH, 6 * D_MODEL), 0.02),
        "norm_mix_g": 1.0 + nrm(5, (DEPTH, D_MODEL), 0.02),
        "norm_ffn_g": 1.0 + nrm(6, (DEPTH, D_MODEL), 0.02),
        "w_in": nrm(7, (DEPTH, D_MODEL, P_IN), D_MODEL ** -0.5),
        "w_out": nrm(8, (DEPTH, MIX_WIDTH, D_MODEL), MIX_WIDTH ** -0.5),
        "diff_qnorm_g": 1.0 + nrm(9, (DEPTH, DIFF_QK_DIM), 0.02),
        "diff_knorm_g": 1.0 + nrm(10, (DEPTH, DIFF_QK_DIM), 0.02),
        "diff_lambda": nrm(11, (DEPTH, 4, DIFF_QK_DIM), 0.1),
        "diff_subln_g": 1.0 + nrm(12, (DEPTH, 2 * DIFF_QK_DIM), 0.02),
        "fox_qnorm_g": 1.0 + nrm(13, (DEPTH, HEAD_DIM), 0.02),
        "fox_knorm_g": 1.0 + nrm(14, (DEPTH, HEAD_DIM), 0.02),
        "fox_b_f": 3.0 + nrm(15, (DEPTH, HPM), 0.1),
        "nsa_qnorm_g": 1.0 + nrm(16, (DEPTH, HEAD_DIM), 0.02),
        "nsa_knorm_g": 1.0 + nrm(17, (DEPTH, 3, HEAD_DIM), 0.02),
        "nsa_pe": nrm(18, (DEPTH, 2, CMP_LEN, HEAD_DIM), 0.5),
        "nsa_phi_w": nrm(19, (DEPTH, 2, CMP_LEN * HEAD_DIM, HEAD_DIM), (CMP_LEN * HEAD_DIM) ** -0.5),
        "ffn_w_up": nrm(20, (DEPTH, D_MODEL, 2 * D_FF), D_MODEL ** -0.5),
        "ffn_conv_w": nrm(21, (DEPTH, CONV_WIDTH, 1, D_FF), CONV_WIDTH ** -0.5),
        "ffn_conv_b": nrm(22, (DEPTH, D_FF), 0.02),
        "ffn_w_down": nrm(23, (DEPTH, D_FF, D_MODEL), D_FF ** -0.5),
    }


def reference(x, c, rel_bias, ada_w, ada_b, norm_mix_g, norm_ffn_g, w_in, w_out,
              diff_qnorm_g, diff_knorm_g, diff_lambda, diff_subln_g,
              fox_qnorm_g, fox_knorm_g, fox_b_f,
              nsa_qnorm_g, nsa_knorm_g, nsa_pe, nsa_phi_w,
              ffn_w_up, ffn_conv_w, ffn_conv_b, ffn_w_down):
    b, s_len, _ = x.shape
    table_diff = rel_bias[:, :HPM]
    table_nsa = rel_bias[:, HPM:]
    split_pts = _split_points(SPLIT_SIZES)
    c_act = jax.nn.silu(c)
    for l in range(DEPTH):
        mod = c_act @ ada_w[l] + ada_b[l]
        sh_a, sc_a, gt_a, sh_m, sc_m, gt_m = [m[:, None, :] for m in jnp.split(mod, 6, axis=-1)]
        h = rmsnorm(x, norm_mix_g[l]) * (1.0 + sc_a) + sh_a
        proj = h @ w_in[l]
        (a_q, a_k, a_v, b_q, b_k, b_v, b_f, c_q, c_k, c_v,
         d_q, d_kc, d_vc, d_ks, d_vs, d_kw, d_vw, d_g) = jnp.split(proj, split_pts, axis=-1)
        lam_init = 0.8 - 0.6 * math.exp(-0.3 * l)
        o_a = diff_attention(_heads(a_q, HPM).reshape(b, HPM, s_len, 2, DIFF_QK_DIM),
                             _heads(a_k, HPM).reshape(b, HPM, s_len, 2, DIFF_QK_DIM),
                             _heads(a_v, HPM), diff_qnorm_g[l], diff_knorm_g[l], diff_lambda[l],
                             diff_subln_g[l], table_diff, lam_init)
        o_b = forgetting_attention(_heads(b_q, HPM), _heads(b_k, HPM), _heads(b_v, HPM),
                                   b_f + fox_b_f[l], fox_qnorm_g[l], fox_knorm_g[l])
        o_c = stick_breaking_attention(_heads(c_q, HPM), _heads(c_k, HPM), _heads(c_v, HPM))
        o_d = native_sparse_attention(_heads(d_q, HPM), d_kc, d_vc, d_ks, d_vs, d_kw, d_vw, d_g,
                                      nsa_qnorm_g[l], nsa_knorm_g[l], nsa_pe[l], nsa_phi_w[l], table_nsa)
        mixed = jnp.concatenate([o_a, o_b, o_c, o_d], axis=-1) @ w_out[l]
        x = x + gt_a * mixed
        h = rmsnorm(x, norm_ffn_g[l]) * (1.0 + sc_m) + sh_m
        x = x + gt_m * conv_ffn(h, ffn_w_up[l], ffn_conv_w[l], ffn_conv_b[l], ffn_w_down[l])
    return x
```

```python
import functools
import math

import numpy as np
import jax
import jax.numpy as jnp
from jax import lax
from jax.experimental import pallas as pl
from jax.experimental.pallas import tpu as pltpu

HEAD_DIM = 64
HPM = 4
GW = HPM * HEAD_DIM
DIFF_QK_DIM = HEAD_DIM // 2
NUM_BUCKETS = 32
MAX_EXACT = NUM_BUCKETS // 2
MAX_DISTANCE = 128
CMP_LEN = 32
CMP_STRIDE = 16
SEL_LEN = 64
N_SEL = 16
WINDOW = 512
FORCED_SCORE = 1.0e4
NEG = -1.0e30
EPS = 1e-6
CONV_WIDTH = 3

T = 256
TM = 256
FF_CHUNK = 256
VMEM_LIMIT = 56 * 1024 * 1024

F32 = jnp.float32
BF16 = jnp.bfloat16


def _dot(a, b):
    return jnp.dot(a, b, preferred_element_type=F32)


def _split2(x):
    hi = x.astype(BF16)
    lo = (x - hi.astype(F32)).astype(BF16)
    return hi, lo


def _log_sigmoid(z):
    return jnp.minimum(z, 0.0) - jnp.log(1.0 + jnp.exp(-jnp.abs(z)))


def _sigmoid(z):
    return 1.0 / (1.0 + jnp.exp(-z))


def _whole(shape):
    nd = len(shape)
    return pl.BlockSpec(shape, lambda *_: (0,) * nd)


def _smem():
    return pl.BlockSpec(memory_space=pltpu.SMEM)


def _t5_bucket_np(dist):
    n = np.maximum(dist, 0)
    ratio = math.log(MAX_DISTANCE / MAX_EXACT)
    out = None
    for dt in (np.float32, np.float64):
        large = MAX_EXACT + (np.log(np.maximum(n, 1).astype(dt) / dt(MAX_EXACT)) / dt(ratio)
                             * dt(NUM_BUCKETS - MAX_EXACT)).astype(np.int32)
        b = np.where(n < MAX_EXACT, n, np.minimum(large, NUM_BUCKETS - 1)).astype(np.int32)
        assert out is None or np.array_equal(out, b)
        out = b
    return out


def _bucket_tiles():
    r = np.arange(T)[:, None]
    c = np.arange(T)[None, :]
    diag = _t5_bucket_np(c - r)
    near = _t5_bucket_np(T + c - r)
    rb = np.arange(2 * T // CMP_STRIDE)[:, None]
    band = _t5_bucket_np(c - CMP_STRIDE * rb + (T - CMP_LEN + 1))
    return np.concatenate([diag, near, band], axis=0)


def _block_ones(n, group):
    i = np.arange(n)
    return (i[:, None] // group == i[None, :] // group).astype(np.float32)


def _bias_kernel(tbl_ref, bkt_ref, out_ref):
    b = bkt_ref[...]
    for h in range(2 * HPM):
        acc = jnp.zeros(b.shape, F32)
        for k in range(NUM_BUCKETS):
            acc = jnp.where(b == k, tbl_ref[k, h], acc)
        out_ref[h] = acc


def _bias_tiles(rel_bias):
    bkt = jnp.asarray(_bucket_tiles())
    return pl.pallas_call(
        _bias_kernel,
        out_shape=jax.ShapeDtypeStruct((2 * HPM,) + bkt.shape, F32),
        in_specs=[_smem(), _whole(bkt.shape)],
        out_specs=_whole((2 * HPM,) + bkt.shape),
        name="bias_tiles",
    )(rel_bias, bkt)


def _adaln_kernel(c_ref, w_ref, b_ref, o_ref):
    c = c_ref[...]
    ca = c * _sigmoid(c)
    o_ref[0] = _dot(ca.astype(BF16), w_ref[0].astype(BF16)) + b_ref[0]


def _adaln(c, ada_w, ada_b):
    depth, d, n = ada_w.shape
    bsz = c.shape[0]
    rows = -(-bsz // 8) * 8
    cp = jnp.pad(c, ((0, rows - bsz), (0, 0)))
    tn = 1536
    out = pl.pallas_call(
        _adaln_kernel,
        out_shape=jax.ShapeDtypeStruct((depth, rows, n), F32),
        grid=(depth, n // tn),
        in_specs=[pl.BlockSpec((rows, d), lambda l, j: (0, 0)),
                  pl.BlockSpec((1, d, tn), lambda l, j: (l, 0, j)),
                  pl.BlockSpec((1, 1, tn), lambda l, j: (l, 0, j))],
        out_specs=pl.BlockSpec((1, rows, tn), lambda l, j: (l, 0, j)),
        name="adaln",
    )(cp, ada_w, ada_b.reshape(depth, 1, n))
    return out[:, :bsz]


N_PROJ = 11 * GW + 2 * 128


def _inproj_kernel(x_ref, g_ref, sc_ref, sh_ref, w_ref, gains_ref, bd32_ref, bd64_ref,
                   aq_ref, ak_ref, av_ref, bq_ref, bk_ref, bv_ref, cq_ref, ck_ref, cv_ref, dq_ref,
                   dks_ref, dkw_ref, dvs_ref, dvw_ref, dkc_ref, dvc_ref, sm_ref):
    x = x_ref[0]
    ms = jnp.mean(x * x, axis=-1, keepdims=True)
    h = x * lax.rsqrt(ms + EPS) * g_ref[...]
    h = h * (1.0 + sc_ref[0]) + sh_ref[0]
    hb = h.astype(BF16)

    def proj(group, width=GW):
        off = group * GW
        return _dot(hb, w_ref[:, off:off + width])

    def segnorm(y, bd_ref, inv_n, gain_row):
        hi, lo = _split2(y * y)
        ss = _dot(hi, bd_ref[...]) + _dot(lo, bd_ref[...])
        return y * lax.rsqrt(ss * inv_n + EPS) * gains_ref[gain_row:gain_row + 1, :]

    def put_t(ref, y):
        ref[0, 0] = y.T.astype(BF16)

    put_t(aq_ref, segnorm(proj(0), bd32_ref, 1.0 / DIFF_QK_DIM, 0))
    ak_ref[0] = segnorm(proj(1), bd32_ref, 1.0 / DIFF_QK_DIM, 1).astype(BF16)
    put_t(av_ref, proj(2))
    put_t(bq_ref, segnorm(proj(3), bd64_ref, 1.0 / HEAD_DIM, 2))
    bk_ref[0] = segnorm(proj(4), bd64_ref, 1.0 / HEAD_DIM, 3).astype(BF16)
    put_t(bv_ref, proj(5))
    put_t(cq_ref, proj(6) * (HEAD_DIM ** -0.5))
    ck_ref[0] = proj(7).astype(BF16)
    put_t(cv_ref, proj(8))
    put_t(dq_ref, segnorm(proj(9), bd64_ref, 1.0 / HEAD_DIM, 4))

    y = proj(10)
    yn = segnorm(y, bd64_ref, 1.0 / HEAD_DIM, 5)
    dks_ref[0] = yn[:, 0:HEAD_DIM].astype(BF16)
    dkw_ref[0] = yn[:, 2 * HEAD_DIM:3 * HEAD_DIM].astype(BF16)
    yt = y.T
    dvs_ref[0, 0] = yt[HEAD_DIM:2 * HEAD_DIM].astype(BF16)
    dvw_ref[0, 0] = yt[3 * HEAD_DIM:4 * HEAD_DIM].astype(BF16)

    cmp = _dot(hb, w_ref[:, 11 * GW:11 * GW + 128])
    dkc_ref[0] = cmp[:, 0:HEAD_DIM]
    dvc_ref[0] = cmp[:, HEAD_DIM:2 * HEAD_DIM]
    sm_ref[0] = _dot(hb, w_ref[:, 11 * GW + 128:11 * GW + 256])


def _inproj(x, g, sc, sh, w_re, gains, bd32, bd64):
    bsz, s, d = x.shape
    nk = s // T
    tok = lambda width, dt: jax.ShapeDtypeStruct((bsz, s, width), dt)
    tr = lambda rows: jax.ShapeDtypeStruct((bsz, nk, rows, T), BF16)
    tok_spec = lambda width: pl.BlockSpec((1, TM, width), lambda b, i: (b, i, 0))
    tr_spec = lambda rows: pl.BlockSpec((1, 1, rows, T), lambda b, i: (b, i, 0, 0))
    vec = pl.BlockSpec((1, 1, d), lambda b, i: (b, 0, 0))
    out_shape = [tr(GW), tok(GW, BF16), tr(GW)] * 3 + [tr(GW)] + [
        tok(HEAD_DIM, BF16), tok(HEAD_DIM, BF16), tr(HEAD_DIM), tr(HEAD_DIM),
        tok(HEAD_DIM, F32), tok(HEAD_DIM, F32), tok(128, F32)]
    out_specs = [tr_spec(GW), tok_spec(GW), tr_spec(GW)] * 3 + [tr_spec(GW)] + [
        tok_spec(HEAD_DIM), tok_spec(HEAD_DIM), tr_spec(HEAD_DIM), tr_spec(HEAD_DIM),
        tok_spec(HEAD_DIM), tok_spec(HEAD_DIM), tok_spec(128)]
    return pl.pallas_call(
        _inproj_kernel,
        out_shape=out_shape,
        grid=(bsz, s // TM),
        in_specs=[pl.BlockSpec((1, TM, d), lambda b, i: (b, i, 0)),
                  _whole((1, d)), vec, vec,
                  _whole(w_re.shape), _whole(gains.shape), _whole(bd32.shape), _whole(bd64.shape)],
        out_specs=out_specs,
        compiler_params=pltpu.CompilerParams(
            dimension_semantics=("parallel", "parallel"), vmem_limit_bytes=VMEM_LIMIT),
        name="inproj",
    )(x, g, sc, sh, w_re, gains, bd32, bd64)


def _scalars_kernel(sm_ref, bf_ref, tri_ref, ck_ref, st_ref):
    s = sm_ref.shape[1]
    carry = jnp.zeros((1, 128), F32)
    for blk in range(s // T):
        rows = slice(blk * T, (blk + 1) * T)
        lf = _log_sigmoid(sm_ref[0, rows, :] + bf_ref[...])
        h1 = lf.astype(BF16)
        r1 = lf - h1.astype(F32)
        h2 = r1.astype(BF16)
        h3 = (r1 - h2.astype(F32)).astype(BF16)
        tri = tri_ref[...]
        cb = _dot(tri, h1) + _dot(tri, h2) + _dot(tri, h3) + carry
        ck_ref[0, rows, :] = cb
        carry = cb[T - 1:T, :]
        col = lax.broadcasted_iota(jnp.int32, (T, 128), 1)
        comb = jnp.where(col < HPM, cb, _sigmoid(sm_ref[0, rows, :]))
        st_ref[0, :, rows] = comb.T[0:16]


def _scalars(smalls, bf_row, tri):
    bsz, s, _ = smalls.shape
    return pl.pallas_call(
        _scalars_kernel,
        out_shape=[jax.ShapeDtypeStruct((bsz, s, 128), F32), jax.ShapeDtypeStruct((bsz, 16, s), F32)],
        grid=(bsz,),
        in_specs=[pl.BlockSpec((1, s, 128), lambda b: (b, 0, 0)), _whole((1, 128)), _whole((T, T))],
        out_specs=[pl.BlockSpec((1, s, 128), lambda b: (b, 0, 0)),
                   pl.BlockSpec((1, 16, s), lambda b: (b, 0, 0))],
        compiler_params=pltpu.CompilerParams(dimension_semantics=("parallel",)),
        name="token_scalars",
    )(smalls, bf_row, tri)


def _compress_kernel(xk_ref, xv_ref, pe_ref, w_ref, gk_ref, kc_ref, vct_ref):
    half = CMP_STRIDE * HEAD_DIM
    n = xk_ref.shape[1]

    def windows(x, pe_a, pe_b, wi):
        ya = _dot((x + pe_a).astype(BF16), w_ref[wi, 0:half, :])
        yb = _dot((x + pe_b).astype(BF16), w_ref[wi, half:2 * half, :])
        return ya + pltpu.roll(yb, n - 1, 0)

    kc = windows(xk_ref[0], pe_ref[0:1], pe_ref[1:2], 0)
    ss = jnp.sum(kc * kc, axis=-1, keepdims=True)
    kc = kc * lax.rsqrt(ss * (1.0 / HEAD_DIM) + EPS) * gk_ref[...]
    kc_ref[0] = kc[:, 0:HEAD_DIM].astype(BF16)
    vc = windows(xv_ref[0], pe_ref[2:3], pe_ref[3:4], 1)
    vct_ref[0] = vc.T[0:HEAD_DIM].astype(BF16)


def _compress(xk, xv, pe_rows, w_pad, gk_pad):
    bsz, n, width = xk.shape
    return pl.pallas_call(
        _compress_kernel,
        out_shape=[jax.ShapeDtypeStruct((bsz, n, HEAD_DIM), BF16),
                   jax.ShapeDtypeStruct((bsz, HEAD_DIM, n), BF16)],
        grid=(bsz,),
        in_specs=[pl.BlockSpec((1, n, width), lambda b: (b, 0, 0)),
                  pl.BlockSpec((1, n, width), lambda b: (b, 0, 0)),
                  _whole(pe_rows.shape), _whole(w_pad.shape), _whole(gk_pad.shape)],
        out_specs=[pl.BlockSpec((1, n, HEAD_DIM), lambda b: (b, 0, 0)),
                   pl.BlockSpec((1, HEAD_DIM, n), lambda b: (b, 0, 0))],
        compiler_params=pltpu.CompilerParams(dimension_semantics=("parallel",)),
        name="nsa_compress",
    )(xk, xv, pe_rows, w_pad, gk_pad)


def _masked_q_blocks(qt, seg):
    qf = qt.astype(F32)
    row = lax.broadcasted_iota(jnp.int32, qf.shape, 0)
    blocks = [jnp.where((row >= seg * i) & (row < seg * (i + 1)), qf, 0.0).astype(BF16)
              for i in range(GW // seg)]
    return jnp.concatenate(blocks, axis=1)


def _softmax_update(s, pv, m_ref, l_ref, acc_ref, idx, rows):
    m_old = m_ref[idx:idx + 1, :]
    m_new = jnp.maximum(m_old, jnp.max(s, axis=0, keepdims=True))
    alpha = jnp.exp(m_old - m_new)
    p = jnp.exp(s - m_new)
    l_ref[idx:idx + 1, :] = alpha * l_ref[idx:idx + 1, :] + jnp.sum(p, axis=0, keepdims=True)
    acc_ref[rows, :] = alpha * acc_ref[rows, :] + pv(p.astype(BF16))
    m_ref[idx:idx + 1, :] = m_new


def _causal_tile():
    r = lax.broadcasted_iota(jnp.int32, (T, T), 0)
    c = lax.broadcasted_iota(jnp.int32, (T, T), 1)
    return r, c


def _attn_call(kernel, name, bsz, nq, in_arrays, in_specs, scratch, out_width=GW):
    return pl.pallas_call(
        kernel,
        out_shape=jax.ShapeDtypeStruct((bsz, nq * T, out_width), BF16),
        grid=(bsz, nq),
        in_specs=in_specs,
        out_specs=pl.BlockSpec((1, T, out_width), lambda b, i: (b, i, 0)),
        scratch_shapes=scratch,
        compiler_params=pltpu.CompilerParams(
            dimension_semantics=("parallel", "arbitrary"), vmem_limit_bytes=VMEM_LIMIT),
        name=name,
    )(*in_arrays)


def _q_spec(rows=GW):
    return pl.BlockSpec((1, 1, rows, T), lambda b, i: (b, i, 0, 0))


def _seq_spec(shape):
    nd = len(shape)
    return pl.BlockSpec((1,) + tuple(shape[1:]), lambda b, i: (b,) + (0,) * (nd - 1))


def _diff_kernel(far_ref, cst_ref, qt_ref, k_ref, vt_ref, bias_ref, lam_ref, subg_ref, o_ref,
                 m_ref, l_ref, acc_ref):
    qi = pl.program_id(1)
    qz = _masked_q_blocks(qt_ref[0, 0], DIFF_QK_DIM)
    m_ref[...] = jnp.full(m_ref.shape, NEG, F32)
    l_ref[...] = jnp.zeros(l_ref.shape, F32)
    acc_ref[...] = jnp.zeros(acc_ref.shape, F32)
    r, c = _causal_tile()

    def step(ki, mode):
        kt = k_ref[0, ki]
        for h in range(HPM):
            vth = vt_ref[0, ki, h * HEAD_DIM:(h + 1) * HEAD_DIM, :]
            s2 = _dot(kt, qz[:, 2 * h * T:(2 * h + 2) * T])
            for mp in range(2):
                s = s2[:, mp * T:(mp + 1) * T]
                if mode == "far":
                    s = s + far_ref[h]
                elif mode == "near":
                    s = s + bias_ref[h, T:2 * T, :]
                else:
                    s = jnp.where(r <= c, s + bias_ref[h, 0:T, :], NEG)
                idx = 2 * h + mp
                _softmax_update(s, lambda p: _dot(vth, p), m_ref, l_ref, acc_ref, idx,
                                slice(idx * HEAD_DIM, (idx + 1) * HEAD_DIM))

    def far_body(ki, carry):
        step(ki, "far")
        return carry

    lax.fori_loop(0, jnp.maximum(qi - 1, 0), far_body, 0)

    @pl.when(qi >= 1)
    def _():
        step(qi - 1, "near")

    step(qi, "diag")

    lam_init = cst_ref[0]
    lv = lam_ref[...]
    lam = (jnp.exp(jnp.sum(lv[0:1] * lv[1:2], axis=-1, keepdims=True))
           - jnp.exp(jnp.sum(lv[2:3] * lv[3:4], axis=-1, keepdims=True)) + lam_init)
    outs = []
    for h in range(HPM):
        o1 = acc_ref[(2 * h) * HEAD_DIM:(2 * h + 1) * HEAD_DIM, :] / l_ref[2 * h:2 * h + 1, :]
        o2 = acc_ref[(2 * h + 1) * HEAD_DIM:(2 * h + 2) * HEAD_DIM, :] / l_ref[2 * h + 1:2 * h + 2, :]
        o = o1 - lam * o2
        ms = jnp.mean(o * o, axis=0, keepdims=True)
        outs.append(o * lax.rsqrt(ms + EPS) * subg_ref[...] * (1.0 - lam_init))
    o_ref[0] = jnp.concatenate(outs, axis=0).T.astype(BF16)


def _diff_attention(far, cst, qt, k4, vt, bias, lam, subg):
    bsz, nq = qt.shape[0], qt.shape[1]
    return _attn_call(
        _diff_kernel, "diff_attention", bsz, nq,
        [far, cst, qt, k4, vt, bias, lam, subg],
        [_smem(), _smem(), _q_spec(), _seq_spec(k4.shape), _seq_spec(vt.shape),
         _whole(bias.shape), _whole(lam.shape), _whole(subg.shape)],
        [pltpu.VMEM((8, T), F32), pltpu.VMEM((8, T), F32), pltpu.VMEM((2 * GW, T), F32)])


def _fox_kernel(qt_ref, k_ref, vt_ref, cq_ref, ck_ref, o_ref, m_ref, l_ref, acc_ref):
    qi = pl.program_id(1)
    qz = _masked_q_blocks(qt_ref[0, 0], HEAD_DIM)
    m_ref[...] = jnp.full(m_ref.shape, NEG, F32)
    l_ref[...] = jnp.zeros(l_ref.shape, F32)
    acc_ref[...] = jnp.zeros(acc_ref.shape, F32)
    r, c = _causal_tile()

    def step(ki, diag):
        kt = k_ref[0, ki]
        ckt = ck_ref[0, ki]
        for h in range(HPM):
            vth = vt_ref[0, ki, h * HEAD_DIM:(h + 1) * HEAD_DIM, :]
            s = _dot(kt, qz[:, h * T:(h + 1) * T])
            s = s + (cq_ref[0, h:h + 1, :] - ckt[:, h:h + 1])
            if diag:
                s = jnp.where(r <= c, s, NEG)
            _softmax_update(s, lambda p: _dot(vth, p), m_ref, l_ref, acc_ref, h,
                            slice(h * HEAD_DIM, (h + 1) * HEAD_DIM))

    def body(ki, carry):
        step(ki, False)
        return carry

    lax.fori_loop(0, qi, body, 0)
    step(qi, True)
    outs = [acc_ref[h * HEAD_DIM:(h + 1) * HEAD_DIM, :] / l_ref[h:h + 1, :] for h in range(HPM)]
    o_ref[0] = jnp.concatenate(outs, axis=0).T.astype(BF16)


def _fox_attention(qt, k4, vt, st, ck4):
    bsz, nq = qt.shape[0], qt.shape[1]
    return _attn_call(
        _fox_kernel, "forgetting_attention", bsz, nq,
        [qt, k4, vt, st, ck4],
        [_q_spec(), _seq_spec(k4.shape), _seq_spec(vt.shape),
         pl.BlockSpec((1, 16, T), lambda b, i: (b, 0, i)), _seq_spec(ck4.shape)],
        [pltpu.VMEM((8, T), F32), pltpu.VMEM((8, T), F32), pltpu.VMEM((GW, T), F32)])


def _sb_kernel(qt_ref, k_ref, vt_ref, tri_ref, o_ref, run_ref, acc_ref):
    qi = pl.program_id(1)
    qz = _masked_q_blocks(qt_ref[0, 0], HEAD_DIM)
    run_ref[...] = jnp.zeros(run_ref.shape, F32)
    acc_ref[...] = jnp.zeros(acc_ref.shape, F32)
    r, c = _causal_tile()

    def step(ki, diag):
        kt = k_ref[0, ki]
        for h in range(HPM):
            vth = vt_ref[0, ki, h * HEAD_DIM:(h + 1) * HEAD_DIM, :]
            z = _dot(kt, qz[:, h * T:(h + 1) * T])
            log_beta = _log_sigmoid(z)
            log_keep = log_beta - z
            if diag:
                log_keep = jnp.where(r < c, log_keep, 0.0)
            hi, lo = _split2(log_keep)
            tail = _dot(tri_ref[...], hi) + _dot(tri_ref[...], lo)
            a = jnp.exp(log_beta + tail + run_ref[h:h + 1, :])
            if diag:
                a = jnp.where(r < c, a, 0.0)
            rows = slice(h * HEAD_DIM, (h + 1) * HEAD_DIM)
            acc_ref[rows, :] = acc_ref[rows, :] + _dot(vth, a.astype(BF16))
            run_ref[h:h + 1, :] = run_ref[h:h + 1, :] + jnp.sum(log_keep, axis=0, keepdims=True)

    step(qi, True)

    def body(i, carry):
        step(qi - 1 - i, False)
        return carry

    lax.fori_loop(0, qi, body, 0)
    o_ref[0] = acc_ref[...].T.astype(BF16)


def _sb_attention(qt, k4, vt, tri):
    bsz, nq = qt.shape[0], qt.shape[1]
    return _attn_call(
        _sb_kernel, "stick_breaking_attention", bsz, nq,
        [qt, k4, vt, tri],
        [_q_spec(), _seq_spec(k4.shape), _seq_spec(vt.shape), _whole(tri.shape)],
        [pltpu.VMEM((8, T), F32), pltpu.VMEM((GW, T), F32)])


def _nsa_kernel(far_ref, qt_ref, kc_ref, vct_ref, ks_ref, vst_ref, kw_ref, vwt_ref, g_ref,
                bias_ref, ovt_ref, o_ref,
                sc_ref, imp_ref, sel_ref, m_ref, l_ref, acc_ref, oc_ref):
    qi = pl.program_id(1)
    q0 = qi * T
    n_cmp = kc_ref.shape[1]
    n_blk = ovt_ref.shape[0]
    n_sel = min(N_SEL, n_blk)
    band_rows = 2 * T // CMP_STRIDE
    qt = qt_ref[0, 0]
    heads = [qt[h * HEAD_DIM:(h + 1) * HEAD_DIM, :] for h in range(HPM)]
    r, c = _causal_tile()

    kc = kc_ref[0]
    ci = lax.broadcasted_iota(jnp.int32, (n_cmp, T), 0)
    ti = q0 + lax.broadcasted_iota(jnp.int32, (n_cmp, T), 1)
    valid_c = ti - CMP_STRIDE * ci - (CMP_LEN - 1) >= 0
    pc_sum = jnp.zeros((n_cmp, T), F32)
    for h in range(HPM):
        sc_ref[...] = _dot(kc, heads[h])
        delta = bias_ref[HPM + h, 2 * T:2 * T + band_rows, :] - far_ref[HPM + h]

        @pl.when(qi == 0)
        def _():
            half = band_rows // 2
            sc_ref[0:half, :] = sc_ref[0:half, :] + delta[half:band_rows]

        @pl.when(qi > 0)
        def _():
            c0 = pl.multiple_of(qi * (T // CMP_STRIDE) - band_rows // 2, 8)
            sc_ref[pl.ds(c0, band_rows), :] = sc_ref[pl.ds(c0, band_rows), :] + delta

        s = jnp.where(valid_c, sc_ref[...] + far_ref[HPM + h], NEG)
        e = jnp.exp(s - jnp.max(s, axis=0, keepdims=True))
        p = jnp.where(valid_c, e / jnp.sum(e, axis=0, keepdims=True), 0.0)
        pc_sum = pc_sum + p
        oc_ref[h * HEAD_DIM:(h + 1) * HEAD_DIM, :] = _dot(vct_ref[0], p.astype(BF16))

    hi, lo = _split2(pc_sum)
    imp = _dot(ovt_ref[...], hi) + _dot(ovt_ref[...], lo)
    ji = lax.broadcasted_iota(jnp.int32, (n_blk, T), 0)
    tq = q0 + lax.broadcasted_iota(jnp.int32, (n_blk, T), 1)
    forced = (ji == lax.shift_right_logical(tq, int(math.log2(SEL_LEN)))) | (ji == 0)
    imp = jnp.where(forced, FORCED_SCORE, imp)
    imp = jnp.where(ji * SEL_LEN <= tq, imp, -1.0)
    imp_ref[...] = imp

    def rank_body(jp, cnt):
        row = imp_ref[pl.ds(jp, 1), :]
        tie = jnp.where(row == imp, 1.0, 0.0) * jnp.where(ji > jp, 1.0, 0.0)
        return cnt + jnp.where(row > imp, 1.0, 0.0) + tie

    cnt = lax.fori_loop(0, n_blk, rank_body, jnp.zeros((n_blk, T), F32))
    sel_ref[...] = jnp.where(cnt < float(n_sel), 1.0, 0.0)

    m_ref[...] = jnp.full(m_ref.shape, NEG, F32)
    l_ref[...] = jnp.zeros(l_ref.shape, F32)
    acc_ref[...] = jnp.zeros(acc_ref.shape, F32)

    def tile_update(br, kt, vtt, mask, mode):
        for h in range(HPM):
            s = _dot(kt, heads[h])
            if mode == "far":
                s = s + far_ref[HPM + h]
            elif mode == "near":
                s = s + bias_ref[HPM + h, T:2 * T, :]
            else:
                s = s + bias_ref[HPM + h, 0:T, :]
            if mask is not None:
                s = jnp.where(mask, s, NEG)
            idx = br * HPM + h
            _softmax_update(s, lambda p: _dot(vtt, p), m_ref, l_ref, acc_ref, idx,
                            slice(idx * HEAD_DIM, (idx + 1) * HEAD_DIM))

    def sel_mask(ki):
        per = T // SEL_LEN
        rows = [jnp.broadcast_to(sel_ref[pl.ds(ki * per + i, 1), :], (SEL_LEN, T)) for i in range(per)]
        return jnp.concatenate(rows, axis=0) > 0.5

    def sel_far(ki, carry):
        tile_update(0, ks_ref[0, ki], vst_ref[0, ki], sel_mask(ki), "far")
        return carry

    lax.fori_loop(0, jnp.maximum(qi - 1, 0), sel_far, 0)

    @pl.when(qi >= 1)
    def _():
        tile_update(0, ks_ref[0, qi - 1], vst_ref[0, qi - 1], sel_mask(qi - 1), "near")

    tile_update(0, ks_ref[0, qi], vst_ref[0, qi], sel_mask(qi) & (r <= c), "diag")

    @pl.when(qi >= 2)
    def _():
        tile_update(1, kw_ref[0, qi - 2], vwt_ref[0, qi - 2], c < r, "far")

    @pl.when(qi >= 1)
    def _():
        tile_update(1, kw_ref[0, qi - 1], vwt_ref[0, qi - 1], None, "near")

    tile_update(1, kw_ref[0, qi], vwt_ref[0, qi], r <= c, "diag")

    outs = []
    for h in range(HPM):
        rows = slice(h * HEAD_DIM, (h + 1) * HEAD_DIM)
        o_s = acc_ref[rows, :] / l_ref[h:h + 1, :]
        wrows = slice((HPM + h) * HEAD_DIM, (HPM + h + 1) * HEAD_DIM)
        o_w = acc_ref[wrows, :] / l_ref[HPM + h:HPM + h + 1, :]
        g = lambda br: g_ref[0, HPM + br * HPM + h:HPM + br * HPM + h + 1, :]
        outs.append(g(0) * oc_ref[rows, :] + g(1) * o_s + g(2) * o_w)
    o_ref[0] = jnp.concatenate(outs, axis=0).T.astype(BF16)


def _nsa_attention(far, qt, kc, vct, ks4, vst, kw4, vwt, st, bias, ovt):
    bsz, nq = qt.shape[0], qt.shape[1]
    n_cmp = kc.shape[1]
    n_blk = ovt.shape[0]
    return _attn_call(
        _nsa_kernel, "native_sparse_attention", bsz, nq,
        [far, qt, kc, vct, ks4, vst, kw4, vwt, st, bias, ovt],
        [_smem(), _q_spec(), _seq_spec(kc.shape), _seq_spec(vct.shape),
         _seq_spec(ks4.shape), _seq_spec(vst.shape), _seq_spec(kw4.shape), _seq_spec(vwt.shape),
         pl.BlockSpec((1, 16, T), lambda b, i: (b, 0, i)), _whole(bias.shape), _whole(ovt.shape)],
        [pltpu.VMEM((n_cmp, T), F32), pltpu.VMEM((n_blk, T), F32), pltpu.VMEM((n_blk, T), F32),
         pltpu.VMEM((8, T), F32), pltpu.VMEM((8, T), F32), pltpu.VMEM((2 * GW, T), F32),
         pltpu.VMEM((GW, T), F32)])


def _post_kernel(x_ref, oa_ref, ob_ref, oc_ref, od_ref, wo_ref, gta_ref, g_ref, sc_ref, sh_ref,
                 gtm_ref, wup_ref, cw_ref, cb_ref, wdn_ref, y_ref, carry_ref):
    d_ff = wdn_ref.shape[0]
    mixed = (_dot(oa_ref[0], wo_ref[0:GW]) + _dot(ob_ref[0], wo_ref[GW:2 * GW])
             + _dot(oc_ref[0], wo_ref[2 * GW:3 * GW]) + _dot(od_ref[0], wo_ref[3 * GW:4 * GW]))
    x1 = x_ref[0] + gta_ref[0] * mixed
    ms = jnp.mean(x1 * x1, axis=-1, keepdims=True)
    h = x1 * lax.rsqrt(ms + EPS) * g_ref[...]
    hb = (h * (1.0 + sc_ref[0]) + sh_ref[0]).astype(BF16)

    @pl.when(pl.program_id(1) == 0)
    def _():
        carry_ref[...] = jnp.zeros(carry_ref.shape, F32)

    row = lax.broadcasted_iota(jnp.int32, (TM, FF_CHUNK), 0)
    y = jnp.zeros((TM, x1.shape[1]), F32)
    for ch in range(d_ff // FF_CHUNK):
        cols = slice(ch * FF_CHUNK, (ch + 1) * FF_CHUNK)
        gate = _dot(hb, wup_ref[:, cols])
        val = _dot(hb, wup_ref[:, d_ff + ch * FF_CHUNK:d_ff + (ch + 1) * FF_CHUNK])
        prev = carry_ref[:, cols]
        g1 = jnp.where(row == 0, prev[7:8], pltpu.roll(gate, 1, 0))
        g2 = jnp.where(row == 0, prev[6:7], jnp.where(row == 1, prev[7:8], pltpu.roll(gate, 2, 0)))
        carry_ref[:, cols] = gate[TM - 8:TM]
        conv = cw_ref[0:1, cols] * g2 + cw_ref[1:2, cols] * g1 + cw_ref[2:3, cols] * gate + cb_ref[:, cols]
        act = conv * _sigmoid(conv) * val
        y = y + _dot(act.astype(BF16), wdn_ref[cols, :])
    y_ref[0] = x1 + gtm_ref[0] * y


def _post(x, oa, ob, oc, od, wo, gta, g, sc, sh, gtm, wup, cw, cb, wdn):
    bsz, s, d = x.shape
    tok = lambda width: pl.BlockSpec((1, TM, width), lambda b, i: (b, i, 0))
    vec = pl.BlockSpec((1, 1, d), lambda b, i: (b, 0, 0))
    const = lambda a: pl.BlockSpec(a.shape, lambda b, i: (0,) * a.ndim, pipeline_mode=pl.Buffered(1))
    return pl.pallas_call(
        _post_kernel,
        out_shape=jax.ShapeDtypeStruct((bsz, s, d), F32),
        grid=(bsz, s // TM),
        in_specs=[tok(d), tok(GW), tok(GW), tok(GW), tok(GW), const(wo), vec, _whole((1, d)), vec, vec,
                  vec, const(wup), _whole(cw.shape), _whole(cb.shape), const(wdn)],
        out_specs=tok(d),
        scratch_shapes=[pltpu.VMEM((8, wdn.shape[0]), F32)],
        compiler_params=pltpu.CompilerParams(
            dimension_semantics=("parallel", "arbitrary"), vmem_limit_bytes=VMEM_LIMIT),
        name="out_proj_mlp",
    )(x, oa, ob, oc, od, wo, gta, g, sc, sh, gtm, wup, cw, cb, wdn)


def _repack_w_in(w):
    d = w.shape[0]
    off = {}
    pos = 0
    for name, size in (("a_q", GW), ("a_k", GW), ("a_v", GW), ("b_q", GW), ("b_k", GW), ("b_v", GW),
                       ("b_f", HPM), ("c_q", GW), ("c_k", GW), ("c_v", GW), ("d_q", GW),
                       ("d_kc", HEAD_DIM), ("d_vc", HEAD_DIM), ("d_ks", HEAD_DIM), ("d_vs", HEAD_DIM),
                       ("d_kw", HEAD_DIM), ("d_vw", HEAD_DIM), ("d_g", 3 * HPM)):
        off[name] = (pos, size)
        pos += size
    assert pos == w.shape[1]
    col = lambda n: w[:, off[n][0]:off[n][0] + off[n][1]]
    order = ["a_q", "a_k", "a_v", "b_q", "b_k", "b_v", "c_q", "c_k", "c_v", "d_q",
             "d_ks", "d_vs", "d_kw", "d_vw", "d_kc", "d_vc", "b_f", "d_g"]
    parts = [col(n) for n in order] + [jnp.zeros((d, 128 - HPM - 3 * HPM), w.dtype)]
    out = jnp.concatenate(parts, axis=1).astype(BF16)
    assert out.shape[1] == N_PROJ
    return out


def kernel(x, c, rel_bias, ada_w, ada_b, norm_mix_g, norm_ffn_g, w_in, w_out, diff_qnorm_g, diff_knorm_g, diff_lambda, diff_subln_g, fox_qnorm_g, fox_knorm_g, fox_b_f, nsa_qnorm_g, nsa_knorm_g, nsa_pe, nsa_phi_w, ffn_w_up, ffn_conv_w, ffn_conv_b, ffn_w_down):
    bsz, s, d = x.shape
    depth = ada_w.shape[0]
    assert s % T == 0 and WINDOW == 2 * T and T % SEL_LEN == 0 and TM == T and d == 4 * GW
    nk = s // T
    n_chunk = s // CMP_STRIDE
    n_blk = s // SEL_LEN
    d_ff = ffn_w_down.shape[1]
    assert d_ff % FF_CHUNK == 0

    bd32 = jnp.asarray(_block_ones(GW, DIFF_QK_DIM), BF16)
    bd64 = jnp.asarray(_block_ones(GW, HEAD_DIM), BF16)
    ti = np.arange(T)
    tri_incl = jnp.asarray((ti[None, :] <= ti[:, None]).astype(np.float32), BF16)
    tri_later = jnp.asarray((ti[None, :] > ti[:, None]).astype(np.float32), BF16)
    c_start = np.arange(n_chunk) * CMP_STRIDE
    s_start = np.arange(n_blk) * SEL_LEN
    ov = ((c_start[None, :] < s_start[:, None] + SEL_LEN) & (s_start[:, None] < c_start[None, :] + CMP_LEN)
          & (np.arange(n_chunk)[None, :] < n_chunk - 1))
    ovt = jnp.asarray(ov.astype(np.float32), BF16)

    bias = _bias_tiles(rel_bias)
    far = rel_bias[NUM_BUCKETS - 1]
    mod = _adaln(c, ada_w, ada_b)

    for l in range(depth):
        sh_a, sc_a, gt_a, sh_m, sc_m, gt_m = [m.reshape(bsz, 1, d) for m in jnp.split(mod[l], 6, axis=-1)]
        w_re = _repack_w_in(w_in[l])
        scale_a = DIFF_QK_DIM ** -0.5
        scale = HEAD_DIM ** -0.5
        ones = jnp.ones((HEAD_DIM,), F32)
        gains = jnp.stack([
            jnp.tile(diff_qnorm_g[l] * scale_a, GW // DIFF_QK_DIM), jnp.tile(diff_knorm_g[l], GW // DIFF_QK_DIM),
            jnp.tile(fox_qnorm_g[l] * scale, HPM), jnp.tile(fox_knorm_g[l], HPM),
            jnp.tile(nsa_qnorm_g[l] * scale, HPM),
            jnp.concatenate([nsa_knorm_g[l, 1], ones, nsa_knorm_g[l, 2], ones]),
            jnp.ones((GW,), F32), jnp.ones((GW,), F32)])
        (a_qt, a_k, a_vt, b_qt, b_k, b_vt, c_qt, c_k, c_vt, d_qt,
         d_ks, d_kw, d_vst, d_vwt, d_kc, d_vc, smalls) = _inproj(
            x, norm_mix_g[l].reshape(1, d), sc_a, sh_a, w_re, gains, bd32, bd64)

        bf_row = jnp.pad(fox_b_f[l], (0, 128 - HPM)).reshape(1, 128)
        ck, st = _scalars(smalls, bf_row, tri_incl)

        half = CMP_STRIDE * HEAD_DIM
        pe_rows = nsa_pe[l].reshape(4, half)
        w_pad = jnp.pad(nsa_phi_w[l], ((0, 0), (0, 0), (0, 128 - HEAD_DIM))).astype(BF16)
        gk_pad = jnp.pad(nsa_knorm_g[l, 0], (0, 128 - HEAD_DIM)).reshape(1, 128)
        kc, vct = _compress(d_kc.reshape(bsz, n_chunk, half), d_vc.reshape(bsz, n_chunk, half),
                            pe_rows, w_pad, gk_pad)

        lam_init = 0.8 - 0.6 * math.exp(-0.3 * l)
        cst = jnp.full((1,), lam_init, F32)
        k4 = lambda a: a.reshape(bsz, nk, T, a.shape[-1])
        o_a = _diff_attention(far, cst, a_qt, k4(a_k), a_vt, bias, diff_lambda[l],
                              diff_subln_g[l].reshape(HEAD_DIM, 1))
        o_b = _fox_attention(b_qt, k4(b_k), b_vt, st, k4(ck))
        o_c = _sb_attention(c_qt, k4(c_k), c_vt, tri_later)
        o_d = _nsa_attention(far, d_qt, kc, vct, k4(d_ks), d_vst, k4(d_kw), d_vwt, st, bias, ovt)

        x = _post(x, o_a, o_b, o_c, o_d, w_out[l].astype(BF16), gt_a, norm_ffn_g[l].reshape(1, d),
                  sc_m, sh_m, gt_m, ffn_w_up[l].astype(BF16), ffn_conv_w[l].reshape(CONV_WIDTH, d_ff),
                  ffn_conv_b[l].reshape(1, d_ff), ffn_w_down[l].astype(BF16))
    return x
```

```python
import math

import numpy as np
import jax
import jax.numpy as jnp
from jax import lax
from jax.experimental import pallas as pl
from jax.experimental.pallas import tpu as pltpu

HEAD_DIM = 64
HPM = 4
GW = HPM * HEAD_DIM
DIFF_QK_DIM = HEAD_DIM // 2
NUM_BUCKETS = 32
MAX_EXACT = NUM_BUCKETS // 2
MAX_DISTANCE = 128
CMP_LEN = 32
CMP_STRIDE = 16
SEL_LEN = 64
N_SEL = 16
WINDOW = 512
FORCED_SCORE = 1.0e4
NEG = -1.0e30
EPS = 1e-6
CONV_WIDTH = 3
LOG2E = 1.4426950408889634

T = 256
TM = T
FF_CHUNK = 256
ONES_ROWS = 16
VROWS = HEAD_DIM + ONES_ROWS
BAND_ROWS = 2 * T // CMP_STRIDE
VMEM_LIMIT = 56 * 1024 * 1024

F32 = jnp.float32
BF16 = jnp.bfloat16


def _dot(a, b):
    return jnp.dot(a, b, preferred_element_type=F32)


def _split2(x):
    hi = x.astype(BF16)
    lo = (x - hi.astype(F32)).astype(BF16)
    return hi, lo


def _log_sigmoid(z):
    return jnp.minimum(z, 0.0) - jnp.log(1.0 + jnp.exp(-jnp.abs(z)))


def _sigmoid(z):
    return 1.0 / (1.0 + jnp.exp(-z))


def _whole(shape):
    nd = len(shape)
    return pl.BlockSpec(shape, lambda *_: (0,) * nd)


def _smem():
    return pl.BlockSpec(memory_space=pltpu.SMEM)


def _t5_bucket_np(dist):
    n = np.maximum(dist, 0)
    ratio = math.log(MAX_DISTANCE / MAX_EXACT)
    out = None
    for dt in (np.float32, np.float64):
        large = MAX_EXACT + (np.log(np.maximum(n, 1).astype(dt) / dt(MAX_EXACT)) / dt(ratio)
                             * dt(NUM_BUCKETS - MAX_EXACT)).astype(np.int32)
        b = np.where(n < MAX_EXACT, n, np.minimum(large, NUM_BUCKETS - 1)).astype(np.int32)
        assert out is None or np.array_equal(out, b)
        out = b
    return out


def _bucket_tiles():
    r = np.arange(T)[:, None]
    c = np.arange(T)[None, :]
    diag = _t5_bucket_np(c - r)
    near = _t5_bucket_np(T + c - r)
    rb = np.arange(BAND_ROWS)[:, None]
    band = _t5_bucket_np(c - CMP_STRIDE * rb + (T - CMP_LEN + 1))
    return np.concatenate([diag, near, band], axis=0)


def _block_ones(n, group):
    i = np.arange(n)
    return (i[:, None] // group == i[None, :] // group).astype(np.float32)


def _bias_kernel(tbl_ref, bkt_ref, a_ref, d_ref):
    b = bkt_ref[...]
    for h in range(2 * HPM):
        acc = jnp.zeros(b.shape, F32)
        for k in range(NUM_BUCKETS):
            acc = jnp.where(b == k, tbl_ref[k, h] * LOG2E, acc)
        if h < HPM:
            a_ref[h] = acc[0:2 * T]
        else:
            d_ref[:, (h - HPM) * T:(h - HPM + 1) * T] = acc


def _bias_tiles(rel_bias):
    bkt = jnp.asarray(_bucket_tiles())
    rows = bkt.shape[0]
    return pl.pallas_call(
        _bias_kernel,
        out_shape=[jax.ShapeDtypeStruct((HPM, 2 * T, T), F32), jax.ShapeDtypeStruct((rows, HPM * T), F32)],
        in_specs=[_smem(), _whole(bkt.shape)],
        out_specs=[_whole((HPM, 2 * T, T)), _whole((rows, HPM * T))],
        name="bias_tiles",
    )(rel_bias, bkt)


def _adaln_kernel(c_ref, w_ref, b_ref, o_ref):
    c = c_ref[...]
    ca = c * _sigmoid(c)
    o_ref[0] = _dot(ca.astype(BF16), w_ref[0].astype(BF16)) + b_ref[0]


def _adaln(c, ada_w, ada_b):
    depth, d, n = ada_w.shape
    bsz = c.shape[0]
    rows = -(-bsz // 8) * 8
    cp = jnp.pad(c, ((0, rows - bsz), (0, 0)))
    tn = 1536
    out = pl.pallas_call(
        _adaln_kernel,
        out_shape=jax.ShapeDtypeStruct((depth, rows, n), F32),
        grid=(depth, n // tn),
        in_specs=[pl.BlockSpec((rows, d), lambda l, j: (0, 0)),
                  pl.BlockSpec((1, d, tn), lambda l, j: (l, 0, j)),
                  pl.BlockSpec((1, 1, tn), lambda l, j: (l, 0, j))],
        out_specs=pl.BlockSpec((1, rows, tn), lambda l, j: (l, 0, j)),
        name="adaln",
    )(cp, ada_w, ada_b.reshape(depth, 1, n))
    return out[:, :bsz]


N_PROJ = 11 * GW + 2 * 128


def _inproj_kernel(x_ref, g_ref, sc_ref, sh_ref, w_ref, gains_ref, bd32_ref, bd64_ref,
                   aq_ref, ak_ref, av_ref, bq_ref, bk_ref, bv_ref, cq_ref, ck_ref, cv_ref, dq_ref,
                   dks_ref, dkw_ref, dvs_ref, dvw_ref, dkc_ref, dvc_ref, sm_ref):
    x = x_ref[0]
    ms = jnp.mean(x * x, axis=-1, keepdims=True)
    h = x * lax.rsqrt(ms + EPS) * g_ref[...]
    h = h * (1.0 + sc_ref[0]) + sh_ref[0]
    hb = h.astype(BF16)

    def proj(group, width=GW):
        off = group * GW
        return _dot(hb, w_ref[:, off:off + width])

    def segnorm(y, bd_ref, inv_n, gain_row):
        hi, lo = _split2(y * y)
        ss = _dot(hi, bd_ref[...]) + _dot(lo, bd_ref[...])
        return y * lax.rsqrt(ss * inv_n + EPS) * gains_ref[gain_row:gain_row + 1, :]

    def put_t(ref, y):
        ref[0, 0] = y.T.astype(BF16)

    put_t(aq_ref, segnorm(proj(0), bd32_ref, 1.0 / DIFF_QK_DIM, 0))
    ak_ref[0] = segnorm(proj(1), bd32_ref, 1.0 / DIFF_QK_DIM, 1).astype(BF16)
    put_t(av_ref, proj(2))
    put_t(bq_ref, segnorm(proj(3), bd64_ref, 1.0 / HEAD_DIM, 2))
    bk_ref[0] = segnorm(proj(4), bd64_ref, 1.0 / HEAD_DIM, 3).astype(BF16)
    put_t(bv_ref, proj(5))
    put_t(cq_ref, proj(6) * (HEAD_DIM ** -0.5 * LOG2E))
    ck_ref[0] = proj(7).astype(BF16)
    put_t(cv_ref, proj(8))
    put_t(dq_ref, segnorm(proj(9), bd64_ref, 1.0 / HEAD_DIM, 4))

    y = proj(10)
    yn = segnorm(y, bd64_ref, 1.0 / HEAD_DIM, 5)
    dks_ref[0] = yn[:, 0:HEAD_DIM].astype(BF16)
    dkw_ref[0] = yn[:, 2 * HEAD_DIM:3 * HEAD_DIM].astype(BF16)
    yt = y.T
    dvs_ref[0, 0] = yt[HEAD_DIM:2 * HEAD_DIM].astype(BF16)
    dvw_ref[0, 0] = yt[3 * HEAD_DIM:4 * HEAD_DIM].astype(BF16)

    cmp = _dot(hb, w_ref[:, 11 * GW:11 * GW + 128])
    dkc_ref[0] = cmp[:, 0:HEAD_DIM]
    dvc_ref[0] = cmp[:, HEAD_DIM:2 * HEAD_DIM]
    sm_ref[0] = _dot(hb, w_ref[:, 11 * GW + 128:11 * GW + 256])


def _inproj(x, g, sc, sh, w_re, gains, bd32, bd64):
    bsz, s, d = x.shape
    nk = s // T
    tok = lambda width, dt: jax.ShapeDtypeStruct((bsz, s, width), dt)
    tr = lambda rows: jax.ShapeDtypeStruct((bsz, nk, rows, T), BF16)
    tok_spec = lambda width: pl.BlockSpec((1, TM, width), lambda b, i: (b, i, 0))
    tr_spec = lambda rows: pl.BlockSpec((1, 1, rows, T), lambda b, i: (b, i, 0, 0))
    vec = pl.BlockSpec((1, 1, d), lambda b, i: (b, 0, 0))
    out_shape = [tr(GW), tok(GW, BF16), tr(GW)] * 3 + [tr(GW)] + [
        tok(HEAD_DIM, BF16), tok(HEAD_DIM, BF16), tr(HEAD_DIM), tr(HEAD_DIM),
        tok(HEAD_DIM, F32), tok(HEAD_DIM, F32), tok(128, F32)]
    out_specs = [tr_spec(GW), tok_spec(GW), tr_spec(GW)] * 3 + [tr_spec(GW)] + [
        tok_spec(HEAD_DIM), tok_spec(HEAD_DIM), tr_spec(HEAD_DIM), tr_spec(HEAD_DIM),
        tok_spec(HEAD_DIM), tok_spec(HEAD_DIM), tok_spec(128)]
    return pl.pallas_call(
        _inproj_kernel,
        out_shape=out_shape,
        grid=(bsz, s // TM),
        in_specs=[pl.BlockSpec((1, TM, d), lambda b, i: (b, i, 0)),
                  _whole((1, d)), vec, vec,
                  _whole(w_re.shape), _whole(gains.shape), _whole(bd32.shape), _whole(bd64.shape)],
        out_specs=out_specs,
        compiler_params=pltpu.CompilerParams(
            dimension_semantics=("parallel", "parallel"), vmem_limit_bytes=VMEM_LIMIT),
        name="inproj",
    )(x, g, sc, sh, w_re, gains, bd32, bd64)


def _scalars_kernel(sm_ref, bf_ref, tri_ref, ck_ref, st_ref):
    s = sm_ref.shape[1]
    carry = jnp.zeros((1, 128), F32)
    for blk in range(s // T):
        rows = slice(blk * T, (blk + 1) * T)
        lf = _log_sigmoid(sm_ref[0, rows, :] + bf_ref[...])
        h1 = lf.astype(BF16)
        r1 = lf - h1.astype(F32)
        h2 = r1.astype(BF16)
        h3 = (r1 - h2.astype(F32)).astype(BF16)
        tri = tri_ref[...]
        cb = _dot(tri, h1) + _dot(tri, h2) + _dot(tri, h3) + carry
        carry = cb[T - 1:T, :]
        cb2 = cb * LOG2E
        ck_ref[0, rows, :] = cb2
        col = lax.broadcasted_iota(jnp.int32, (T, 128), 1)
        comb = jnp.where(col < HPM, cb2, _sigmoid(sm_ref[0, rows, :]))
        st_ref[0, :, rows] = comb.T[0:16]


def _scalars(smalls, bf_row, tri):
    bsz, s, _ = smalls.shape
    return pl.pallas_call(
        _scalars_kernel,
        out_shape=[jax.ShapeDtypeStruct((bsz, s, 128), F32), jax.ShapeDtypeStruct((bsz, 16, s), F32)],
        grid=(bsz,),
        in_specs=[pl.BlockSpec((1, s, 128), lambda b: (b, 0, 0)), _whole((1, 128)), _whole((T, T))],
        out_specs=[pl.BlockSpec((1, s, 128), lambda b: (b, 0, 0)),
                   pl.BlockSpec((1, 16, s), lambda b: (b, 0, 0))],
        compiler_params=pltpu.CompilerParams(dimension_semantics=("parallel",)),
        name="token_scalars",
    )(smalls, bf_row, tri)


def _compress_kernel(xk_ref, xv_ref, pe_ref, w_ref, gk_ref, kc_ref, vct_ref):
    half = CMP_STRIDE * HEAD_DIM
    n = xk_ref.shape[1]

    def windows(x, pe_a, pe_b, wi):
        ya = _dot((x + pe_a).astype(BF16), w_ref[wi, 0:half, :])
        yb = _dot((x + pe_b).astype(BF16), w_ref[wi, half:2 * half, :])
        return ya + pltpu.roll(yb, n - 1, 0)

    kc = windows(xk_ref[0], pe_ref[0:1], pe_ref[1:2], 0)
    ss = jnp.sum(kc * kc, axis=-1, keepdims=True)
    kc = kc * lax.rsqrt(ss * (1.0 / HEAD_DIM) + EPS) * gk_ref[...]
    kc_ref[0] = kc[:, 0:HEAD_DIM].astype(BF16)
    vc = windows(xv_ref[0], pe_ref[2:3], pe_ref[3:4], 1)
    vct_ref[0] = vc.T[0:HEAD_DIM].astype(BF16)


def _compress(xk, xv, pe_rows, w_pad, gk_pad):
    bsz, n, width = xk.shape
    return pl.pallas_call(
        _compress_kernel,
        out_shape=[jax.ShapeDtypeStruct((bsz, n, HEAD_DIM), BF16),
                   jax.ShapeDtypeStruct((bsz, HEAD_DIM, n), BF16)],
        grid=(bsz,),
        in_specs=[pl.BlockSpec((1, n, width), lambda b: (b, 0, 0)),
                  pl.BlockSpec((1, n, width), lambda b: (b, 0, 0)),
                  _whole(pe_rows.shape), _whole(w_pad.shape), _whole(gk_pad.shape)],
        out_specs=[pl.BlockSpec((1, n, HEAD_DIM), lambda b: (b, 0, 0)),
                   pl.BlockSpec((1, HEAD_DIM, n), lambda b: (b, 0, 0))],
        compiler_params=pltpu.CompilerParams(dimension_semantics=("parallel",)),
        name="nsa_compress",
    )(xk, xv, pe_rows, w_pad, gk_pad)


def _masked_q_blocks(qt, seg):
    qf = qt.astype(F32)
    row = lax.broadcasted_iota(jnp.int32, qf.shape, 0)
    blocks = [jnp.where((row >= seg * i) & (row < seg * (i + 1)), qf, 0.0).astype(BF16)
              for i in range(GW // seg)]
    return jnp.concatenate(blocks, axis=1)


def _with_ones(vt):
    return jnp.concatenate([vt, jnp.ones((ONES_ROWS, vt.shape[1]), BF16)], axis=0)


def _softmax_probs(ss, m_ref, idx, shift=None):
    m_old = m_ref[idx]
    mx = jnp.max(ss[0], axis=0, keepdims=True)
    for s in ss[1:]:
        mx = jnp.maximum(mx, jnp.max(s, axis=0, keepdims=True))
    if shift is not None:
        mx = mx + shift
    m_new = jnp.maximum(m_old, mx)
    m_ref[idx] = m_new
    alpha = jnp.exp2(m_old - m_new)
    sub = m_new if shift is None else m_new - shift
    return alpha, [jnp.exp2(s - sub).astype(BF16) for s in ss]


def _accumulate(acc_ref, rows, alpha, vts, ps):
    upd = _dot(vts[0], ps[0])
    for vt, p in zip(vts[1:], ps[1:]):
        upd = upd + _dot(vt, p)
    acc_ref[rows, :] = alpha * acc_ref[rows, :] + upd


def _normalized(acc_ref, idx):
    base = idx * VROWS
    return acc_ref[base:base + HEAD_DIM, :] / acc_ref[base + HEAD_DIM:base + HEAD_DIM + 1, :]


def _causal_tile(width=T):
    r = lax.broadcasted_iota(jnp.int32, (T, width), 0)
    c = lax.broadcasted_iota(jnp.int32, (T, width), 1)
    if width != T:
        c = jnp.bitwise_and(c, T - 1)
    return r, c


def _for_tile_groups(qi, process):
    n_far = jnp.maximum(qi - 1, 0)
    n_pair = lax.shift_right_logical(n_far, 1)
    odd = jnp.bitwise_and(n_far, 1) == 1

    def pair(j, carry):
        process([(2 * j, "far"), (2 * j + 1, "far")])
        return carry

    lax.fori_loop(0, n_pair, pair, 0)

    @pl.when(qi == 0)
    def _():
        process([(qi, "diag")])

    @pl.when((qi >= 1) & jnp.logical_not(odd))
    def _():
        process([(qi - 1, "near"), (qi, "diag")])

    @pl.when((qi >= 1) & odd)
    def _():
        process([(qi - 2, "far"), (qi - 1, "near"), (qi, "diag")])


def _attn_call(kernel, name, bsz, nq, in_arrays, in_specs, scratch, out_width=GW):
    return pl.pallas_call(
        kernel,
        out_shape=jax.ShapeDtypeStruct((bsz, nq * T, out_width), BF16),
        grid=(bsz, nq),
        in_specs=in_specs,
        out_specs=pl.BlockSpec((1, T, out_width), lambda b, i: (b, i, 0)),
        scratch_shapes=scratch,
        compiler_params=pltpu.CompilerParams(
            dimension_semantics=("parallel", "arbitrary"), vmem_limit_bytes=VMEM_LIMIT),
        name=name,
    )(*in_arrays)


def _q_spec(rows=GW):
    return pl.BlockSpec((1, 1, rows, T), lambda b, i: (b, i, 0, 0))


def _seq_spec(shape):
    nd = len(shape)
    return pl.BlockSpec((1,) + tuple(shape[1:]), lambda b, i: (b,) + (0,) * (nd - 1))


def _diff_kernel(far_ref, cst_ref, qt_ref, k_ref, vt_ref, bias_ref, lam_ref, subg_ref, o_ref,
                 m_ref, acc_ref):
    qi = pl.program_id(1)
    qz = _masked_q_blocks(qt_ref[0, 0], DIFF_QK_DIM)
    m_ref[...] = jnp.full(m_ref.shape, NEG, F32)
    acc_ref[...] = jnp.zeros(acc_ref.shape, F32)
    r, c = _causal_tile()

    def process(tiles):
        scores = [[_dot(k_ref[0, ki], qz[:, 4 * g * T:4 * (g + 1) * T]) for g in range(2)]
                  for ki, _ in tiles]
        probs = []
        for idx in range(2 * HPM):
            h = idx // 2
            ss = []
            for (ki, mode), sc in zip(tiles, scores):
                s = sc[idx // 4][:, (idx % 4) * T:(idx % 4 + 1) * T]
                if mode == "far":
                    s = s + far_ref[h]
                elif mode == "near":
                    s = s + bias_ref[h, T:2 * T, :]
                else:
                    s = jnp.where(r <= c, s + bias_ref[h, 0:T, :], NEG)
                ss.append(s)
            probs.append(_softmax_probs(ss, m_ref, idx))
        for idx in range(2 * HPM):
            h = idx // 2
            vts = [_with_ones(vt_ref[0, ki, h * HEAD_DIM:(h + 1) * HEAD_DIM, :]) for ki, _ in tiles]
            alpha, ps = probs[idx]
            _accumulate(acc_ref, slice(idx * VROWS, (idx + 1) * VROWS), alpha, vts, ps)

    _for_tile_groups(qi, process)

    lam_init = cst_ref[0]
    lv = lam_ref[...]
    lam = (jnp.exp(jnp.sum(lv[0:1] * lv[1:2], axis=-1, keepdims=True))
           - jnp.exp(jnp.sum(lv[2:3] * lv[3:4], axis=-1, keepdims=True)) + lam_init)
    outs = []
    for h in range(HPM):
        o = _normalized(acc_ref, 2 * h) - lam * _normalized(acc_ref, 2 * h + 1)
        ms = jnp.mean(o * o, axis=0, keepdims=True)
        outs.append(o * lax.rsqrt(ms + EPS) * subg_ref[...] * (1.0 - lam_init))
    o_ref[0] = jnp.concatenate(outs, axis=0).T.astype(BF16)


def _diff_attention(far, cst, qt, k4, vt, bias, lam, subg):
    bsz, nq = qt.shape[0], qt.shape[1]
    return _attn_call(
        _diff_kernel, "diff_attention", bsz, nq,
        [far, cst, qt, k4, vt, bias, lam, subg],
        [_smem(), _smem(), _q_spec(), _seq_spec(k4.shape), _seq_spec(vt.shape),
         _whole(bias.shape), _whole(lam.shape), _whole(subg.shape)],
        [pltpu.VMEM((2 * HPM, 1, T), F32), pltpu.VMEM((2 * HPM * VROWS, T), F32)])


def _fox_kernel(qt_ref, k_ref, vt_ref, cq_ref, ck_ref, o_ref, m_ref, acc_ref):
    qi = pl.program_id(1)
    qz = _masked_q_blocks(qt_ref[0, 0], HEAD_DIM)
    m_ref[...] = jnp.full(m_ref.shape, NEG, F32)
    acc_ref[...] = jnp.zeros(acc_ref.shape, F32)
    r, c = _causal_tile()

    def process(tiles):
        scores = [_dot(k_ref[0, ki], qz) for ki, _ in tiles]
        probs = []
        for h in range(HPM):
            ss = []
            for (ki, mode), sc in zip(tiles, scores):
                s = sc[:, h * T:(h + 1) * T] - ck_ref[0, ki][:, h:h + 1]
                if mode == "diag":
                    s = jnp.where(r <= c, s, NEG)
                ss.append(s)
            probs.append(_softmax_probs(ss, m_ref, h, shift=cq_ref[0, h:h + 1, :]))
        for h in range(HPM):
            vts = [_with_ones(vt_ref[0, ki, h * HEAD_DIM:(h + 1) * HEAD_DIM, :]) for ki, _ in tiles]
            alpha, ps = probs[h]
            _accumulate(acc_ref, slice(h * VROWS, (h + 1) * VROWS), alpha, vts, ps)

    _for_tile_groups(qi, process)
    outs = [_normalized(acc_ref, h) for h in range(HPM)]
    o_ref[0] = jnp.concatenate(outs, axis=0).T.astype(BF16)


def _fox_attention(qt, k4, vt, st, ck4):
    bsz, nq = qt.shape[0], qt.shape[1]
    return _attn_call(
        _fox_kernel, "forgetting_attention", bsz, nq,
        [qt, k4, vt, st, ck4],
        [_q_spec(), _seq_spec(k4.shape), _seq_spec(vt.shape),
         pl.BlockSpec((1, 16, T), lambda b, i: (b, 0, i)), _seq_spec(ck4.shape)],
        [pltpu.VMEM((HPM, 1, T), F32), pltpu.VMEM((HPM * VROWS, T), F32)])


def _sb_kernel(qt_ref, k_ref, vt_ref, tri_ref, o_ref, run_ref, acc_ref):
    qi = pl.program_id(1)
    qz = _masked_q_blocks(qt_ref[0, 0], HEAD_DIM)
    run_ref[...] = jnp.zeros(run_ref.shape, F32)
    acc_ref[...] = jnp.zeros(acc_ref.shape, F32)
    r, c = _causal_tile()

    def process(tiles):
        scores = [_dot(k_ref[0, ki], qz) for ki, _ in tiles]
        weights = []
        for h in range(HPM):
            run = run_ref[h]
            ws = []
            for (ki, diag), sc in zip(tiles, scores):
                z = sc[:, h * T:(h + 1) * T]
                log_beta = jnp.minimum(z, 0.0) - jnp.log2(1.0 + jnp.exp2(-jnp.abs(z)))
                log_keep = log_beta - z
                if diag:
                    log_keep = jnp.where(r < c, log_keep, 0.0)
                hi, lo = _split2(log_keep)
                tail = _dot(tri_ref[...], hi) + _dot(tri_ref[...], lo)
                a = jnp.exp2(log_beta + tail + run)
                if diag:
                    a = jnp.where(r < c, a, 0.0)
                ws.append(a.astype(BF16))
                run = run + jnp.sum(log_keep, axis=0, keepdims=True)
            run_ref[h] = run
            weights.append(ws)
        for h in range(HPM):
            rows = slice(h * HEAD_DIM, (h + 1) * HEAD_DIM)
            upd = None
            for (ki, _), a in zip(tiles, weights[h]):
                term = _dot(vt_ref[0, ki, rows, :], a)
                upd = term if upd is None else upd + term
            acc_ref[rows, :] = acc_ref[rows, :] + upd

    odd = jnp.bitwise_and(qi, 1) == 1

    @pl.when(odd)
    def _():
        process([(qi, True), (qi - 1, False)])

    @pl.when(jnp.logical_not(odd))
    def _():
        process([(qi, True)])

    base = qi - 1 - jnp.bitwise_and(qi, 1)

    def pair(j, carry):
        process([(base - 2 * j, False), (base - 2 * j - 1, False)])
        return carry

    lax.fori_loop(0, lax.shift_right_logical(qi, 1), pair, 0)
    o_ref[0] = acc_ref[...].T.astype(BF16)


def _sb_attention(qt, k4, vt, tri):
    bsz, nq = qt.shape[0], qt.shape[1]
    return _attn_call(
        _sb_kernel, "stick_breaking_attention", bsz, nq,
        [qt, k4, vt, tri],
        [_q_spec(), _seq_spec(k4.shape), _seq_spec(vt.shape), _whole(tri.shape)],
        [pltpu.VMEM((HPM, 1, T), F32), pltpu.VMEM((GW, T), F32)])


def _nsa_kernel(far_ref, qt_ref, kc_ref, vct_ref, ks_ref, vst_ref, kw_ref, vwt_ref, g_ref,
                bias_ref, ovt_ref, o_ref,
                sc_ref, imp_ref, sel_ref, m_ref, acc_ref):
    qi = pl.program_id(1)
    q0 = qi * T
    n_cmp = kc_ref.shape[1]
    n_blk = ovt_ref.shape[0]
    n_sel = min(N_SEL, n_blk)
    w4 = HPM * T
    qt = qt_ref[0, 0]
    qcat = jnp.concatenate([qt[h * HEAD_DIM:(h + 1) * HEAD_DIM, :] for h in range(HPM)], axis=1)
    far_row = jnp.concatenate([jnp.full((1, T), far_ref[HPM + h], F32) for h in range(HPM)], axis=1)
    r, c = _causal_tile(w4)

    sc_ref[...] = _dot(kc_ref[0], qcat)
    delta = bias_ref[2 * T:2 * T + BAND_ROWS, :] - far_row

    @pl.when(qi == 0)
    def _():
        half = BAND_ROWS // 2
        sc_ref[0:half, :] = sc_ref[0:half, :] + delta[half:BAND_ROWS]

    @pl.when(qi > 0)
    def _():
        c0 = pl.multiple_of(qi * (T // CMP_STRIDE) - BAND_ROWS // 2, 8)
        sc_ref[pl.ds(c0, BAND_ROWS), :] = sc_ref[pl.ds(c0, BAND_ROWS), :] + delta

    ci = lax.broadcasted_iota(jnp.int32, (n_cmp, w4), 0)
    ti = q0 + jnp.bitwise_and(lax.broadcasted_iota(jnp.int32, (n_cmp, w4), 1), T - 1)
    valid_c = ti - CMP_STRIDE * ci - (CMP_LEN - 1) >= 0
    s = jnp.where(valid_c, sc_ref[...] + far_row, NEG)
    e = jnp.exp2(s - jnp.max(s, axis=0, keepdims=True))
    p = jnp.where(valid_c, e / jnp.sum(e, axis=0, keepdims=True), 0.0)
    o_cmp = _dot(vct_ref[0], p.astype(BF16))
    pc_sum = p[:, 0:T]
    for h in range(1, HPM):
        pc_sum = pc_sum + p[:, h * T:(h + 1) * T]

    hi, lo = _split2(pc_sum)
    imp = _dot(ovt_ref[...], hi) + _dot(ovt_ref[...], lo)
    ji = lax.broadcasted_iota(jnp.int32, (n_blk, T), 0)
    tq = q0 + lax.broadcasted_iota(jnp.int32, (n_blk, T), 1)
    forced = (ji == lax.shift_right_logical(tq, int(math.log2(SEL_LEN)))) | (ji == 0)
    imp = jnp.where(forced, FORCED_SCORE, imp)
    imp = jnp.where(ji * SEL_LEN <= tq, imp, -1.0)
    imp_ref[...] = imp

    def rank_body(jp, cnt):
        row = imp_ref[pl.ds(jp, 1), :]
        tie = jnp.where(row == imp, 1.0, 0.0) * jnp.where(ji > jp, 1.0, 0.0)
        return cnt + jnp.where(row > imp, 1.0, 0.0) + tie

    cnt = lax.fori_loop(0, n_blk, rank_body, jnp.zeros((n_blk, T), F32))
    sel = jnp.where(cnt < float(n_sel), 1.0, 0.0)
    sel_ref[...] = jnp.concatenate([sel] * HPM, axis=1)

    m_ref[...] = jnp.full(m_ref.shape, NEG, F32)
    acc_ref[...] = jnp.zeros(acc_ref.shape, F32)

    def biased(s, mode):
        if mode == "far":
            return s + far_row
        if mode == "near":
            return s + bias_ref[T:2 * T, :]
        return s + bias_ref[0:T, :]

    def update(state, ss, vts):
        alpha, ps = _softmax_probs(ss, m_ref, state)
        _accumulate(acc_ref, slice(state * VROWS, (state + 1) * VROWS), alpha, vts, ps)

    def sel_mask(ki):
        per = T // SEL_LEN
        rows = [jnp.broadcast_to(sel_ref[pl.ds(ki * per + i, 1), :], (SEL_LEN, w4)) for i in range(per)]
        return jnp.concatenate(rows, axis=0) > 0.5

    def process_selected(tiles):
        scores = [_dot(ks_ref[0, ki], qcat) for ki, _ in tiles]
        ss = []
        for (ki, mode), sc in zip(tiles, scores):
            mask = sel_mask(ki)
            if mode == "diag":
                mask = mask & (r <= c)
            ss.append(jnp.where(mask, biased(sc, mode), NEG))
        update(0, ss, [_with_ones(vst_ref[0, ki]) for ki, _ in tiles])

    _for_tile_groups(qi, process_selected)

    def process_window(tiles):
        scores = [_dot(kw_ref[0, ki], qcat) for ki, _, _ in tiles]
        ss = []
        for (ki, mode, mask), sc in zip(tiles, scores):
            s = biased(sc, mode)
            ss.append(s if mask is None else jnp.where(mask, s, NEG))
        update(1, ss, [_with_ones(vwt_ref[0, ki]) for ki, _, _ in tiles])

    back = WINDOW // T

    def window_tiles(n):
        tiles = [(qi, "diag", r <= c)]
        for d in range(1, n + 1):
            mode = "near" if d == 1 else "far"
            tiles.append((qi - d, mode, (c < r) if d == back else None))
        return tiles

    for n in range(back + 1):
        cond = (qi == n) if n < back else (qi >= back)

        @pl.when(cond)
        def _():
            process_window(window_tiles(n))

    o_sel = acc_ref[0:HEAD_DIM, :] / acc_ref[HEAD_DIM:HEAD_DIM + 1, :]
    o_win = acc_ref[VROWS:VROWS + HEAD_DIM, :] / acc_ref[VROWS + HEAD_DIM:VROWS + HEAD_DIM + 1, :]
    gate = lambda br: jnp.concatenate(
        [g_ref[0, HPM + br * HPM + h:HPM + br * HPM + h + 1, :] for h in range(HPM)], axis=1)
    o = gate(0) * o_cmp + gate(1) * o_sel + gate(2) * o_win
    ot = jnp.concatenate([o[:, h * T:(h + 1) * T] for h in range(HPM)], axis=0)
    o_ref[0] = ot.T.astype(BF16)


def _nsa_attention(far, qt, kc, vct, ks4, vst, kw4, vwt, st, bias, ovt):
    bsz, nq = qt.shape[0], qt.shape[1]
    n_cmp = kc.shape[1]
    n_blk = ovt.shape[0]
    w4 = HPM * T
    return _attn_call(
        _nsa_kernel, "native_sparse_attention", bsz, nq,
        [far, qt, kc, vct, ks4, vst, kw4, vwt, st, bias, ovt],
        [_smem(), _q_spec(), _seq_spec(kc.shape), _seq_spec(vct.shape),
         _seq_spec(ks4.shape), _seq_spec(vst.shape), _seq_spec(kw4.shape), _seq_spec(vwt.shape),
         pl.BlockSpec((1, 16, T), lambda b, i: (b, 0, i)), _whole(bias.shape), _whole(ovt.shape)],
        [pltpu.VMEM((n_cmp, w4), F32), pltpu.VMEM((n_blk, T), F32), pltpu.VMEM((n_blk, w4), F32),
         pltpu.VMEM((2, 1, w4), F32), pltpu.VMEM((2 * VROWS, w4), F32)])


def _post_kernel(x_ref, oa_ref, ob_ref, oc_ref, od_ref, wo_ref, gta_ref, g_ref, sc_ref, sh_ref,
                 gtm_ref, wup_ref, cw_ref, cb_ref, wdn_ref, y_ref, carry_ref):
    d_ff = wdn_ref.shape[0]
    mixed = (_dot(oa_ref[0], wo_ref[0:GW]) + _dot(ob_ref[0], wo_ref[GW:2 * GW])
             + _dot(oc_ref[0], wo_ref[2 * GW:3 * GW]) + _dot(od_ref[0], wo_ref[3 * GW:4 * GW]))
    x1 = x_ref[0] + gta_ref[0] * mixed
    ms = jnp.mean(x1 * x1, axis=-1, keepdims=True)
    h = x1 * lax.rsqrt(ms + EPS) * g_ref[...]
    hb = (h * (1.0 + sc_ref[0]) + sh_ref[0]).astype(BF16)

    @pl.when(pl.program_id(1) == 0)
    def _():
        carry_ref[...] = jnp.zeros(carry_ref.shape, F32)

    row = lax.broadcasted_iota(jnp.int32, (TM, FF_CHUNK), 0)
    y = jnp.zeros((TM, x1.shape[1]), F32)
    for ch in range(d_ff // FF_CHUNK):
        cols = slice(ch * FF_CHUNK, (ch + 1) * FF_CHUNK)
        gate = _dot(hb, wup_ref[:, cols])
        val = _dot(hb, wup_ref[:, d_ff + ch * FF_CHUNK:d_ff + (ch + 1) * FF_CHUNK])
        prev = carry_ref[:, cols]
        g1 = jnp.where(row == 0, prev[7:8], pltpu.roll(gate, 1, 0))
        g2 = jnp.where(row == 0, prev[6:7], jnp.where(row == 1, prev[7:8], pltpu.roll(gate, 2, 0)))
        carry_ref[:, cols] = gate[TM - 8:TM]
        conv = cw_ref[0:1, cols] * g2 + cw_ref[1:2, cols] * g1 + cw_ref[2:3, cols] * gate + cb_ref[:, cols]
        act = conv * _sigmoid(conv) * val
        y = y + _dot(act.astype(BF16), wdn_ref[cols, :])
    y_ref[0] = x1 + gtm_ref[0] * y


def _post(x, oa, ob, oc, od, wo, gta, g, sc, sh, gtm, wup, cw, cb, wdn):
    bsz, s, d = x.shape
    tok = lambda width: pl.BlockSpec((1, TM, width), lambda b, i: (b, i, 0))
    vec = pl.BlockSpec((1, 1, d), lambda b, i: (b, 0, 0))
    const = lambda a: pl.BlockSpec(a.shape, lambda b, i: (0,) * a.ndim, pipeline_mode=pl.Buffered(1))
    return pl.pallas_call(
        _post_kernel,
        out_shape=jax.ShapeDtypeStruct((bsz, s, d), F32),
        grid=(bsz, s // TM),
        in_specs=[tok(d), tok(GW), tok(GW), tok(GW), tok(GW), const(wo), vec, _whole((1, d)), vec, vec,
                  vec, const(wup), _whole(cw.shape), _whole(cb.shape), const(wdn)],
        out_specs=tok(d),
        scratch_shapes=[pltpu.VMEM((8, wdn.shape[0]), F32)],
        compiler_params=pltpu.CompilerParams(
            dimension_semantics=("parallel", "arbitrary"), vmem_limit_bytes=VMEM_LIMIT),
        name="out_proj_mlp",
    )(x, oa, ob, oc, od, wo, gta, g, sc, sh, gtm, wup, cw, cb, wdn)


def _repack_w_in(w):
    d = w.shape[0]
    off = {}
    pos = 0
    for name, size in (("a_q", GW), ("a_k", GW), ("a_v", GW), ("b_q", GW), ("b_k", GW), ("b_v", GW),
                       ("b_f", HPM), ("c_q", GW), ("c_k", GW), ("c_v", GW), ("d_q", GW),
                       ("d_kc", HEAD_DIM), ("d_vc", HEAD_DIM), ("d_ks", HEAD_DIM), ("d_vs", HEAD_DIM),
                       ("d_kw", HEAD_DIM), ("d_vw", HEAD_DIM), ("d_g", 3 * HPM)):
        off[name] = (pos, size)
        pos += size
    assert pos == w.shape[1]
    col = lambda n: w[:, off[n][0]:off[n][0] + off[n][1]]
    order = ["a_q", "a_k", "a_v", "b_q", "b_k", "b_v", "c_q", "c_k", "c_v", "d_q",
             "d_ks", "d_vs", "d_kw", "d_vw", "d_kc", "d_vc", "b_f", "d_g"]
    parts = [col(n) for n in order] + [jnp.zeros((d, 128 - HPM - 3 * HPM), w.dtype)]
    out = jnp.concatenate(parts, axis=1).astype(BF16)
    assert out.shape[1] == N_PROJ
    return out


def kernel(x, c, rel_bias, ada_w, ada_b, norm_mix_g, norm_ffn_g, w_in, w_out, diff_qnorm_g, diff_knorm_g, diff_lambda, diff_subln_g, fox_qnorm_g, fox_knorm_g, fox_b_f, nsa_qnorm_g, nsa_knorm_g, nsa_pe, nsa_phi_w, ffn_w_up, ffn_conv_w, ffn_conv_b, ffn_w_down):
    bsz, s, d = x.shape
    depth = ada_w.shape[0]
    assert s % T == 0 and WINDOW % T == 0 and T % SEL_LEN == 0 and d == 4 * GW
    assert T & (T - 1) == 0 and BAND_ROWS // 2 >= (MAX_DISTANCE + CMP_LEN) // CMP_STRIDE
    nk = s // T
    n_chunk = s // CMP_STRIDE
    n_blk = s // SEL_LEN
    d_ff = ffn_w_down.shape[1]
    assert d_ff % FF_CHUNK == 0

    bd32 = jnp.asarray(_block_ones(GW, DIFF_QK_DIM), BF16)
    bd64 = jnp.asarray(_block_ones(GW, HEAD_DIM), BF16)
    ti = np.arange(T)
    tri_incl = jnp.asarray((ti[None, :] <= ti[:, None]).astype(np.float32), BF16)
    tri_later = jnp.asarray((ti[None, :] > ti[:, None]).astype(np.float32), BF16)
    c_start = np.arange(n_chunk) * CMP_STRIDE
    s_start = np.arange(n_blk) * SEL_LEN
    ov = ((c_start[None, :] < s_start[:, None] + SEL_LEN) & (s_start[:, None] < c_start[None, :] + CMP_LEN)
          & (np.arange(n_chunk)[None, :] < n_chunk - 1))
    ovt = jnp.asarray(ov.astype(np.float32), BF16)

    bias_a, bias_d = _bias_tiles(rel_bias)
    far = rel_bias[NUM_BUCKETS - 1] * LOG2E
    mod = _adaln(c, ada_w, ada_b)

    for l in range(depth):
        sh_a, sc_a, gt_a, sh_m, sc_m, gt_m = [m.reshape(bsz, 1, d) for m in jnp.split(mod[l], 6, axis=-1)]
        w_re = _repack_w_in(w_in[l])
        scale_a = DIFF_QK_DIM ** -0.5 * LOG2E
        scale = HEAD_DIM ** -0.5 * LOG2E
        ones = jnp.ones((HEAD_DIM,), F32)
        gains = jnp.stack([
            jnp.tile(diff_qnorm_g[l] * scale_a, GW // DIFF_QK_DIM), jnp.tile(diff_knorm_g[l], GW // DIFF_QK_DIM),
            jnp.tile(fox_qnorm_g[l] * scale, HPM), jnp.tile(fox_knorm_g[l], HPM),
            jnp.tile(nsa_qnorm_g[l] * scale, HPM),
            jnp.concatenate([nsa_knorm_g[l, 1], ones, nsa_knorm_g[l, 2], ones]),
            jnp.ones((GW,), F32), jnp.ones((GW,), F32)])
        (a_qt, a_k, a_vt, b_qt, b_k, b_vt, c_qt, c_k, c_vt, d_qt,
         d_ks, d_kw, d_vst, d_vwt, d_kc, d_vc, smalls) = _inproj(
            x, norm_mix_g[l].reshape(1, d), sc_a, sh_a, w_re, gains, bd32, bd64)

        bf_row = jnp.pad(fox_b_f[l], (0, 128 - HPM)).reshape(1, 128)
        ck, st = _scalars(smalls, bf_row, tri_incl)

        half = CMP_STRIDE * HEAD_DIM
        pe_rows = nsa_pe[l].reshape(4, half)
        w_pad = jnp.pad(nsa_phi_w[l], ((0, 0), (0, 0), (0, 128 - HEAD_DIM))).astype(BF16)
        gk_pad = jnp.pad(nsa_knorm_g[l, 0], (0, 128 - HEAD_DIM)).reshape(1, 128)
        kc, vct = _compress(d_kc.reshape(bsz, n_chunk, half), d_vc.reshape(bsz, n_chunk, half),
                            pe_rows, w_pad, gk_pad)

        lam_init = 0.8 - 0.6 * math.exp(-0.3 * l)
        cst = jnp.full((1,), lam_init, F32)
        k4 = lambda a: a.reshape(bsz, nk, T, a.shape[-1])
        o_a = _diff_attention(far, cst, a_qt, k4(a_k), a_vt, bias_a, diff_lambda[l],
                              diff_subln_g[l].reshape(HEAD_DIM, 1))
        o_b = _fox_attention(b_qt, k4(b_k), b_vt, st, k4(ck))
        o_c = _sb_attention(c_qt, k4(c_k), c_vt, tri_later)
        o_d = _nsa_attention(far, d_qt, kc, vct, k4(d_ks), d_vst, k4(d_kw), d_vwt, st, bias_d, ovt)

        x = _post(x, o_a, o_b, o_c, o_d, w_out[l].astype(BF16), gt_a, norm_ffn_g[l].reshape(1, d),
                  sc_m, sh_m, gt_m, ffn_w_up[l].astype(BF16), ffn_conv_w[l].reshape(CONV_WIDTH, d_ff),
                  ffn_conv_b[l].reshape(1, d_ff), ffn_w_down[l].astype(BF16))
    return x
```

```python
import math

import numpy as np
import jax
import jax.numpy as jnp
from jax import lax
from jax.experimental import pallas as pl
from jax.experimental.pallas import tpu as pltpu

HEAD_DIM = 64
HPM = 4
GW = HPM * HEAD_DIM
DIFF_QK_DIM = HEAD_DIM // 2
NUM_BUCKETS = 32
MAX_EXACT = NUM_BUCKETS // 2
MAX_DISTANCE = 128
CMP_LEN = 32
CMP_STRIDE = 16
SEL_LEN = 64
N_SEL = 16
WINDOW = 512
FORCED_SCORE = 1.0e4
NEG = -1.0e30
EPS = 1e-6
CONV_WIDTH = 3
LOG2E = 1.4426950408889634

T = 256
TM = T
TM_POST = 256
FF_CHUNK = 256
ONES_ROWS = 16
VROWS = HEAD_DIM + ONES_ROWS
BAND_ROWS = 2 * T // CMP_STRIDE
VMEM_LIMIT = 56 * 1024 * 1024

F32 = jnp.float32
BF16 = jnp.bfloat16


def _dot(a, b):
    return jnp.dot(a, b, preferred_element_type=F32)


def _split2(x):
    hi = x.astype(BF16)
    lo = (x - hi.astype(F32)).astype(BF16)
    return hi, lo


def _log_sigmoid(z):
    return jnp.minimum(z, 0.0) - jnp.log(1.0 + jnp.exp(-jnp.abs(z)))


def _sigmoid(z):
    return 1.0 / (1.0 + jnp.exp(-z))


def _whole(shape):
    nd = len(shape)
    return pl.BlockSpec(shape, lambda *_: (0,) * nd)


def _resident(shape):
    nd = len(shape)
    return pl.BlockSpec(shape, lambda *_: (0,) * nd, pipeline_mode=pl.Buffered(1))


def _smem():
    return pl.BlockSpec(memory_space=pltpu.SMEM)


def _t5_bucket_np(dist):
    n = np.maximum(dist, 0)
    ratio = math.log(MAX_DISTANCE / MAX_EXACT)
    out = None
    for dt in (np.float32, np.float64):
        large = MAX_EXACT + (np.log(np.maximum(n, 1).astype(dt) / dt(MAX_EXACT)) / dt(ratio)
                             * dt(NUM_BUCKETS - MAX_EXACT)).astype(np.int32)
        b = np.where(n < MAX_EXACT, n, np.minimum(large, NUM_BUCKETS - 1)).astype(np.int32)
        assert out is None or np.array_equal(out, b)
        out = b
    return out


def _bucket_tiles():
    r = np.arange(T)[:, None]
    c = np.arange(T)[None, :]
    diag = _t5_bucket_np(c - r)
    near = _t5_bucket_np(T + c - r)
    rb = np.arange(BAND_ROWS)[:, None]
    band = _t5_bucket_np(c - CMP_STRIDE * rb + (T - CMP_LEN + 1))
    return np.concatenate([diag, near, band], axis=0)


def _block_ones(n, group):
    i = np.arange(n)
    return (i[:, None] // group == i[None, :] // group).astype(np.float32)


def _bias_kernel(tbl_ref, bkt_ref, a_ref, d_ref):
    b = bkt_ref[...]
    for h in range(2 * HPM):
        acc = jnp.zeros(b.shape, F32)
        for k in range(NUM_BUCKETS):
            acc = jnp.where(b == k, tbl_ref[k, h] * LOG2E, acc)
        if h < HPM:
            a_ref[h] = acc[0:2 * T]
        else:
            d_ref[:, (h - HPM) * T:(h - HPM + 1) * T] = acc


def _bias_tiles(rel_bias):
    bkt = jnp.asarray(_bucket_tiles())
    rows = bkt.shape[0]
    return pl.pallas_call(
        _bias_kernel,
        out_shape=[jax.ShapeDtypeStruct((HPM, 2 * T, T), F32), jax.ShapeDtypeStruct((rows, HPM * T), F32)],
        in_specs=[_smem(), _whole(bkt.shape)],
        out_specs=[_whole((HPM, 2 * T, T)), _whole((rows, HPM * T))],
        name="bias_tiles",
    )(rel_bias, bkt)


def _adaln_kernel(c_ref, w_ref, b_ref, o_ref):
    c = c_ref[...]
    ca = c * _sigmoid(c)
    o_ref[0] = _dot(ca.astype(BF16), w_ref[0].astype(BF16)) + b_ref[0]


def _adaln(c, ada_w, ada_b):
    depth, d, n = ada_w.shape
    bsz = c.shape[0]
    rows = -(-bsz // 8) * 8
    cp = jnp.pad(c, ((0, rows - bsz), (0, 0)))
    tn = 1536
    out = pl.pallas_call(
        _adaln_kernel,
        out_shape=jax.ShapeDtypeStruct((depth, rows, n), F32),
        grid=(depth, n // tn),
        in_specs=[pl.BlockSpec((rows, d), lambda l, j: (0, 0)),
                  pl.BlockSpec((1, d, tn), lambda l, j: (l, 0, j)),
                  pl.BlockSpec((1, 1, tn), lambda l, j: (l, 0, j))],
        out_specs=pl.BlockSpec((1, rows, tn), lambda l, j: (l, 0, j)),
        name="adaln",
    )(cp, ada_w, ada_b.reshape(depth, 1, n))
    return out[:, :bsz]


N_PROJ = 11 * GW + 2 * 128


def _inproj_kernel(x_ref, g_ref, sc_ref, sh_ref, w_ref, gains_ref, bd32_ref, bd64_ref,
                   aq_ref, ak_ref, av_ref, bq_ref, bk_ref, bv_ref, cq_ref, ck_ref, cv_ref, dq_ref,
                   dks_ref, dkw_ref, dvs_ref, dvw_ref, dkc_ref, dvc_ref, sm_ref):
    x = x_ref[0]
    ms = jnp.mean(x * x, axis=-1, keepdims=True)
    h = x * lax.rsqrt(ms + EPS) * g_ref[...]
    h = h * (1.0 + sc_ref[0]) + sh_ref[0]
    hb = h.astype(BF16)

    def proj(group, width=GW):
        off = group * GW
        return _dot(hb, w_ref[:, off:off + width])

    def segnorm(y, bd_ref, inv_n, gain_row):
        hi, lo = _split2(y * y)
        ss = _dot(hi, bd_ref[...]) + _dot(lo, bd_ref[...])
        return y * lax.rsqrt(ss * inv_n + EPS) * gains_ref[gain_row:gain_row + 1, :]

    def put_t(ref, y):
        ref[0, 0] = y.T.astype(BF16)

    raw0 = proj(0)
    raw1 = proj(1)
    put_t(aq_ref, segnorm(raw0, bd32_ref, 1.0 / DIFF_QK_DIM, 0))
    put_t(av_ref, proj(2))
    raw3 = proj(3)
    ak_ref[0] = segnorm(raw1, bd32_ref, 1.0 / DIFF_QK_DIM, 1).astype(BF16)
    put_t(bv_ref, proj(5))
    raw4 = proj(4)
    put_t(bq_ref, segnorm(raw3, bd64_ref, 1.0 / HEAD_DIM, 2))
    put_t(cq_ref, proj(6) * (HEAD_DIM ** -0.5 * LOG2E))
    raw9 = proj(9)
    bk_ref[0] = segnorm(raw4, bd64_ref, 1.0 / HEAD_DIM, 3).astype(BF16)
    ck_ref[0] = proj(7).astype(BF16)
    y = proj(10)
    put_t(dq_ref, segnorm(raw9, bd64_ref, 1.0 / HEAD_DIM, 4))
    put_t(cv_ref, proj(8))
    cmp = _dot(hb, w_ref[:, 11 * GW:11 * GW + 128])
    dkc_ref[0] = cmp[:, 0:HEAD_DIM]
    dvc_ref[0] = cmp[:, HEAD_DIM:2 * HEAD_DIM]
    sm_ref[0] = _dot(hb, w_ref[:, 11 * GW + 128:11 * GW + 256])

    yn = segnorm(y, bd64_ref, 1.0 / HEAD_DIM, 5)
    dks_ref[0] = yn[:, 0:HEAD_DIM].astype(BF16)
    dkw_ref[0] = yn[:, 2 * HEAD_DIM:3 * HEAD_DIM].astype(BF16)
    yt = y.T
    dvs_ref[0, 0] = yt[HEAD_DIM:2 * HEAD_DIM].astype(BF16)
    dvw_ref[0, 0] = yt[3 * HEAD_DIM:4 * HEAD_DIM].astype(BF16)


def _inproj(x, g, sc, sh, w_re, gains, bd32, bd64):
    bsz, s, d = x.shape
    nk = s // T
    tok = lambda width, dt: jax.ShapeDtypeStruct((bsz, s, width), dt)
    tr = lambda rows: jax.ShapeDtypeStruct((bsz, nk, rows, T), BF16)
    tok_spec = lambda width: pl.BlockSpec((1, TM, width), lambda b, i: (b, i, 0))
    tr_spec = lambda rows: pl.BlockSpec((1, 1, rows, T), lambda b, i: (b, i, 0, 0))
    vec = pl.BlockSpec((1, 1, d), lambda b, i: (b, 0, 0))
    out_shape = [tr(GW), tok(GW, BF16), tr(GW)] * 3 + [tr(GW)] + [
        tok(HEAD_DIM, BF16), tok(HEAD_DIM, BF16), tr(HEAD_DIM), tr(HEAD_DIM),
        tok(HEAD_DIM, F32), tok(HEAD_DIM, F32), tok(128, F32)]
    out_specs = [tr_spec(GW), tok_spec(GW), tr_spec(GW)] * 3 + [tr_spec(GW)] + [
        tok_spec(HEAD_DIM), tok_spec(HEAD_DIM), tr_spec(HEAD_DIM), tr_spec(HEAD_DIM),
        tok_spec(HEAD_DIM), tok_spec(HEAD_DIM), tok_spec(128)]
    return pl.pallas_call(
        _inproj_kernel,
        out_shape=out_shape,
        grid=(bsz, s // TM),
        in_specs=[pl.BlockSpec((1, TM, d), lambda b, i: (b, i, 0)),
                  _whole((1, d)), vec, vec,
                  _whole(w_re.shape), _whole(gains.shape), _whole(bd32.shape), _whole(bd64.shape)],
        out_specs=out_specs,
        compiler_params=pltpu.CompilerParams(
            dimension_semantics=("parallel", "parallel"), vmem_limit_bytes=VMEM_LIMIT),
        name="inproj",
    )(x, g, sc, sh, w_re, gains, bd32, bd64)


def _scalars_kernel(sm_ref, bf_ref, tri_ref, ck_ref, st_ref):
    s = sm_ref.shape[1]
    carry = jnp.zeros((1, 128), F32)
    for blk in range(s // T):
        rows = slice(blk * T, (blk + 1) * T)
        lf = _log_sigmoid(sm_ref[0, rows, :] + bf_ref[...])
        h1 = lf.astype(BF16)
        r1 = lf - h1.astype(F32)
        h2 = r1.astype(BF16)
        h3 = (r1 - h2.astype(F32)).astype(BF16)
        tri = tri_ref[...]
        cb = _dot(tri, h1) + _dot(tri, h2) + _dot(tri, h3) + carry
        carry = cb[T - 1:T, :]
        cb2 = cb * LOG2E
        ck_ref[0, rows, :] = cb2
        col = lax.broadcasted_iota(jnp.int32, (T, 128), 1)
        comb = jnp.where(col < HPM, cb2, _sigmoid(sm_ref[0, rows, :]))
        st_ref[0, :, rows] = comb.T[0:16]


def _scalars(smalls, bf_row, tri):
    bsz, s, _ = smalls.shape
    return pl.pallas_call(
        _scalars_kernel,
        out_shape=[jax.ShapeDtypeStruct((bsz, s, 128), F32), jax.ShapeDtypeStruct((bsz, 16, s), F32)],
        grid=(bsz,),
        in_specs=[pl.BlockSpec((1, s, 128), lambda b: (b, 0, 0)), _whole((1, 128)), _whole((T, T))],
        out_specs=[pl.BlockSpec((1, s, 128), lambda b: (b, 0, 0)),
                   pl.BlockSpec((1, 16, s), lambda b: (b, 0, 0))],
        compiler_params=pltpu.CompilerParams(dimension_semantics=("parallel",)),
        name="token_scalars",
    )(smalls, bf_row, tri)


def _compress_kernel(xk_ref, xv_ref, pe_ref, w_ref, gk_ref, kc_ref, vct_ref):
    half = CMP_STRIDE * HEAD_DIM
    n = xk_ref.shape[1]

    def windows(x, pe_a, pe_b, wi):
        ya = _dot((x + pe_a).astype(BF16), w_ref[wi, 0:half, :])
        yb = _dot((x + pe_b).astype(BF16), w_ref[wi, half:2 * half, :])
        return ya + pltpu.roll(yb, n - 1, 0)

    kc = windows(xk_ref[0], pe_ref[0:1], pe_ref[1:2], 0)
    ss = jnp.sum(kc * kc, axis=-1, keepdims=True)
    kc = kc * lax.rsqrt(ss * (1.0 / HEAD_DIM) + EPS) * gk_ref[...]
    kc_ref[0] = kc[:, 0:HEAD_DIM].astype(BF16)
    vc = windows(xv_ref[0], pe_ref[2:3], pe_ref[3:4], 1)
    vct_ref[0] = vc.T[0:HEAD_DIM].astype(BF16)


def _compress(xk, xv, pe_rows, w_pad, gk_pad):
    bsz, n, width = xk.shape
    return pl.pallas_call(
        _compress_kernel,
        out_shape=[jax.ShapeDtypeStruct((bsz, n, HEAD_DIM), BF16),
                   jax.ShapeDtypeStruct((bsz, HEAD_DIM, n), BF16)],
        grid=(bsz,),
        in_specs=[pl.BlockSpec((1, n, width), lambda b: (b, 0, 0)),
                  pl.BlockSpec((1, n, width), lambda b: (b, 0, 0)),
                  _whole(pe_rows.shape), _whole(w_pad.shape), _whole(gk_pad.shape)],
        out_specs=[pl.BlockSpec((1, n, HEAD_DIM), lambda b: (b, 0, 0)),
                   pl.BlockSpec((1, HEAD_DIM, n), lambda b: (b, 0, 0))],
        compiler_params=pltpu.CompilerParams(dimension_semantics=("parallel",)),
        name="nsa_compress",
    )(xk, xv, pe_rows, w_pad, gk_pad)


def _head_rows(qt, h):
    return qt[h * HEAD_DIM:(h + 1) * HEAD_DIM, :]


def _head_cols(kt, h):
    return kt[:, h * HEAD_DIM:(h + 1) * HEAD_DIM]


def _split_maps(qh):
    qf = qh.astype(F32)
    row = lax.broadcasted_iota(jnp.int32, qf.shape, 0)
    first = jnp.where(row < DIFF_QK_DIM, qf, 0.0).astype(BF16)
    second = jnp.where(row >= DIFF_QK_DIM, qf, 0.0).astype(BF16)
    return jnp.concatenate([first, second], axis=1)


def _with_ones(vt):
    return jnp.concatenate([vt, jnp.ones((ONES_ROWS, vt.shape[1]), BF16)], axis=0)


def _softmax_probs(ss, m_ref, idx, shift=None):
    m_old = m_ref[idx]
    mx = jnp.max(ss[0], axis=0, keepdims=True)
    for s in ss[1:]:
        mx = jnp.maximum(mx, jnp.max(s, axis=0, keepdims=True))
    if shift is not None:
        mx = mx + shift
    m_new = jnp.maximum(m_old, mx)
    m_ref[idx] = m_new
    alpha = jnp.exp2(m_old - m_new)
    sub = m_new if shift is None else m_new - shift
    return alpha, [jnp.exp2(s - sub).astype(BF16) for s in ss]


def _accumulate(acc_ref, rows, alpha, vts, ps):
    upd = _dot(vts[0], ps[0])
    for vt, p in zip(vts[1:], ps[1:]):
        upd = upd + _dot(vt, p)
    acc_ref[rows, :] = alpha * acc_ref[rows, :] + upd


def _normalized(acc_ref, idx):
    base = idx * VROWS
    return acc_ref[base:base + HEAD_DIM, :] / acc_ref[base + HEAD_DIM:base + HEAD_DIM + 1, :]


def _causal_tile(width=T):
    r = lax.broadcasted_iota(jnp.int32, (T, width), 0)
    c = lax.broadcasted_iota(jnp.int32, (T, width), 1)
    if width != T:
        c = jnp.bitwise_and(c, T - 1)
    return r, c


def _for_tile_groups(qi, process, paired=True):
    n_far = jnp.maximum(qi - 1, 0)
    if not paired:
        def single(j, carry):
            process([(j, "far")])
            return carry

        lax.fori_loop(0, n_far, single, 0)

        @pl.when(qi == 0)
        def _():
            process([(qi, "diag")])

        @pl.when(qi >= 1)
        def _():
            process([(qi - 1, "near"), (qi, "diag")])

        return

    n_pair = lax.shift_right_logical(n_far, 1)
    odd = jnp.bitwise_and(n_far, 1) == 1

    def pair(j, carry):
        process([(2 * j, "far"), (2 * j + 1, "far")])
        return carry

    lax.fori_loop(0, n_pair, pair, 0)

    @pl.when(qi == 0)
    def _():
        process([(qi, "diag")])

    @pl.when((qi >= 1) & jnp.logical_not(odd))
    def _():
        process([(qi - 1, "near"), (qi, "diag")])

    @pl.when((qi >= 1) & odd)
    def _():
        process([(qi - 2, "far"), (qi - 1, "near"), (qi, "diag")])


def _attn_call(kernel, name, bsz, nq, in_arrays, in_specs, scratch, out_width=GW):
    return pl.pallas_call(
        kernel,
        out_shape=jax.ShapeDtypeStruct((bsz, nq * T, out_width), BF16),
        grid=(bsz, nq),
        in_specs=in_specs,
        out_specs=pl.BlockSpec((1, T, out_width), lambda b, i: (b, i, 0)),
        scratch_shapes=scratch,
        compiler_params=pltpu.CompilerParams(
            dimension_semantics=("parallel", "arbitrary"), vmem_limit_bytes=VMEM_LIMIT),
        name=name,
    )(*in_arrays)


def _q_spec(rows=GW):
    return pl.BlockSpec((1, 1, rows, T), lambda b, i: (b, i, 0, 0))


def _seq_spec(shape):
    nd = len(shape)
    return pl.BlockSpec((1,) + tuple(shape[1:]), lambda b, i: (b,) + (0,) * (nd - 1))


def _diff_kernel(far_ref, cst_ref, qt_ref, k_ref, vt_ref, bias_ref, lam_ref, subg_ref, o_ref,
                 m_ref, acc_ref):
    qi = pl.program_id(1)
    qmaps = [_split_maps(_head_rows(qt_ref[0, 0], h)) for h in range(HPM)]
    m_ref[...] = jnp.full(m_ref.shape, NEG, F32)
    acc_ref[...] = jnp.zeros(acc_ref.shape, F32)
    r, c = _causal_tile()

    def process(tiles):
        scores = [[_dot(_head_cols(k_ref[0, ki], h), qmaps[h]) for ki, _ in tiles] for h in range(HPM)]
        probs = []
        for idx in range(2 * HPM):
            h = idx // 2
            ss = []
            for (ki, mode), sc in zip(tiles, scores[h]):
                s = sc[:, (idx % 2) * T:(idx % 2 + 1) * T]
                if mode == "far":
                    s = s + far_ref[h]
                elif mode == "near":
                    s = s + bias_ref[h, T:2 * T, :]
                else:
                    s = jnp.where(r <= c, s + bias_ref[h, 0:T, :], NEG)
                ss.append(s)
            probs.append(_softmax_probs(ss, m_ref, idx))
        for idx in range(2 * HPM):
            h = idx // 2
            vts = [_with_ones(vt_ref[0, ki, h * HEAD_DIM:(h + 1) * HEAD_DIM, :]) for ki, _ in tiles]
            alpha, ps = probs[idx]
            _accumulate(acc_ref, slice(idx * VROWS, (idx + 1) * VROWS), alpha, vts, ps)

    _for_tile_groups(qi, process)

    lam_init = cst_ref[0]
    lv = lam_ref[...]
    lam = (jnp.exp(jnp.sum(lv[0:1] * lv[1:2], axis=-1, keepdims=True))
           - jnp.exp(jnp.sum(lv[2:3] * lv[3:4], axis=-1, keepdims=True)) + lam_init)
    outs = []
    for h in range(HPM):
        o = _normalized(acc_ref, 2 * h) - lam * _normalized(acc_ref, 2 * h + 1)
        ms = jnp.mean(o * o, axis=0, keepdims=True)
        outs.append(o * lax.rsqrt(ms + EPS) * subg_ref[...] * (1.0 - lam_init))
    o_ref[0] = jnp.concatenate(outs, axis=0).T.astype(BF16)


def _diff_attention(far, cst, qt, k4, vt, bias, lam, subg):
    bsz, nq = qt.shape[0], qt.shape[1]
    return _attn_call(
        _diff_kernel, "diff_attention", bsz, nq,
        [far, cst, qt, k4, vt, bias, lam, subg],
        [_smem(), _smem(), _q_spec(), _seq_spec(k4.shape), _seq_spec(vt.shape),
         _resident(bias.shape), _whole(lam.shape), _whole(subg.shape)],
        [pltpu.VMEM((2 * HPM, 1, T), F32), pltpu.VMEM((2 * HPM * VROWS, T), F32)])


def _fox_kernel(qt_ref, k_ref, vt_ref, cq_ref, ck_ref, o_ref, m_ref, acc_ref):
    qi = pl.program_id(1)
    qh = [_head_rows(qt_ref[0, 0], h) for h in range(HPM)]
    m_ref[...] = jnp.full(m_ref.shape, NEG, F32)
    acc_ref[...] = jnp.zeros(acc_ref.shape, F32)
    r, c = _causal_tile()

    def process(tiles):
        scores = [[_dot(_head_cols(k_ref[0, ki], h), qh[h]) for ki, _ in tiles] for h in range(HPM)]
        probs = []
        for h in range(HPM):
            ss = []
            for (ki, mode), sc in zip(tiles, scores[h]):
                s = sc - ck_ref[0, ki][:, h:h + 1]
                if mode == "diag":
                    s = jnp.where(r <= c, s, NEG)
                ss.append(s)
            probs.append(_softmax_probs(ss, m_ref, h, shift=cq_ref[0, h:h + 1, :]))
        for h in range(HPM):
            vts = [_with_ones(vt_ref[0, ki, h * HEAD_DIM:(h + 1) * HEAD_DIM, :]) for ki, _ in tiles]
            alpha, ps = probs[h]
            _accumulate(acc_ref, slice(h * VROWS, (h + 1) * VROWS), alpha, vts, ps)

    _for_tile_groups(qi, process)
    outs = [_normalized(acc_ref, h) for h in range(HPM)]
    o_ref[0] = jnp.concatenate(outs, axis=0).T.astype(BF16)


def _fox_attention(qt, k4, vt, st, ck4):
    bsz, nq = qt.shape[0], qt.shape[1]
    return _attn_call(
        _fox_kernel, "forgetting_attention", bsz, nq,
        [qt, k4, vt, st, ck4],
        [_q_spec(), _seq_spec(k4.shape), _seq_spec(vt.shape),
         pl.BlockSpec((1, 16, T), lambda b, i: (b, 0, i)), _seq_spec(ck4.shape)],
        [pltpu.VMEM((HPM, 1, T), F32), pltpu.VMEM((HPM * VROWS, T), F32)])


def _sb_kernel(qt_ref, k_ref, vt_ref, tri_ref, o_ref, run_ref, acc_ref):
    qi = pl.program_id(1)
    qh = [_head_rows(qt_ref[0, 0], h) for h in range(HPM)]
    run_ref[...] = jnp.zeros(run_ref.shape, F32)
    acc_ref[...] = jnp.zeros(acc_ref.shape, F32)
    r, c = _causal_tile()

    def process(tiles):
        scores = [[_dot(_head_cols(k_ref[0, ki], h), qh[h]) for ki, _ in tiles] for h in range(HPM)]
        weights = []
        for h in range(HPM):
            run = run_ref[h]
            ws = []
            for (ki, diag), z in zip(tiles, scores[h]):
                log_beta = jnp.minimum(z, 0.0) - jnp.log2(1.0 + jnp.exp2(-jnp.abs(z)))
                log_keep = log_beta - z
                if diag:
                    log_keep = jnp.where(r < c, log_keep, 0.0)
                tail = _dot(tri_ref[...], log_keep.astype(BF16))
                a = jnp.exp2(log_beta + tail + run)
                if diag:
                    a = jnp.where(r < c, a, 0.0)
                ws.append(a.astype(BF16))
                run = run + jnp.sum(log_keep, axis=0, keepdims=True)
            run_ref[h] = run
            weights.append(ws)
        for h in range(HPM):
            rows = slice(h * HEAD_DIM, (h + 1) * HEAD_DIM)
            upd = None
            for (ki, _), a in zip(tiles, weights[h]):
                term = _dot(vt_ref[0, ki, rows, :], a)
                upd = term if upd is None else upd + term
            acc_ref[rows, :] = acc_ref[rows, :] + upd

    odd = jnp.bitwise_and(qi, 1) == 1

    @pl.when(odd)
    def _():
        process([(qi, True), (qi - 1, False)])

    @pl.when(jnp.logical_not(odd))
    def _():
        process([(qi, True)])

    base = qi - 1 - jnp.bitwise_and(qi, 1)

    def pair(j, carry):
        process([(base - 2 * j, False), (base - 2 * j - 1, False)])
        return carry

    lax.fori_loop(0, lax.shift_right_logical(qi, 1), pair, 0)
    o_ref[0] = acc_ref[...].T.astype(BF16)


def _sb_attention(qt, k4, vt, tri):
    bsz, nq = qt.shape[0], qt.shape[1]
    return _attn_call(
        _sb_kernel, "stick_breaking_attention", bsz, nq,
        [qt, k4, vt, tri],
        [_q_spec(), _seq_spec(k4.shape), _seq_spec(vt.shape), _resident(tri.shape)],
        [pltpu.VMEM((HPM, 1, T), F32), pltpu.VMEM((GW, T), F32)])


def _nsa_kernel(far_ref, qt_ref, kc_ref, vct_ref, ks_ref, vst_ref, kw_ref, vwt_ref, g_ref,
                bias_ref, ovt_ref, o_ref,
                sc_ref, imp_ref, sel_ref, m_ref, acc_ref):
    qi = pl.program_id(1)
    q0 = qi * T
    n_cmp = kc_ref.shape[1]
    n_blk = ovt_ref.shape[0]
    n_sel = min(N_SEL, n_blk)
    w4 = HPM * T
    qt = qt_ref[0, 0]
    qcat = jnp.concatenate([qt[h * HEAD_DIM:(h + 1) * HEAD_DIM, :] for h in range(HPM)], axis=1)
    far_row = jnp.concatenate([jnp.full((1, T), far_ref[HPM + h], F32) for h in range(HPM)], axis=1)
    r, c = _causal_tile(w4)

    sc_ref[...] = _dot(kc_ref[0], qcat)
    delta = bias_ref[2 * T:2 * T + BAND_ROWS, :] - far_row

    @pl.when(qi == 0)
    def _():
        half = BAND_ROWS // 2
        sc_ref[0:half, :] = sc_ref[0:half, :] + delta[half:BAND_ROWS]

    @pl.when(qi > 0)
    def _():
        c0 = pl.multiple_of(qi * (T // CMP_STRIDE) - BAND_ROWS // 2, 8)
        sc_ref[pl.ds(c0, BAND_ROWS), :] = sc_ref[pl.ds(c0, BAND_ROWS), :] + delta

    ci = lax.broadcasted_iota(jnp.int32, (n_cmp, w4), 0)
    ti = q0 + jnp.bitwise_and(lax.broadcasted_iota(jnp.int32, (n_cmp, w4), 1), T - 1)
    valid_c = ti - CMP_STRIDE * ci - (CMP_LEN - 1) >= 0
    s = jnp.where(valid_c, sc_ref[...] + far_row, NEG)
    e = jnp.exp2(s - jnp.max(s, axis=0, keepdims=True))
    p = jnp.where(valid_c, e / jnp.sum(e, axis=0, keepdims=True), 0.0)
    o_cmp = _dot(vct_ref[0], p.astype(BF16))
    pc_sum = p[:, 0:T]
    for h in range(1, HPM):
        pc_sum = pc_sum + p[:, h * T:(h + 1) * T]

    hi, lo = _split2(pc_sum)
    imp = _dot(ovt_ref[...], hi) + _dot(ovt_ref[...], lo)
    ji = lax.broadcasted_iota(jnp.int32, (n_blk, T), 0)
    tq = q0 + lax.broadcasted_iota(jnp.int32, (n_blk, T), 1)
    forced = (ji == lax.shift_right_logical(tq, int(math.log2(SEL_LEN)))) | (ji == 0)
    imp = jnp.where(forced, FORCED_SCORE, imp)
    imp = jnp.where(ji * SEL_LEN <= tq, imp, -1.0)
    imp_ref[...] = imp

    def rank_body(jp, cnt):
        row = imp_ref[pl.ds(jp, 1), :]
        tie = jnp.where(row == imp, 1.0, 0.0) * jnp.where(ji > jp, 1.0, 0.0)
        return cnt + jnp.where(row > imp, 1.0, 0.0) + tie

    cnt = lax.fori_loop(0, n_blk, rank_body, jnp.zeros((n_blk, T), F32))
    sel = jnp.where(cnt < float(n_sel), 1.0, 0.0)
    sel_ref[...] = jnp.concatenate([sel] * HPM, axis=1)

    m_ref[...] = jnp.full(m_ref.shape, NEG, F32)
    acc_ref[...] = jnp.zeros(acc_ref.shape, F32)

    def biased(s, mode):
        if mode == "far":
            return s + far_row
        if mode == "near":
            return s + bias_ref[T:2 * T, :]
        return s + bias_ref[0:T, :]

    def update(state, ss, vts):
        alpha, ps = _softmax_probs(ss, m_ref, state)
        _accumulate(acc_ref, slice(state * VROWS, (state + 1) * VROWS), alpha, vts, ps)

    def sel_mask(ki):
        per = T // SEL_LEN
        rows = [jnp.broadcast_to(sel_ref[pl.ds(ki * per + i, 1), :], (SEL_LEN, w4)) for i in range(per)]
        return jnp.concatenate(rows, axis=0) > 0.5

    def process_selected(tiles):
        scores = [_dot(ks_ref[0, ki], qcat) for ki, _ in tiles]
        ss = []
        for (ki, mode), sc in zip(tiles, scores):
            mask = sel_mask(ki)
            if mode == "diag":
                mask = mask & (r <= c)
            ss.append(jnp.where(mask, biased(sc, mode), NEG))
        update(0, ss, [_with_ones(vst_ref[0, ki]) for ki, _ in tiles])

    _for_tile_groups(qi, process_selected)

    def process_window(tiles):
        scores = [_dot(kw_ref[0, ki], qcat) for ki, _, _ in tiles]
        ss = []
        for (ki, mode, mask), sc in zip(tiles, scores):
            s = biased(sc, mode)
            ss.append(s if mask is None else jnp.where(mask, s, NEG))
        update(1, ss, [_with_ones(vwt_ref[0, ki]) for ki, _, _ in tiles])

    back = WINDOW // T

    def window_tiles(n):
        tiles = [(qi, "diag", r <= c)]
        for d in range(1, n + 1):
            mode = "near" if d == 1 else "far"
            tiles.append((qi - d, mode, (c < r) if d == back else None))
        return tiles

    for n in range(back + 1):
        cond = (qi == n) if n < back else (qi >= back)

        @pl.when(cond)
        def _():
            process_window(window_tiles(n))

    o_sel = acc_ref[0:HEAD_DIM, :] / acc_ref[HEAD_DIM:HEAD_DIM + 1, :]
    o_win = acc_ref[VROWS:VROWS + HEAD_DIM, :] / acc_ref[VROWS + HEAD_DIM:VROWS + HEAD_DIM + 1, :]
    gate = lambda br: jnp.concatenate(
        [g_ref[0, HPM + br * HPM + h:HPM + br * HPM + h + 1, :] for h in range(HPM)], axis=1)
    o = gate(0) * o_cmp + gate(1) * o_sel + gate(2) * o_win
    ot = jnp.concatenate([o[:, h * T:(h + 1) * T] for h in range(HPM)], axis=0)
    o_ref[0] = ot.T.astype(BF16)


def _nsa_attention(far, qt, kc, vct, ks4, vst, kw4, vwt, st, bias, ovt):
    bsz, nq = qt.shape[0], qt.shape[1]
    n_cmp = kc.shape[1]
    n_blk = ovt.shape[0]
    w4 = HPM * T
    return _attn_call(
        _nsa_kernel, "native_sparse_attention", bsz, nq,
        [far, qt, kc, vct, ks4, vst, kw4, vwt, st, bias, ovt],
        [_smem(), _q_spec(), _seq_spec(kc.shape), _seq_spec(vct.shape),
         _seq_spec(ks4.shape), _seq_spec(vst.shape), _seq_spec(kw4.shape), _seq_spec(vwt.shape),
         pl.BlockSpec((1, 16, T), lambda b, i: (b, 0, i)), _resident(bias.shape), _whole(ovt.shape)],
        [pltpu.VMEM((n_cmp, w4), F32), pltpu.VMEM((n_blk, T), F32), pltpu.VMEM((n_blk, w4), F32),
         pltpu.VMEM((2, 1, w4), F32), pltpu.VMEM((2 * VROWS, w4), F32)])


def _post_kernel(x_ref, oa_ref, ob_ref, oc_ref, od_ref, wo_ref, gta_ref, g_ref, sc_ref, sh_ref,
                 gtm_ref, wup_ref, cw_ref, cb_ref, wdn_ref, y_ref, carry_ref):
    d_ff = wdn_ref.shape[0]
    mixed = (_dot(oa_ref[0], wo_ref[0:GW]) + _dot(ob_ref[0], wo_ref[GW:2 * GW])
             + _dot(oc_ref[0], wo_ref[2 * GW:3 * GW]) + _dot(od_ref[0], wo_ref[3 * GW:4 * GW]))
    x1 = x_ref[0] + gta_ref[0] * mixed
    ms = jnp.mean(x1 * x1, axis=-1, keepdims=True)
    h = x1 * lax.rsqrt(ms + EPS) * g_ref[...]
    hb = (h * (1.0 + sc_ref[0]) + sh_ref[0]).astype(BF16)

    @pl.when(pl.program_id(1) == 0)
    def _():
        carry_ref[...] = jnp.zeros(carry_ref.shape, F32)

    def up(ch):
        gate = _dot(hb, wup_ref[:, ch * FF_CHUNK:(ch + 1) * FF_CHUNK])
        val = _dot(hb, wup_ref[:, d_ff + ch * FF_CHUNK:d_ff + (ch + 1) * FF_CHUNK])
        return gate, val

    row = lax.broadcasted_iota(jnp.int32, (TM_POST, FF_CHUNK), 0)
    y = jnp.zeros((TM_POST, x1.shape[1]), F32)
    n_chunks = d_ff // FF_CHUNK
    ahead = up(0)
    for ch in range(n_chunks):
        cols = slice(ch * FF_CHUNK, (ch + 1) * FF_CHUNK)
        gate, val = ahead
        if ch + 1 < n_chunks:
            ahead = up(ch + 1)
        prev = carry_ref[:, cols]
        g1 = jnp.where(row == 0, prev[7:8], pltpu.roll(gate, 1, 0))
        g2 = jnp.where(row == 0, prev[6:7], jnp.where(row == 1, prev[7:8], pltpu.roll(gate, 2, 0)))
        carry_ref[:, cols] = gate[TM_POST - 8:TM_POST]
        conv = cw_ref[0:1, cols] * g2 + cw_ref[1:2, cols] * g1 + cw_ref[2:3, cols] * gate + cb_ref[:, cols]
        act = conv * _sigmoid(conv) * val
        y = y + _dot(act.astype(BF16), wdn_ref[cols, :])
    y_ref[0] = x1 + gtm_ref[0] * y


def _post(x, oa, ob, oc, od, wo, gta, g, sc, sh, gtm, wup, cw, cb, wdn):
    bsz, s, d = x.shape
    tok = lambda width: pl.BlockSpec((1, TM_POST, width), lambda b, i: (b, i, 0))
    vec = pl.BlockSpec((1, 1, d), lambda b, i: (b, 0, 0))
    const = lambda a: pl.BlockSpec(a.shape, lambda b, i: (0,) * a.ndim, pipeline_mode=pl.Buffered(1))
    return pl.pallas_call(
        _post_kernel,
        out_shape=jax.ShapeDtypeStruct((bsz, s, d), F32),
        grid=(bsz, s // TM_POST),
        in_specs=[tok(d), tok(GW), tok(GW), tok(GW), tok(GW), const(wo), vec, _whole((1, d)), vec, vec,
                  vec, const(wup), _whole(cw.shape), _whole(cb.shape), const(wdn)],
        out_specs=tok(d),
        scratch_shapes=[pltpu.VMEM((8, wdn.shape[0]), F32)],
        compiler_params=pltpu.CompilerParams(
            dimension_semantics=("parallel", "arbitrary"), vmem_limit_bytes=VMEM_LIMIT),
        name="out_proj_mlp",
    )(x, oa, ob, oc, od, wo, gta, g, sc, sh, gtm, wup, cw, cb, wdn)


def _repack_w_in(w):
    d = w.shape[0]
    off = {}
    pos = 0
    for name, size in (("a_q", GW), ("a_k", GW), ("a_v", GW), ("b_q", GW), ("b_k", GW), ("b_v", GW),
                       ("b_f", HPM), ("c_q", GW), ("c_k", GW), ("c_v", GW), ("d_q", GW),
                       ("d_kc", HEAD_DIM), ("d_vc", HEAD_DIM), ("d_ks", HEAD_DIM), ("d_vs", HEAD_DIM),
                       ("d_kw", HEAD_DIM), ("d_vw", HEAD_DIM), ("d_g", 3 * HPM)):
        off[name] = (pos, size)
        pos += size
    assert pos == w.shape[1]
    col = lambda n: w[:, off[n][0]:off[n][0] + off[n][1]]
    order = ["a_q", "a_k", "a_v", "b_q", "b_k", "b_v", "c_q", "c_k", "c_v", "d_q",
             "d_ks", "d_vs", "d_kw", "d_vw", "d_kc", "d_vc", "b_f", "d_g"]
    parts = [col(n) for n in order] + [jnp.zeros((d, 128 - HPM - 3 * HPM), w.dtype)]
    out = jnp.concatenate(parts, axis=1).astype(BF16)
    assert out.shape[1] == N_PROJ
    return out


def kernel(x, c, rel_bias, ada_w, ada_b, norm_mix_g, norm_ffn_g, w_in, w_out, diff_qnorm_g, diff_knorm_g, diff_lambda, diff_subln_g, fox_qnorm_g, fox_knorm_g, fox_b_f, nsa_qnorm_g, nsa_knorm_g, nsa_pe, nsa_phi_w, ffn_w_up, ffn_conv_w, ffn_conv_b, ffn_w_down):
    bsz, s, d = x.shape
    depth = ada_w.shape[0]
    assert s % T == 0 and WINDOW % T == 0 and T % SEL_LEN == 0 and d == 4 * GW
    assert T & (T - 1) == 0 and BAND_ROWS // 2 >= (MAX_DISTANCE + CMP_LEN) // CMP_STRIDE
    nk = s // T
    n_chunk = s // CMP_STRIDE
    n_blk = s // SEL_LEN
    d_ff = ffn_w_down.shape[1]
    assert d_ff % FF_CHUNK == 0

    bd32 = jnp.asarray(_block_ones(GW, DIFF_QK_DIM), BF16)
    bd64 = jnp.asarray(_block_ones(GW, HEAD_DIM), BF16)
    ti = np.arange(T)
    tri_incl = jnp.asarray((ti[None, :] <= ti[:, None]).astype(np.float32), BF16)
    tri_later = jnp.asarray((ti[None, :] > ti[:, None]).astype(np.float32), BF16)
    c_start = np.arange(n_chunk) * CMP_STRIDE
    s_start = np.arange(n_blk) * SEL_LEN
    ov = ((c_start[None, :] < s_start[:, None] + SEL_LEN) & (s_start[:, None] < c_start[None, :] + CMP_LEN)
          & (np.arange(n_chunk)[None, :] < n_chunk - 1))
    ovt = jnp.asarray(ov.astype(np.float32), BF16)

    bias_a, bias_d = _bias_tiles(rel_bias)
    far = rel_bias[NUM_BUCKETS - 1] * LOG2E
    mod = _adaln(c, ada_w, ada_b)

    for l in range(depth):
        sh_a, sc_a, gt_a, sh_m, sc_m, gt_m = [m.reshape(bsz, 1, d) for m in jnp.split(mod[l], 6, axis=-1)]
        w_re = _repack_w_in(w_in[l])
        scale_a = DIFF_QK_DIM ** -0.5 * LOG2E
        scale = HEAD_DIM ** -0.5 * LOG2E
        ones = jnp.ones((HEAD_DIM,), F32)
        gains = jnp.stack([
            jnp.tile(diff_qnorm_g[l] * scale_a, GW // DIFF_QK_DIM), jnp.tile(diff_knorm_g[l], GW // DIFF_QK_DIM),
            jnp.tile(fox_qnorm_g[l] * scale, HPM), jnp.tile(fox_knorm_g[l], HPM),
            jnp.tile(nsa_qnorm_g[l] * scale, HPM),
            jnp.concatenate([nsa_knorm_g[l, 1], ones, nsa_knorm_g[l, 2], ones]),
            jnp.ones((GW,), F32), jnp.ones((GW,), F32)])
        (a_qt, a_k, a_vt, b_qt, b_k, b_vt, c_qt, c_k, c_vt, d_qt,
         d_ks, d_kw, d_vst, d_vwt, d_kc, d_vc, smalls) = _inproj(
            x, norm_mix_g[l].reshape(1, d), sc_a, sh_a, w_re, gains, bd32, bd64)

        bf_row = jnp.pad(fox_b_f[l], (0, 128 - HPM)).reshape(1, 128)
        ck, st = _scalars(smalls, bf_row, tri_incl)

        half = CMP_STRIDE * HEAD_DIM
        pe_rows = nsa_pe[l].reshape(4, half)
        w_pad = jnp.pad(nsa_phi_w[l], ((0, 0), (0, 0), (0, 128 - HEAD_DIM))).astype(BF16)
        gk_pad = jnp.pad(nsa_knorm_g[l, 0], (0, 128 - HEAD_DIM)).reshape(1, 128)
        kc, vct = _compress(d_kc.reshape(bsz, n_chunk, half), d_vc.reshape(bsz, n_chunk, half),
                            pe_rows, w_pad, gk_pad)

        lam_init = 0.8 - 0.6 * math.exp(-0.3 * l)
        cst = jnp.full((1,), lam_init, F32)
        k4 = lambda a: a.reshape(bsz, nk, T, a.shape[-1])
        o_a = _diff_attention(far, cst, a_qt, k4(a_k), a_vt, bias_a, diff_lambda[l],
                              diff_subln_g[l].reshape(HEAD_DIM, 1))
        o_b = _fox_attention(b_qt, k4(b_k), b_vt, st, k4(ck))
        o_c = _sb_attention(c_qt, k4(c_k), c_vt, tri_later)
        o_d = _nsa_attention(far, d_qt, kc, vct, k4(d_ks), d_vst, k4(d_kw), d_vwt, st, bias_d, ovt)

        x = _post(x, o_a, o_b, o_c, o_d, w_out[l].astype(BF16), gt_a, norm_ffn_g[l].reshape(1, d),
                  sc_m, sh_m, gt_m, ffn_w_up[l].astype(BF16), ffn_conv_w[l].reshape(CONV_WIDTH, d_ff),
                  ffn_conv_b[l].reshape(1, d_ff), ffn_w_down[l].astype(BF16))
    return x
```

```python
import math

import numpy as np
import jax
import jax.numpy as jnp
from jax import lax
from jax.experimental import pallas as pl
from jax.experimental.pallas import tpu as pltpu

HEAD_DIM = 64
HPM = 4
GW = HPM * HEAD_DIM
DIFF_QK_DIM = HEAD_DIM // 2
NUM_BUCKETS = 32
MAX_EXACT = NUM_BUCKETS // 2
MAX_DISTANCE = 128
CMP_LEN = 32
CMP_STRIDE = 16
SEL_LEN = 64
N_SEL = 16
WINDOW = 512
FORCED_SCORE = 1.0e4
NEG = -1.0e30
EPS = 1e-6
CONV_WIDTH = 3
LOG2E = 1.4426950408889634

T = 256
TM = T
TM_POST = 256
FF_CHUNK = 256
ONES_ROWS = 16
VROWS = HEAD_DIM + ONES_ROWS
BAND_ROWS = 2 * T // CMP_STRIDE
VMEM_LIMIT = 56 * 1024 * 1024

BND_FOX, BND_DIFF, BND_SEL, BND_WIN, BND_SIZE = 0, 2, 7, 12, 24
BOUND_MARGIN = 1.02
MAX_LOG2_SPAN = 100.0

F32 = jnp.float32
BF16 = jnp.bfloat16


def _dot(a, b):
    return jnp.dot(a, b, preferred_element_type=F32)


def _split2(x):
    hi = x.astype(BF16)
    lo = (x - hi.astype(F32)).astype(BF16)
    return hi, lo


def _log_sigmoid(z):
    return jnp.minimum(z, 0.0) - jnp.log(1.0 + jnp.exp(-jnp.abs(z)))


def _sigmoid(z):
    return 1.0 / (1.0 + jnp.exp(-z))


def _whole(shape):
    nd = len(shape)
    return pl.BlockSpec(shape, lambda *_: (0,) * nd)


def _resident(shape):
    nd = len(shape)
    return pl.BlockSpec(shape, lambda *_: (0,) * nd, pipeline_mode=pl.Buffered(1))


def _smem():
    return pl.BlockSpec(memory_space=pltpu.SMEM)


def _t5_bucket_np(dist):
    n = np.maximum(dist, 0)
    ratio = math.log(MAX_DISTANCE / MAX_EXACT)
    out = None
    for dt in (np.float32, np.float64):
        large = MAX_EXACT + (np.log(np.maximum(n, 1).astype(dt) / dt(MAX_EXACT)) / dt(ratio)
                             * dt(NUM_BUCKETS - MAX_EXACT)).astype(np.int32)
        b = np.where(n < MAX_EXACT, n, np.minimum(large, NUM_BUCKETS - 1)).astype(np.int32)
        assert out is None or np.array_equal(out, b)
        out = b
    return out


def _bucket_tiles():
    r = np.arange(T)[:, None]
    c = np.arange(T)[None, :]
    diag = _t5_bucket_np(c - r)
    near = _t5_bucket_np(T + c - r)
    rb = np.arange(BAND_ROWS)[:, None]
    band = _t5_bucket_np(c - CMP_STRIDE * rb + (T - CMP_LEN + 1))
    return np.concatenate([diag, near, band], axis=0)


def _block_ones(n, group):
    i = np.arange(n)
    return (i[:, None] // group == i[None, :] // group).astype(np.float32)


def _bias_kernel(tbl_ref, bkt_ref, a_ref, d_ref):
    b = bkt_ref[...]
    for h in range(2 * HPM):
        acc = jnp.zeros(b.shape, F32)
        for k in range(NUM_BUCKETS):
            acc = jnp.where(b == k, tbl_ref[k, h] * LOG2E, acc)
        if h < HPM:
            a_ref[h] = acc[0:2 * T]
        else:
            d_ref[:, (h - HPM) * T:(h - HPM + 1) * T] = acc


def _bias_tiles(rel_bias):
    bkt = jnp.asarray(_bucket_tiles())
    rows = bkt.shape[0]
    return pl.pallas_call(
        _bias_kernel,
        out_shape=[jax.ShapeDtypeStruct((HPM, 2 * T, T), F32), jax.ShapeDtypeStruct((rows, HPM * T), F32)],
        in_specs=[_smem(), _whole(bkt.shape)],
        out_specs=[_whole((HPM, 2 * T, T)), _whole((rows, HPM * T))],
        name="bias_tiles",
    )(rel_bias, bkt)


def _adaln_kernel(c_ref, w_ref, b_ref, o_ref):
    c = c_ref[...]
    ca = c * _sigmoid(c)
    o_ref[0] = _dot(ca.astype(BF16), w_ref[0].astype(BF16)) + b_ref[0]


def _adaln(c, ada_w, ada_b):
    depth, d, n = ada_w.shape
    bsz = c.shape[0]
    rows = -(-bsz // 8) * 8
    cp = jnp.pad(c, ((0, rows - bsz), (0, 0)))
    tn = 1536
    out = pl.pallas_call(
        _adaln_kernel,
        out_shape=jax.ShapeDtypeStruct((depth, rows, n), F32),
        grid=(depth, n // tn),
        in_specs=[pl.BlockSpec((rows, d), lambda l, j: (0, 0)),
                  pl.BlockSpec((1, d, tn), lambda l, j: (l, 0, j)),
                  pl.BlockSpec((1, 1, tn), lambda l, j: (l, 0, j))],
        out_specs=pl.BlockSpec((1, rows, tn), lambda l, j: (l, 0, j)),
        name="adaln",
    )(cp, ada_w, ada_b.reshape(depth, 1, n))
    return out[:, :bsz]


N_PROJ = 11 * GW + 2 * 128


def _inproj_kernel(x_ref, g_ref, sc_ref, sh_ref, w_ref, gains_ref, bd32_ref, bd64_ref,
                   aq_ref, ak_ref, av_ref, bq_ref, bk_ref, bv_ref, cq_ref, ck_ref, cv_ref, dq_ref,
                   dks_ref, dkw_ref, dvs_ref, dvw_ref, dkc_ref, dvc_ref, sm_ref):
    x = x_ref[0]
    ms = jnp.mean(x * x, axis=-1, keepdims=True)
    h = x * lax.rsqrt(ms + EPS) * g_ref[...]
    h = h * (1.0 + sc_ref[0]) + sh_ref[0]
    hb = h.astype(BF16)

    def proj(group, width=GW):
        off = group * GW
        return _dot(hb, w_ref[:, off:off + width])

    def segnorm(y, bd_ref, inv_n, gain_row):
        hi, lo = _split2(y * y)
        ss = _dot(hi, bd_ref[...]) + _dot(lo, bd_ref[...])
        return y * lax.rsqrt(ss * inv_n + EPS) * gains_ref[gain_row:gain_row + 1, :]

    def put_t(ref, y):
        ref[0, 0] = y.T.astype(BF16)

    raw0 = proj(0)
    raw1 = proj(1)
    put_t(aq_ref, segnorm(raw0, bd32_ref, 1.0 / DIFF_QK_DIM, 0))
    put_t(av_ref, proj(2))
    raw3 = proj(3)
    ak_ref[0] = segnorm(raw1, bd32_ref, 1.0 / DIFF_QK_DIM, 1).astype(BF16)
    put_t(bv_ref, proj(5))
    raw4 = proj(4)
    put_t(bq_ref, segnorm(raw3, bd64_ref, 1.0 / HEAD_DIM, 2))
    put_t(cq_ref, proj(6) * (HEAD_DIM ** -0.5 * LOG2E))
    raw9 = proj(9)
    bk_ref[0] = segnorm(raw4, bd64_ref, 1.0 / HEAD_DIM, 3).astype(BF16)
    ck_ref[0] = proj(7).astype(BF16)
    y = proj(10)
    put_t(dq_ref, segnorm(raw9, bd64_ref, 1.0 / HEAD_DIM, 4))
    put_t(cv_ref, proj(8))
    cmp = _dot(hb, w_ref[:, 11 * GW:11 * GW + 128])
    dkc_ref[0] = cmp[:, 0:HEAD_DIM]
    dvc_ref[0] = cmp[:, HEAD_DIM:2 * HEAD_DIM]
    sm_ref[0] = _dot(hb, w_ref[:, 11 * GW + 128:11 * GW + 256])

    yn = segnorm(y, bd64_ref, 1.0 / HEAD_DIM, 5)
    dks_ref[0] = yn[:, 0:HEAD_DIM].astype(BF16)
    dkw_ref[0] = yn[:, 2 * HEAD_DIM:3 * HEAD_DIM].astype(BF16)
    yt = y.T
    dvs_ref[0, 0] = yt[HEAD_DIM:2 * HEAD_DIM].astype(BF16)
    dvw_ref[0, 0] = yt[3 * HEAD_DIM:4 * HEAD_DIM].astype(BF16)


def _inproj(x, g, sc, sh, w_re, gains, bd32, bd64):
    bsz, s, d = x.shape
    nk = s // T
    tok = lambda width, dt: jax.ShapeDtypeStruct((bsz, s, width), dt)
    tr = lambda rows: jax.ShapeDtypeStruct((bsz, nk, rows, T), BF16)
    tok_spec = lambda width: pl.BlockSpec((1, TM, width), lambda b, i: (b, i, 0))
    tr_spec = lambda rows: pl.BlockSpec((1, 1, rows, T), lambda b, i: (b, i, 0, 0))
    vec = pl.BlockSpec((1, 1, d), lambda b, i: (b, 0, 0))
    out_shape = [tr(GW), tok(GW, BF16), tr(GW)] * 3 + [tr(GW)] + [
        tok(HEAD_DIM, BF16), tok(HEAD_DIM, BF16), tr(HEAD_DIM), tr(HEAD_DIM),
        tok(HEAD_DIM, F32), tok(HEAD_DIM, F32), tok(128, F32)]
    out_specs = [tr_spec(GW), tok_spec(GW), tr_spec(GW)] * 3 + [tr_spec(GW)] + [
        tok_spec(HEAD_DIM), tok_spec(HEAD_DIM), tr_spec(HEAD_DIM), tr_spec(HEAD_DIM),
        tok_spec(HEAD_DIM), tok_spec(HEAD_DIM), tok_spec(128)]
    return pl.pallas_call(
        _inproj_kernel,
        out_shape=out_shape,
        grid=(bsz, s // TM),
        in_specs=[pl.BlockSpec((1, TM, d), lambda b, i: (b, i, 0)),
                  _whole((1, d)), vec, vec,
                  _whole(w_re.shape), _whole(gains.shape), _whole(bd32.shape), _whole(bd64.shape)],
        out_specs=out_specs,
        compiler_params=pltpu.CompilerParams(
            dimension_semantics=("parallel", "parallel"), vmem_limit_bytes=VMEM_LIMIT),
        name="inproj",
    )(x, g, sc, sh, w_re, gains, bd32, bd64)


def _scalars_kernel(sm_ref, bf_ref, tri_ref, ck_ref, st_ref):
    s = sm_ref.shape[1]
    carry = jnp.zeros((1, 128), F32)
    for blk in range(s // T):
        rows = slice(blk * T, (blk + 1) * T)
        lf = _log_sigmoid(sm_ref[0, rows, :] + bf_ref[...])
        h1 = lf.astype(BF16)
        r1 = lf - h1.astype(F32)
        h2 = r1.astype(BF16)
        h3 = (r1 - h2.astype(F32)).astype(BF16)
        tri = tri_ref[...]
        cb = _dot(tri, h1) + _dot(tri, h2) + _dot(tri, h3) + carry
        carry = cb[T - 1:T, :]
        cb2 = cb * LOG2E
        ck_ref[0, rows, :] = cb2
        col = lax.broadcasted_iota(jnp.int32, (T, 128), 1)
        comb = jnp.where(col < HPM, cb2, _sigmoid(sm_ref[0, rows, :]))
        st_ref[0, :, rows] = comb.T[0:16]


def _scalars(smalls, bf_row, tri):
    bsz, s, _ = smalls.shape
    return pl.pallas_call(
        _scalars_kernel,
        out_shape=[jax.ShapeDtypeStruct((bsz, s, 128), F32), jax.ShapeDtypeStruct((bsz, 16, s), F32)],
        grid=(bsz,),
        in_specs=[pl.BlockSpec((1, s, 128), lambda b: (b, 0, 0)), _whole((1, 128)), _whole((T, T))],
        out_specs=[pl.BlockSpec((1, s, 128), lambda b: (b, 0, 0)),
                   pl.BlockSpec((1, 16, s), lambda b: (b, 0, 0))],
        compiler_params=pltpu.CompilerParams(dimension_semantics=("parallel",)),
        name="token_scalars",
    )(smalls, bf_row, tri)


def _compress_kernel(xk_ref, xv_ref, pe_ref, w_ref, gk_ref, kc_ref, vct_ref):
    half = CMP_STRIDE * HEAD_DIM
    n = xk_ref.shape[1]

    def windows(x, pe_a, pe_b, wi):
        ya = _dot((x + pe_a).astype(BF16), w_ref[wi, 0:half, :])
        yb = _dot((x + pe_b).astype(BF16), w_ref[wi, half:2 * half, :])
        return ya + pltpu.roll(yb, n - 1, 0)

    kc = windows(xk_ref[0], pe_ref[0:1], pe_ref[1:2], 0)
    ss = jnp.sum(kc * kc, axis=-1, keepdims=True)
    kc = kc * lax.rsqrt(ss * (1.0 / HEAD_DIM) + EPS) * gk_ref[...]
    kc_ref[0] = kc[:, 0:HEAD_DIM].astype(BF16)
    vc = windows(xv_ref[0], pe_ref[2:3], pe_ref[3:4], 1)
    vct_ref[0] = vc.T[0:HEAD_DIM].astype(BF16)


def _compress(xk, xv, pe_rows, w_pad, gk_pad):
    bsz, n, width = xk.shape
    return pl.pallas_call(
        _compress_kernel,
        out_shape=[jax.ShapeDtypeStruct((bsz, n, HEAD_DIM), BF16),
                   jax.ShapeDtypeStruct((bsz, HEAD_DIM, n), BF16)],
        grid=(bsz,),
        in_specs=[pl.BlockSpec((1, n, width), lambda b: (b, 0, 0)),
                  pl.BlockSpec((1, n, width), lambda b: (b, 0, 0)),
                  _whole(pe_rows.shape), _whole(w_pad.shape), _whole(gk_pad.shape)],
        out_specs=[pl.BlockSpec((1, n, HEAD_DIM), lambda b: (b, 0, 0)),
                   pl.BlockSpec((1, HEAD_DIM, n), lambda b: (b, 0, 0))],
        compiler_params=pltpu.CompilerParams(dimension_semantics=("parallel",)),
        name="nsa_compress",
    )(xk, xv, pe_rows, w_pad, gk_pad)


def _head_rows(qt, h):
    return qt[h * HEAD_DIM:(h + 1) * HEAD_DIM, :]


def _head_cols(kt, h):
    return kt[:, h * HEAD_DIM:(h + 1) * HEAD_DIM]


def _split_maps(qh):
    qf = qh.astype(F32)
    row = lax.broadcasted_iota(jnp.int32, qf.shape, 0)
    first = jnp.where(row < DIFF_QK_DIM, qf, 0.0).astype(BF16)
    second = jnp.where(row >= DIFF_QK_DIM, qf, 0.0).astype(BF16)
    return jnp.concatenate([first, second], axis=1)


def _with_ones(vt):
    return jnp.concatenate([vt, jnp.ones((ONES_ROWS, vt.shape[1]), BF16)], axis=0)


def _softmax_probs(ss, m_ref, idx, shift=None, fixed=None):
    if fixed is not None:
        already = isinstance(fixed, float) and fixed == 0.0
        return None, [jnp.exp2(s if already else s - fixed).astype(BF16) for s in ss]
    m_old = m_ref[idx]
    mx = jnp.max(ss[0], axis=0, keepdims=True)
    for s in ss[1:]:
        mx = jnp.maximum(mx, jnp.max(s, axis=0, keepdims=True))
    if shift is not None:
        mx = mx + shift
    m_new = jnp.maximum(m_old, mx)
    m_ref[idx] = m_new
    alpha = jnp.exp2(m_old - m_new)
    sub = m_new if shift is None else m_new - shift
    return alpha, [jnp.exp2(s - sub).astype(BF16) for s in ss]


def _accumulate(acc_ref, rows, alpha, vts, ps):
    upd = _dot(vts[0], ps[0])
    for vt, p in zip(vts[1:], ps[1:]):
        upd = upd + _dot(vt, p)
    acc_ref[rows, :] = (acc_ref[rows, :] if alpha is None else alpha * acc_ref[rows, :]) + upd


def _normalized(acc_ref, idx):
    base = idx * VROWS
    return acc_ref[base:base + HEAD_DIM, :] / acc_ref[base + HEAD_DIM:base + HEAD_DIM + 1, :]


def _causal_tile(width=T):
    r = lax.broadcasted_iota(jnp.int32, (T, width), 0)
    c = lax.broadcasted_iota(jnp.int32, (T, width), 1)
    if width != T:
        c = jnp.bitwise_and(c, T - 1)
    return r, c


def _for_tile_groups(qi, process, paired=True):
    n_far = jnp.maximum(qi - 1, 0)
    if not paired:
        def single(j, carry):
            process([(j, "far")])
            return carry

        lax.fori_loop(0, n_far, single, 0)

        @pl.when(qi == 0)
        def _():
            process([(qi, "diag")])

        @pl.when(qi >= 1)
        def _():
            process([(qi - 1, "near"), (qi, "diag")])

        return

    n_pair = lax.shift_right_logical(n_far, 1)
    odd = jnp.bitwise_and(n_far, 1) == 1

    def pair(j, carry):
        process([(2 * j, "far"), (2 * j + 1, "far")])
        return carry

    lax.fori_loop(0, n_pair, pair, 0)

    @pl.when(qi == 0)
    def _():
        process([(qi, "diag")])

    @pl.when((qi >= 1) & jnp.logical_not(odd))
    def _():
        process([(qi - 1, "near"), (qi, "diag")])

    @pl.when((qi >= 1) & odd)
    def _():
        process([(qi - 2, "far"), (qi - 1, "near"), (qi, "diag")])


def _run_bounded_or_online(bounded_ok, qi, process):
    @pl.when(bounded_ok)
    def _():
        _for_tile_groups(qi, lambda tiles: process(tiles, True))

    @pl.when(jnp.logical_not(bounded_ok))
    def _():
        _for_tile_groups(qi, lambda tiles: process(tiles, False))


def _attn_call(kernel, name, bsz, nq, in_arrays, in_specs, scratch, out_width=GW):
    return pl.pallas_call(
        kernel,
        out_shape=jax.ShapeDtypeStruct((bsz, nq * T, out_width), BF16),
        grid=(bsz, nq),
        in_specs=in_specs,
        out_specs=pl.BlockSpec((1, T, out_width), lambda b, i: (b, i, 0)),
        scratch_shapes=scratch,
        compiler_params=pltpu.CompilerParams(
            dimension_semantics=("parallel", "arbitrary"), vmem_limit_bytes=VMEM_LIMIT),
        name=name,
    )(*in_arrays)


def _q_spec(rows=GW):
    return pl.BlockSpec((1, 1, rows, T), lambda b, i: (b, i, 0, 0))


def _seq_spec(shape):
    nd = len(shape)
    return pl.BlockSpec((1,) + tuple(shape[1:]), lambda b, i: (b,) + (0,) * (nd - 1))


def _diff_kernel(far_ref, cst_ref, bnd_ref, qt_ref, k_ref, vt_ref, bias_ref, lam_ref, subg_ref, o_ref,
                 m_ref, acc_ref):
    qi = pl.program_id(1)
    qmaps = [_split_maps(_head_rows(qt_ref[0, 0], h)) for h in range(HPM)]
    m_ref[...] = jnp.full(m_ref.shape, NEG, F32)
    acc_ref[...] = jnp.zeros(acc_ref.shape, F32)
    r, c = _causal_tile()

    def process(tiles, bounded):
        scores = [[_dot(_head_cols(k_ref[0, ki], h), qmaps[h]) for ki, _ in tiles] for h in range(HPM)]
        probs = []
        for idx in range(2 * HPM):
            h = idx // 2
            bound = bnd_ref[BND_DIFF + 1 + h] if bounded else None
            ss = []
            for (ki, mode), sc in zip(tiles, scores[h]):
                s = sc[:, (idx % 2) * T:(idx % 2 + 1) * T]
                if mode == "far":
                    s = s + ((far_ref[h] - bound) if bounded else far_ref[h])
                elif mode == "near":
                    s = s + bias_ref[h, T:2 * T, :]
                else:
                    s = jnp.where(r <= c, s + bias_ref[h, 0:T, :], NEG)
                if bounded and mode != "far":
                    s = s - bound
                ss.append(s)
            probs.append(_softmax_probs(ss, m_ref, idx, fixed=0.0 if bounded else None))
        for idx in range(2 * HPM):
            h = idx // 2
            vts = [_with_ones(vt_ref[0, ki, h * HEAD_DIM:(h + 1) * HEAD_DIM, :]) for ki, _ in tiles]
            alpha, ps = probs[idx]
            _accumulate(acc_ref, slice(idx * VROWS, (idx + 1) * VROWS), alpha, vts, ps)

    _run_bounded_or_online(bnd_ref[BND_DIFF] > 0.5, qi, process)

    lam_init = cst_ref[0]
    lv = lam_ref[...]
    lam = (jnp.exp(jnp.sum(lv[0:1] * lv[1:2], axis=-1, keepdims=True))
           - jnp.exp(jnp.sum(lv[2:3] * lv[3:4], axis=-1, keepdims=True)) + lam_init)
    outs = []
    for h in range(HPM):
        o = _normalized(acc_ref, 2 * h) - lam * _normalized(acc_ref, 2 * h + 1)
        ms = jnp.mean(o * o, axis=0, keepdims=True)
        outs.append(o * lax.rsqrt(ms + EPS) * subg_ref[...] * (1.0 - lam_init))
    o_ref[0] = jnp.concatenate(outs, axis=0).T.astype(BF16)


def _diff_attention(far, cst, bnd, qt, k4, vt, bias, lam, subg):
    bsz, nq = qt.shape[0], qt.shape[1]
    return _attn_call(
        _diff_kernel, "diff_attention", bsz, nq,
        [far, cst, bnd, qt, k4, vt, bias, lam, subg],
        [_smem(), _smem(), _smem(), _q_spec(), _seq_spec(k4.shape), _seq_spec(vt.shape),
         _resident(bias.shape), _whole(lam.shape), _whole(subg.shape)],
        [pltpu.VMEM((2 * HPM, 1, T), F32), pltpu.VMEM((2 * HPM * VROWS, T), F32)])


def _fox_kernel(bnd_ref, qt_ref, k_ref, vt_ref, cq_ref, ck_ref, o_ref, m_ref, acc_ref):
    qi = pl.program_id(1)
    qh = [_head_rows(qt_ref[0, 0], h) for h in range(HPM)]
    m_ref[...] = jnp.full(m_ref.shape, NEG, F32)
    acc_ref[...] = jnp.zeros(acc_ref.shape, F32)
    r, c = _causal_tile()

    def process(tiles, bounded):
        scores = [[_dot(_head_cols(k_ref[0, ki], h), qh[h]) for ki, _ in tiles] for h in range(HPM)]
        probs = []
        for h in range(HPM):
            ss = []
            for (ki, mode), sc in zip(tiles, scores[h]):
                s = sc - ck_ref[0, ki][:, h:h + 1]
                if mode == "diag":
                    s = jnp.where(r <= c, s, NEG)
                ss.append(s)
            cq = cq_ref[0, h:h + 1, :]
            fixed = (bnd_ref[BND_FOX + 1] - cq) if bounded else None
            probs.append(_softmax_probs(ss, m_ref, h, shift=cq, fixed=fixed))
        for h in range(HPM):
            vts = [_with_ones(vt_ref[0, ki, h * HEAD_DIM:(h + 1) * HEAD_DIM, :]) for ki, _ in tiles]
            alpha, ps = probs[h]
            _accumulate(acc_ref, slice(h * VROWS, (h + 1) * VROWS), alpha, vts, ps)

    _run_bounded_or_online(bnd_ref[BND_FOX] > 0.5, qi, process)
    outs = [_normalized(acc_ref, h) for h in range(HPM)]
    o_ref[0] = jnp.concatenate(outs, axis=0).T.astype(BF16)


def _fox_attention(bnd, qt, k4, vt, st, ck4):
    bsz, nq = qt.shape[0], qt.shape[1]
    return _attn_call(
        _fox_kernel, "forgetting_attention", bsz, nq,
        [bnd, qt, k4, vt, st, ck4],
        [_smem(), _q_spec(), _seq_spec(k4.shape), _seq_spec(vt.shape),
         pl.BlockSpec((1, 16, T), lambda b, i: (b, 0, i)), _seq_spec(ck4.shape)],
        [pltpu.VMEM((HPM, 1, T), F32), pltpu.VMEM((HPM * VROWS, T), F32)])


def _sb_kernel(qt_ref, k_ref, vt_ref, tri_ref, o_ref, run_ref, acc_ref):
    qi = pl.program_id(1)
    qh = [_head_rows(qt_ref[0, 0], h) for h in range(HPM)]
    run_ref[...] = jnp.zeros(run_ref.shape, F32)
    acc_ref[...] = jnp.zeros(acc_ref.shape, F32)
    r, c = _causal_tile()

    def process(tiles):
        scores = [[_dot(_head_cols(k_ref[0, ki], h), qh[h]) for ki, _ in tiles] for h in range(HPM)]
        weights = []
        for h in range(HPM):
            run = run_ref[h]
            ws = []
            for (ki, diag), z in zip(tiles, scores[h]):
                log_beta = jnp.minimum(z, 0.0) - jnp.log2(1.0 + jnp.exp2(-jnp.abs(z)))
                log_keep = log_beta - z
                if diag:
                    log_keep = jnp.where(r < c, log_keep, 0.0)
                tail = _dot(tri_ref[...], log_keep.astype(BF16))
                a = jnp.exp2(log_beta + tail + run)
                if diag:
                    a = jnp.where(r < c, a, 0.0)
                ws.append(a.astype(BF16))
                run = run + jnp.sum(log_keep, axis=0, keepdims=True)
            run_ref[h] = run
            weights.append(ws)
        for h in range(HPM):
            rows = slice(h * HEAD_DIM, (h + 1) * HEAD_DIM)
            upd = None
            for (ki, _), a in zip(tiles, weights[h]):
                term = _dot(vt_ref[0, ki, rows, :], a)
                upd = term if upd is None else upd + term
            acc_ref[rows, :] = acc_ref[rows, :] + upd

    odd = jnp.bitwise_and(qi, 1) == 1

    @pl.when(odd)
    def _():
        process([(qi, True), (qi - 1, False)])

    @pl.when(jnp.logical_not(odd))
    def _():
        process([(qi, True)])

    base = qi - 1 - jnp.bitwise_and(qi, 1)

    def pair(j, carry):
        process([(base - 2 * j, False), (base - 2 * j - 1, False)])
        return carry

    lax.fori_loop(0, lax.shift_right_logical(qi, 1), pair, 0)
    o_ref[0] = acc_ref[...].T.astype(BF16)


def _sb_attention(qt, k4, vt, tri):
    bsz, nq = qt.shape[0], qt.shape[1]
    return _attn_call(
        _sb_kernel, "stick_breaking_attention", bsz, nq,
        [qt, k4, vt, tri],
        [_q_spec(), _seq_spec(k4.shape), _seq_spec(vt.shape), _resident(tri.shape)],
        [pltpu.VMEM((HPM, 1, T), F32), pltpu.VMEM((GW, T), F32)])


def _nsa_kernel(far_ref, bnd_ref, qt_ref, kc_ref, vct_ref, ks_ref, vst_ref, kw_ref, vwt_ref, g_ref,
                bias_ref, ovt_ref, o_ref,
                sc_ref, imp_ref, sel_ref, m_ref, acc_ref):
    qi = pl.program_id(1)
    q0 = qi * T
    n_cmp = kc_ref.shape[1]
    n_blk = ovt_ref.shape[0]
    n_sel = min(N_SEL, n_blk)
    w4 = HPM * T
    qt = qt_ref[0, 0]
    qcat = jnp.concatenate([qt[h * HEAD_DIM:(h + 1) * HEAD_DIM, :] for h in range(HPM)], axis=1)
    far_row = jnp.concatenate([jnp.full((1, T), far_ref[HPM + h], F32) for h in range(HPM)], axis=1)
    r, c = _causal_tile(w4)

    sc_ref[...] = _dot(kc_ref[0], qcat)
    delta = bias_ref[2 * T:2 * T + BAND_ROWS, :] - far_row

    @pl.when(qi == 0)
    def _():
        half = BAND_ROWS // 2
        sc_ref[0:half, :] = sc_ref[0:half, :] + delta[half:BAND_ROWS]

    @pl.when(qi > 0)
    def _():
        c0 = pl.multiple_of(qi * (T // CMP_STRIDE) - BAND_ROWS // 2, 8)
        sc_ref[pl.ds(c0, BAND_ROWS), :] = sc_ref[pl.ds(c0, BAND_ROWS), :] + delta

    ci = lax.broadcasted_iota(jnp.int32, (n_cmp, w4), 0)
    ti = q0 + jnp.bitwise_and(lax.broadcasted_iota(jnp.int32, (n_cmp, w4), 1), T - 1)
    valid_c = ti - CMP_STRIDE * ci - (CMP_LEN - 1) >= 0
    s = jnp.where(valid_c, sc_ref[...] + far_row, NEG)
    e = jnp.exp2(s - jnp.max(s, axis=0, keepdims=True))
    p = jnp.where(valid_c, e / jnp.sum(e, axis=0, keepdims=True), 0.0)
    o_cmp = _dot(vct_ref[0], p.astype(BF16))
    pc_sum = p[:, 0:T]
    for h in range(1, HPM):
        pc_sum = pc_sum + p[:, h * T:(h + 1) * T]

    hi, lo = _split2(pc_sum)
    imp = _dot(ovt_ref[...], hi) + _dot(ovt_ref[...], lo)
    ji = lax.broadcasted_iota(jnp.int32, (n_blk, T), 0)
    tq = q0 + lax.broadcasted_iota(jnp.int32, (n_blk, T), 1)
    forced = (ji == lax.shift_right_logical(tq, int(math.log2(SEL_LEN)))) | (ji == 0)
    imp = jnp.where(forced, FORCED_SCORE, imp)
    imp = jnp.where(ji * SEL_LEN <= tq, imp, -1.0)
    imp_ref[...] = imp

    def rank_body(jp, cnt):
        row = imp_ref[pl.ds(jp, 1), :]
        tie = jnp.where(row == imp, 1.0, 0.0) * jnp.where(ji > jp, 1.0, 0.0)
        return cnt + jnp.where(row > imp, 1.0, 0.0) + tie

    n_seen = jnp.minimum((qi + 1) * (T // SEL_LEN), n_blk)
    cnt = lax.fori_loop(0, n_seen, rank_body, jnp.zeros((n_blk, T), F32))
    sel = jnp.where(cnt < float(n_sel), 1.0, 0.0)
    sel_ref[...] = jnp.concatenate([sel] * HPM, axis=1)

    m_ref[...] = jnp.full(m_ref.shape, NEG, F32)
    acc_ref[...] = jnp.zeros(acc_ref.shape, F32)

    def bound_row(base):
        return jnp.concatenate([jnp.full((1, T), bnd_ref[base + 1 + h], F32) for h in range(HPM)], axis=1)

    def biased(s, mode, bound):
        if mode == "far":
            return s + (far_row if bound is None else far_row - bound)
        tile = bias_ref[T:2 * T, :] if mode == "near" else bias_ref[0:T, :]
        return s + tile if bound is None else s + tile - bound

    def update(state, ss, vts, bounded):
        alpha, ps = _softmax_probs(ss, m_ref, state, fixed=0.0 if bounded else None)
        _accumulate(acc_ref, slice(state * VROWS, (state + 1) * VROWS), alpha, vts, ps)

    def sel_mask(ki):
        per = T // SEL_LEN
        rows = [jnp.broadcast_to(sel_ref[pl.ds(ki * per + i, 1), :], (SEL_LEN, w4)) for i in range(per)]
        return jnp.concatenate(rows, axis=0) > 0.5

    def process_selected(tiles, bounded):
        bound = bound_row(BND_SEL) if bounded else None
        scores = [_dot(ks_ref[0, ki], qcat) for ki, _ in tiles]
        ss = []
        for (ki, mode), sc in zip(tiles, scores):
            mask = sel_mask(ki)
            if mode == "diag":
                mask = mask & (r <= c)
            ss.append(jnp.where(mask, biased(sc, mode, bound), NEG))
        update(0, ss, [_with_ones(vst_ref[0, ki]) for ki, _ in tiles], bounded)

    _run_bounded_or_online(bnd_ref[BND_SEL] > 0.5, qi, process_selected)

    def process_window(tiles, bounded):
        bound = bound_row(BND_WIN) if bounded else None
        scores = [_dot(kw_ref[0, ki], qcat) for ki, _, _ in tiles]
        ss = []
        for (ki, mode, mask), sc in zip(tiles, scores):
            s = biased(sc, mode, bound)
            ss.append(s if mask is None else jnp.where(mask, s, NEG))
        update(1, ss, [_with_ones(vwt_ref[0, ki]) for ki, _, _ in tiles], bounded)

    back = WINDOW // T

    def window_tiles(n):
        tiles = [(qi, "diag", r <= c)]
        for d in range(1, n + 1):
            mode = "near" if d == 1 else "far"
            tiles.append((qi - d, mode, (c < r) if d == back else None))
        return tiles

    win_bounded = bnd_ref[BND_WIN] > 0.5
    for n in range(back + 1):
        cond = (qi == n) if n < back else (qi >= back)

        @pl.when(cond & win_bounded)
        def _():
            process_window(window_tiles(n), True)

        @pl.when(cond & jnp.logical_not(win_bounded))
        def _():
            process_window(window_tiles(n), False)

    o_sel = acc_ref[0:HEAD_DIM, :] / acc_ref[HEAD_DIM:HEAD_DIM + 1, :]
    o_win = acc_ref[VROWS:VROWS + HEAD_DIM, :] / acc_ref[VROWS + HEAD_DIM:VROWS + HEAD_DIM + 1, :]
    gate = lambda br: jnp.concatenate(
        [g_ref[0, HPM + br * HPM + h:HPM + br * HPM + h + 1, :] for h in range(HPM)], axis=1)
    o = gate(0) * o_cmp + gate(1) * o_sel + gate(2) * o_win
    ot = jnp.concatenate([o[:, h * T:(h + 1) * T] for h in range(HPM)], axis=0)
    o_ref[0] = ot.T.astype(BF16)


def _nsa_attention(far, bnd, qt, kc, vct, ks4, vst, kw4, vwt, st, bias, ovt):
    bsz, nq = qt.shape[0], qt.shape[1]
    n_cmp = kc.shape[1]
    n_blk = ovt.shape[0]
    w4 = HPM * T
    return _attn_call(
        _nsa_kernel, "native_sparse_attention", bsz, nq,
        [far, bnd, qt, kc, vct, ks4, vst, kw4, vwt, st, bias, ovt],
        [_smem(), _smem(), _q_spec(), _seq_spec(kc.shape), _seq_spec(vct.shape),
         _seq_spec(ks4.shape), _seq_spec(vst.shape), _seq_spec(kw4.shape), _seq_spec(vwt.shape),
         pl.BlockSpec((1, 16, T), lambda b, i: (b, 0, i)), _resident(bias.shape), _whole(ovt.shape)],
        [pltpu.VMEM((n_cmp, w4), F32), pltpu.VMEM((n_blk, T), F32), pltpu.VMEM((n_blk, w4), F32),
         pltpu.VMEM((2, 1, w4), F32), pltpu.VMEM((2 * VROWS, w4), F32)])


def _post_kernel(x_ref, oa_ref, ob_ref, oc_ref, od_ref, wo_ref, gta_ref, g_ref, sc_ref, sh_ref,
                 gtm_ref, wup_ref, cw_ref, cb_ref, wdn_ref, y_ref, carry_ref):
    d_ff = wdn_ref.shape[0]
    mixed = (_dot(oa_ref[0], wo_ref[0:GW]) + _dot(ob_ref[0], wo_ref[GW:2 * GW])
             + _dot(oc_ref[0], wo_ref[2 * GW:3 * GW]) + _dot(od_ref[0], wo_ref[3 * GW:4 * GW]))
    x1 = x_ref[0] + gta_ref[0] * mixed
    ms = jnp.mean(x1 * x1, axis=-1, keepdims=True)
    h = x1 * lax.rsqrt(ms + EPS) * g_ref[...]
    hb = (h * (1.0 + sc_ref[0]) + sh_ref[0]).astype(BF16)

    @pl.when(pl.program_id(1) == 0)
    def _():
        carry_ref[...] = jnp.zeros(carry_ref.shape, F32)

    def up(ch):
        gate = _dot(hb, wup_ref[:, ch * FF_CHUNK:(ch + 1) * FF_CHUNK])
        val = _dot(hb, wup_ref[:, d_ff + ch * FF_CHUNK:d_ff + (ch + 1) * FF_CHUNK])
        return gate, val

    row = lax.broadcasted_iota(jnp.int32, (TM_POST, FF_CHUNK), 0)
    y = jnp.zeros((TM_POST, x1.shape[1]), F32)
    n_chunks = d_ff // FF_CHUNK
    ahead = up(0)
    for ch in range(n_chunks):
        cols = slice(ch * FF_CHUNK, (ch + 1) * FF_CHUNK)
        gate, val = ahead
        if ch + 1 < n_chunks:
            ahead = up(ch + 1)
        prev = carry_ref[:, cols]
        g1 = jnp.where(row == 0, prev[7:8], pltpu.roll(gate, 1, 0))
        g2 = jnp.where(row == 0, prev[6:7], jnp.where(row == 1, prev[7:8], pltpu.roll(gate, 2, 0)))
        carry_ref[:, cols] = gate[TM_POST - 8:TM_POST]
        conv = cw_ref[0:1, cols] * g2 + cw_ref[1:2, cols] * g1 + cw_ref[2:3, cols] * gate + cb_ref[:, cols]
        act = conv * _sigmoid(conv) * val
        y = y + _dot(act.astype(BF16), wdn_ref[cols, :])
    y_ref[0] = x1 + gtm_ref[0] * y


def _post(x, oa, ob, oc, od, wo, gta, g, sc, sh, gtm, wup, cw, cb, wdn):
    bsz, s, d = x.shape
    tok = lambda width: pl.BlockSpec((1, TM_POST, width), lambda b, i: (b, i, 0))
    vec = pl.BlockSpec((1, 1, d), lambda b, i: (b, 0, 0))
    const = lambda a: pl.BlockSpec(a.shape, lambda b, i: (0,) * a.ndim, pipeline_mode=pl.Buffered(1))
    return pl.pallas_call(
        _post_kernel,
        out_shape=jax.ShapeDtypeStruct((bsz, s, d), F32),
        grid=(bsz, s // TM_POST),
        in_specs=[tok(d), tok(GW), tok(GW), tok(GW), tok(GW), const(wo), vec, _whole((1, d)), vec, vec,
                  vec, const(wup), _whole(cw.shape), _whole(cb.shape), const(wdn)],
        out_specs=tok(d),
        scratch_shapes=[pltpu.VMEM((8, wdn.shape[0]), F32)],
        compiler_params=pltpu.CompilerParams(
            dimension_semantics=("parallel", "arbitrary"), vmem_limit_bytes=VMEM_LIMIT),
        name="out_proj_mlp",
    )(x, oa, ob, oc, od, wo, gta, g, sc, sh, gtm, wup, cw, cb, wdn)


def _repack_w_in(w):
    d = w.shape[0]
    off = {}
    pos = 0
    for name, size in (("a_q", GW), ("a_k", GW), ("a_v", GW), ("b_q", GW), ("b_k", GW), ("b_v", GW),
                       ("b_f", HPM), ("c_q", GW), ("c_k", GW), ("c_v", GW), ("d_q", GW),
                       ("d_kc", HEAD_DIM), ("d_vc", HEAD_DIM), ("d_ks", HEAD_DIM), ("d_vs", HEAD_DIM),
                       ("d_kw", HEAD_DIM), ("d_vw", HEAD_DIM), ("d_g", 3 * HPM)):
        off[name] = (pos, size)
        pos += size
    assert pos == w.shape[1]
    col = lambda n: w[:, off[n][0]:off[n][0] + off[n][1]]
    order = ["a_q", "a_k", "a_v", "b_q", "b_k", "b_v", "c_q", "c_k", "c_v", "d_q",
             "d_ks", "d_vs", "d_kw", "d_vw", "d_kc", "d_vc", "b_f", "d_g"]
    parts = [col(n) for n in order] + [jnp.zeros((d, 128 - HPM - 3 * HPM), w.dtype)]
    out = jnp.concatenate(parts, axis=1).astype(BF16)
    assert out.shape[1] == N_PROJ
    return out


def _score_bounds(rel_bias, gq_a, gk_a, gq_b, gk_b, gq_d, gk_s, gk_w):
    amax = lambda g: jnp.max(jnp.abs(g))
    tb = rel_bias * LOG2E
    bmax, bmin = jnp.max(tb, axis=0), jnp.min(tb, axis=0)
    span = jnp.max(bmax - bmin)
    qk = lambda n, gq, gk, scale: n * amax(gq) * amax(gk) * (scale * LOG2E * BOUND_MARGIN)
    ok = lambda b, sp: (2.0 * b + sp <= MAX_LOG2_SPAN).astype(F32).reshape(1)
    b_fox = qk(HEAD_DIM, gq_b, gk_b, HEAD_DIM ** -0.5)
    b_a = qk(DIFF_QK_DIM, gq_a, gk_a, DIFF_QK_DIM ** -0.5)
    b_s = qk(HEAD_DIM, gq_d, gk_s, HEAD_DIM ** -0.5)
    b_w = qk(HEAD_DIM, gq_d, gk_w, HEAD_DIM ** -0.5)
    vec = jnp.concatenate([
        ok(b_fox, 0.0), b_fox.reshape(1),
        ok(b_a, span), b_a + bmax[:HPM],
        ok(b_s, span), b_s + bmax[HPM:],
        ok(b_w, span), b_w + bmax[HPM:]])
    return jnp.pad(vec, (0, BND_SIZE - vec.shape[0]))


def kernel(x, c, rel_bias, ada_w, ada_b, norm_mix_g, norm_ffn_g, w_in, w_out, diff_qnorm_g, diff_knorm_g, diff_lambda, diff_subln_g, fox_qnorm_g, fox_knorm_g, fox_b_f, nsa_qnorm_g, nsa_knorm_g, nsa_pe, nsa_phi_w, ffn_w_up, ffn_conv_w, ffn_conv_b, ffn_w_down):
    bsz, s, d = x.shape
    depth = ada_w.shape[0]
    assert s % T == 0 and WINDOW % T == 0 and T % SEL_LEN == 0 and d == 4 * GW
    assert T & (T - 1) == 0 and BAND_ROWS // 2 >= (MAX_DISTANCE + CMP_LEN) // CMP_STRIDE
    nk = s // T
    n_chunk = s // CMP_STRIDE
    n_blk = s // SEL_LEN
    d_ff = ffn_w_down.shape[1]
    assert d_ff % FF_CHUNK == 0

    bd32 = jnp.asarray(_block_ones(GW, DIFF_QK_DIM), BF16)
    bd64 = jnp.asarray(_block_ones(GW, HEAD_DIM), BF16)
    ti = np.arange(T)
    tri_incl = jnp.asarray((ti[None, :] <= ti[:, None]).astype(np.float32), BF16)
    tri_later = jnp.asarray((ti[None, :] > ti[:, None]).astype(np.float32), BF16)
    c_start = np.arange(n_chunk) * CMP_STRIDE
    s_start = np.arange(n_blk) * SEL_LEN
    ov = ((c_start[None, :] < s_start[:, None] + SEL_LEN) & (s_start[:, None] < c_start[None, :] + CMP_LEN)
          & (np.arange(n_chunk)[None, :] < n_chunk - 1))
    ovt = jnp.asarray(ov.astype(np.float32), BF16)

    bias_a, bias_d = _bias_tiles(rel_bias)
    far = rel_bias[NUM_BUCKETS - 1] * LOG2E
    mod = _adaln(c, ada_w, ada_b)

    for l in range(depth):
        sh_a, sc_a, gt_a, sh_m, sc_m, gt_m = [m.reshape(bsz, 1, d) for m in jnp.split(mod[l], 6, axis=-1)]
        w_re = _repack_w_in(w_in[l])
        scale_a = DIFF_QK_DIM ** -0.5 * LOG2E
        scale = HEAD_DIM ** -0.5 * LOG2E
        ones = jnp.ones((HEAD_DIM,), F32)
        gains = jnp.stack([
            jnp.tile(diff_qnorm_g[l] * scale_a, GW // DIFF_QK_DIM), jnp.tile(diff_knorm_g[l], GW // DIFF_QK_DIM),
            jnp.tile(fox_qnorm_g[l] * scale, HPM), jnp.tile(fox_knorm_g[l], HPM),
            jnp.tile(nsa_qnorm_g[l] * scale, HPM),
            jnp.concatenate([nsa_knorm_g[l, 1], ones, nsa_knorm_g[l, 2], ones]),
            jnp.ones((GW,), F32), jnp.ones((GW,), F32)])
        (a_qt, a_k, a_vt, b_qt, b_k, b_vt, c_qt, c_k, c_vt, d_qt,
         d_ks, d_kw, d_vst, d_vwt, d_kc, d_vc, smalls) = _inproj(
            x, norm_mix_g[l].reshape(1, d), sc_a, sh_a, w_re, gains, bd32, bd64)

        bf_row = jnp.pad(fox_b_f[l], (0, 128 - HPM)).reshape(1, 128)
        ck, st = _scalars(smalls, bf_row, tri_incl)

        half = CMP_STRIDE * HEAD_DIM
        pe_rows = nsa_pe[l].reshape(4, half)
        w_pad = jnp.pad(nsa_phi_w[l], ((0, 0), (0, 0), (0, 128 - HEAD_DIM))).astype(BF16)
        gk_pad = jnp.pad(nsa_knorm_g[l, 0], (0, 128 - HEAD_DIM)).reshape(1, 128)
        kc, vct = _compress(d_kc.reshape(bsz, n_chunk, half), d_vc.reshape(bsz, n_chunk, half),
                            pe_rows, w_pad, gk_pad)

        bnd = _score_bounds(rel_bias, diff_qnorm_g[l], diff_knorm_g[l], fox_qnorm_g[l], fox_knorm_g[l],
                            nsa_qnorm_g[l], nsa_knorm_g[l, 1], nsa_knorm_g[l, 2])
        lam_init = 0.8 - 0.6 * math.exp(-0.3 * l)
        cst = jnp.full((1,), lam_init, F32)
        k4 = lambda a: a.reshape(bsz, nk, T, a.shape[-1])
        o_a = _diff_attention(far, cst, bnd, a_qt, k4(a_k), a_vt, bias_a, diff_lambda[l],
                              diff_subln_g[l].reshape(HEAD_DIM, 1))
        o_b = _fox_attention(bnd, b_qt, k4(b_k), b_vt, st, k4(ck))
        o_c = _sb_attention(c_qt, k4(c_k), c_vt, tri_later)
        o_d = _nsa_attention(far, bnd, d_qt, kc, vct, k4(d_ks), d_vst, k4(d_kw), d_vwt, st, bias_d, ovt)

        x = _post(x, o_a, o_b, o_c, o_d, w_out[l].astype(BF16), gt_a, norm_ffn_g[l].reshape(1, d),
                  sc_m, sh_m, gt_m, ffn_w_up[l].astype(BF16), ffn_conv_w[l].reshape(CONV_WIDTH, d_ff),
                  ffn_conv_b[l].reshape(1, d_ff), ffn_w_down[l].astype(BF16))
    return x
```

```python
import math

import numpy as np
import jax
import jax.numpy as jnp
from jax import lax
from jax.experimental import pallas as pl
from jax.experimental.pallas import tpu as pltpu

HEAD_DIM = 64
HPM = 4
GW = HPM * HEAD_DIM
DIFF_QK_DIM = HEAD_DIM // 2
NUM_BUCKETS = 32
MAX_EXACT = NUM_BUCKETS // 2
MAX_DISTANCE = 128
CMP_LEN = 32
CMP_STRIDE = 16
SEL_LEN = 64
N_SEL = 16
WINDOW = 512
FORCED_SCORE = 1.0e4
NEG = -1.0e30
EPS = 1e-6
CONV_WIDTH = 3
LOG2E = 1.4426950408889634

T = 256
TM = T
TM_POST = 256
FF_CHUNK = 256
ONES_ROWS = 16
VROWS = HEAD_DIM + ONES_ROWS
BAND_ROWS = 2 * T // CMP_STRIDE
VMEM_LIMIT = 56 * 1024 * 1024

BND_FOX, BND_DIFF, BND_SEL, BND_WIN, BND_SIZE = 0, 2, 7, 12, 24
BOUND_MARGIN = 1.02
MAX_LOG2_SPAN = 100.0

F32 = jnp.float32
BF16 = jnp.bfloat16


def _dot(a, b):
    return jnp.dot(a, b, preferred_element_type=F32)


def _split2(x):
    hi = x.astype(BF16)
    lo = (x - hi.astype(F32)).astype(BF16)
    return hi, lo


def _log_sigmoid(z):
    return jnp.minimum(z, 0.0) - jnp.log(1.0 + jnp.exp(-jnp.abs(z)))


def _sigmoid(z):
    return 1.0 / (1.0 + jnp.exp(-z))


def _whole(shape):
    nd = len(shape)
    return pl.BlockSpec(shape, lambda *_: (0,) * nd)


def _resident(shape):
    nd = len(shape)
    return pl.BlockSpec(shape, lambda *_: (0,) * nd, pipeline_mode=pl.Buffered(1))


def _smem():
    return pl.BlockSpec(memory_space=pltpu.SMEM)


def _t5_bucket_np(dist):
    n = np.maximum(dist, 0)
    ratio = math.log(MAX_DISTANCE / MAX_EXACT)
    out = None
    for dt in (np.float32, np.float64):
        large = MAX_EXACT + (np.log(np.maximum(n, 1).astype(dt) / dt(MAX_EXACT)) / dt(ratio)
                             * dt(NUM_BUCKETS - MAX_EXACT)).astype(np.int32)
        b = np.where(n < MAX_EXACT, n, np.minimum(large, NUM_BUCKETS - 1)).astype(np.int32)
        assert out is None or np.array_equal(out, b)
        out = b
    return out


def _bucket_tiles():
    r = np.arange(T)[:, None]
    c = np.arange(T)[None, :]
    diag = _t5_bucket_np(c - r)
    near = _t5_bucket_np(T + c - r)
    rb = np.arange(BAND_ROWS)[:, None]
    band = _t5_bucket_np(c - CMP_STRIDE * rb + (T - CMP_LEN + 1))
    return np.concatenate([diag, near, band], axis=0)


def _block_ones(n, group):
    i = np.arange(n)
    return (i[:, None] // group == i[None, :] // group).astype(np.float32)


def _bias_kernel(tbl_ref, bkt_ref, a_ref, d_ref):
    b = bkt_ref[...]
    for h in range(2 * HPM):
        acc = jnp.zeros(b.shape, F32)
        for k in range(NUM_BUCKETS):
            acc = jnp.where(b == k, tbl_ref[k, h] * LOG2E, acc)
        if h < HPM:
            a_ref[h] = acc[0:2 * T]
        else:
            d_ref[:, (h - HPM) * T:(h - HPM + 1) * T] = acc


def _bias_tiles(rel_bias):
    bkt = jnp.asarray(_bucket_tiles())
    rows = bkt.shape[0]
    return pl.pallas_call(
        _bias_kernel,
        out_shape=[jax.ShapeDtypeStruct((HPM, 2 * T, T), F32), jax.ShapeDtypeStruct((rows, HPM * T), F32)],
        in_specs=[_smem(), _whole(bkt.shape)],
        out_specs=[_whole((HPM, 2 * T, T)), _whole((rows, HPM * T))],
        name="bias_tiles",
    )(rel_bias, bkt)


def _adaln_kernel(c_ref, w_ref, b_ref, o_ref):
    c = c_ref[...]
    ca = c * _sigmoid(c)
    o_ref[0] = _dot(ca.astype(BF16), w_ref[0].astype(BF16)) + b_ref[0]


def _adaln(c, ada_w, ada_b):
    depth, d, n = ada_w.shape
    bsz = c.shape[0]
    rows = -(-bsz // 8) * 8
    cp = jnp.pad(c, ((0, rows - bsz), (0, 0)))
    tn = 1536
    out = pl.pallas_call(
        _adaln_kernel,
        out_shape=jax.ShapeDtypeStruct((depth, rows, n), F32),
        grid=(depth, n // tn),
        in_specs=[pl.BlockSpec((rows, d), lambda l, j: (0, 0)),
                  pl.BlockSpec((1, d, tn), lambda l, j: (l, 0, j)),
                  pl.BlockSpec((1, 1, tn), lambda l, j: (l, 0, j))],
        out_specs=pl.BlockSpec((1, rows, tn), lambda l, j: (l, 0, j)),
        name="adaln",
    )(cp, ada_w, ada_b.reshape(depth, 1, n))
    return out[:, :bsz]


N_PROJ = 11 * GW + 2 * 128


def _inproj_kernel(x_ref, g_ref, sc_ref, sh_ref, w_ref, gains_ref, bd32_ref, bd64_ref,
                   aq_ref, ak_ref, av_ref, bq_ref, bk_ref, bv_ref, cq_ref, ck_ref, cv_ref, dq_ref,
                   dks_ref, dkw_ref, dvs_ref, dvw_ref, dkc_ref, dvc_ref, sm_ref):
    x = x_ref[0]
    ms = jnp.mean(x * x, axis=-1, keepdims=True)
    h = x * lax.rsqrt(ms + EPS) * g_ref[...]
    h = h * (1.0 + sc_ref[0]) + sh_ref[0]
    hb = h.astype(BF16)

    def proj(group, width=GW):
        off = group * GW
        return _dot(hb, w_ref[:, off:off + width])

    def segnorm(y, bd_ref, inv_n, gain_row):
        hi, lo = _split2(y * y)
        ss = _dot(hi, bd_ref[...]) + _dot(lo, bd_ref[...])
        return y * lax.rsqrt(ss * inv_n + EPS) * gains_ref[gain_row:gain_row + 1, :]

    def put_t(ref, y):
        ref[0, 0] = y.T.astype(BF16)

    raw0 = proj(0)
    raw1 = proj(1)
    put_t(aq_ref, segnorm(raw0, bd32_ref, 1.0 / DIFF_QK_DIM, 0))
    put_t(av_ref, proj(2))
    raw3 = proj(3)
    ak_ref[0] = segnorm(raw1, bd32_ref, 1.0 / DIFF_QK_DIM, 1).astype(BF16)
    put_t(bv_ref, proj(5))
    raw4 = proj(4)
    put_t(bq_ref, segnorm(raw3, bd64_ref, 1.0 / HEAD_DIM, 2))
    put_t(cq_ref, proj(6) * (HEAD_DIM ** -0.5 * LOG2E))
    raw9 = proj(9)
    bk_ref[0] = segnorm(raw4, bd64_ref, 1.0 / HEAD_DIM, 3).astype(BF16)
    ck_ref[0] = proj(7).astype(BF16)
    y = proj(10)
    put_t(dq_ref, segnorm(raw9, bd64_ref, 1.0 / HEAD_DIM, 4))
    put_t(cv_ref, proj(8))
    cmp = _dot(hb, w_ref[:, 11 * GW:11 * GW + 128])
    dkc_ref[0] = cmp[:, 0:HEAD_DIM]
    dvc_ref[0] = cmp[:, HEAD_DIM:2 * HEAD_DIM]
    sm_ref[0] = _dot(hb, w_ref[:, 11 * GW + 128:11 * GW + 256])

    yn = segnorm(y, bd64_ref, 1.0 / HEAD_DIM, 5)
    dks_ref[0] = yn[:, 0:HEAD_DIM].astype(BF16)
    dkw_ref[0] = yn[:, 2 * HEAD_DIM:3 * HEAD_DIM].astype(BF16)
    yt = y.T
    dvs_ref[0, 0] = yt[HEAD_DIM:2 * HEAD_DIM].astype(BF16)
    dvw_ref[0, 0] = yt[3 * HEAD_DIM:4 * HEAD_DIM].astype(BF16)


def _inproj(x, g, sc, sh, w_re, gains, bd32, bd64):
    bsz, s, d = x.shape
    nk = s // T
    tok = lambda width, dt: jax.ShapeDtypeStruct((bsz, s, width), dt)
    tr = lambda rows: jax.ShapeDtypeStruct((bsz, nk, rows, T), BF16)
    tok_spec = lambda width: pl.BlockSpec((1, TM, width), lambda b, i: (b, i, 0))
    tr_spec = lambda rows: pl.BlockSpec((1, 1, rows, T), lambda b, i: (b, i, 0, 0))
    vec = pl.BlockSpec((1, 1, d), lambda b, i: (b, 0, 0))
    out_shape = [tr(GW), tok(GW, BF16), tr(GW)] * 3 + [tr(GW)] + [
        tok(HEAD_DIM, BF16), tok(HEAD_DIM, BF16), tr(HEAD_DIM), tr(HEAD_DIM),
        tok(HEAD_DIM, F32), tok(HEAD_DIM, F32), tok(128, F32)]
    out_specs = [tr_spec(GW), tok_spec(GW), tr_spec(GW)] * 3 + [tr_spec(GW)] + [
        tok_spec(HEAD_DIM), tok_spec(HEAD_DIM), tr_spec(HEAD_DIM), tr_spec(HEAD_DIM),
        tok_spec(HEAD_DIM), tok_spec(HEAD_DIM), tok_spec(128)]
    return pl.pallas_call(
        _inproj_kernel,
        out_shape=out_shape,
        grid=(bsz, s // TM),
        in_specs=[pl.BlockSpec((1, TM, d), lambda b, i: (b, i, 0)),
                  _whole((1, d)), vec, vec,
                  _whole(w_re.shape), _whole(gains.shape), _whole(bd32.shape), _whole(bd64.shape)],
        out_specs=out_specs,
        compiler_params=pltpu.CompilerParams(
            dimension_semantics=("parallel", "parallel"), vmem_limit_bytes=VMEM_LIMIT),
        name="inproj",
    )(x, g, sc, sh, w_re, gains, bd32, bd64)


def _scalars_kernel(sm_ref, bf_ref, tri_ref, ck_ref, st_ref):
    s = sm_ref.shape[1]
    carry = jnp.zeros((1, 128), F32)
    for blk in range(s // T):
        rows = slice(blk * T, (blk + 1) * T)
        lf = _log_sigmoid(sm_ref[0, rows, :] + bf_ref[...])
        h1 = lf.astype(BF16)
        r1 = lf - h1.astype(F32)
        h2 = r1.astype(BF16)
        h3 = (r1 - h2.astype(F32)).astype(BF16)
        tri = tri_ref[...]
        cb = _dot(tri, h1) + _dot(tri, h2) + _dot(tri, h3) + carry
        carry = cb[T - 1:T, :]
        cb2 = cb * LOG2E
        ck_ref[0, rows, :] = cb2
        col = lax.broadcasted_iota(jnp.int32, (T, 128), 1)
        comb = jnp.where(col < HPM, cb2, _sigmoid(sm_ref[0, rows, :]))
        st_ref[0, :, rows] = comb.T[0:16]


def _scalars(smalls, bf_row, tri):
    bsz, s, _ = smalls.shape
    return pl.pallas_call(
        _scalars_kernel,
        out_shape=[jax.ShapeDtypeStruct((bsz, s, 128), F32), jax.ShapeDtypeStruct((bsz, 16, s), F32)],
        grid=(bsz,),
        in_specs=[pl.BlockSpec((1, s, 128), lambda b: (b, 0, 0)), _whole((1, 128)), _whole((T, T))],
        out_specs=[pl.BlockSpec((1, s, 128), lambda b: (b, 0, 0)),
                   pl.BlockSpec((1, 16, s), lambda b: (b, 0, 0))],
        compiler_params=pltpu.CompilerParams(dimension_semantics=("parallel",)),
        name="token_scalars",
    )(smalls, bf_row, tri)


def _compress_kernel(xk_ref, xv_ref, pe_ref, w_ref, gk_ref, kc_ref, vct_ref):
    half = CMP_STRIDE * HEAD_DIM
    n = xk_ref.shape[1]

    def windows(x, pe_a, pe_b, wi):
        ya = _dot((x + pe_a).astype(BF16), w_ref[wi, 0:half, :])
        yb = _dot((x + pe_b).astype(BF16), w_ref[wi, half:2 * half, :])
        return ya + pltpu.roll(yb, n - 1, 0)

    kc = windows(xk_ref[0], pe_ref[0:1], pe_ref[1:2], 0)
    ss = jnp.sum(kc * kc, axis=-1, keepdims=True)
    kc = kc * lax.rsqrt(ss * (1.0 / HEAD_DIM) + EPS) * gk_ref[...]
    kc_ref[0] = kc[:, 0:HEAD_DIM].astype(BF16)
    vc = windows(xv_ref[0], pe_ref[2:3], pe_ref[3:4], 1)
    vct_ref[0] = vc.T[0:HEAD_DIM].astype(BF16)


def _compress(xk, xv, pe_rows, w_pad, gk_pad):
    bsz, n, width = xk.shape
    return pl.pallas_call(
        _compress_kernel,
        out_shape=[jax.ShapeDtypeStruct((bsz, n, HEAD_DIM), BF16),
                   jax.ShapeDtypeStruct((bsz, HEAD_DIM, n), BF16)],
        grid=(bsz,),
        in_specs=[pl.BlockSpec((1, n, width), lambda b: (b, 0, 0)),
                  pl.BlockSpec((1, n, width), lambda b: (b, 0, 0)),
                  _whole(pe_rows.shape), _whole(w_pad.shape), _whole(gk_pad.shape)],
        out_specs=[pl.BlockSpec((1, n, HEAD_DIM), lambda b: (b, 0, 0)),
                   pl.BlockSpec((1, HEAD_DIM, n), lambda b: (b, 0, 0))],
        compiler_params=pltpu.CompilerParams(dimension_semantics=("parallel",)),
        name="nsa_compress",
    )(xk, xv, pe_rows, w_pad, gk_pad)


def _head_rows(qt, h):
    return qt[h * HEAD_DIM:(h + 1) * HEAD_DIM, :]


def _head_cols(kt, h):
    return kt[:, h * HEAD_DIM:(h + 1) * HEAD_DIM]


def _split_maps(qh):
    qf = qh.astype(F32)
    row = lax.broadcasted_iota(jnp.int32, qf.shape, 0)
    first = jnp.where(row < DIFF_QK_DIM, qf, 0.0).astype(BF16)
    second = jnp.where(row >= DIFF_QK_DIM, qf, 0.0).astype(BF16)
    return jnp.concatenate([first, second], axis=1)


def _with_ones(vt):
    return jnp.concatenate([vt, jnp.ones((ONES_ROWS, vt.shape[1]), BF16)], axis=0)


def _softmax_probs(ss, m_ref, idx, shift=None, fixed=None):
    if fixed is not None:
        already = isinstance(fixed, float) and fixed == 0.0
        return None, [jnp.exp2(s if already else s - fixed).astype(BF16) for s in ss]
    m_old = m_ref[idx]
    mx = jnp.max(ss[0], axis=0, keepdims=True)
    for s in ss[1:]:
        mx = jnp.maximum(mx, jnp.max(s, axis=0, keepdims=True))
    if shift is not None:
        mx = mx + shift
    m_new = jnp.maximum(m_old, mx)
    m_ref[idx] = m_new
    alpha = jnp.exp2(m_old - m_new)
    sub = m_new if shift is None else m_new - shift
    return alpha, [jnp.exp2(s - sub).astype(BF16) for s in ss]


def _accumulate(acc_ref, rows, alpha, vts, ps):
    upd = _dot(vts[0], ps[0])
    for vt, p in zip(vts[1:], ps[1:]):
        upd = upd + _dot(vt, p)
    acc_ref[rows, :] = (acc_ref[rows, :] if alpha is None else alpha * acc_ref[rows, :]) + upd


def _normalized(acc_ref, idx):
    base = idx * VROWS
    return acc_ref[base:base + HEAD_DIM, :] / acc_ref[base + HEAD_DIM:base + HEAD_DIM + 1, :]


def _causal_tile(width=T):
    r = lax.broadcasted_iota(jnp.int32, (T, width), 0)
    c = lax.broadcasted_iota(jnp.int32, (T, width), 1)
    if width != T:
        c = jnp.bitwise_and(c, T - 1)
    return r, c


def _for_tile_groups(qi, process, paired=True):
    n_far = jnp.maximum(qi - 1, 0)
    if not paired:
        def single(j, carry):
            process([(j, "far")])
            return carry

        lax.fori_loop(0, n_far, single, 0)

        @pl.when(qi == 0)
        def _():
            process([(qi, "diag")])

        @pl.when(qi >= 1)
        def _():
            process([(qi - 1, "near"), (qi, "diag")])

        return

    n_pair = lax.shift_right_logical(n_far, 1)
    odd = jnp.bitwise_and(n_far, 1) == 1

    def pair(j, carry):
        process([(2 * j, "far"), (2 * j + 1, "far")])
        return carry

    lax.fori_loop(0, n_pair, pair, 0)

    @pl.when(qi == 0)
    def _():
        process([(qi, "diag")])

    @pl.when((qi >= 1) & jnp.logical_not(odd))
    def _():
        process([(qi - 1, "near"), (qi, "diag")])

    @pl.when((qi >= 1) & odd)
    def _():
        process([(qi - 2, "far"), (qi - 1, "near"), (qi, "diag")])


def _for_independent_tiles(qi, process, group=4):
    assert group == 4
    n_far = jnp.maximum(qi - 1, 0)
    n_group = lax.shift_right_logical(n_far, 2)
    rest = jnp.bitwise_and(n_far, 3)
    odd = jnp.bitwise_and(n_far, 1) == 1

    def many(j, carry):
        process([(4 * j + i, "far") for i in range(4)])
        return carry

    lax.fori_loop(0, n_group, many, 0)

    @pl.when(rest >= 2)
    def _():
        process([(4 * n_group, "far"), (4 * n_group + 1, "far")])

    @pl.when(qi == 0)
    def _():
        process([(qi, "diag")])

    @pl.when((qi >= 1) & jnp.logical_not(odd))
    def _():
        process([(qi - 1, "near"), (qi, "diag")])

    @pl.when((qi >= 1) & odd)
    def _():
        process([(qi - 2, "far"), (qi - 1, "near"), (qi, "diag")])


def _run_bounded_or_online(bounded_ok, qi, process):
    @pl.when(bounded_ok)
    def _():
        _for_independent_tiles(qi, lambda tiles: process(tiles, True))

    @pl.when(jnp.logical_not(bounded_ok))
    def _():
        _for_tile_groups(qi, lambda tiles: process(tiles, False))


def _attn_call(kernel, name, bsz, nq, in_arrays, in_specs, scratch, out_width=GW):
    return pl.pallas_call(
        kernel,
        out_shape=jax.ShapeDtypeStruct((bsz, nq * T, out_width), BF16),
        grid=(bsz, nq),
        in_specs=in_specs,
        out_specs=pl.BlockSpec((1, T, out_width), lambda b, i: (b, i, 0)),
        scratch_shapes=scratch,
        compiler_params=pltpu.CompilerParams(
            dimension_semantics=("parallel", "arbitrary"), vmem_limit_bytes=VMEM_LIMIT),
        name=name,
    )(*in_arrays)


def _q_spec(rows=GW):
    return pl.BlockSpec((1, 1, rows, T), lambda b, i: (b, i, 0, 0))


def _seq_spec(shape):
    nd = len(shape)
    return pl.BlockSpec((1,) + tuple(shape[1:]), lambda b, i: (b,) + (0,) * (nd - 1))


def _diff_kernel(far_ref, cst_ref, bnd_ref, qt_ref, k_ref, vt_ref, bias_ref, lam_ref, subg_ref, o_ref,
                 m_ref, acc_ref):
    qi = pl.program_id(1)
    qmaps = [_split_maps(_head_rows(qt_ref[0, 0], h)) for h in range(HPM)]
    m_ref[...] = jnp.full(m_ref.shape, NEG, F32)
    acc_ref[...] = jnp.zeros(acc_ref.shape, F32)
    r, c = _causal_tile()

    def process(tiles, bounded):
        scores = [[_dot(_head_cols(k_ref[0, ki], h), qmaps[h]) for ki, _ in tiles] for h in range(HPM)]
        probs = []
        for idx in range(2 * HPM):
            h = idx // 2
            bound = bnd_ref[BND_DIFF + 1 + h] if bounded else None
            ss = []
            for (ki, mode), sc in zip(tiles, scores[h]):
                s = sc[:, (idx % 2) * T:(idx % 2 + 1) * T]
                if mode == "far":
                    s = s + ((far_ref[h] - bound) if bounded else far_ref[h])
                elif mode == "near":
                    s = s + bias_ref[h, T:2 * T, :]
                else:
                    s = jnp.where(r <= c, s + bias_ref[h, 0:T, :], NEG)
                if bounded and mode != "far":
                    s = s - bound
                ss.append(s)
            probs.append(_softmax_probs(ss, m_ref, idx, fixed=0.0 if bounded else None))
        for idx in range(2 * HPM):
            h = idx // 2
            vts = [_with_ones(vt_ref[0, ki, h * HEAD_DIM:(h + 1) * HEAD_DIM, :]) for ki, _ in tiles]
            alpha, ps = probs[idx]
            _accumulate(acc_ref, slice(idx * VROWS, (idx + 1) * VROWS), alpha, vts, ps)

    _run_bounded_or_online(bnd_ref[BND_DIFF] > 0.5, qi, process)

    lam_init = cst_ref[0]
    lv = lam_ref[...]
    lam = (jnp.exp(jnp.sum(lv[0:1] * lv[1:2], axis=-1, keepdims=True))
           - jnp.exp(jnp.sum(lv[2:3] * lv[3:4], axis=-1, keepdims=True)) + lam_init)
    outs = []
    for h in range(HPM):
        o = _normalized(acc_ref, 2 * h) - lam * _normalized(acc_ref, 2 * h + 1)
        ms = jnp.mean(o * o, axis=0, keepdims=True)
        outs.append(o * lax.rsqrt(ms + EPS) * subg_ref[...] * (1.0 - lam_init))
    o_ref[0] = jnp.concatenate(outs, axis=0).T.astype(BF16)


def _diff_attention(far, cst, bnd, qt, k4, vt, bias, lam, subg):
    bsz, nq = qt.shape[0], qt.shape[1]
    return _attn_call(
        _diff_kernel, "diff_attention", bsz, nq,
        [far, cst, bnd, qt, k4, vt, bias, lam, subg],
        [_smem(), _smem(), _smem(), _q_spec(), _seq_spec(k4.shape), _seq_spec(vt.shape),
         _resident(bias.shape), _whole(lam.shape), _whole(subg.shape)],
        [pltpu.VMEM((2 * HPM, 1, T), F32), pltpu.VMEM((2 * HPM * VROWS, T), F32)])


def _fox_kernel(bnd_ref, qt_ref, k_ref, vt_ref, cq_ref, ck_ref, o_ref, m_ref, acc_ref):
    qi = pl.program_id(1)
    qh = [_head_rows(qt_ref[0, 0], h) for h in range(HPM)]
    m_ref[...] = jnp.full(m_ref.shape, NEG, F32)
    acc_ref[...] = jnp.zeros(acc_ref.shape, F32)
    r, c = _causal_tile()

    def process(tiles, bounded):
        scores = [[_dot(_head_cols(k_ref[0, ki], h), qh[h]) for ki, _ in tiles] for h in range(HPM)]
        probs = []
        for h in range(HPM):
            ss = []
            for (ki, mode), sc in zip(tiles, scores[h]):
                s = sc - ck_ref[0, ki][:, h:h + 1]
                if mode == "diag":
                    s = jnp.where(r <= c, s, NEG)
                ss.append(s)
            cq = cq_ref[0, h:h + 1, :]
            fixed = (bnd_ref[BND_FOX + 1] - cq) if bounded else None
            probs.append(_softmax_probs(ss, m_ref, h, shift=cq, fixed=fixed))
        for h in range(HPM):
            vts = [_with_ones(vt_ref[0, ki, h * HEAD_DIM:(h + 1) * HEAD_DIM, :]) for ki, _ in tiles]
            alpha, ps = probs[h]
            _accumulate(acc_ref, slice(h * VROWS, (h + 1) * VROWS), alpha, vts, ps)

    _run_bounded_or_online(bnd_ref[BND_FOX] > 0.5, qi, process)
    outs = [_normalized(acc_ref, h) for h in range(HPM)]
    o_ref[0] = jnp.concatenate(outs, axis=0).T.astype(BF16)


def _fox_attention(bnd, qt, k4, vt, st, ck4):
    bsz, nq = qt.shape[0], qt.shape[1]
    return _attn_call(
        _fox_kernel, "forgetting_attention", bsz, nq,
        [bnd, qt, k4, vt, st, ck4],
        [_smem(), _q_spec(), _seq_spec(k4.shape), _seq_spec(vt.shape),
         pl.BlockSpec((1, 16, T), lambda b, i: (b, 0, i)), _seq_spec(ck4.shape)],
        [pltpu.VMEM((HPM, 1, T), F32), pltpu.VMEM((HPM * VROWS, T), F32)])


def _sb_kernel(qt_ref, k_ref, vt_ref, tri_ref, o_ref, run_ref, acc_ref):
    qi = pl.program_id(1)
    qh = [_head_rows(qt_ref[0, 0], h) for h in range(HPM)]
    run_ref[...] = jnp.zeros(run_ref.shape, F32)
    acc_ref[...] = jnp.zeros(acc_ref.shape, F32)
    r, c = _causal_tile()

    def process(tiles):
        scores = [[_dot(_head_cols(k_ref[0, ki], h), qh[h]) for ki, _ in tiles] for h in range(HPM)]
        weights = []
        for h in range(HPM):
            run = run_ref[h]
            ws = []
            for (ki, diag), z in zip(tiles, scores[h]):
                log_beta = jnp.minimum(z, 0.0) - jnp.log2(1.0 + jnp.exp2(-jnp.abs(z)))
                log_keep = log_beta - z
                if diag:
                    log_keep = jnp.where(r < c, log_keep, 0.0)
                keep_b = log_keep.astype(BF16)
                tail = _dot(tri_ref[...], keep_b) + run
                a = jnp.exp2(log_beta + tail)
                if diag:
                    a = jnp.where(r < c, a, 0.0)
                ws.append(a.astype(BF16))
                run = tail[0:1, :] + keep_b[0:1, :].astype(F32)
            run_ref[h] = run
            weights.append(ws)
        for h in range(HPM):
            rows = slice(h * HEAD_DIM, (h + 1) * HEAD_DIM)
            upd = None
            for (ki, _), a in zip(tiles, weights[h]):
                term = _dot(vt_ref[0, ki, rows, :], a)
                upd = term if upd is None else upd + term
            acc_ref[rows, :] = acc_ref[rows, :] + upd

    rest = jnp.bitwise_and(qi, 3)
    for n in range(4):
        @pl.when(rest == n)
        def _():
            process([(qi, True)] + [(qi - d, False) for d in range(1, n + 1)])

    base = qi - 1 - rest

    def four(j, carry):
        process([(base - 4 * j - i, False) for i in range(4)])
        return carry

    lax.fori_loop(0, lax.shift_right_logical(qi, 2), four, 0)
    o_ref[0] = acc_ref[...].T.astype(BF16)


def _sb_attention(qt, k4, vt, tri):
    bsz, nq = qt.shape[0], qt.shape[1]
    return _attn_call(
        _sb_kernel, "stick_breaking_attention", bsz, nq,
        [qt, k4, vt, tri],
        [_q_spec(), _seq_spec(k4.shape), _seq_spec(vt.shape), _resident(tri.shape)],
        [pltpu.VMEM((HPM, 1, T), F32), pltpu.VMEM((GW, T), F32)])


def _nsa_kernel(far_ref, bnd_ref, qt_ref, kc_ref, vct_ref, ks_ref, vst_ref, kw_ref, vwt_ref, g_ref,
                bias_ref, ovt_ref, o_ref,
                sc_ref, imp_ref, sel_ref, m_ref, acc_ref):
    qi = pl.program_id(1)
    q0 = qi * T
    n_cmp = kc_ref.shape[1]
    n_blk = ovt_ref.shape[0]
    n_sel = min(N_SEL, n_blk)
    w4 = HPM * T
    qt = qt_ref[0, 0]
    qcat = jnp.concatenate([qt[h * HEAD_DIM:(h + 1) * HEAD_DIM, :] for h in range(HPM)], axis=1)
    far_row = jnp.concatenate([jnp.full((1, T), far_ref[HPM + h], F32) for h in range(HPM)], axis=1)
    r, c = _causal_tile(w4)

    sc_ref[...] = _dot(kc_ref[0], qcat)
    delta = bias_ref[2 * T:2 * T + BAND_ROWS, :] - far_row

    @pl.when(qi == 0)
    def _():
        half = BAND_ROWS // 2
        sc_ref[0:half, :] = sc_ref[0:half, :] + delta[half:BAND_ROWS]

    @pl.when(qi > 0)
    def _():
        c0 = pl.multiple_of(qi * (T // CMP_STRIDE) - BAND_ROWS // 2, 8)
        sc_ref[pl.ds(c0, BAND_ROWS), :] = sc_ref[pl.ds(c0, BAND_ROWS), :] + delta

    ci = lax.broadcasted_iota(jnp.int32, (n_cmp, w4), 0)
    ti = q0 + jnp.bitwise_and(lax.broadcasted_iota(jnp.int32, (n_cmp, w4), 1), T - 1)
    valid_c = ti - CMP_STRIDE * ci - (CMP_LEN - 1) >= 0
    s = jnp.where(valid_c, sc_ref[...] + far_row, NEG)
    e = jnp.exp2(s - jnp.max(s, axis=0, keepdims=True))
    p = jnp.where(valid_c, e / jnp.sum(e, axis=0, keepdims=True), 0.0)
    o_cmp = _dot(vct_ref[0], p.astype(BF16))
    pc_sum = p[:, 0:T]
    for h in range(1, HPM):
        pc_sum = pc_sum + p[:, h * T:(h + 1) * T]

    hi, lo = _split2(pc_sum)
    imp = _dot(ovt_ref[...], hi) + _dot(ovt_ref[...], lo)
    ji = lax.broadcasted_iota(jnp.int32, (n_blk, T), 0)
    tq = q0 + lax.broadcasted_iota(jnp.int32, (n_blk, T), 1)
    forced = (ji == lax.shift_right_logical(tq, int(math.log2(SEL_LEN)))) | (ji == 0)
    imp = jnp.where(forced, FORCED_SCORE, imp)
    imp = jnp.where(ji * SEL_LEN <= tq, imp, -1.0)
    imp_ref[...] = imp

    def rank_body(jp, cnt):
        row = imp_ref[pl.ds(jp, 1), :]
        tie = jnp.where(row == imp, 1.0, 0.0) * jnp.where(ji > jp, 1.0, 0.0)
        return cnt + jnp.where(row > imp, 1.0, 0.0) + tie

    n_seen = jnp.minimum((qi + 1) * (T // SEL_LEN), n_blk)
    cnt = lax.fori_loop(0, n_seen, rank_body, jnp.zeros((n_blk, T), F32))
    sel = jnp.where(cnt < float(n_sel), 1.0, 0.0)
    sel_ref[...] = jnp.concatenate([sel] * HPM, axis=1)

    m_ref[...] = jnp.full(m_ref.shape, NEG, F32)
    acc_ref[...] = jnp.zeros(acc_ref.shape, F32)

    def bound_row(base):
        return jnp.concatenate([jnp.full((1, T), bnd_ref[base + 1 + h], F32) for h in range(HPM)], axis=1)

    def biased(s, mode, bound):
        if mode == "far":
            return s + (far_row if bound is None else far_row - bound)
        tile = bias_ref[T:2 * T, :] if mode == "near" else bias_ref[0:T, :]
        return s + tile if bound is None else s + tile - bound

    def update(state, ss, vts, bounded):
        alpha, ps = _softmax_probs(ss, m_ref, state, fixed=0.0 if bounded else None)
        _accumulate(acc_ref, slice(state * VROWS, (state + 1) * VROWS), alpha, vts, ps)

    def sel_mask(ki):
        per = T // SEL_LEN
        rows = [jnp.broadcast_to(sel_ref[pl.ds(ki * per + i, 1), :], (SEL_LEN, w4)) for i in range(per)]
        return jnp.concatenate(rows, axis=0) > 0.5

    def process_selected(tiles, bounded):
        bound = bound_row(BND_SEL) if bounded else None
        scores = [_dot(ks_ref[0, ki], qcat) for ki, _ in tiles]
        ss = []
        for (ki, mode), sc in zip(tiles, scores):
            mask = sel_mask(ki)
            if mode == "diag":
                mask = mask & (r <= c)
            ss.append(jnp.where(mask, biased(sc, mode, bound), NEG))
        update(0, ss, [_with_ones(vst_ref[0, ki]) for ki, _ in tiles], bounded)

    _run_bounded_or_online(bnd_ref[BND_SEL] > 0.5, qi, process_selected)

    def process_window(tiles, bounded):
        bound = bound_row(BND_WIN) if bounded else None
        scores = [_dot(kw_ref[0, ki], qcat) for ki, _, _ in tiles]
        ss = []
        for (ki, mode, mask), sc in zip(tiles, scores):
            s = biased(sc, mode, bound)
            ss.append(s if mask is None else jnp.where(mask, s, NEG))
        update(1, ss, [_with_ones(vwt_ref[0, ki]) for ki, _, _ in tiles], bounded)

    back = WINDOW // T

    def window_tiles(n):
        tiles = [(qi, "diag", r <= c)]
        for d in range(1, n + 1):
            mode = "near" if d == 1 else "far"
            tiles.append((qi - d, mode, (c < r) if d == back else None))
        return tiles

    win_bounded = bnd_ref[BND_WIN] > 0.5
    for n in range(back + 1):
        cond = (qi == n) if n < back else (qi >= back)

        @pl.when(cond & win_bounded)
        def _():
            process_window(window_tiles(n), True)

        @pl.when(cond & jnp.logical_not(win_bounded))
        def _():
            process_window(window_tiles(n), False)

    o_sel = acc_ref[0:HEAD_DIM, :] / acc_ref[HEAD_DIM:HEAD_DIM + 1, :]
    o_win = acc_ref[VROWS:VROWS + HEAD_DIM, :] / acc_ref[VROWS + HEAD_DIM:VROWS + HEAD_DIM + 1, :]
    gate = lambda br: jnp.concatenate(
        [g_ref[0, HPM + br * HPM + h:HPM + br * HPM + h + 1, :] for h in range(HPM)], axis=1)
    o = gate(0) * o_cmp + gate(1) * o_sel + gate(2) * o_win
    ot = jnp.concatenate([o[:, h * T:(h + 1) * T] for h in range(HPM)], axis=0)
    o_ref[0] = ot.T.astype(BF16)


def _nsa_attention(far, bnd, qt, kc, vct, ks4, vst, kw4, vwt, st, bias, ovt):
    bsz, nq = qt.shape[0], qt.shape[1]
    n_cmp = kc.shape[1]
    n_blk = ovt.shape[0]
    w4 = HPM * T
    return _attn_call(
        _nsa_kernel, "native_sparse_attention", bsz, nq,
        [far, bnd, qt, kc, vct, ks4, vst, kw4, vwt, st, bias, ovt],
        [_smem(), _smem(), _q_spec(), _seq_spec(kc.shape), _seq_spec(vct.shape),
         _seq_spec(ks4.shape), _seq_spec(vst.shape), _seq_spec(kw4.shape), _seq_spec(vwt.shape),
         pl.BlockSpec((1, 16, T), lambda b, i: (b, 0, i)), _resident(bias.shape), _whole(ovt.shape)],
        [pltpu.VMEM((n_cmp, w4), F32), pltpu.VMEM((n_blk, T), F32), pltpu.VMEM((n_blk, w4), F32),
         pltpu.VMEM((2, 1, w4), F32), pltpu.VMEM((2 * VROWS, w4), F32)])


def _post_kernel(x_ref, oa_ref, ob_ref, oc_ref, od_ref, wo_ref, gta_ref, g_ref, sc_ref, sh_ref,
                 gtm_ref, wup_ref, cw_ref, cb_ref, wdn_ref, y_ref, carry_ref):
    d_ff = wdn_ref.shape[0]
    mixed = (_dot(oa_ref[0], wo_ref[0:GW]) + _dot(ob_ref[0], wo_ref[GW:2 * GW])
             + _dot(oc_ref[0], wo_ref[2 * GW:3 * GW]) + _dot(od_ref[0], wo_ref[3 * GW:4 * GW]))
    x1 = x_ref[0] + gta_ref[0] * mixed
    ms = jnp.mean(x1 * x1, axis=-1, keepdims=True)
    h = x1 * lax.rsqrt(ms + EPS) * g_ref[...]
    hb = (h * (1.0 + sc_ref[0]) + sh_ref[0]).astype(BF16)

    @pl.when(pl.program_id(1) == 0)
    def _():
        carry_ref[...] = jnp.zeros(carry_ref.shape, F32)

    def up(ch):
        gate = _dot(hb, wup_ref[:, ch * FF_CHUNK:(ch + 1) * FF_CHUNK])
        val = _dot(hb, wup_ref[:, d_ff + ch * FF_CHUNK:d_ff + (ch + 1) * FF_CHUNK])
        return gate, val

    row = lax.broadcasted_iota(jnp.int32, (TM_POST, FF_CHUNK), 0)
    y = jnp.zeros((TM_POST, x1.shape[1]), F32)
    n_chunks = d_ff // FF_CHUNK
    ahead = up(0)
    for ch in range(n_chunks):
        cols = slice(ch * FF_CHUNK, (ch + 1) * FF_CHUNK)
        gate, val = ahead
        if ch + 1 < n_chunks:
            ahead = up(ch + 1)
        prev = carry_ref[:, cols]
        g1 = jnp.where(row == 0, prev[7:8], pltpu.roll(gate, 1, 0))
        g2 = jnp.where(row == 0, prev[6:7], jnp.where(row == 1, prev[7:8], pltpu.roll(gate, 2, 0)))
        carry_ref[:, cols] = gate[TM_POST - 8:TM_POST]
        conv = cw_ref[0:1, cols] * g2 + cw_ref[1:2, cols] * g1 + cw_ref[2:3, cols] * gate + cb_ref[:, cols]
        act = conv * _sigmoid(conv) * val
        y = y + _dot(act.astype(BF16), wdn_ref[cols, :])
    y_ref[0] = x1 + gtm_ref[0] * y


def _post(x, oa, ob, oc, od, wo, gta, g, sc, sh, gtm, wup, cw, cb, wdn):
    bsz, s, d = x.shape
    tok = lambda width: pl.BlockSpec((1, TM_POST, width), lambda b, i: (b, i, 0))
    vec = pl.BlockSpec((1, 1, d), lambda b, i: (b, 0, 0))
    const = lambda a: pl.BlockSpec(a.shape, lambda b, i: (0,) * a.ndim, pipeline_mode=pl.Buffered(1))
    return pl.pallas_call(
        _post_kernel,
        out_shape=jax.ShapeDtypeStruct((bsz, s, d), F32),
        grid=(bsz, s // TM_POST),
        in_specs=[tok(d), tok(GW), tok(GW), tok(GW), tok(GW), const(wo), vec, _whole((1, d)), vec, vec,
                  vec, const(wup), _whole(cw.shape), _whole(cb.shape), const(wdn)],
        out_specs=tok(d),
        scratch_shapes=[pltpu.VMEM((8, wdn.shape[0]), F32)],
        compiler_params=pltpu.CompilerParams(
            dimension_semantics=("parallel", "arbitrary"), vmem_limit_bytes=VMEM_LIMIT),
        name="out_proj_mlp",
    )(x, oa, ob, oc, od, wo, gta, g, sc, sh, gtm, wup, cw, cb, wdn)


def _repack_w_in(w):
    d = w.shape[0]
    off = {}
    pos = 0
    for name, size in (("a_q", GW), ("a_k", GW), ("a_v", GW), ("b_q", GW), ("b_k", GW), ("b_v", GW),
                       ("b_f", HPM), ("c_q", GW), ("c_k", GW), ("c_v", GW), ("d_q", GW),
                       ("d_kc", HEAD_DIM), ("d_vc", HEAD_DIM), ("d_ks", HEAD_DIM), ("d_vs", HEAD_DIM),
                       ("d_kw", HEAD_DIM), ("d_vw", HEAD_DIM), ("d_g", 3 * HPM)):
        off[name] = (pos, size)
        pos += size
    assert pos == w.shape[1]
    col = lambda n: w[:, off[n][0]:off[n][0] + off[n][1]]
    order = ["a_q", "a_k", "a_v", "b_q", "b_k", "b_v", "c_q", "c_k", "c_v", "d_q",
             "d_ks", "d_vs", "d_kw", "d_vw", "d_kc", "d_vc", "b_f", "d_g"]
    parts = [col(n) for n in order] + [jnp.zeros((d, 128 - HPM - 3 * HPM), w.dtype)]
    out = jnp.concatenate(parts, axis=1).astype(BF16)
    assert out.shape[1] == N_PROJ
    return out


def _score_bounds(rel_bias, gq_a, gk_a, gq_b, gk_b, gq_d, gk_s, gk_w):
    amax = lambda g: jnp.max(jnp.abs(g))
    tb = rel_bias * LOG2E
    bmax, bmin = jnp.max(tb, axis=0), jnp.min(tb, axis=0)
    span = jnp.max(bmax - bmin)
    qk = lambda n, gq, gk, scale: n * amax(gq) * amax(gk) * (scale * LOG2E * BOUND_MARGIN)
    ok = lambda b, sp: (2.0 * b + sp <= MAX_LOG2_SPAN).astype(F32).reshape(1)
    b_fox = qk(HEAD_DIM, gq_b, gk_b, HEAD_DIM ** -0.5)
    b_a = qk(DIFF_QK_DIM, gq_a, gk_a, DIFF_QK_DIM ** -0.5)
    b_s = qk(HEAD_DIM, gq_d, gk_s, HEAD_DIM ** -0.5)
    b_w = qk(HEAD_DIM, gq_d, gk_w, HEAD_DIM ** -0.5)
    vec = jnp.concatenate([
        ok(b_fox, 0.0), b_fox.reshape(1),
        ok(b_a, span), b_a + bmax[:HPM],
        ok(b_s, span), b_s + bmax[HPM:],
        ok(b_w, span), b_w + bmax[HPM:]])
    return jnp.pad(vec, (0, BND_SIZE - vec.shape[0]))


def kernel(x, c, rel_bias, ada_w, ada_b, norm_mix_g, norm_ffn_g, w_in, w_out, diff_qnorm_g, diff_knorm_g, diff_lambda, diff_subln_g, fox_qnorm_g, fox_knorm_g, fox_b_f, nsa_qnorm_g, nsa_knorm_g, nsa_pe, nsa_phi_w, ffn_w_up, ffn_conv_w, ffn_conv_b, ffn_w_down):
    bsz, s, d = x.shape
    depth = ada_w.shape[0]
    assert s % T == 0 and WINDOW % T == 0 and T % SEL_LEN == 0 and d == 4 * GW
    assert T & (T - 1) == 0 and BAND_ROWS // 2 >= (MAX_DISTANCE + CMP_LEN) // CMP_STRIDE
    nk = s // T
    n_chunk = s // CMP_STRIDE
    n_blk = s // SEL_LEN
    d_ff = ffn_w_down.shape[1]
    assert d_ff % FF_CHUNK == 0

    bd32 = jnp.asarray(_block_ones(GW, DIFF_QK_DIM), BF16)
    bd64 = jnp.asarray(_block_ones(GW, HEAD_DIM), BF16)
    ti = np.arange(T)
    tri_incl = jnp.asarray((ti[None, :] <= ti[:, None]).astype(np.float32), BF16)
    tri_later = jnp.asarray((ti[None, :] > ti[:, None]).astype(np.float32), BF16)
    c_start = np.arange(n_chunk) * CMP_STRIDE
    s_start = np.arange(n_blk) * SEL_LEN
    ov = ((c_start[None, :] < s_start[:, None] + SEL_LEN) & (s_start[:, None] < c_start[None, :] + CMP_LEN)
          & (np.arange(n_chunk)[None, :] < n_chunk - 1))
    ovt = jnp.asarray(ov.astype(np.float32), BF16)

    bias_a, bias_d = _bias_tiles(rel_bias)
    far = rel_bias[NUM_BUCKETS - 1] * LOG2E
    mod = _adaln(c, ada_w, ada_b)

    for l in range(depth):
        sh_a, sc_a, gt_a, sh_m, sc_m, gt_m = [m.reshape(bsz, 1, d) for m in jnp.split(mod[l], 6, axis=-1)]
        w_re = _repack_w_in(w_in[l])
        scale_a = DIFF_QK_DIM ** -0.5 * LOG2E
        scale = HEAD_DIM ** -0.5 * LOG2E
        ones = jnp.ones((HEAD_DIM,), F32)
        gains = jnp.stack([
            jnp.tile(diff_qnorm_g[l] * scale_a, GW // DIFF_QK_DIM), jnp.tile(diff_knorm_g[l], GW // DIFF_QK_DIM),
            jnp.tile(fox_qnorm_g[l] * scale, HPM), jnp.tile(fox_knorm_g[l], HPM),
            jnp.tile(nsa_qnorm_g[l] * scale, HPM),
            jnp.concatenate([nsa_knorm_g[l, 1], ones, nsa_knorm_g[l, 2], ones]),
            jnp.ones((GW,), F32), jnp.ones((GW,), F32)])
        (a_qt, a_k, a_vt, b_qt, b_k, b_vt, c_qt, c_k, c_vt, d_qt,
         d_ks, d_kw, d_vst, d_vwt, d_kc, d_vc, smalls) = _inproj(
            x, norm_mix_g[l].reshape(1, d), sc_a, sh_a, w_re, gains, bd32, bd64)

        bf_row = jnp.pad(fox_b_f[l], (0, 128 - HPM)).reshape(1, 128)
        ck, st = _scalars(smalls, bf_row, tri_incl)

        half = CMP_STRIDE * HEAD_DIM
        pe_rows = nsa_pe[l].reshape(4, half)
        w_pad = jnp.pad(nsa_phi_w[l], ((0, 0), (0, 0), (0, 128 - HEAD_DIM))).astype(BF16)
        gk_pad = jnp.pad(nsa_knorm_g[l, 0], (0, 128 - HEAD_DIM)).reshape(1, 128)
        kc, vct = _compress(d_kc.reshape(bsz, n_chunk, half), d_vc.reshape(bsz, n_chunk, half),
                            pe_rows, w_pad, gk_pad)

        bnd = _score_bounds(rel_bias, diff_qnorm_g[l], diff_knorm_g[l], fox_qnorm_g[l], fox_knorm_g[l],
                            nsa_qnorm_g[l], nsa_knorm_g[l, 1], nsa_knorm_g[l, 2])
        lam_init = 0.8 - 0.6 * math.exp(-0.3 * l)
        cst = jnp.full((1,), lam_init, F32)
        k4 = lambda a: a.reshape(bsz, nk, T, a.shape[-1])
        o_a = _diff_attention(far, cst, bnd, a_qt, k4(a_k), a_vt, bias_a, diff_lambda[l],
                              diff_subln_g[l].reshape(HEAD_DIM, 1))
        o_b = _fox_attention(bnd, b_qt, k4(b_k), b_vt, st, k4(ck))
        o_c = _sb_attention(c_qt, k4(c_k), c_vt, tri_later)
        o_d = _nsa_attention(far, bnd, d_qt, kc, vct, k4(d_ks), d_vst, k4(d_kw), d_vwt, st, bias_d, ovt)

        x = _post(x, o_a, o_b, o_c, o_d, w_out[l].astype(BF16), gt_a, norm_ffn_g[l].reshape(1, d),
                  sc_m, sh_m, gt_m, ffn_w_up[l].astype(BF16), ffn_conv_w[l].reshape(CONV_WIDTH, d_ff),
                  ffn_conv_b[l].reshape(1, d_ff), ffn_w_down[l].astype(BF16))
    return x
```

```python
import math

import numpy as np
import jax
import jax.numpy as jnp
from jax import lax
from jax.experimental import pallas as pl
from jax.experimental.pallas import tpu as pltpu

HEAD_DIM = 64
HPM = 4
GW = HPM * HEAD_DIM
DIFF_QK_DIM = HEAD_DIM // 2
NUM_BUCKETS = 32
MAX_EXACT = NUM_BUCKETS // 2
MAX_DISTANCE = 128
CMP_LEN = 32
CMP_STRIDE = 16
SEL_LEN = 64
N_SEL = 16
WINDOW = 512
FORCED_SCORE = 1.0e4
NEG = -1.0e30
EPS = 1e-6
CONV_WIDTH = 3
LOG2E = 1.4426950408889634

T = 256
TM = T
TM_POST = 256
FF_CHUNK = 256
ONES_ROWS = 16
VROWS = HEAD_DIM + ONES_ROWS
BAND_ROWS = 2 * T // CMP_STRIDE
VMEM_LIMIT = 56 * 1024 * 1024

BND_FOX, BND_DIFF, BND_SEL, BND_WIN, BND_SIZE = 0, 2, 7, 12, 24
BOUND_MARGIN = 1.02
MAX_LOG2_SPAN = 100.0

F32 = jnp.float32
BF16 = jnp.bfloat16


def _dot(a, b):
    return jnp.dot(a, b, preferred_element_type=F32)


def _split2(x):
    hi = x.astype(BF16)
    lo = (x - hi.astype(F32)).astype(BF16)
    return hi, lo


def _log_sigmoid(z):
    return jnp.minimum(z, 0.0) - jnp.log(1.0 + jnp.exp(-jnp.abs(z)))


def _sigmoid(z):
    return 1.0 / (1.0 + jnp.exp(-z))


def _whole(shape):
    nd = len(shape)
    return pl.BlockSpec(shape, lambda *_: (0,) * nd)


def _resident(shape):
    nd = len(shape)
    return pl.BlockSpec(shape, lambda *_: (0,) * nd, pipeline_mode=pl.Buffered(1))


def _smem():
    return pl.BlockSpec(memory_space=pltpu.SMEM)


def _t5_bucket_np(dist):
    n = np.maximum(dist, 0)
    ratio = math.log(MAX_DISTANCE / MAX_EXACT)
    out = None
    for dt in (np.float32, np.float64):
        large = MAX_EXACT + (np.log(np.maximum(n, 1).astype(dt) / dt(MAX_EXACT)) / dt(ratio)
                             * dt(NUM_BUCKETS - MAX_EXACT)).astype(np.int32)
        b = np.where(n < MAX_EXACT, n, np.minimum(large, NUM_BUCKETS - 1)).astype(np.int32)
        assert out is None or np.array_equal(out, b)
        out = b
    return out


def _bucket_tiles():
    r = np.arange(T)[:, None]
    c = np.arange(T)[None, :]
    diag = _t5_bucket_np(c - r)
    near = _t5_bucket_np(T + c - r)
    rb = np.arange(BAND_ROWS)[:, None]
    band = _t5_bucket_np(c - CMP_STRIDE * rb + (T - CMP_LEN + 1))
    return np.concatenate([diag, near, band], axis=0)


def _block_ones(n, group):
    i = np.arange(n)
    return (i[:, None] // group == i[None, :] // group).astype(np.float32)


def _bias_kernel(tbl_ref, bkt_ref, a_ref, d_ref):
    b = bkt_ref[...]
    for h in range(2 * HPM):
        acc = jnp.zeros(b.shape, F32)
        for k in range(NUM_BUCKETS):
            acc = jnp.where(b == k, tbl_ref[k, h] * LOG2E, acc)
        if h < HPM:
            a_ref[h] = acc[0:2 * T]
        else:
            d_ref[:, (h - HPM) * T:(h - HPM + 1) * T] = acc


def _bias_tiles(rel_bias):
    bkt = jnp.asarray(_bucket_tiles())
    rows = bkt.shape[0]
    return pl.pallas_call(
        _bias_kernel,
        out_shape=[jax.ShapeDtypeStruct((HPM, 2 * T, T), F32), jax.ShapeDtypeStruct((rows, HPM * T), F32)],
        in_specs=[_smem(), _whole(bkt.shape)],
        out_specs=[_whole((HPM, 2 * T, T)), _whole((rows, HPM * T))],
        name="bias_tiles",
    )(rel_bias, bkt)


def _adaln_kernel(c_ref, w_ref, b_ref, o_ref):
    c = c_ref[...]
    ca = c * _sigmoid(c)
    o_ref[0] = _dot(ca.astype(BF16), w_ref[0].astype(BF16)) + b_ref[0]


def _adaln(c, ada_w, ada_b):
    depth, d, n = ada_w.shape
    bsz = c.shape[0]
    rows = -(-bsz // 8) * 8
    cp = jnp.pad(c, ((0, rows - bsz), (0, 0)))
    tn = 1536
    out = pl.pallas_call(
        _adaln_kernel,
        out_shape=jax.ShapeDtypeStruct((depth, rows, n), F32),
        grid=(depth, n // tn),
        in_specs=[pl.BlockSpec((rows, d), lambda l, j: (0, 0)),
                  pl.BlockSpec((1, d, tn), lambda l, j: (l, 0, j)),
                  pl.BlockSpec((1, 1, tn), lambda l, j: (l, 0, j))],
        out_specs=pl.BlockSpec((1, rows, tn), lambda l, j: (l, 0, j)),
        name="adaln",
    )(cp, ada_w, ada_b.reshape(depth, 1, n))
    return out[:, :bsz]


N_PROJ = 11 * GW + 2 * 128


def _inproj_kernel(x_ref, g_ref, sc_ref, sh_ref, w_ref, gains_ref, bd32_ref, bd64_ref,
                   aq_ref, ak_ref, av_ref, bq_ref, bk_ref, bv_ref, cq_ref, ck_ref, cv_ref, dq_ref,
                   dks_ref, dkw_ref, dvs_ref, dvw_ref, cmp_ref, sm_ref):
    x = x_ref[0]
    ms = jnp.mean(x * x, axis=-1, keepdims=True)
    h = x * lax.rsqrt(ms + EPS) * g_ref[...]
    h = h * (1.0 + sc_ref[0]) + sh_ref[0]
    hb = h.astype(BF16)

    def proj(group, width=GW):
        off = group * GW
        return _dot(hb, w_ref[0, :, off:off + width])

    def segnorm(y, bd_ref, inv_n, gain_row):
        hi, lo = _split2(y * y)
        ss = _dot(hi, bd_ref[...]) + _dot(lo, bd_ref[...])
        return y * lax.rsqrt(ss * inv_n + EPS) * gains_ref[gain_row:gain_row + 1, :]

    def put_t(ref, y):
        ref[0, 0] = y.T.astype(BF16)

    raw0 = proj(0)
    raw1 = proj(1)
    put_t(aq_ref, segnorm(raw0, bd32_ref, 1.0 / DIFF_QK_DIM, 0))
    put_t(av_ref, proj(2))
    raw3 = proj(3)
    ak_ref[0] = segnorm(raw1, bd32_ref, 1.0 / DIFF_QK_DIM, 1).astype(BF16)
    put_t(bv_ref, proj(5))
    raw4 = proj(4)
    put_t(bq_ref, segnorm(raw3, bd64_ref, 1.0 / HEAD_DIM, 2))
    put_t(cq_ref, proj(6) * (HEAD_DIM ** -0.5 * LOG2E))
    raw9 = proj(9)
    bk_ref[0] = segnorm(raw4, bd64_ref, 1.0 / HEAD_DIM, 3).astype(BF16)
    ck_ref[0] = proj(7).astype(BF16)
    y = proj(10)
    put_t(dq_ref, segnorm(raw9, bd64_ref, 1.0 / HEAD_DIM, 4))
    put_t(cv_ref, proj(8))
    cmp_ref[0] = proj(11, 128)
    sm_ref[0] = _dot(hb, w_ref[0, :, 11 * GW + 128:11 * GW + 256])

    yn = segnorm(y, bd64_ref, 1.0 / HEAD_DIM, 5)
    dks_ref[0] = yn[:, 0:HEAD_DIM].astype(BF16)
    dkw_ref[0] = yn[:, 2 * HEAD_DIM:3 * HEAD_DIM].astype(BF16)
    yt = y.T
    dvs_ref[0, 0] = yt[HEAD_DIM:2 * HEAD_DIM].astype(BF16)
    dvw_ref[0, 0] = yt[3 * HEAD_DIM:4 * HEAD_DIM].astype(BF16)


def _inproj(layer, x, g, sc, sh, w_re, gains, bd32, bd64):
    bsz, s, d = x.shape
    nk = s // T
    tok = lambda width, dt: jax.ShapeDtypeStruct((bsz, s, width), dt)
    tr = lambda rows: jax.ShapeDtypeStruct((bsz, nk, rows, T), BF16)
    tok_spec = lambda width: pl.BlockSpec((1, TM, width), lambda b, i: (b, i, 0))
    tr_spec = lambda rows: pl.BlockSpec((1, 1, rows, T), lambda b, i: (b, i, 0, 0))
    vec = pl.BlockSpec((1, 1, d), lambda b, i: (b, 0, 0))
    out_shape = [tr(GW), tok(GW, BF16), tr(GW)] * 3 + [tr(GW)] + [
        tok(HEAD_DIM, BF16), tok(HEAD_DIM, BF16), tr(HEAD_DIM), tr(HEAD_DIM),
        tok(128, F32), tok(128, F32)]
    out_specs = [tr_spec(GW), tok_spec(GW), tr_spec(GW)] * 3 + [tr_spec(GW)] + [
        tok_spec(HEAD_DIM), tok_spec(HEAD_DIM), tr_spec(HEAD_DIM), tr_spec(HEAD_DIM),
        tok_spec(128), tok_spec(128)]
    return pl.pallas_call(
        _inproj_kernel,
        out_shape=out_shape,
        grid=(bsz, s // TM),
        in_specs=[pl.BlockSpec((1, TM, d), lambda b, i: (b, i, 0)),
                  _whole((1, d)), vec, vec,
                  pl.BlockSpec((1,) + w_re.shape[1:], lambda b, i: (layer, 0, 0)),
                  _whole(gains.shape), _whole(bd32.shape), _whole(bd64.shape)],
        out_specs=out_specs,
        compiler_params=pltpu.CompilerParams(
            dimension_semantics=("parallel", "parallel"), vmem_limit_bytes=VMEM_LIMIT),
        name="inproj",
    )(x, g, sc, sh, w_re, gains, bd32, bd64)


def _scalars_kernel(sm_ref, bf_ref, tri_ref, ck_ref, st_ref):
    s = sm_ref.shape[1]
    carry = jnp.zeros((1, 128), F32)
    for blk in range(s // T):
        rows = slice(blk * T, (blk + 1) * T)
        lf = _log_sigmoid(sm_ref[0, rows, :] + bf_ref[...])
        h1 = lf.astype(BF16)
        r1 = lf - h1.astype(F32)
        h2 = r1.astype(BF16)
        h3 = (r1 - h2.astype(F32)).astype(BF16)
        tri = tri_ref[...]
        cb = _dot(tri, h1) + _dot(tri, h2) + _dot(tri, h3) + carry
        carry = cb[T - 1:T, :]
        cb2 = cb * LOG2E
        ck_ref[0, rows, :] = cb2
        col = lax.broadcasted_iota(jnp.int32, (T, 128), 1)
        comb = jnp.where(col < HPM, cb2, _sigmoid(sm_ref[0, rows, :]))
        st_ref[0, :, rows] = comb.T[0:16]


def _scalars(smalls, bf_row, tri):
    bsz, s, _ = smalls.shape
    return pl.pallas_call(
        _scalars_kernel,
        out_shape=[jax.ShapeDtypeStruct((bsz, s, 128), F32), jax.ShapeDtypeStruct((bsz, 16, s), F32)],
        grid=(bsz,),
        in_specs=[pl.BlockSpec((1, s, 128), lambda b: (b, 0, 0)), _whole((1, 128)), _whole((T, T))],
        out_specs=[pl.BlockSpec((1, s, 128), lambda b: (b, 0, 0)),
                   pl.BlockSpec((1, 16, s), lambda b: (b, 0, 0))],
        compiler_params=pltpu.CompilerParams(dimension_semantics=("parallel",)),
        name="token_scalars",
    )(smalls, bf_row, tri)


def _compress_kernel(x_ref, pe_ref, w_ref, gk_ref, kc_ref, vct_ref):
    n = x_ref.shape[1] // CMP_STRIDE
    chunks = jnp.concatenate(
        [x_ref[0, pl.ds(i, n, stride=CMP_STRIDE), :] for i in range(CMP_STRIDE)], axis=1)
    ya = _dot((chunks + pe_ref[0:1]).astype(BF16), w_ref[0])
    yb = _dot((chunks + pe_ref[1:2]).astype(BF16), w_ref[1])
    y = ya + pltpu.roll(yb, n - 1, 0)
    lane = lax.broadcasted_iota(jnp.int32, y.shape, 1)
    ss = jnp.sum(jnp.where(lane < HEAD_DIM, y * y, 0.0), axis=-1, keepdims=True)
    kc = y * lax.rsqrt(ss * (1.0 / HEAD_DIM) + EPS) * gk_ref[...]
    kc_ref[0] = kc[:, 0:HEAD_DIM].astype(BF16)
    vct_ref[0] = y.T[HEAD_DIM:2 * HEAD_DIM].astype(BF16)


def _compress(cmp, pe_rows, w_blk, gk_pad):
    bsz, s, width = cmp.shape
    n = s // CMP_STRIDE
    return pl.pallas_call(
        _compress_kernel,
        out_shape=[jax.ShapeDtypeStruct((bsz, n, HEAD_DIM), BF16),
                   jax.ShapeDtypeStruct((bsz, HEAD_DIM, n), BF16)],
        grid=(bsz,),
        in_specs=[pl.BlockSpec((1, s, width), lambda b: (b, 0, 0)),
                  _whole(pe_rows.shape), _whole(w_blk.shape), _whole(gk_pad.shape)],
        out_specs=[pl.BlockSpec((1, n, HEAD_DIM), lambda b: (b, 0, 0)),
                   pl.BlockSpec((1, HEAD_DIM, n), lambda b: (b, 0, 0))],
        compiler_params=pltpu.CompilerParams(dimension_semantics=("parallel",)),
        name="nsa_compress",
    )(cmp, pe_rows, w_blk, gk_pad)


def _head_rows(qt, h):
    return qt[h * HEAD_DIM:(h + 1) * HEAD_DIM, :]


def _head_cols(kt, h):
    return kt[:, h * HEAD_DIM:(h + 1) * HEAD_DIM]


def _split_maps(qh):
    qf = qh.astype(F32)
    row = lax.broadcasted_iota(jnp.int32, qf.shape, 0)
    first = jnp.where(row < DIFF_QK_DIM, qf, 0.0).astype(BF16)
    second = jnp.where(row >= DIFF_QK_DIM, qf, 0.0).astype(BF16)
    return jnp.concatenate([first, second], axis=1)


def _with_ones(vt):
    return jnp.concatenate([vt, jnp.ones((ONES_ROWS, vt.shape[1]), BF16)], axis=0)


def _softmax_probs(ss, m_ref, idx, shift=None, fixed=None):
    if fixed is not None:
        already = isinstance(fixed, float) and fixed == 0.0
        return None, [jnp.exp2(s if already else s - fixed).astype(BF16) for s in ss]
    m_old = m_ref[idx]
    mx = jnp.max(ss[0], axis=0, keepdims=True)
    for s in ss[1:]:
        mx = jnp.maximum(mx, jnp.max(s, axis=0, keepdims=True))
    if shift is not None:
        mx = mx + shift
    m_new = jnp.maximum(m_old, mx)
    m_ref[idx] = m_new
    alpha = jnp.exp2(m_old - m_new)
    sub = m_new if shift is None else m_new - shift
    return alpha, [jnp.exp2(s - sub).astype(BF16) for s in ss]


def _accumulate(acc_ref, rows, alpha, vts, ps):
    upd = _dot(vts[0], ps[0])
    for vt, p in zip(vts[1:], ps[1:]):
        upd = upd + _dot(vt, p)
    acc_ref[rows, :] = (acc_ref[rows, :] if alpha is None else alpha * acc_ref[rows, :]) + upd


def _normalized(acc_ref, idx):
    base = idx * VROWS
    return acc_ref[base:base + HEAD_DIM, :] / acc_ref[base + HEAD_DIM:base + HEAD_DIM + 1, :]


def _causal_tile(width=T):
    r = lax.broadcasted_iota(jnp.int32, (T, width), 0)
    c = lax.broadcasted_iota(jnp.int32, (T, width), 1)
    if width != T:
        c = jnp.bitwise_and(c, T - 1)
    return r, c


def _for_tile_groups(qi, process, paired=True):
    n_far = jnp.maximum(qi - 1, 0)
    if not paired:
        def single(j, carry):
            process([(j, "far")])
            return carry

        lax.fori_loop(0, n_far, single, 0)

        @pl.when(qi == 0)
        def _():
            process([(qi, "diag")])

        @pl.when(qi >= 1)
        def _():
            process([(qi - 1, "near"), (qi, "diag")])

        return

    n_pair = lax.shift_right_logical(n_far, 1)
    odd = jnp.bitwise_and(n_far, 1) == 1

    def pair(j, carry):
        process([(2 * j, "far"), (2 * j + 1, "far")])
        return carry

    lax.fori_loop(0, n_pair, pair, 0)

    @pl.when(qi == 0)
    def _():
        process([(qi, "diag")])

    @pl.when((qi >= 1) & jnp.logical_not(odd))
    def _():
        process([(qi - 1, "near"), (qi, "diag")])

    @pl.when((qi >= 1) & odd)
    def _():
        process([(qi - 2, "far"), (qi - 1, "near"), (qi, "diag")])


def _for_independent_tiles(qi, process, group=4):
    assert group == 4
    n_far = jnp.maximum(qi - 1, 0)
    n_group = lax.shift_right_logical(n_far, 2)
    rest = jnp.bitwise_and(n_far, 3)
    odd = jnp.bitwise_and(n_far, 1) == 1

    def many(j, carry):
        process([(4 * j + i, "far") for i in range(4)])
        return carry

    lax.fori_loop(0, n_group, many, 0)

    @pl.when(rest >= 2)
    def _():
        process([(4 * n_group, "far"), (4 * n_group + 1, "far")])

    @pl.when(qi == 0)
    def _():
        process([(qi, "diag")])

    @pl.when((qi >= 1) & jnp.logical_not(odd))
    def _():
        process([(qi - 1, "near"), (qi, "diag")])

    @pl.when((qi >= 1) & odd)
    def _():
        process([(qi - 2, "far"), (qi - 1, "near"), (qi, "diag")])


def _run_bounded_or_online(bounded_ok, qi, process):
    @pl.when(bounded_ok)
    def _():
        _for_independent_tiles(qi, lambda tiles: process(tiles, True))

    @pl.when(jnp.logical_not(bounded_ok))
    def _():
        _for_tile_groups(qi, lambda tiles: process(tiles, False))


def _attn_call(kernel, name, bsz, nq, in_arrays, in_specs, scratch, out_width=GW):
    return pl.pallas_call(
        kernel,
        out_shape=jax.ShapeDtypeStruct((bsz, nq * T, out_width), BF16),
        grid=(bsz, nq),
        in_specs=in_specs,
        out_specs=pl.BlockSpec((1, T, out_width), lambda b, i: (b, i, 0)),
        scratch_shapes=scratch,
        compiler_params=pltpu.CompilerParams(
            dimension_semantics=("parallel", "arbitrary"), vmem_limit_bytes=VMEM_LIMIT),
        name=name,
    )(*in_arrays)


def _q_spec(rows=GW):
    return pl.BlockSpec((1, 1, rows, T), lambda b, i: (b, i, 0, 0))


def _seq_spec(shape):
    nd = len(shape)
    return pl.BlockSpec((1,) + tuple(shape[1:]), lambda b, i: (b,) + (0,) * (nd - 1))


def _diff_kernel(far_ref, cst_ref, bnd_ref, qt_ref, k_ref, vt_ref, bias_ref, lam_ref, subg_ref, o_ref,
                 m_ref, acc_ref):
    qi = pl.program_id(1)
    qmaps = [_split_maps(_head_rows(qt_ref[0, 0], h)) for h in range(HPM)]
    m_ref[...] = jnp.full(m_ref.shape, NEG, F32)
    acc_ref[...] = jnp.zeros(acc_ref.shape, F32)
    r, c = _causal_tile()

    def process(tiles, bounded):
        scores = [[_dot(_head_cols(k_ref[0, ki], h), qmaps[h]) for ki, _ in tiles] for h in range(HPM)]
        probs = []
        for idx in range(2 * HPM):
            h = idx // 2
            bound = bnd_ref[BND_DIFF + 1 + h] if bounded else None
            ss = []
            for (ki, mode), sc in zip(tiles, scores[h]):
                s = sc[:, (idx % 2) * T:(idx % 2 + 1) * T]
                if mode == "far":
                    s = s + ((far_ref[h] - bound) if bounded else far_ref[h])
                elif mode == "near":
                    s = s + bias_ref[h, T:2 * T, :]
                else:
                    s = jnp.where(r <= c, s + bias_ref[h, 0:T, :], NEG)
                if bounded and mode != "far":
                    s = s - bound
                ss.append(s)
            probs.append(_softmax_probs(ss, m_ref, idx, fixed=0.0 if bounded else None))
        for idx in range(2 * HPM):
            h = idx // 2
            vts = [_with_ones(vt_ref[0, ki, h * HEAD_DIM:(h + 1) * HEAD_DIM, :]) for ki, _ in tiles]
            alpha, ps = probs[idx]
            _accumulate(acc_ref, slice(idx * VROWS, (idx + 1) * VROWS), alpha, vts, ps)

    _run_bounded_or_online(bnd_ref[BND_DIFF] > 0.5, qi, process)

    lam_init = cst_ref[0]
    lv = lam_ref[...]
    lam = (jnp.exp(jnp.sum(lv[0:1] * lv[1:2], axis=-1, keepdims=True))
           - jnp.exp(jnp.sum(lv[2:3] * lv[3:4], axis=-1, keepdims=True)) + lam_init)
    outs = []
    for h in range(HPM):
        o = _normalized(acc_ref, 2 * h) - lam * _normalized(acc_ref, 2 * h + 1)
        ms = jnp.mean(o * o, axis=0, keepdims=True)
        outs.append(o * lax.rsqrt(ms + EPS) * subg_ref[...] * (1.0 - lam_init))
    o_ref[0] = jnp.concatenate(outs, axis=0).T.astype(BF16)


def _diff_attention(far, cst, bnd, qt, k4, vt, bias, lam, subg):
    bsz, nq = qt.shape[0], qt.shape[1]
    return _attn_call(
        _diff_kernel, "diff_attention", bsz, nq,
        [far, cst, bnd, qt, k4, vt, bias, lam, subg],
        [_smem(), _smem(), _smem(), _q_spec(), _seq_spec(k4.shape), _seq_spec(vt.shape),
         _resident(bias.shape), _whole(lam.shape), _whole(subg.shape)],
        [pltpu.VMEM((2 * HPM, 1, T), F32), pltpu.VMEM((2 * HPM * VROWS, T), F32)])


def _fox_kernel(bnd_ref, qt_ref, k_ref, vt_ref, cq_ref, ck_ref, o_ref, m_ref, acc_ref):
    qi = pl.program_id(1)
    qh = [_head_rows(qt_ref[0, 0], h) for h in range(HPM)]
    m_ref[...] = jnp.full(m_ref.shape, NEG, F32)
    acc_ref[...] = jnp.zeros(acc_ref.shape, F32)
    r, c = _causal_tile()

    def process(tiles, bounded):
        scores = [[_dot(_head_cols(k_ref[0, ki], h), qh[h]) for ki, _ in tiles] for h in range(HPM)]
        probs = []
        for h in range(HPM):
            ss = []
            for (ki, mode), sc in zip(tiles, scores[h]):
                s = sc - ck_ref[0, ki][:, h:h + 1]
                if mode == "diag":
                    s = jnp.where(r <= c, s, NEG)
                ss.append(s)
            cq = cq_ref[0, h:h + 1, :]
            fixed = (bnd_ref[BND_FOX + 1] - cq) if bounded else None
            probs.append(_softmax_probs(ss, m_ref, h, shift=cq, fixed=fixed))
        for h in range(HPM):
            vts = [_with_ones(vt_ref[0, ki, h * HEAD_DIM:(h + 1) * HEAD_DIM, :]) for ki, _ in tiles]
            alpha, ps = probs[h]
            _accumulate(acc_ref, slice(h * VROWS, (h + 1) * VROWS), alpha, vts, ps)

    _run_bounded_or_online(bnd_ref[BND_FOX] > 0.5, qi, process)
    outs = [_normalized(acc_ref, h) for h in range(HPM)]
    o_ref[0] = jnp.concatenate(outs, axis=0).T.astype(BF16)


def _fox_attention(bnd, qt, k4, vt, st, ck4):
    bsz, nq = qt.shape[0], qt.shape[1]
    return _attn_call(
        _fox_kernel, "forgetting_attention", bsz, nq,
        [bnd, qt, k4, vt, st, ck4],
        [_smem(), _q_spec(), _seq_spec(k4.shape), _seq_spec(vt.shape),
         pl.BlockSpec((1, 16, T), lambda b, i: (b, 0, i)), _seq_spec(ck4.shape)],
        [pltpu.VMEM((HPM, 1, T), F32), pltpu.VMEM((HPM * VROWS, T), F32)])


def _sb_kernel(qt_ref, k_ref, vt_ref, tri_ref, o_ref, run_ref, acc_ref):
    qi = pl.program_id(1)
    qh = [_head_rows(qt_ref[0, 0], h) for h in range(HPM)]
    run_ref[...] = jnp.zeros(run_ref.shape, F32)
    acc_ref[...] = jnp.zeros(acc_ref.shape, F32)
    r, c = _causal_tile()

    def process(tiles):
        scores = [[_dot(_head_cols(k_ref[0, ki], h), qh[h]) for ki, _ in tiles] for h in range(HPM)]
        weights = []
        for h in range(HPM):
            run = run_ref[h]
            ws = []
            for (ki, diag), z in zip(tiles, scores[h]):
                log_beta = jnp.minimum(z, 0.0) - jnp.log2(1.0 + jnp.exp2(-jnp.abs(z)))
                log_keep = log_beta - z
                if diag:
                    log_keep = jnp.where(r < c, log_keep, 0.0)
                keep_b = log_keep.astype(BF16)
                tail = _dot(tri_ref[...], keep_b) + run
                a = jnp.exp2(log_beta + tail)
                if diag:
                    a = jnp.where(r < c, a, 0.0)
                ws.append(a.astype(BF16))
                run = tail[0:1, :] + keep_b[0:1, :].astype(F32)
            run_ref[h] = run
            weights.append(ws)
        for h in range(HPM):
            rows = slice(h * HEAD_DIM, (h + 1) * HEAD_DIM)
            upd = None
            for (ki, _), a in zip(tiles, weights[h]):
                term = _dot(vt_ref[0, ki, rows, :], a)
                upd = term if upd is None else upd + term
            acc_ref[rows, :] = acc_ref[rows, :] + upd

    rest = jnp.bitwise_and(qi, 3)
    for n in range(4):
        @pl.when(rest == n)
        def _():
            process([(qi, True)] + [(qi - d, False) for d in range(1, n + 1)])

    base = qi - 1 - rest

    def four(j, carry):
        process([(base - 4 * j - i, False) for i in range(4)])
        return carry

    lax.fori_loop(0, lax.shift_right_logical(qi, 2), four, 0)
    o_ref[0] = acc_ref[...].T.astype(BF16)


def _sb_attention(qt, k4, vt, tri):
    bsz, nq = qt.shape[0], qt.shape[1]
    return _attn_call(
        _sb_kernel, "stick_breaking_attention", bsz, nq,
        [qt, k4, vt, tri],
        [_q_spec(), _seq_spec(k4.shape), _seq_spec(vt.shape), _resident(tri.shape)],
        [pltpu.VMEM((HPM, 1, T), F32), pltpu.VMEM((GW, T), F32)])


def _nsa_kernel(far_ref, bnd_ref, qt_ref, kc_ref, vct_ref, ks_ref, vst_ref, kw_ref, vwt_ref, g_ref,
                bias_ref, ovt_ref, o_ref,
                sc_ref, imp_ref, sel_ref, m_ref, acc_ref):
    qi = pl.program_id(1)
    q0 = qi * T
    n_cmp = kc_ref.shape[1]
    n_blk = ovt_ref.shape[0]
    n_sel = min(N_SEL, n_blk)
    w4 = HPM * T
    qt = qt_ref[0, 0]
    qcat = jnp.concatenate([qt[h * HEAD_DIM:(h + 1) * HEAD_DIM, :] for h in range(HPM)], axis=1)
    far_row = jnp.concatenate([jnp.full((1, T), far_ref[HPM + h], F32) for h in range(HPM)], axis=1)
    r, c = _causal_tile(w4)

    sc_ref[...] = _dot(kc_ref[0], qcat)
    delta = bias_ref[2 * T:2 * T + BAND_ROWS, :] - far_row

    @pl.when(qi == 0)
    def _():
        half = BAND_ROWS // 2
        sc_ref[0:half, :] = sc_ref[0:half, :] + delta[half:BAND_ROWS]

    @pl.when(qi > 0)
    def _():
        c0 = pl.multiple_of(qi * (T // CMP_STRIDE) - BAND_ROWS // 2, 8)
        sc_ref[pl.ds(c0, BAND_ROWS), :] = sc_ref[pl.ds(c0, BAND_ROWS), :] + delta

    ci = lax.broadcasted_iota(jnp.int32, (n_cmp, w4), 0)
    ti = q0 + jnp.bitwise_and(lax.broadcasted_iota(jnp.int32, (n_cmp, w4), 1), T - 1)
    valid_c = ti - CMP_STRIDE * ci - (CMP_LEN - 1) >= 0
    s = jnp.where(valid_c, sc_ref[...] + far_row, NEG)
    e = jnp.exp2(s - jnp.max(s, axis=0, keepdims=True))
    p = jnp.where(valid_c, e / jnp.sum(e, axis=0, keepdims=True), 0.0)
    o_cmp = _dot(vct_ref[0], p.astype(BF16))
    pc_sum = p[:, 0:T]
    for h in range(1, HPM):
        pc_sum = pc_sum + p[:, h * T:(h + 1) * T]

    hi, lo = _split2(pc_sum)
    imp = _dot(ovt_ref[...], hi) + _dot(ovt_ref[...], lo)
    ji = lax.broadcasted_iota(jnp.int32, (n_blk, T), 0)
    tq = q0 + lax.broadcasted_iota(jnp.int32, (n_blk, T), 1)
    forced = (ji == lax.shift_right_logical(tq, int(math.log2(SEL_LEN)))) | (ji == 0)
    imp = jnp.where(forced, FORCED_SCORE, imp)
    imp = jnp.where(ji * SEL_LEN <= tq, imp, -1.0)
    imp_ref[...] = imp

    def rank_body(jp, cnt):
        row = imp_ref[pl.ds(jp, 1), :]
        tie = jnp.where(row == imp, 1.0, 0.0) * jnp.where(ji > jp, 1.0, 0.0)
        return cnt + jnp.where(row > imp, 1.0, 0.0) + tie

    n_seen = jnp.minimum((qi + 1) * (T // SEL_LEN), n_blk)
    cnt = lax.fori_loop(0, n_seen, rank_body, jnp.zeros((n_blk, T), F32))
    sel = jnp.where(cnt < float(n_sel), 1.0, 0.0)
    sel_ref[...] = jnp.concatenate([sel] * HPM, axis=1)

    m_ref[...] = jnp.full(m_ref.shape, NEG, F32)
    acc_ref[...] = jnp.zeros(acc_ref.shape, F32)

    def bound_row(base):
        return jnp.concatenate([jnp.full((1, T), bnd_ref[base + 1 + h], F32) for h in range(HPM)], axis=1)

    def biased(s, mode, bound):
        if mode == "far":
            return s + (far_row if bound is None else far_row - bound)
        tile = bias_ref[T:2 * T, :] if mode == "near" else bias_ref[0:T, :]
        return s + tile if bound is None else s + tile - bound

    def update(state, ss, vts, bounded):
        alpha, ps = _softmax_probs(ss, m_ref, state, fixed=0.0 if bounded else None)
        _accumulate(acc_ref, slice(state * VROWS, (state + 1) * VROWS), alpha, vts, ps)

    def sel_mask(ki):
        per = T // SEL_LEN
        rows = [jnp.broadcast_to(sel_ref[pl.ds(ki * per + i, 1), :], (SEL_LEN, w4)) for i in range(per)]
        return jnp.concatenate(rows, axis=0) > 0.5

    def process_selected(tiles, bounded):
        bound = bound_row(BND_SEL) if bounded else None
        scores = [_dot(ks_ref[0, ki], qcat) for ki, _ in tiles]
        ss = []
        for (ki, mode), sc in zip(tiles, scores):
            mask = sel_mask(ki)
            if mode == "diag":
                mask = mask & (r <= c)
            ss.append(jnp.where(mask, biased(sc, mode, bound), NEG))
        update(0, ss, [_with_ones(vst_ref[0, ki]) for ki, _ in tiles], bounded)

    _run_bounded_or_online(bnd_ref[BND_SEL] > 0.5, qi, process_selected)

    def process_window(tiles, bounded):
        bound = bound_row(BND_WIN) if bounded else None
        scores = [_dot(kw_ref[0, ki], qcat) for ki, _, _ in tiles]
        ss = []
        for (ki, mode, mask), sc in zip(tiles, scores):
            s = biased(sc, mode, bound)
            ss.append(s if mask is None else jnp.where(mask, s, NEG))
        update(1, ss, [_with_ones(vwt_ref[0, ki]) for ki, _, _ in tiles], bounded)

    back = WINDOW // T

    def window_tiles(n):
        tiles = [(qi, "diag", r <= c)]
        for d in range(1, n + 1):
            mode = "near" if d == 1 else "far"
            tiles.append((qi - d, mode, (c < r) if d == back else None))
        return tiles

    win_bounded = bnd_ref[BND_WIN] > 0.5
    for n in range(back + 1):
        cond = (qi == n) if n < back else (qi >= back)

        @pl.when(cond & win_bounded)
        def _():
            process_window(window_tiles(n), True)

        @pl.when(cond & jnp.logical_not(win_bounded))
        def _():
            process_window(window_tiles(n), False)

    o_sel = acc_ref[0:HEAD_DIM, :] / acc_ref[HEAD_DIM:HEAD_DIM + 1, :]
    o_win = acc_ref[VROWS:VROWS + HEAD_DIM, :] / acc_ref[VROWS + HEAD_DIM:VROWS + HEAD_DIM + 1, :]
    gate = lambda br: jnp.concatenate(
        [g_ref[0, HPM + br * HPM + h:HPM + br * HPM + h + 1, :] for h in range(HPM)], axis=1)
    o = gate(0) * o_cmp + gate(1) * o_sel + gate(2) * o_win
    ot = jnp.concatenate([o[:, h * T:(h + 1) * T] for h in range(HPM)], axis=0)
    o_ref[0] = ot.T.astype(BF16)


def _nsa_attention(far, bnd, qt, kc, vct, ks4, vst, kw4, vwt, st, bias, ovt):
    bsz, nq = qt.shape[0], qt.shape[1]
    n_cmp = kc.shape[1]
    n_blk = ovt.shape[0]
    w4 = HPM * T
    return _attn_call(
        _nsa_kernel, "native_sparse_attention", bsz, nq,
        [far, bnd, qt, kc, vct, ks4, vst, kw4, vwt, st, bias, ovt],
        [_smem(), _smem(), _q_spec(), _seq_spec(kc.shape), _seq_spec(vct.shape),
         _seq_spec(ks4.shape), _seq_spec(vst.shape), _seq_spec(kw4.shape), _seq_spec(vwt.shape),
         pl.BlockSpec((1, 16, T), lambda b, i: (b, 0, i)), _resident(bias.shape), _whole(ovt.shape)],
        [pltpu.VMEM((n_cmp, w4), F32), pltpu.VMEM((n_blk, T), F32), pltpu.VMEM((n_blk, w4), F32),
         pltpu.VMEM((2, 1, w4), F32), pltpu.VMEM((2 * VROWS, w4), F32)])


def _post_kernel(x_ref, oa_ref, ob_ref, oc_ref, od_ref, wo_ref, gta_ref, g_ref, sc_ref, sh_ref,
                 gtm_ref, wup_ref, cw_ref, cb_ref, wdn_ref, y_ref, carry_ref):
    d_ff = wdn_ref.shape[1]
    mixed = (_dot(oa_ref[0], wo_ref[0, 0:GW]) + _dot(ob_ref[0], wo_ref[0, GW:2 * GW])
             + _dot(oc_ref[0], wo_ref[0, 2 * GW:3 * GW]) + _dot(od_ref[0], wo_ref[0, 3 * GW:4 * GW]))
    x1 = x_ref[0] + gta_ref[0] * mixed
    ms = jnp.mean(x1 * x1, axis=-1, keepdims=True)
    h = x1 * lax.rsqrt(ms + EPS) * g_ref[...]
    hb = (h * (1.0 + sc_ref[0]) + sh_ref[0]).astype(BF16)

    @pl.when(pl.program_id(1) == 0)
    def _():
        carry_ref[...] = jnp.zeros(carry_ref.shape, F32)

    def up(ch):
        gate = _dot(hb, wup_ref[0, :, ch * FF_CHUNK:(ch + 1) * FF_CHUNK])
        val = _dot(hb, wup_ref[0, :, d_ff + ch * FF_CHUNK:d_ff + (ch + 1) * FF_CHUNK])
        return gate, val

    row = lax.broadcasted_iota(jnp.int32, (TM_POST, FF_CHUNK), 0)
    y = jnp.zeros((TM_POST, x1.shape[1]), F32)
    n_chunks = d_ff // FF_CHUNK
    ahead = up(0)
    for ch in range(n_chunks):
        cols = slice(ch * FF_CHUNK, (ch + 1) * FF_CHUNK)
        gate, val = ahead
        if ch + 1 < n_chunks:
            ahead = up(ch + 1)
        prev = carry_ref[:, cols]
        g1 = jnp.where(row == 0, prev[7:8], pltpu.roll(gate, 1, 0))
        g2 = jnp.where(row == 0, prev[6:7], jnp.where(row == 1, prev[7:8], pltpu.roll(gate, 2, 0)))
        carry_ref[:, cols] = gate[TM_POST - 8:TM_POST]
        conv = (cw_ref[0, 0:1, cols] * g2 + cw_ref[0, 1:2, cols] * g1 + cw_ref[0, 2:3, cols] * gate
                + cb_ref[0, :, cols])
        act = conv * _sigmoid(conv) * val
        y = y + _dot(act.astype(BF16), wdn_ref[0, cols, :])
    y_ref[0] = x1 + gtm_ref[0] * y


def _post(layer, x, oa, ob, oc, od, wo, gta, g, sc, sh, gtm, wup, cw, cb, wdn):
    bsz, s, d = x.shape
    tok = lambda width: pl.BlockSpec((1, TM_POST, width), lambda b, i: (b, i, 0))
    vec = pl.BlockSpec((1, 1, d), lambda b, i: (b, 0, 0))
    of_layer = lambda a: pl.BlockSpec((1,) + a.shape[1:], lambda b, i: (layer,) + (0,) * (a.ndim - 1),
                                      pipeline_mode=pl.Buffered(1))
    return pl.pallas_call(
        _post_kernel,
        out_shape=jax.ShapeDtypeStruct((bsz, s, d), F32),
        grid=(bsz, s // TM_POST),
        in_specs=[tok(d), tok(GW), tok(GW), tok(GW), tok(GW), of_layer(wo), vec, _whole((1, d)), vec, vec,
                  vec, of_layer(wup), of_layer(cw), of_layer(cb), of_layer(wdn)],
        out_specs=tok(d),
        scratch_shapes=[pltpu.VMEM((8, wdn.shape[1]), F32)],
        compiler_params=pltpu.CompilerParams(
            dimension_semantics=("parallel", "arbitrary"), vmem_limit_bytes=VMEM_LIMIT),
        name="out_proj_mlp",
    )(x, oa, ob, oc, od, wo, gta, g, sc, sh, gtm, wup, cw, cb, wdn)


def _repack_w_in(w):
    off = {}
    pos = 0
    for name, size in (("a_q", GW), ("a_k", GW), ("a_v", GW), ("b_q", GW), ("b_k", GW), ("b_v", GW),
                       ("b_f", HPM), ("c_q", GW), ("c_k", GW), ("c_v", GW), ("d_q", GW),
                       ("d_kc", HEAD_DIM), ("d_vc", HEAD_DIM), ("d_ks", HEAD_DIM), ("d_vs", HEAD_DIM),
                       ("d_kw", HEAD_DIM), ("d_vw", HEAD_DIM), ("d_g", 3 * HPM)):
        off[name] = (pos, size)
        pos += size
    assert pos == w.shape[-1]
    col = lambda n: w[..., off[n][0]:off[n][0] + off[n][1]]
    order = ["a_q", "a_k", "a_v", "b_q", "b_k", "b_v", "c_q", "c_k", "c_v", "d_q",
             "d_ks", "d_vs", "d_kw", "d_vw", "d_kc", "d_vc", "b_f", "d_g"]
    parts = [col(n) for n in order] + [jnp.zeros(w.shape[:-1] + (128 - HPM - 3 * HPM,), w.dtype)]
    out = jnp.concatenate(parts, axis=-1).astype(BF16)
    assert out.shape[-1] == N_PROJ
    return out


def _compress_params(pe, phi_w):
    half = CMP_LEN // 2
    pe_rows = jnp.concatenate([pe[0], pe[1]], axis=1).reshape(2, half * 2 * HEAD_DIM)
    wk = phi_w[0].reshape(CMP_LEN, HEAD_DIM, HEAD_DIM)
    wv = phi_w[1].reshape(CMP_LEN, HEAD_DIM, HEAD_DIM)
    zero = jnp.zeros_like(wk)
    blk = jnp.concatenate([jnp.concatenate([wk, zero], axis=2), jnp.concatenate([zero, wv], axis=2)], axis=1)
    return pe_rows, blk.reshape(2, half * 2 * HEAD_DIM, 2 * HEAD_DIM).astype(BF16)


def _score_bounds(rel_bias, gq_a, gk_a, gq_b, gk_b, gq_d, gk_s, gk_w):
    amax = lambda g: jnp.max(jnp.abs(g))
    tb = rel_bias * LOG2E
    bmax, bmin = jnp.max(tb, axis=0), jnp.min(tb, axis=0)
    span = jnp.max(bmax - bmin)
    qk = lambda n, gq, gk, scale: n * amax(gq) * amax(gk) * (scale * LOG2E * BOUND_MARGIN)
    ok = lambda b, sp: (2.0 * b + sp <= MAX_LOG2_SPAN).astype(F32).reshape(1)
    b_fox = qk(HEAD_DIM, gq_b, gk_b, HEAD_DIM ** -0.5)
    b_a = qk(DIFF_QK_DIM, gq_a, gk_a, DIFF_QK_DIM ** -0.5)
    b_s = qk(HEAD_DIM, gq_d, gk_s, HEAD_DIM ** -0.5)
    b_w = qk(HEAD_DIM, gq_d, gk_w, HEAD_DIM ** -0.5)
    vec = jnp.concatenate([
        ok(b_fox, 0.0), b_fox.reshape(1),
        ok(b_a, span), b_a + bmax[:HPM],
        ok(b_s, span), b_s + bmax[HPM:],
        ok(b_w, span), b_w + bmax[HPM:]])
    return jnp.pad(vec, (0, BND_SIZE - vec.shape[0]))


def kernel(x, c, rel_bias, ada_w, ada_b, norm_mix_g, norm_ffn_g, w_in, w_out, diff_qnorm_g, diff_knorm_g, diff_lambda, diff_subln_g, fox_qnorm_g, fox_knorm_g, fox_b_f, nsa_qnorm_g, nsa_knorm_g, nsa_pe, nsa_phi_w, ffn_w_up, ffn_conv_w, ffn_conv_b, ffn_w_down):
    bsz, s, d = x.shape
    depth = ada_w.shape[0]
    assert s % T == 0 and WINDOW % T == 0 and T % SEL_LEN == 0 and d == 4 * GW
    assert T & (T - 1) == 0 and BAND_ROWS // 2 >= (MAX_DISTANCE + CMP_LEN) // CMP_STRIDE
    nk = s // T
    n_chunk = s // CMP_STRIDE
    n_blk = s // SEL_LEN
    d_ff = ffn_w_down.shape[1]
    assert d_ff % FF_CHUNK == 0

    bd32 = jnp.asarray(_block_ones(GW, DIFF_QK_DIM), BF16)
    bd64 = jnp.asarray(_block_ones(GW, HEAD_DIM), BF16)
    ti = np.arange(T)
    tri_incl = jnp.asarray((ti[None, :] <= ti[:, None]).astype(np.float32), BF16)
    tri_later = jnp.asarray((ti[None, :] > ti[:, None]).astype(np.float32), BF16)
    c_start = np.arange(n_chunk) * CMP_STRIDE
    s_start = np.arange(n_blk) * SEL_LEN
    ov = ((c_start[None, :] < s_start[:, None] + SEL_LEN) & (s_start[:, None] < c_start[None, :] + CMP_LEN)
          & (np.arange(n_chunk)[None, :] < n_chunk - 1))
    ovt = jnp.asarray(ov.astype(np.float32), BF16)

    bias_a, bias_d = _bias_tiles(rel_bias)
    far = rel_bias[NUM_BUCKETS - 1] * LOG2E
    mod = _adaln(c, ada_w, ada_b)

    w_re = _repack_w_in(w_in)
    wo_b, wup_b, wdn_b = w_out.astype(BF16), ffn_w_up.astype(BF16), ffn_w_down.astype(BF16)
    conv_w = ffn_conv_w.reshape(depth, CONV_WIDTH, d_ff)
    conv_b = ffn_conv_b.reshape(depth, 1, d_ff)

    for l in range(depth):
        sh_a, sc_a, gt_a, sh_m, sc_m, gt_m = [m.reshape(bsz, 1, d) for m in jnp.split(mod[l], 6, axis=-1)]
        scale_a = DIFF_QK_DIM ** -0.5 * LOG2E
        scale = HEAD_DIM ** -0.5 * LOG2E
        ones = jnp.ones((HEAD_DIM,), F32)
        gains = jnp.stack([
            jnp.tile(diff_qnorm_g[l] * scale_a, GW // DIFF_QK_DIM), jnp.tile(diff_knorm_g[l], GW // DIFF_QK_DIM),
            jnp.tile(fox_qnorm_g[l] * scale, HPM), jnp.tile(fox_knorm_g[l], HPM),
            jnp.tile(nsa_qnorm_g[l] * scale, HPM),
            jnp.concatenate([nsa_knorm_g[l, 1], ones, nsa_knorm_g[l, 2], ones]),
            jnp.ones((GW,), F32), jnp.ones((GW,), F32)])
        (a_qt, a_k, a_vt, b_qt, b_k, b_vt, c_qt, c_k, c_vt, d_qt,
         d_ks, d_kw, d_vst, d_vwt, d_cmp, smalls) = _inproj(
            l, x, norm_mix_g[l].reshape(1, d), sc_a, sh_a, w_re, gains, bd32, bd64)

        bf_row = jnp.pad(fox_b_f[l], (0, 128 - HPM)).reshape(1, 128)
        ck, st = _scalars(smalls, bf_row, tri_incl)

        pe_rows, w_blk = _compress_params(nsa_pe[l], nsa_phi_w[l])
        gk_pad = jnp.pad(nsa_knorm_g[l, 0], (0, 128 - HEAD_DIM)).reshape(1, 128)
        kc, vct = _compress(d_cmp, pe_rows, w_blk, gk_pad)

        bnd = _score_bounds(rel_bias, diff_qnorm_g[l], diff_knorm_g[l], fox_qnorm_g[l], fox_knorm_g[l],
                            nsa_qnorm_g[l], nsa_knorm_g[l, 1], nsa_knorm_g[l, 2])
        lam_init = 0.8 - 0.6 * math.exp(-0.3 * l)
        cst = jnp.full((1,), lam_init, F32)
        k4 = lambda a: a.reshape(bsz, nk, T, a.shape[-1])
        o_a = _diff_attention(far, cst, bnd, a_qt, k4(a_k), a_vt, bias_a, diff_lambda[l],
                              diff_subln_g[l].reshape(HEAD_DIM, 1))
        o_b = _fox_attention(bnd, b_qt, k4(b_k), b_vt, st, k4(ck))
        o_c = _sb_attention(c_qt, k4(c_k), c_vt, tri_later)
        o_d = _nsa_attention(far, bnd, d_qt, kc, vct, k4(d_ks), d_vst, k4(d_kw), d_vwt, st, bias_d, ovt)

        x = _post(l, x, o_a, o_b, o_c, o_d, wo_b, gt_a, norm_ffn_g[l].reshape(1, d),
                  sc_m, sh_m, gt_m, wup_b, conv_w, conv_b, wdn_b)
    return x
```

```python
import math

import numpy as np
import jax
import jax.numpy as jnp
from jax import lax
from jax.experimental import pallas as pl
from jax.experimental.pallas import tpu as pltpu

HEAD_DIM = 64
HPM = 4
GW = HPM * HEAD_DIM
DIFF_QK_DIM = HEAD_DIM // 2
NUM_BUCKETS = 32
MAX_EXACT = NUM_BUCKETS // 2
MAX_DISTANCE = 128
CMP_LEN = 32
CMP_STRIDE = 16
SEL_LEN = 64
N_SEL = 16
WINDOW = 512
FORCED_SCORE = 1.0e4
NEG = -1.0e30
EPS = 1e-6
CONV_WIDTH = 3
LOG2E = 1.4426950408889634

T = 256
TM = T
TM_POST = 256
FF_CHUNK = 256
ONES_ROWS = 16
VROWS = HEAD_DIM + ONES_ROWS
BAND_ROWS = 2 * T // CMP_STRIDE
VMEM_LIMIT = 56 * 1024 * 1024

BND_FOX, BND_DIFF, BND_SEL, BND_WIN, BND_SIZE = 0, 2, 7, 12, 24
BOUND_MARGIN = 1.02
MAX_LOG2_SPAN = 100.0

F32 = jnp.float32
BF16 = jnp.bfloat16


def _dot(a, b):
    return jnp.dot(a, b, preferred_element_type=F32)


def _split2(x):
    hi = x.astype(BF16)
    lo = (x - hi.astype(F32)).astype(BF16)
    return hi, lo


def _log_sigmoid(z):
    return jnp.minimum(z, 0.0) - jnp.log(1.0 + jnp.exp(-jnp.abs(z)))


def _sigmoid(z):
    return 1.0 / (1.0 + jnp.exp(-z))


def _whole(shape):
    nd = len(shape)
    return pl.BlockSpec(shape, lambda *_: (0,) * nd)


def _resident(shape):
    nd = len(shape)
    return pl.BlockSpec(shape, lambda *_: (0,) * nd, pipeline_mode=pl.Buffered(1))


def _smem():
    return pl.BlockSpec(memory_space=pltpu.SMEM)


def _t5_bucket_np(dist):
    n = np.maximum(dist, 0)
    ratio = math.log(MAX_DISTANCE / MAX_EXACT)
    out = None
    for dt in (np.float32, np.float64):
        large = MAX_EXACT + (np.log(np.maximum(n, 1).astype(dt) / dt(MAX_EXACT)) / dt(ratio)
                             * dt(NUM_BUCKETS - MAX_EXACT)).astype(np.int32)
        b = np.where(n < MAX_EXACT, n, np.minimum(large, NUM_BUCKETS - 1)).astype(np.int32)
        assert out is None or np.array_equal(out, b)
        out = b
    return out


def _bucket_tiles():
    r = np.arange(T)[:, None]
    c = np.arange(T)[None, :]
    diag = _t5_bucket_np(c - r)
    near = _t5_bucket_np(T + c - r)
    rb = np.arange(BAND_ROWS)[:, None]
    band = _t5_bucket_np(c - CMP_STRIDE * rb + (T - CMP_LEN + 1))
    return np.concatenate([diag, near, band], axis=0)


def _block_ones(n, group):
    i = np.arange(n)
    return (i[:, None] // group == i[None, :] // group).astype(np.float32)


def _bias_kernel(tbl_ref, bkt_ref, a_ref, d_ref):
    b = bkt_ref[...]
    for h in range(2 * HPM):
        acc = jnp.zeros(b.shape, F32)
        for k in range(NUM_BUCKETS):
            acc = jnp.where(b == k, tbl_ref[k, h] * LOG2E, acc)
        if h < HPM:
            a_ref[h] = acc[0:2 * T]
        else:
            d_ref[:, (h - HPM) * T:(h - HPM + 1) * T] = acc


def _bias_tiles(rel_bias):
    bkt = jnp.asarray(_bucket_tiles())
    rows = bkt.shape[0]
    return pl.pallas_call(
        _bias_kernel,
        out_shape=[jax.ShapeDtypeStruct((HPM, 2 * T, T), F32), jax.ShapeDtypeStruct((rows, HPM * T), F32)],
        in_specs=[_smem(), _whole(bkt.shape)],
        out_specs=[_whole((HPM, 2 * T, T)), _whole((rows, HPM * T))],
        name="bias_tiles",
    )(rel_bias, bkt)


def _adaln_kernel(c_ref, w_ref, b_ref, o_ref):
    c = c_ref[...]
    ca = c * _sigmoid(c)
    o_ref[0] = _dot(ca.astype(BF16), w_ref[0].astype(BF16)) + b_ref[0]


def _adaln(c, ada_w, ada_b):
    depth, d, n = ada_w.shape
    bsz = c.shape[0]
    rows = -(-bsz // 8) * 8
    cp = jnp.pad(c, ((0, rows - bsz), (0, 0)))
    tn = 1536
    out = pl.pallas_call(
        _adaln_kernel,
        out_shape=jax.ShapeDtypeStruct((depth, rows, n), F32),
        grid=(depth, n // tn),
        in_specs=[pl.BlockSpec((rows, d), lambda l, j: (0, 0)),
                  pl.BlockSpec((1, d, tn), lambda l, j: (l, 0, j)),
                  pl.BlockSpec((1, 1, tn), lambda l, j: (l, 0, j))],
        out_specs=pl.BlockSpec((1, rows, tn), lambda l, j: (l, 0, j)),
        name="adaln",
    )(cp, ada_w, ada_b.reshape(depth, 1, n))
    return out[:, :bsz]


N_PROJ = 11 * GW + 2 * 128


def _inproj_kernel(x_ref, g_ref, sc_ref, sh_ref, w_ref, gains_ref, bd32_ref, bd64_ref,
                   aq_ref, ak_ref, av_ref, bq_ref, bk_ref, bv_ref, cq_ref, ck_ref, cv_ref, dq_ref,
                   dks_ref, dkw_ref, dvs_ref, dvw_ref, cmp_ref, sm_ref):
    x = x_ref[0]
    ms = jnp.mean(x * x, axis=-1, keepdims=True)
    h = x * lax.rsqrt(ms + EPS) * g_ref[...]
    h = h * (1.0 + sc_ref[0]) + sh_ref[0]
    hb = h.astype(BF16)

    def proj(group, width=GW):
        off = group * GW
        return _dot(hb, w_ref[0, :, off:off + width])

    def segnorm(y, bd_ref, inv_n, gain_row):
        ss = _dot((y * y).astype(BF16), bd_ref[...])
        return y * lax.rsqrt(ss * inv_n + EPS) * gains_ref[gain_row:gain_row + 1, :]

    def put_t(ref, y):
        ref[0, 0] = y.T.astype(BF16)

    raw0 = proj(0)
    raw1 = proj(1)
    put_t(aq_ref, segnorm(raw0, bd32_ref, 1.0 / DIFF_QK_DIM, 0))
    put_t(av_ref, proj(2))
    raw3 = proj(3)
    ak_ref[0] = segnorm(raw1, bd32_ref, 1.0 / DIFF_QK_DIM, 1).astype(BF16)
    put_t(bv_ref, proj(5))
    raw4 = proj(4)
    put_t(bq_ref, segnorm(raw3, bd64_ref, 1.0 / HEAD_DIM, 2))
    put_t(cq_ref, proj(6) * (HEAD_DIM ** -0.5 * LOG2E))
    raw9 = proj(9)
    bk_ref[0] = segnorm(raw4, bd64_ref, 1.0 / HEAD_DIM, 3).astype(BF16)
    ck_ref[0] = proj(7).astype(BF16)
    y = proj(10)
    put_t(dq_ref, segnorm(raw9, bd64_ref, 1.0 / HEAD_DIM, 4))
    put_t(cv_ref, proj(8))
    cmp_ref[0] = proj(11, 128)
    sm_ref[0] = _dot(hb, w_ref[0, :, 11 * GW + 128:11 * GW + 256])

    yn = segnorm(y, bd64_ref, 1.0 / HEAD_DIM, 5)
    dks_ref[0] = yn[:, 0:HEAD_DIM].astype(BF16)
    dkw_ref[0] = yn[:, 2 * HEAD_DIM:3 * HEAD_DIM].astype(BF16)
    yt = y.T
    dvs_ref[0, 0] = yt[HEAD_DIM:2 * HEAD_DIM].astype(BF16)
    dvw_ref[0, 0] = yt[3 * HEAD_DIM:4 * HEAD_DIM].astype(BF16)


def _inproj(layer, x, g, sc, sh, w_re, gains, bd32, bd64):
    bsz, s, d = x.shape
    nk = s // T
    tok = lambda width, dt: jax.ShapeDtypeStruct((bsz, s, width), dt)
    tr = lambda rows: jax.ShapeDtypeStruct((bsz, nk, rows, T), BF16)
    tok_spec = lambda width: pl.BlockSpec((1, TM, width), lambda b, i: (b, i, 0))
    tr_spec = lambda rows: pl.BlockSpec((1, 1, rows, T), lambda b, i: (b, i, 0, 0))
    vec = pl.BlockSpec((1, 1, d), lambda b, i: (b, 0, 0))
    out_shape = [tr(GW), tok(GW, BF16), tr(GW)] * 3 + [tr(GW)] + [
        tok(HEAD_DIM, BF16), tok(HEAD_DIM, BF16), tr(HEAD_DIM), tr(HEAD_DIM),
        tok(128, F32), tok(128, F32)]
    out_specs = [tr_spec(GW), tok_spec(GW), tr_spec(GW)] * 3 + [tr_spec(GW)] + [
        tok_spec(HEAD_DIM), tok_spec(HEAD_DIM), tr_spec(HEAD_DIM), tr_spec(HEAD_DIM),
        tok_spec(128), tok_spec(128)]
    return pl.pallas_call(
        _inproj_kernel,
        out_shape=out_shape,
        grid=(bsz, s // TM),
        in_specs=[pl.BlockSpec((1, TM, d), lambda b, i: (b, i, 0)),
                  _whole((1, d)), vec, vec,
                  pl.BlockSpec((1,) + w_re.shape[1:], lambda b, i: (layer, 0, 0)),
                  _whole(gains.shape), _whole(bd32.shape), _whole(bd64.shape)],
        out_specs=out_specs,
        compiler_params=pltpu.CompilerParams(
            dimension_semantics=("parallel", "parallel"), vmem_limit_bytes=VMEM_LIMIT),
        name="inproj",
    )(x, g, sc, sh, w_re, gains, bd32, bd64)


def _scalars_kernel(sm_ref, bf_ref, tri_ref, ck_ref, st_ref):
    s = sm_ref.shape[1]
    carry = jnp.zeros((1, 128), F32)
    for blk in range(s // T):
        rows = slice(blk * T, (blk + 1) * T)
        lf = _log_sigmoid(sm_ref[0, rows, :] + bf_ref[...])
        h1 = lf.astype(BF16)
        r1 = lf - h1.astype(F32)
        h2 = r1.astype(BF16)
        h3 = (r1 - h2.astype(F32)).astype(BF16)
        tri = tri_ref[...]
        cb = _dot(tri, h1) + _dot(tri, h2) + _dot(tri, h3) + carry
        carry = cb[T - 1:T, :]
        cb2 = cb * LOG2E
        ck_ref[0, rows, :] = cb2
        col = lax.broadcasted_iota(jnp.int32, (T, 128), 1)
        comb = jnp.where(col < HPM, cb2, _sigmoid(sm_ref[0, rows, :]))
        st_ref[0, :, rows] = comb.T[0:16]


def _scalars(smalls, bf_row, tri):
    bsz, s, _ = smalls.shape
    return pl.pallas_call(
        _scalars_kernel,
        out_shape=[jax.ShapeDtypeStruct((bsz, s, 128), F32), jax.ShapeDtypeStruct((bsz, 16, s), F32)],
        grid=(bsz,),
        in_specs=[pl.BlockSpec((1, s, 128), lambda b: (b, 0, 0)), _whole((1, 128)), _whole((T, T))],
        out_specs=[pl.BlockSpec((1, s, 128), lambda b: (b, 0, 0)),
                   pl.BlockSpec((1, 16, s), lambda b: (b, 0, 0))],
        compiler_params=pltpu.CompilerParams(dimension_semantics=("parallel",)),
        name="token_scalars",
    )(smalls, bf_row, tri)


def _compress_kernel(x_ref, pe_ref, w_ref, gk_ref, kc_ref, vct_ref):
    n = x_ref.shape[1] // CMP_STRIDE
    chunks = jnp.concatenate(
        [x_ref[0, pl.ds(i, n, stride=CMP_STRIDE), :] for i in range(CMP_STRIDE)], axis=1)
    ya = _dot((chunks + pe_ref[0:1]).astype(BF16), w_ref[0])
    yb = _dot((chunks + pe_ref[1:2]).astype(BF16), w_ref[1])
    y = ya + pltpu.roll(yb, n - 1, 0)
    lane = lax.broadcasted_iota(jnp.int32, y.shape, 1)
    ss = jnp.sum(jnp.where(lane < HEAD_DIM, y * y, 0.0), axis=-1, keepdims=True)
    kc = y * lax.rsqrt(ss * (1.0 / HEAD_DIM) + EPS) * gk_ref[...]
    kc_ref[0] = kc[:, 0:HEAD_DIM].astype(BF16)
    vct_ref[0] = y.T[HEAD_DIM:2 * HEAD_DIM].astype(BF16)


def _compress(cmp, pe_rows, w_blk, gk_pad):
    bsz, s, width = cmp.shape
    n = s // CMP_STRIDE
    return pl.pallas_call(
        _compress_kernel,
        out_shape=[jax.ShapeDtypeStruct((bsz, n, HEAD_DIM), BF16),
                   jax.ShapeDtypeStruct((bsz, HEAD_DIM, n), BF16)],
        grid=(bsz,),
        in_specs=[pl.BlockSpec((1, s, width), lambda b: (b, 0, 0)),
                  _whole(pe_rows.shape), _whole(w_blk.shape), _whole(gk_pad.shape)],
        out_specs=[pl.BlockSpec((1, n, HEAD_DIM), lambda b: (b, 0, 0)),
                   pl.BlockSpec((1, HEAD_DIM, n), lambda b: (b, 0, 0))],
        compiler_params=pltpu.CompilerParams(dimension_semantics=("parallel",)),
        name="nsa_compress",
    )(cmp, pe_rows, w_blk, gk_pad)


def _head_rows(qt, h):
    return qt[h * HEAD_DIM:(h + 1) * HEAD_DIM, :]


def _head_cols(kt, h):
    return kt[:, h * HEAD_DIM:(h + 1) * HEAD_DIM]


def _split_maps(qh):
    qf = qh.astype(F32)
    row = lax.broadcasted_iota(jnp.int32, qf.shape, 0)
    first = jnp.where(row < DIFF_QK_DIM, qf, 0.0).astype(BF16)
    second = jnp.where(row >= DIFF_QK_DIM, qf, 0.0).astype(BF16)
    return jnp.concatenate([first, second], axis=1)


def _with_ones(vt):
    return jnp.concatenate([vt, jnp.ones((ONES_ROWS, vt.shape[1]), BF16)], axis=0)


def _softmax_probs(ss, m_ref, idx, shift=None, fixed=None):
    if fixed is not None:
        already = isinstance(fixed, float) and fixed == 0.0
        return None, [jnp.exp2(s if already else s - fixed).astype(BF16) for s in ss]
    m_old = m_ref[idx]
    mx = jnp.max(ss[0], axis=0, keepdims=True)
    for s in ss[1:]:
        mx = jnp.maximum(mx, jnp.max(s, axis=0, keepdims=True))
    if shift is not None:
        mx = mx + shift
    m_new = jnp.maximum(m_old, mx)
    m_ref[idx] = m_new
    alpha = jnp.exp2(m_old - m_new)
    sub = m_new if shift is None else m_new - shift
    return alpha, [jnp.exp2(s - sub).astype(BF16) for s in ss]


def _accumulate(acc_ref, rows, alpha, vts, ps):
    upd = _dot(vts[0], ps[0])
    for vt, p in zip(vts[1:], ps[1:]):
        upd = upd + _dot(vt, p)
    acc_ref[rows, :] = (acc_ref[rows, :] if alpha is None else alpha * acc_ref[rows, :]) + upd


def _normalized(acc_ref, idx):
    base = idx * VROWS
    return acc_ref[base:base + HEAD_DIM, :] / acc_ref[base + HEAD_DIM:base + HEAD_DIM + 1, :]


def _causal_tile(width=T):
    r = lax.broadcasted_iota(jnp.int32, (T, width), 0)
    c = lax.broadcasted_iota(jnp.int32, (T, width), 1)
    if width != T:
        c = jnp.bitwise_and(c, T - 1)
    return r, c


def _for_tile_groups(qi, process, paired=True):
    n_far = jnp.maximum(qi - 1, 0)
    if not paired:
        def single(j, carry):
            process([(j, "far")])
            return carry

        lax.fori_loop(0, n_far, single, 0)

        @pl.when(qi == 0)
        def _():
            process([(qi, "diag")])

        @pl.when(qi >= 1)
        def _():
            process([(qi - 1, "near"), (qi, "diag")])

        return

    n_pair = lax.shift_right_logical(n_far, 1)
    odd = jnp.bitwise_and(n_far, 1) == 1

    def pair(j, carry):
        process([(2 * j, "far"), (2 * j + 1, "far")])
        return carry

    lax.fori_loop(0, n_pair, pair, 0)

    @pl.when(qi == 0)
    def _():
        process([(qi, "diag")])

    @pl.when((qi >= 1) & jnp.logical_not(odd))
    def _():
        process([(qi - 1, "near"), (qi, "diag")])

    @pl.when((qi >= 1) & odd)
    def _():
        process([(qi - 2, "far"), (qi - 1, "near"), (qi, "diag")])


def _for_independent_tiles(qi, process, group=4):
    assert group == 4
    n_far = jnp.maximum(qi - 1, 0)
    n_group = lax.shift_right_logical(n_far, 2)
    rest = jnp.bitwise_and(n_far, 3)

    def many(j, carry):
        process([(4 * j + i, "far") for i in range(4)])
        return carry

    lax.fori_loop(0, n_group, many, 0)

    @pl.when(qi == 0)
    def _():
        process([(qi, "diag")])

    for n in range(4):
        @pl.when((qi >= 1) & (rest == n))
        def _():
            process([(4 * n_group + i, "far") for i in range(n)] + [(qi - 1, "near"), (qi, "diag")])


def _run_bounded_or_online(bounded_ok, qi, process):
    @pl.when(bounded_ok)
    def _():
        _for_independent_tiles(qi, lambda tiles: process(tiles, True))

    @pl.when(jnp.logical_not(bounded_ok))
    def _():
        _for_tile_groups(qi, lambda tiles: process(tiles, False))


def _attn_call(kernel, name, bsz, nq, in_arrays, in_specs, scratch, out_width=GW):
    return pl.pallas_call(
        kernel,
        out_shape=jax.ShapeDtypeStruct((bsz, nq * T, out_width), BF16),
        grid=(bsz, nq),
        in_specs=in_specs,
        out_specs=pl.BlockSpec((1, T, out_width), lambda b, i: (b, i, 0)),
        scratch_shapes=scratch,
        compiler_params=pltpu.CompilerParams(
            dimension_semantics=("parallel", "arbitrary"), vmem_limit_bytes=VMEM_LIMIT),
        name=name,
    )(*in_arrays)


def _q_spec(rows=GW):
    return pl.BlockSpec((1, 1, rows, T), lambda b, i: (b, i, 0, 0))


def _seq_spec(shape):
    nd = len(shape)
    return pl.BlockSpec((1,) + tuple(shape[1:]), lambda b, i: (b,) + (0,) * (nd - 1))


def _diff_kernel(far_ref, cst_ref, bnd_ref, qt_ref, k_ref, vt_ref, bias_ref, lam_ref, subg_ref, o_ref,
                 m_ref, acc_ref):
    qi = pl.program_id(1)
    qmaps = [_split_maps(_head_rows(qt_ref[0, 0], h)) for h in range(HPM)]
    m_ref[...] = jnp.full(m_ref.shape, NEG, F32)
    acc_ref[...] = jnp.zeros(acc_ref.shape, F32)
    r, c = _causal_tile()

    def process(tiles, bounded):
        scores = [[_dot(_head_cols(k_ref[0, ki], h), qmaps[h]) for ki, _ in tiles] for h in range(HPM)]
        probs = []
        for idx in range(2 * HPM):
            h = idx // 2
            bound = bnd_ref[BND_DIFF + 1 + h] if bounded else None
            ss = []
            for (ki, mode), sc in zip(tiles, scores[h]):
                s = sc[:, (idx % 2) * T:(idx % 2 + 1) * T]
                if mode == "far":
                    s = s + ((far_ref[h] - bound) if bounded else far_ref[h])
                elif mode == "near":
                    s = s + bias_ref[h, T:2 * T, :]
                else:
                    s = jnp.where(r <= c, s + bias_ref[h, 0:T, :], NEG)
                if bounded and mode != "far":
                    s = s - bound
                ss.append(s)
            probs.append(_softmax_probs(ss, m_ref, idx, fixed=0.0 if bounded else None))
        for idx in range(2 * HPM):
            h = idx // 2
            vts = [_with_ones(vt_ref[0, ki, h * HEAD_DIM:(h + 1) * HEAD_DIM, :]) for ki, _ in tiles]
            alpha, ps = probs[idx]
            _accumulate(acc_ref, slice(idx * VROWS, (idx + 1) * VROWS), alpha, vts, ps)

    _run_bounded_or_online(bnd_ref[BND_DIFF] > 0.5, qi, process)

    lam_init = cst_ref[0]
    lv = lam_ref[...]
    lam = (jnp.exp(jnp.sum(lv[0:1] * lv[1:2], axis=-1, keepdims=True))
           - jnp.exp(jnp.sum(lv[2:3] * lv[3:4], axis=-1, keepdims=True)) + lam_init)
    outs = []
    for h in range(HPM):
        o = _normalized(acc_ref, 2 * h) - lam * _normalized(acc_ref, 2 * h + 1)
        ms = jnp.mean(o * o, axis=0, keepdims=True)
        outs.append(o * lax.rsqrt(ms + EPS) * subg_ref[...] * (1.0 - lam_init))
    o_ref[0] = jnp.concatenate(outs, axis=0).T.astype(BF16)


def _diff_attention(far, cst, bnd, qt, k4, vt, bias, lam, subg):
    bsz, nq = qt.shape[0], qt.shape[1]
    return _attn_call(
        _diff_kernel, "diff_attention", bsz, nq,
        [far, cst, bnd, qt, k4, vt, bias, lam, subg],
        [_smem(), _smem(), _smem(), _q_spec(), _seq_spec(k4.shape), _seq_spec(vt.shape),
         _resident(bias.shape), _whole(lam.shape), _whole(subg.shape)],
        [pltpu.VMEM((2 * HPM, 1, T), F32), pltpu.VMEM((2 * HPM * VROWS, T), F32)])


def _fox_kernel(bnd_ref, qt_ref, k_ref, vt_ref, cq_ref, ck_ref, o_ref, m_ref, acc_ref):
    qi = pl.program_id(1)
    qh = [_head_rows(qt_ref[0, 0], h) for h in range(HPM)]
    m_ref[...] = jnp.full(m_ref.shape, NEG, F32)
    acc_ref[...] = jnp.zeros(acc_ref.shape, F32)
    r, c = _causal_tile()

    def process(tiles, bounded):
        scores = [[_dot(_head_cols(k_ref[0, ki], h), qh[h]) for ki, _ in tiles] for h in range(HPM)]
        probs = []
        for h in range(HPM):
            ss = []
            for (ki, mode), sc in zip(tiles, scores[h]):
                s = sc - ck_ref[0, ki][:, h:h + 1]
                if mode == "diag":
                    s = jnp.where(r <= c, s, NEG)
                ss.append(s)
            cq = cq_ref[0, h:h + 1, :]
            fixed = (bnd_ref[BND_FOX + 1] - cq) if bounded else None
            probs.append(_softmax_probs(ss, m_ref, h, shift=cq, fixed=fixed))
        for h in range(HPM):
            vts = [_with_ones(vt_ref[0, ki, h * HEAD_DIM:(h + 1) * HEAD_DIM, :]) for ki, _ in tiles]
            alpha, ps = probs[h]
            _accumulate(acc_ref, slice(h * VROWS, (h + 1) * VROWS), alpha, vts, ps)

    _run_bounded_or_online(bnd_ref[BND_FOX] > 0.5, qi, process)
    outs = [_normalized(acc_ref, h) for h in range(HPM)]
    o_ref[0] = jnp.concatenate(outs, axis=0).T.astype(BF16)


def _fox_attention(bnd, qt, k4, vt, st, ck4):
    bsz, nq = qt.shape[0], qt.shape[1]
    return _attn_call(
        _fox_kernel, "forgetting_attention", bsz, nq,
        [bnd, qt, k4, vt, st, ck4],
        [_smem(), _q_spec(), _seq_spec(k4.shape), _seq_spec(vt.shape),
         pl.BlockSpec((1, 16, T), lambda b, i: (b, 0, i)), _seq_spec(ck4.shape)],
        [pltpu.VMEM((HPM, 1, T), F32), pltpu.VMEM((HPM * VROWS, T), F32)])


def _sb_kernel(qt_ref, k_ref, vt_ref, tri_ref, o_ref, run_ref, acc_ref):
    qi = pl.program_id(1)
    qh = [_head_rows(qt_ref[0, 0], h) for h in range(HPM)]
    run_ref[...] = jnp.zeros(run_ref.shape, F32)
    acc_ref[...] = jnp.zeros(acc_ref.shape, F32)
    r, c = _causal_tile()

    def process(tiles):
        scores = [[_dot(_head_cols(k_ref[0, ki], h), qh[h]) for ki, _ in tiles] for h in range(HPM)]
        weights = []
        for h in range(HPM):
            run = run_ref[h]
            ws = []
            for (ki, diag), z in zip(tiles, scores[h]):
                log_beta = jnp.minimum(z, 0.0) - jnp.log2(1.0 + jnp.exp2(-jnp.abs(z)))
                log_keep = log_beta - z
                if diag:
                    log_keep = jnp.where(r < c, log_keep, 0.0)
                keep_b = log_keep.astype(BF16)
                tail = _dot(tri_ref[...], keep_b) + run
                a = jnp.exp2(log_beta + tail)
                if diag:
                    a = jnp.where(r < c, a, 0.0)
                ws.append(a.astype(BF16))
                run = tail[0:1, :] + keep_b[0:1, :].astype(F32)
            run_ref[h] = run
            weights.append(ws)
        for h in range(HPM):
            rows = slice(h * HEAD_DIM, (h + 1) * HEAD_DIM)
            upd = None
            for (ki, _), a in zip(tiles, weights[h]):
                term = _dot(vt_ref[0, ki, rows, :], a)
                upd = term if upd is None else upd + term
            acc_ref[rows, :] = acc_ref[rows, :] + upd

    rest = jnp.bitwise_and(qi, 3)
    for n in range(4):
        @pl.when(rest == n)
        def _():
            process([(qi, True)] + [(qi - d, False) for d in range(1, n + 1)])

    base = qi - 1 - rest

    def four(j, carry):
        process([(base - 4 * j - i, False) for i in range(4)])
        return carry

    lax.fori_loop(0, lax.shift_right_logical(qi, 2), four, 0)
    o_ref[0] = acc_ref[...].T.astype(BF16)


def _sb_attention(qt, k4, vt, tri):
    bsz, nq = qt.shape[0], qt.shape[1]
    return _attn_call(
        _sb_kernel, "stick_breaking_attention", bsz, nq,
        [qt, k4, vt, tri],
        [_q_spec(), _seq_spec(k4.shape), _seq_spec(vt.shape), _resident(tri.shape)],
        [pltpu.VMEM((HPM, 1, T), F32), pltpu.VMEM((GW, T), F32)])


def _nsa_kernel(far_ref, bnd_ref, qt_ref, kc_ref, vct_ref, ks_ref, vst_ref, kw_ref, vwt_ref, g_ref,
                bias_ref, ovt_ref, o_ref,
                sc_ref, imp_ref, sel_ref, m_ref, acc_ref):
    qi = pl.program_id(1)
    q0 = qi * T
    n_cmp = kc_ref.shape[1]
    n_blk = ovt_ref.shape[0]
    n_sel = min(N_SEL, n_blk)
    w4 = HPM * T
    qt = qt_ref[0, 0]
    qcat = jnp.concatenate([qt[h * HEAD_DIM:(h + 1) * HEAD_DIM, :] for h in range(HPM)], axis=1)
    far_row = jnp.concatenate([jnp.full((1, T), far_ref[HPM + h], F32) for h in range(HPM)], axis=1)
    r, c = _causal_tile(w4)

    sc_ref[...] = _dot(kc_ref[0], qcat)
    delta = bias_ref[2 * T:2 * T + BAND_ROWS, :] - far_row

    @pl.when(qi == 0)
    def _():
        half = BAND_ROWS // 2
        sc_ref[0:half, :] = sc_ref[0:half, :] + delta[half:BAND_ROWS]

    @pl.when(qi > 0)
    def _():
        c0 = pl.multiple_of(qi * (T // CMP_STRIDE) - BAND_ROWS // 2, 8)
        sc_ref[pl.ds(c0, BAND_ROWS), :] = sc_ref[pl.ds(c0, BAND_ROWS), :] + delta

    ci = lax.broadcasted_iota(jnp.int32, (n_cmp, w4), 0)
    ti = q0 + jnp.bitwise_and(lax.broadcasted_iota(jnp.int32, (n_cmp, w4), 1), T - 1)
    valid_c = ti - CMP_STRIDE * ci - (CMP_LEN - 1) >= 0
    s = jnp.where(valid_c, sc_ref[...] + far_row, NEG)
    e = jnp.exp2(s - jnp.max(s, axis=0, keepdims=True))
    p = jnp.where(valid_c, e / jnp.sum(e, axis=0, keepdims=True), 0.0)
    o_cmp = _dot(vct_ref[0], p.astype(BF16))
    pc_sum = p[:, 0:T]
    for h in range(1, HPM):
        pc_sum = pc_sum + p[:, h * T:(h + 1) * T]

    hi, lo = _split2(pc_sum)
    imp = _dot(ovt_ref[...], hi) + _dot(ovt_ref[...], lo)
    ji = lax.broadcasted_iota(jnp.int32, (n_blk, T), 0)
    tq = q0 + lax.broadcasted_iota(jnp.int32, (n_blk, T), 1)
    forced = (ji == lax.shift_right_logical(tq, int(math.log2(SEL_LEN)))) | (ji == 0)
    imp = jnp.where(forced, FORCED_SCORE, imp)
    imp = jnp.where(ji * SEL_LEN <= tq, imp, -1.0)
    imp_ref[...] = imp

    def rank_body(jp, cnt):
        row = imp_ref[pl.ds(jp, 1), :]
        tie = jnp.where(row == imp, 1.0, 0.0) * jnp.where(ji > jp, 1.0, 0.0)
        return cnt + jnp.where(row > imp, 1.0, 0.0) + tie

    n_seen = jnp.minimum((qi + 1) * (T // SEL_LEN), n_blk)
    cnt = lax.fori_loop(0, n_seen, rank_body, jnp.zeros((n_blk, T), F32))
    sel = jnp.where(cnt < float(n_sel), 1.0, 0.0)
    sel_ref[...] = jnp.concatenate([sel] * HPM, axis=1)

    m_ref[...] = jnp.full(m_ref.shape, NEG, F32)
    acc_ref[...] = jnp.zeros(acc_ref.shape, F32)

    def bound_row(base):
        return jnp.concatenate([jnp.full((1, T), bnd_ref[base + 1 + h], F32) for h in range(HPM)], axis=1)

    def biased(s, mode, bound):
        if mode == "far":
            return s + (far_row if bound is None else far_row - bound)
        tile = bias_ref[T:2 * T, :] if mode == "near" else bias_ref[0:T, :]
        return s + tile if bound is None else s + tile - bound

    def update(state, ss, vts, bounded):
        alpha, ps = _softmax_probs(ss, m_ref, state, fixed=0.0 if bounded else None)
        _accumulate(acc_ref, slice(state * VROWS, (state + 1) * VROWS), alpha, vts, ps)

    def sel_mask(ki):
        per = T // SEL_LEN
        rows = [jnp.broadcast_to(sel_ref[pl.ds(ki * per + i, 1), :], (SEL_LEN, w4)) for i in range(per)]
        return jnp.concatenate(rows, axis=0) > 0.5

    def process_selected(tiles, bounded):
        bound = bound_row(BND_SEL) if bounded else None
        scores = [_dot(ks_ref[0, ki], qcat) for ki, _ in tiles]
        ss = []
        for (ki, mode), sc in zip(tiles, scores):
            mask = sel_mask(ki)
            if mode == "diag":
                mask = mask & (r <= c)
            ss.append(jnp.where(mask, biased(sc, mode, bound), NEG))
        update(0, ss, [_with_ones(vst_ref[0, ki]) for ki, _ in tiles], bounded)

    _run_bounded_or_online(bnd_ref[BND_SEL] > 0.5, qi, process_selected)

    def process_window(tiles, bounded):
        bound = bound_row(BND_WIN) if bounded else None
        scores = [_dot(kw_ref[0, ki], qcat) for ki, _, _ in tiles]
        ss = []
        for (ki, mode, mask), sc in zip(tiles, scores):
            s = biased(sc, mode, bound)
            ss.append(s if mask is None else jnp.where(mask, s, NEG))
        update(1, ss, [_with_ones(vwt_ref[0, ki]) for ki, _, _ in tiles], bounded)

    back = WINDOW // T

    def window_tiles(n):
        tiles = [(qi, "diag", r <= c)]
        for d in range(1, n + 1):
            mode = "near" if d == 1 else "far"
            tiles.append((qi - d, mode, (c < r) if d == back else None))
        return tiles

    win_bounded = bnd_ref[BND_WIN] > 0.5
    for n in range(back + 1):
        cond = (qi == n) if n < back else (qi >= back)

        @pl.when(cond & win_bounded)
        def _():
            process_window(window_tiles(n), True)

        @pl.when(cond & jnp.logical_not(win_bounded))
        def _():
            process_window(window_tiles(n), False)

    o_sel = acc_ref[0:HEAD_DIM, :] / acc_ref[HEAD_DIM:HEAD_DIM + 1, :]
    o_win = acc_ref[VROWS:VROWS + HEAD_DIM, :] / acc_ref[VROWS + HEAD_DIM:VROWS + HEAD_DIM + 1, :]
    gate = lambda br: jnp.concatenate(
        [g_ref[0, HPM + br * HPM + h:HPM + br * HPM + h + 1, :] for h in range(HPM)], axis=1)
    o = gate(0) * o_cmp + gate(1) * o_sel + gate(2) * o_win
    ot = jnp.concatenate([o[:, h * T:(h + 1) * T] for h in range(HPM)], axis=0)
    o_ref[0] = ot.T.astype(BF16)


def _nsa_attention(far, bnd, qt, kc, vct, ks4, vst, kw4, vwt, st, bias, ovt):
    bsz, nq = qt.shape[0], qt.shape[1]
    n_cmp = kc.shape[1]
    n_blk = ovt.shape[0]
    w4 = HPM * T
    return _attn_call(
        _nsa_kernel, "native_sparse_attention", bsz, nq,
        [far, bnd, qt, kc, vct, ks4, vst, kw4, vwt, st, bias, ovt],
        [_smem(), _smem(), _q_spec(), _seq_spec(kc.shape), _seq_spec(vct.shape),
         _seq_spec(ks4.shape), _seq_spec(vst.shape), _seq_spec(kw4.shape), _seq_spec(vwt.shape),
         pl.BlockSpec((1, 16, T), lambda b, i: (b, 0, i)), _resident(bias.shape), _whole(ovt.shape)],
        [pltpu.VMEM((n_cmp, w4), F32), pltpu.VMEM((n_blk, T), F32), pltpu.VMEM((n_blk, w4), F32),
         pltpu.VMEM((2, 1, w4), F32), pltpu.VMEM((2 * VROWS, w4), F32)])


def _post_kernel(x_ref, oa_ref, ob_ref, oc_ref, od_ref, wo_ref, gta_ref, g_ref, sc_ref, sh_ref,
                 gtm_ref, wup_ref, cw_ref, cb_ref, wdn_ref, y_ref, carry_ref):
    d_ff = wdn_ref.shape[1]
    mixed = (_dot(oa_ref[0], wo_ref[0, 0:GW]) + _dot(ob_ref[0], wo_ref[0, GW:2 * GW])
             + _dot(oc_ref[0], wo_ref[0, 2 * GW:3 * GW]) + _dot(od_ref[0], wo_ref[0, 3 * GW:4 * GW]))
    x1 = x_ref[0] + gta_ref[0] * mixed
    ms = jnp.mean(x1 * x1, axis=-1, keepdims=True)
    h = x1 * lax.rsqrt(ms + EPS) * g_ref[...]
    hb = (h * (1.0 + sc_ref[0]) + sh_ref[0]).astype(BF16)

    @pl.when(pl.program_id(1) == 0)
    def _():
        carry_ref[...] = jnp.zeros(carry_ref.shape, F32)

    def up(ch):
        gate = _dot(hb, wup_ref[0, :, ch * FF_CHUNK:(ch + 1) * FF_CHUNK])
        val = _dot(hb, wup_ref[0, :, d_ff + ch * FF_CHUNK:d_ff + (ch + 1) * FF_CHUNK])
        return gate, val

    row = lax.broadcasted_iota(jnp.int32, (TM_POST, FF_CHUNK), 0)
    y = jnp.zeros((TM_POST, x1.shape[1]), F32)
    n_chunks = d_ff // FF_CHUNK
    ahead = up(0)
    for ch in range(n_chunks):
        cols = slice(ch * FF_CHUNK, (ch + 1) * FF_CHUNK)
        gate, val = ahead
        if ch + 1 < n_chunks:
            ahead = up(ch + 1)
        prev = carry_ref[:, cols]
        g1 = jnp.where(row == 0, prev[7:8], pltpu.roll(gate, 1, 0))
        g2 = jnp.where(row == 0, prev[6:7], jnp.where(row == 1, prev[7:8], pltpu.roll(gate, 2, 0)))
        carry_ref[:, cols] = gate[TM_POST - 8:TM_POST]
        conv = (cw_ref[0, 0:1, cols] * g2 + cw_ref[0, 1:2, cols] * g1 + cw_ref[0, 2:3, cols] * gate
                + cb_ref[0, :, cols])
        act = conv * _sigmoid(conv) * val
        y = y + _dot(act.astype(BF16), wdn_ref[0, cols, :])
    y_ref[0] = x1 + gtm_ref[0] * y


def _post(layer, x, oa, ob, oc, od, wo, gta, g, sc, sh, gtm, wup, cw, cb, wdn):
    bsz, s, d = x.shape
    tok = lambda width: pl.BlockSpec((1, TM_POST, width), lambda b, i: (b, i, 0))
    vec = pl.BlockSpec((1, 1, d), lambda b, i: (b, 0, 0))
    of_layer = lambda a: pl.BlockSpec((1,) + a.shape[1:], lambda b, i: (layer,) + (0,) * (a.ndim - 1),
                                      pipeline_mode=pl.Buffered(1))
    return pl.pallas_call(
        _post_kernel,
        out_shape=jax.ShapeDtypeStruct((bsz, s, d), F32),
        grid=(bsz, s // TM_POST),
        in_specs=[tok(d), tok(GW), tok(GW), tok(GW), tok(GW), of_layer(wo), vec, _whole((1, d)), vec, vec,
                  vec, of_layer(wup), of_layer(cw), of_layer(cb), of_layer(wdn)],
        out_specs=tok(d),
        scratch_shapes=[pltpu.VMEM((8, wdn.shape[1]), F32)],
        compiler_params=pltpu.CompilerParams(
            dimension_semantics=("parallel", "arbitrary"), vmem_limit_bytes=VMEM_LIMIT),
        name="out_proj_mlp",
    )(x, oa, ob, oc, od, wo, gta, g, sc, sh, gtm, wup, cw, cb, wdn)


def _repack_w_in(w):
    off = {}
    pos = 0
    for name, size in (("a_q", GW), ("a_k", GW), ("a_v", GW), ("b_q", GW), ("b_k", GW), ("b_v", GW),
                       ("b_f", HPM), ("c_q", GW), ("c_k", GW), ("c_v", GW), ("d_q", GW),
                       ("d_kc", HEAD_DIM), ("d_vc", HEAD_DIM), ("d_ks", HEAD_DIM), ("d_vs", HEAD_DIM),
                       ("d_kw", HEAD_DIM), ("d_vw", HEAD_DIM), ("d_g", 3 * HPM)):
        off[name] = (pos, size)
        pos += size
    assert pos == w.shape[-1]
    col = lambda n: w[..., off[n][0]:off[n][0] + off[n][1]]
    order = ["a_q", "a_k", "a_v", "b_q", "b_k", "b_v", "c_q", "c_k", "c_v", "d_q",
             "d_ks", "d_vs", "d_kw", "d_vw", "d_kc", "d_vc", "b_f", "d_g"]
    parts = [col(n) for n in order] + [jnp.zeros(w.shape[:-1] + (128 - HPM - 3 * HPM,), w.dtype)]
    out = jnp.concatenate(parts, axis=-1).astype(BF16)
    assert out.shape[-1] == N_PROJ
    return out


def _compress_params(pe, phi_w):
    half = CMP_LEN // 2
    pe_rows = jnp.concatenate([pe[0], pe[1]], axis=1).reshape(2, half * 2 * HEAD_DIM)
    wk = phi_w[0].reshape(CMP_LEN, HEAD_DIM, HEAD_DIM)
    wv = phi_w[1].reshape(CMP_LEN, HEAD_DIM, HEAD_DIM)
    zero = jnp.zeros_like(wk)
    blk = jnp.concatenate([jnp.concatenate([wk, zero], axis=2), jnp.concatenate([zero, wv], axis=2)], axis=1)
    return pe_rows, blk.reshape(2, half * 2 * HEAD_DIM, 2 * HEAD_DIM).astype(BF16)


def _score_bounds(rel_bias, gq_a, gk_a, gq_b, gk_b, gq_d, gk_s, gk_w):
    amax = lambda g: jnp.max(jnp.abs(g))
    tb = rel_bias * LOG2E
    bmax, bmin = jnp.max(tb, axis=0), jnp.min(tb, axis=0)
    span = jnp.max(bmax - bmin)
    qk = lambda n, gq, gk, scale: n * amax(gq) * amax(gk) * (scale * LOG2E * BOUND_MARGIN)
    ok = lambda b, sp: (2.0 * b + sp <= MAX_LOG2_SPAN).astype(F32).reshape(1)
    b_fox = qk(HEAD_DIM, gq_b, gk_b, HEAD_DIM ** -0.5)
    b_a = qk(DIFF_QK_DIM, gq_a, gk_a, DIFF_QK_DIM ** -0.5)
    b_s = qk(HEAD_DIM, gq_d, gk_s, HEAD_DIM ** -0.5)
    b_w = qk(HEAD_DIM, gq_d, gk_w, HEAD_DIM ** -0.5)
    vec = jnp.concatenate([
        ok(b_fox, 0.0), b_fox.reshape(1),
        ok(b_a, span), b_a + bmax[:HPM],
        ok(b_s, span), b_s + bmax[HPM:],
        ok(b_w, span), b_w + bmax[HPM:]])
    return jnp.pad(vec, (0, BND_SIZE - vec.shape[0]))


def kernel(x, c, rel_bias, ada_w, ada_b, norm_mix_g, norm_ffn_g, w_in, w_out, diff_qnorm_g, diff_knorm_g, diff_lambda, diff_subln_g, fox_qnorm_g, fox_knorm_g, fox_b_f, nsa_qnorm_g, nsa_knorm_g, nsa_pe, nsa_phi_w, ffn_w_up, ffn_conv_w, ffn_conv_b, ffn_w_down):
    bsz, s, d = x.shape
    depth = ada_w.shape[0]
    assert s % T == 0 and WINDOW % T == 0 and T % SEL_LEN == 0 and d == 4 * GW
    assert T & (T - 1) == 0 and BAND_ROWS // 2 >= (MAX_DISTANCE + CMP_LEN) // CMP_STRIDE
    nk = s // T
    n_chunk = s // CMP_STRIDE
    n_blk = s // SEL_LEN
    d_ff = ffn_w_down.shape[1]
    assert d_ff % FF_CHUNK == 0

    bd32 = jnp.asarray(_block_ones(GW, DIFF_QK_DIM), BF16)
    bd64 = jnp.asarray(_block_ones(GW, HEAD_DIM), BF16)
    ti = np.arange(T)
    tri_incl = jnp.asarray((ti[None, :] <= ti[:, None]).astype(np.float32), BF16)
    tri_later = jnp.asarray((ti[None, :] > ti[:, None]).astype(np.float32), BF16)
    c_start = np.arange(n_chunk) * CMP_STRIDE
    s_start = np.arange(n_blk) * SEL_LEN
    ov = ((c_start[None, :] < s_start[:, None] + SEL_LEN) & (s_start[:, None] < c_start[None, :] + CMP_LEN)
          & (np.arange(n_chunk)[None, :] < n_chunk - 1))
    ovt = jnp.asarray(ov.astype(np.float32), BF16)

    bias_a, bias_d = _bias_tiles(rel_bias)
    far = rel_bias[NUM_BUCKETS - 1] * LOG2E
    mod = _adaln(c, ada_w, ada_b)

    w_re = _repack_w_in(w_in)
    wo_b, wup_b, wdn_b = w_out.astype(BF16), ffn_w_up.astype(BF16), ffn_w_down.astype(BF16)
    conv_w = ffn_conv_w.reshape(depth, CONV_WIDTH, d_ff)
    conv_b = ffn_conv_b.reshape(depth, 1, d_ff)

    for l in range(depth):
        sh_a, sc_a, gt_a, sh_m, sc_m, gt_m = [m.reshape(bsz, 1, d) for m in jnp.split(mod[l], 6, axis=-1)]
        scale_a = DIFF_QK_DIM ** -0.5 * LOG2E
        scale = HEAD_DIM ** -0.5 * LOG2E
        ones = jnp.ones((HEAD_DIM,), F32)
        gains = jnp.stack([
            jnp.tile(diff_qnorm_g[l] * scale_a, GW // DIFF_QK_DIM), jnp.tile(diff_knorm_g[l], GW // DIFF_QK_DIM),
            jnp.tile(fox_qnorm_g[l] * scale, HPM), jnp.tile(fox_knorm_g[l], HPM),
            jnp.tile(nsa_qnorm_g[l] * scale, HPM),
            jnp.concatenate([nsa_knorm_g[l, 1], ones, nsa_knorm_g[l, 2], ones]),
            jnp.ones((GW,), F32), jnp.ones((GW,), F32)])
        (a_qt, a_k, a_vt, b_qt, b_k, b_vt, c_qt, c_k, c_vt, d_qt,
         d_ks, d_kw, d_vst, d_vwt, d_cmp, smalls) = _inproj(
            l, x, norm_mix_g[l].reshape(1, d), sc_a, sh_a, w_re, gains, bd32, bd64)

        bf_row = jnp.pad(fox_b_f[l], (0, 128 - HPM)).reshape(1, 128)
        ck, st = _scalars(smalls, bf_row, tri_incl)

        pe_rows, w_blk = _compress_params(nsa_pe[l], nsa_phi_w[l])
        gk_pad = jnp.pad(nsa_knorm_g[l, 0], (0, 128 - HEAD_DIM)).reshape(1, 128)
        kc, vct = _compress(d_cmp, pe_rows, w_blk, gk_pad)

        bnd = _score_bounds(rel_bias, diff_qnorm_g[l], diff_knorm_g[l], fox_qnorm_g[l], fox_knorm_g[l],
                            nsa_qnorm_g[l], nsa_knorm_g[l, 1], nsa_knorm_g[l, 2])
        lam_init = 0.8 - 0.6 * math.exp(-0.3 * l)
        cst = jnp.full((1,), lam_init, F32)
        k4 = lambda a: a.reshape(bsz, nk, T, a.shape[-1])
        o_a = _diff_attention(far, cst, bnd, a_qt, k4(a_k), a_vt, bias_a, diff_lambda[l],
                              diff_subln_g[l].reshape(HEAD_DIM, 1))
        o_b = _fox_attention(bnd, b_qt, k4(b_k), b_vt, st, k4(ck))
        o_c = _sb_attention(c_qt, k4(c_k), c_vt, tri_later)
        o_d = _nsa_attention(far, bnd, d_qt, kc, vct, k4(d_ks), d_vst, k4(d_kw), d_vwt, st, bias_d, ovt)

        x = _post(l, x, o_a, o_b, o_c, o_d, wo_b, gt_a, norm_ffn_g[l].reshape(1, d),
                  sc_m, sh_m, gt_m, wup_b, conv_w, conv_b, wdn_b)
    return x
```

```python
import math

import numpy as np
import jax
import jax.numpy as jnp
from jax import lax
from jax.experimental import pallas as pl
from jax.experimental.pallas import tpu as pltpu

HEAD_DIM = 64
HPM = 4
GW = HPM * HEAD_DIM
DIFF_QK_DIM = HEAD_DIM // 2
NUM_BUCKETS = 32
MAX_EXACT = NUM_BUCKETS // 2
MAX_DISTANCE = 128
CMP_LEN = 32
CMP_STRIDE = 16
SEL_LEN = 64
N_SEL = 16
WINDOW = 512
FORCED_SCORE = 1.0e4
NEG = -1.0e30
EPS = 1e-6
CONV_WIDTH = 3
LOG2E = 1.4426950408889634

T = 256
TM = T
TM_POST = 256
FF_CHUNK = 256
ONES_ROWS = 16
VROWS = HEAD_DIM + ONES_ROWS
BAND_ROWS = 2 * T // CMP_STRIDE
VMEM_LIMIT = 56 * 1024 * 1024

BND_FOX, BND_DIFF, BND_SEL, BND_WIN, BND_SIZE = 0, 2, 7, 12, 24
BOUND_MARGIN = 1.02
MAX_LOG2_SPAN = 100.0

F32 = jnp.float32
BF16 = jnp.bfloat16


def _dot(a, b):
    return jnp.dot(a, b, preferred_element_type=F32)


def _split2(x):
    hi = x.astype(BF16)
    lo = (x - hi.astype(F32)).astype(BF16)
    return hi, lo


def _log_sigmoid(z):
    return jnp.minimum(z, 0.0) - jnp.log(1.0 + jnp.exp(-jnp.abs(z)))


def _sigmoid(z):
    return 1.0 / (1.0 + jnp.exp(-z))


def _whole(shape):
    nd = len(shape)
    return pl.BlockSpec(shape, lambda *_: (0,) * nd)


def _resident(shape):
    nd = len(shape)
    return pl.BlockSpec(shape, lambda *_: (0,) * nd, pipeline_mode=pl.Buffered(1))


def _smem():
    return pl.BlockSpec(memory_space=pltpu.SMEM)


def _t5_bucket_np(dist):
    n = np.maximum(dist, 0)
    ratio = math.log(MAX_DISTANCE / MAX_EXACT)
    out = None
    for dt in (np.float32, np.float64):
        large = MAX_EXACT + (np.log(np.maximum(n, 1).astype(dt) / dt(MAX_EXACT)) / dt(ratio)
                             * dt(NUM_BUCKETS - MAX_EXACT)).astype(np.int32)
        b = np.where(n < MAX_EXACT, n, np.minimum(large, NUM_BUCKETS - 1)).astype(np.int32)
        assert out is None or np.array_equal(out, b)
        out = b
    return out


def _bucket_tiles():
    r = np.arange(T)[:, None]
    c = np.arange(T)[None, :]
    diag = _t5_bucket_np(c - r)
    near = _t5_bucket_np(T + c - r)
    rb = np.arange(BAND_ROWS)[:, None]
    band = _t5_bucket_np(c - CMP_STRIDE * rb + (T - CMP_LEN + 1))
    return np.concatenate([diag, near, band], axis=0)


def _block_ones(n, group):
    i = np.arange(n)
    return (i[:, None] // group == i[None, :] // group).astype(np.float32)


def _bias_kernel(tbl_ref, bkt_ref, a_ref, d_ref):
    b = bkt_ref[...]
    for h in range(2 * HPM):
        acc = jnp.zeros(b.shape, F32)
        for k in range(NUM_BUCKETS):
            acc = jnp.where(b == k, tbl_ref[k, h] * LOG2E, acc)
        if h < HPM:
            a_ref[h] = acc[0:2 * T]
        else:
            d_ref[:, (h - HPM) * T:(h - HPM + 1) * T] = acc


def _bias_tiles(rel_bias):
    bkt = jnp.asarray(_bucket_tiles())
    rows = bkt.shape[0]
    return pl.pallas_call(
        _bias_kernel,
        out_shape=[jax.ShapeDtypeStruct((HPM, 2 * T, T), F32), jax.ShapeDtypeStruct((rows, HPM * T), F32)],
        in_specs=[_smem(), _whole(bkt.shape)],
        out_specs=[_whole((HPM, 2 * T, T)), _whole((rows, HPM * T))],
        name="bias_tiles",
    )(rel_bias, bkt)


def _adaln_kernel(c_ref, w_ref, b_ref, o_ref):
    c = c_ref[...]
    ca = c * _sigmoid(c)
    o_ref[0] = _dot(ca.astype(BF16), w_ref[0].astype(BF16)) + b_ref[0]


def _adaln(c, ada_w, ada_b):
    depth, d, n = ada_w.shape
    bsz = c.shape[0]
    rows = -(-bsz // 8) * 8
    cp = jnp.pad(c, ((0, rows - bsz), (0, 0)))
    tn = 1536
    out = pl.pallas_call(
        _adaln_kernel,
        out_shape=jax.ShapeDtypeStruct((depth, rows, n), F32),
        grid=(depth, n // tn),
        in_specs=[pl.BlockSpec((rows, d), lambda l, j: (0, 0)),
                  pl.BlockSpec((1, d, tn), lambda l, j: (l, 0, j)),
                  pl.BlockSpec((1, 1, tn), lambda l, j: (l, 0, j))],
        out_specs=pl.BlockSpec((1, rows, tn), lambda l, j: (l, 0, j)),
        name="adaln",
    )(cp, ada_w, ada_b.reshape(depth, 1, n))
    return out[:, :bsz]


N_PROJ = 11 * GW + 2 * 128


def _inproj_kernel(x_ref, g_ref, sc_ref, sh_ref, w_ref, gains_ref, bd32_ref, bd64_ref,
                   aq_ref, ak_ref, av_ref, bq_ref, bk_ref, bv_ref, cq_ref, ck_ref, cv_ref, dq_ref,
                   dks_ref, dkw_ref, dvs_ref, dvw_ref, cmp_ref, sm_ref):
    x = x_ref[0]
    ms = jnp.mean(x * x, axis=-1, keepdims=True)
    h = x * lax.rsqrt(ms + EPS) * g_ref[...]
    h = h * (1.0 + sc_ref[0]) + sh_ref[0]
    hb = h.astype(BF16)

    def proj(group, width=GW):
        off = group * GW
        return _dot(hb, w_ref[0, :, off:off + width])

    def segnorm(y, bd_ref, inv_n, gain_row):
        ss = _dot((y * y).astype(BF16), bd_ref[...])
        return y * lax.rsqrt(ss * inv_n + EPS) * gains_ref[gain_row:gain_row + 1, :]

    def put_t(ref, y):
        ref[0, 0] = y.T.astype(BF16)

    raw0 = proj(0)
    raw1 = proj(1)
    put_t(aq_ref, segnorm(raw0, bd32_ref, 1.0 / DIFF_QK_DIM, 0))
    put_t(av_ref, proj(2))
    raw3 = proj(3)
    ak_ref[0] = segnorm(raw1, bd32_ref, 1.0 / DIFF_QK_DIM, 1).astype(BF16)
    put_t(bv_ref, proj(5))
    raw4 = proj(4)
    put_t(bq_ref, segnorm(raw3, bd64_ref, 1.0 / HEAD_DIM, 2))
    put_t(cq_ref, proj(6) * (HEAD_DIM ** -0.5 * LOG2E))
    raw9 = proj(9)
    bk_ref[0] = segnorm(raw4, bd64_ref, 1.0 / HEAD_DIM, 3).astype(BF16)
    ck_ref[0] = proj(7).astype(BF16)
    y = proj(10)
    put_t(dq_ref, segnorm(raw9, bd64_ref, 1.0 / HEAD_DIM, 4))
    put_t(cv_ref, proj(8))
    cmp_ref[0] = proj(11, 128)
    sm_ref[0] = _dot(hb, w_ref[0, :, 11 * GW + 128:11 * GW + 256])

    yn = segnorm(y, bd64_ref, 1.0 / HEAD_DIM, 5)
    dks_ref[0] = yn[:, 0:HEAD_DIM].astype(BF16)
    dkw_ref[0] = yn[:, 2 * HEAD_DIM:3 * HEAD_DIM].astype(BF16)
    yt = y.T
    dvs_ref[0, 0] = yt[HEAD_DIM:2 * HEAD_DIM].astype(BF16)
    dvw_ref[0, 0] = yt[3 * HEAD_DIM:4 * HEAD_DIM].astype(BF16)


def _inproj(layer, x, g, sc, sh, w_re, gains, bd32, bd64):
    bsz, s, d = x.shape
    nk = s // T
    tok = lambda width, dt: jax.ShapeDtypeStruct((bsz, s, width), dt)
    tr = lambda rows: jax.ShapeDtypeStruct((bsz, nk, rows, T), BF16)
    tok_spec = lambda width: pl.BlockSpec((1, TM, width), lambda b, i: (b, i, 0))
    tr_spec = lambda rows: pl.BlockSpec((1, 1, rows, T), lambda b, i: (b, i, 0, 0))
    vec = pl.BlockSpec((1, 1, d), lambda b, i: (b, 0, 0))
    out_shape = [tr(GW), tok(GW, BF16), tr(GW)] * 3 + [tr(GW)] + [
        tok(HEAD_DIM, BF16), tok(HEAD_DIM, BF16), tr(HEAD_DIM), tr(HEAD_DIM),
        tok(128, F32), tok(128, F32)]
    out_specs = [tr_spec(GW), tok_spec(GW), tr_spec(GW)] * 3 + [tr_spec(GW)] + [
        tok_spec(HEAD_DIM), tok_spec(HEAD_DIM), tr_spec(HEAD_DIM), tr_spec(HEAD_DIM),
        tok_spec(128), tok_spec(128)]
    return pl.pallas_call(
        _inproj_kernel,
        out_shape=out_shape,
        grid=(bsz, s // TM),
        in_specs=[pl.BlockSpec((1, TM, d), lambda b, i: (b, i, 0)),
                  _whole((1, d)), vec, vec,
                  pl.BlockSpec((1,) + w_re.shape[1:], lambda b, i: (layer, 0, 0)),
                  _whole(gains.shape), _whole(bd32.shape), _whole(bd64.shape)],
        out_specs=out_specs,
        compiler_params=pltpu.CompilerParams(
            dimension_semantics=("parallel", "parallel"), vmem_limit_bytes=VMEM_LIMIT),
        name="inproj",
    )(x, g, sc, sh, w_re, gains, bd32, bd64)


def _scalars_kernel(sm_ref, bf_ref, tri_ref, ck_ref, st_ref):
    s = sm_ref.shape[1]
    carry = jnp.zeros((1, 128), F32)
    for blk in range(s // T):
        rows = slice(blk * T, (blk + 1) * T)
        lf = _log_sigmoid(sm_ref[0, rows, :] + bf_ref[...])
        h1 = lf.astype(BF16)
        r1 = lf - h1.astype(F32)
        h2 = r1.astype(BF16)
        h3 = (r1 - h2.astype(F32)).astype(BF16)
        tri = tri_ref[...]
        cb = _dot(tri, h1) + _dot(tri, h2) + _dot(tri, h3) + carry
        carry = cb[T - 1:T, :]
        cb2 = cb * LOG2E
        ck_ref[0, rows, :] = cb2
        col = lax.broadcasted_iota(jnp.int32, (T, 128), 1)
        comb = jnp.where(col < HPM, cb2, _sigmoid(sm_ref[0, rows, :]))
        st_ref[0, blk] = comb.T[0:16]


def _scalars(smalls, bf_row, tri):
    bsz, s, _ = smalls.shape
    return pl.pallas_call(
        _scalars_kernel,
        out_shape=[jax.ShapeDtypeStruct((bsz, s, 128), F32), jax.ShapeDtypeStruct((bsz, s // T, 16, T), F32)],
        grid=(bsz,),
        in_specs=[pl.BlockSpec((1, s, 128), lambda b: (b, 0, 0)), _whole((1, 128)), _whole((T, T))],
        out_specs=[pl.BlockSpec((1, s, 128), lambda b: (b, 0, 0)),
                   pl.BlockSpec((1, s // T, 16, T), lambda b: (b, 0, 0, 0))],
        compiler_params=pltpu.CompilerParams(dimension_semantics=("parallel",)),
        name="token_scalars",
    )(smalls, bf_row, tri)


def _compress_kernel(x_ref, pe_ref, w_ref, gk_ref, kc_ref, vct_ref):
    n = x_ref.shape[1] // CMP_STRIDE
    chunks = jnp.concatenate(
        [x_ref[0, pl.ds(i, n, stride=CMP_STRIDE), :] for i in range(CMP_STRIDE)], axis=1)
    ya = _dot((chunks + pe_ref[0:1]).astype(BF16), w_ref[0])
    yb = _dot((chunks + pe_ref[1:2]).astype(BF16), w_ref[1])
    y = ya + pltpu.roll(yb, n - 1, 0)
    lane = lax.broadcasted_iota(jnp.int32, y.shape, 1)
    ss = jnp.sum(jnp.where(lane < HEAD_DIM, y * y, 0.0), axis=-1, keepdims=True)
    kc = y * lax.rsqrt(ss * (1.0 / HEAD_DIM) + EPS) * gk_ref[...]
    kc_ref[0] = kc[:, 0:HEAD_DIM].astype(BF16)
    vct_ref[0] = y.T[HEAD_DIM:2 * HEAD_DIM].astype(BF16)


def _compress(cmp, pe_rows, w_blk, gk_pad):
    bsz, s, width = cmp.shape
    n = s // CMP_STRIDE
    return pl.pallas_call(
        _compress_kernel,
        out_shape=[jax.ShapeDtypeStruct((bsz, n, HEAD_DIM), BF16),
                   jax.ShapeDtypeStruct((bsz, HEAD_DIM, n), BF16)],
        grid=(bsz,),
        in_specs=[pl.BlockSpec((1, s, width), lambda b: (b, 0, 0)),
                  _whole(pe_rows.shape), _whole(w_blk.shape), _whole(gk_pad.shape)],
        out_specs=[pl.BlockSpec((1, n, HEAD_DIM), lambda b: (b, 0, 0)),
                   pl.BlockSpec((1, HEAD_DIM, n), lambda b: (b, 0, 0))],
        compiler_params=pltpu.CompilerParams(dimension_semantics=("parallel",)),
        name="nsa_compress",
    )(cmp, pe_rows, w_blk, gk_pad)


def _head_rows(qt, h):
    return qt[h * HEAD_DIM:(h + 1) * HEAD_DIM, :]


def _head_cols(kt, h):
    return kt[:, h * HEAD_DIM:(h + 1) * HEAD_DIM]


def _split_maps(qh):
    qf = qh.astype(F32)
    row = lax.broadcasted_iota(jnp.int32, qf.shape, 0)
    first = jnp.where(row < DIFF_QK_DIM, qf, 0.0).astype(BF16)
    second = jnp.where(row >= DIFF_QK_DIM, qf, 0.0).astype(BF16)
    return jnp.concatenate([first, second], axis=1)


def _with_ones(vt):
    return jnp.concatenate([vt, jnp.ones((ONES_ROWS, vt.shape[1]), BF16)], axis=0)


def _softmax_probs(ss, m_ref, idx, shift=None, fixed=None):
    if fixed is not None:
        already = isinstance(fixed, float) and fixed == 0.0
        return None, [jnp.exp2(s if already else s - fixed).astype(BF16) for s in ss]
    m_old = m_ref[idx]
    mx = jnp.max(ss[0], axis=0, keepdims=True)
    for s in ss[1:]:
        mx = jnp.maximum(mx, jnp.max(s, axis=0, keepdims=True))
    if shift is not None:
        mx = mx + shift
    m_new = jnp.maximum(m_old, mx)
    m_ref[idx] = m_new
    alpha = jnp.exp2(m_old - m_new)
    sub = m_new if shift is None else m_new - shift
    return alpha, [jnp.exp2(s - sub).astype(BF16) for s in ss]


def _accumulate(acc_ref, rows, alpha, vts, ps):
    upd = _dot(vts[0], ps[0])
    for vt, p in zip(vts[1:], ps[1:]):
        upd = upd + _dot(vt, p)
    acc_ref[rows, :] = (acc_ref[rows, :] if alpha is None else alpha * acc_ref[rows, :]) + upd


def _normalized(acc_ref, idx):
    base = idx * VROWS
    return acc_ref[base:base + HEAD_DIM, :] / acc_ref[base + HEAD_DIM:base + HEAD_DIM + 1, :]


def _causal_tile(width=T):
    r = lax.broadcasted_iota(jnp.int32, (T, width), 0)
    c = lax.broadcasted_iota(jnp.int32, (T, width), 1)
    if width != T:
        c = jnp.bitwise_and(c, T - 1)
    return r, c


def _for_tile_groups(qi, process, paired=True):
    n_far = jnp.maximum(qi - 1, 0)
    if not paired:
        def single(j, carry):
            process([(j, "far")])
            return carry

        lax.fori_loop(0, n_far, single, 0)

        @pl.when(qi == 0)
        def _():
            process([(qi, "diag")])

        @pl.when(qi >= 1)
        def _():
            process([(qi - 1, "near"), (qi, "diag")])

        return

    n_pair = lax.shift_right_logical(n_far, 1)
    odd = jnp.bitwise_and(n_far, 1) == 1

    def pair(j, carry):
        process([(2 * j, "far"), (2 * j + 1, "far")])
        return carry

    lax.fori_loop(0, n_pair, pair, 0)

    @pl.when(qi == 0)
    def _():
        process([(qi, "diag")])

    @pl.when((qi >= 1) & jnp.logical_not(odd))
    def _():
        process([(qi - 1, "near"), (qi, "diag")])

    @pl.when((qi >= 1) & odd)
    def _():
        process([(qi - 2, "far"), (qi - 1, "near"), (qi, "diag")])


def _for_independent_tiles(qi, process, group=4):
    assert group == 4
    n_far = jnp.maximum(qi - 1, 0)
    n_group = lax.shift_right_logical(n_far, 2)
    rest = jnp.bitwise_and(n_far, 3)

    def many(j, carry):
        process([(4 * j + i, "far") for i in range(4)])
        return carry

    lax.fori_loop(0, n_group, many, 0)

    @pl.when(qi == 0)
    def _():
        process([(qi, "diag")])

    for n in range(4):
        @pl.when((qi >= 1) & (rest == n))
        def _():
            process([(4 * n_group + i, "far") for i in range(n)] + [(qi - 1, "near"), (qi, "diag")])


def _run_bounded_or_online(bounded_ok, qi, process):
    @pl.when(bounded_ok)
    def _():
        _for_independent_tiles(qi, lambda tiles: process(tiles, True))

    @pl.when(jnp.logical_not(bounded_ok))
    def _():
        _for_tile_groups(qi, lambda tiles: process(tiles, False))


def _attn_call(tile_kernel, name, bsz, nq, in_arrays, in_specs, scratch, out_width=GW):
    def kernel(*refs):
        def query_tile(qi, carry):
            tile_kernel(qi, *refs)
            return carry

        lax.fori_loop(0, nq, query_tile, 0)

    return pl.pallas_call(
        kernel,
        out_shape=jax.ShapeDtypeStruct((bsz, nq * T, out_width), BF16),
        grid=(bsz,),
        in_specs=in_specs,
        out_specs=pl.BlockSpec((1, nq * T, out_width), lambda b: (b, 0, 0)),
        scratch_shapes=scratch,
        compiler_params=pltpu.CompilerParams(
            dimension_semantics=("parallel",), vmem_limit_bytes=VMEM_LIMIT),
        name=name,
    )(*in_arrays)


def _store_query_tile(o_ref, qi, value):
    o_ref[0, pl.ds(pl.multiple_of(qi * T, T), T), :] = value


def _seq_spec(shape):
    nd = len(shape)
    return pl.BlockSpec((1,) + tuple(shape[1:]), lambda b: (b,) + (0,) * (nd - 1))


def _diff_kernel(qi, far_ref, cst_ref, bnd_ref, qt_ref, k_ref, vt_ref, bias_ref, lam_ref, subg_ref, o_ref,
                 m_ref, acc_ref):
    qmaps = [_split_maps(_head_rows(qt_ref[0, qi], h)) for h in range(HPM)]
    m_ref[...] = jnp.full(m_ref.shape, NEG, F32)
    acc_ref[...] = jnp.zeros(acc_ref.shape, F32)
    r, c = _causal_tile()

    def process(tiles, bounded):
        scores = [[_dot(_head_cols(k_ref[0, ki], h), qmaps[h]) for ki, _ in tiles] for h in range(HPM)]
        probs = []
        for idx in range(2 * HPM):
            h = idx // 2
            bound = bnd_ref[BND_DIFF + 1 + h] if bounded else None
            ss = []
            for (ki, mode), sc in zip(tiles, scores[h]):
                s = sc[:, (idx % 2) * T:(idx % 2 + 1) * T]
                if mode == "far":
                    s = s + ((far_ref[h] - bound) if bounded else far_ref[h])
                elif mode == "near":
                    s = s + bias_ref[h, T:2 * T, :]
                else:
                    s = jnp.where(r <= c, s + bias_ref[h, 0:T, :], NEG)
                if bounded and mode != "far":
                    s = s - bound
                ss.append(s)
            probs.append(_softmax_probs(ss, m_ref, idx, fixed=0.0 if bounded else None))
        for idx in range(2 * HPM):
            h = idx // 2
            vts = [_with_ones(vt_ref[0, ki, h * HEAD_DIM:(h + 1) * HEAD_DIM, :]) for ki, _ in tiles]
            alpha, ps = probs[idx]
            _accumulate(acc_ref, slice(idx * VROWS, (idx + 1) * VROWS), alpha, vts, ps)

    _run_bounded_or_online(bnd_ref[BND_DIFF] > 0.5, qi, process)

    lam_init = cst_ref[0]
    lv = lam_ref[...]
    lam = (jnp.exp(jnp.sum(lv[0:1] * lv[1:2], axis=-1, keepdims=True))
           - jnp.exp(jnp.sum(lv[2:3] * lv[3:4], axis=-1, keepdims=True)) + lam_init)
    outs = []
    for h in range(HPM):
        o = _normalized(acc_ref, 2 * h) - lam * _normalized(acc_ref, 2 * h + 1)
        ms = jnp.mean(o * o, axis=0, keepdims=True)
        outs.append(o * lax.rsqrt(ms + EPS) * subg_ref[...] * (1.0 - lam_init))
    _store_query_tile(o_ref, qi, jnp.concatenate(outs, axis=0).T.astype(BF16))


def _diff_attention(far, cst, bnd, qt, k4, vt, bias, lam, subg):
    bsz, nq = qt.shape[0], qt.shape[1]
    return _attn_call(
        _diff_kernel, "diff_attention", bsz, nq,
        [far, cst, bnd, qt, k4, vt, bias, lam, subg],
        [_smem(), _smem(), _smem(), _seq_spec(qt.shape), _seq_spec(k4.shape), _seq_spec(vt.shape),
         _resident(bias.shape), _whole(lam.shape), _whole(subg.shape)],
        [pltpu.VMEM((2 * HPM, 1, T), F32), pltpu.VMEM((2 * HPM * VROWS, T), F32)])


def _fox_kernel(qi, bnd_ref, qt_ref, k_ref, vt_ref, cq_ref, ck_ref, o_ref, m_ref, acc_ref):
    qh = [_head_rows(qt_ref[0, qi], h) for h in range(HPM)]
    m_ref[...] = jnp.full(m_ref.shape, NEG, F32)
    acc_ref[...] = jnp.zeros(acc_ref.shape, F32)
    r, c = _causal_tile()

    def process(tiles, bounded):
        scores = [[_dot(_head_cols(k_ref[0, ki], h), qh[h]) for ki, _ in tiles] for h in range(HPM)]
        probs = []
        for h in range(HPM):
            ss = []
            for (ki, mode), sc in zip(tiles, scores[h]):
                s = sc - ck_ref[0, ki][:, h:h + 1]
                if mode == "diag":
                    s = jnp.where(r <= c, s, NEG)
                ss.append(s)
            cq = cq_ref[0, qi, h:h + 1, :]
            fixed = (bnd_ref[BND_FOX + 1] - cq) if bounded else None
            probs.append(_softmax_probs(ss, m_ref, h, shift=cq, fixed=fixed))
        for h in range(HPM):
            vts = [_with_ones(vt_ref[0, ki, h * HEAD_DIM:(h + 1) * HEAD_DIM, :]) for ki, _ in tiles]
            alpha, ps = probs[h]
            _accumulate(acc_ref, slice(h * VROWS, (h + 1) * VROWS), alpha, vts, ps)

    _run_bounded_or_online(bnd_ref[BND_FOX] > 0.5, qi, process)
    outs = [_normalized(acc_ref, h) for h in range(HPM)]
    _store_query_tile(o_ref, qi, jnp.concatenate(outs, axis=0).T.astype(BF16))


def _fox_attention(bnd, qt, k4, vt, st, ck4):
    bsz, nq = qt.shape[0], qt.shape[1]
    return _attn_call(
        _fox_kernel, "forgetting_attention", bsz, nq,
        [bnd, qt, k4, vt, st, ck4],
        [_smem(), _seq_spec(qt.shape), _seq_spec(k4.shape), _seq_spec(vt.shape),
         _seq_spec(st.shape), _seq_spec(ck4.shape)],
        [pltpu.VMEM((HPM, 1, T), F32), pltpu.VMEM((HPM * VROWS, T), F32)])


def _sb_kernel(qi, qt_ref, k_ref, vt_ref, tri_ref, o_ref, run_ref, acc_ref):
    qh = [_head_rows(qt_ref[0, qi], h) for h in range(HPM)]
    run_ref[...] = jnp.zeros(run_ref.shape, F32)
    acc_ref[...] = jnp.zeros(acc_ref.shape, F32)
    r, c = _causal_tile()

    def process(tiles):
        scores = [[_dot(_head_cols(k_ref[0, ki], h), qh[h]) for ki, _ in tiles] for h in range(HPM)]
        weights = []
        for h in range(HPM):
            run = run_ref[h]
            ws = []
            for (ki, diag), z in zip(tiles, scores[h]):
                log_beta = jnp.minimum(z, 0.0) - jnp.log2(1.0 + jnp.exp2(-jnp.abs(z)))
                log_keep = log_beta - z
                if diag:
                    log_keep = jnp.where(r < c, log_keep, 0.0)
                keep_b = log_keep.astype(BF16)
                tail = _dot(tri_ref[...], keep_b) + run
                a = jnp.exp2(log_beta + tail)
                if diag:
                    a = jnp.where(r < c, a, 0.0)
                ws.append(a.astype(BF16))
                run = tail[0:1, :] + keep_b[0:1, :].astype(F32)
            run_ref[h] = run
            weights.append(ws)
        for h in range(HPM):
            rows = slice(h * HEAD_DIM, (h + 1) * HEAD_DIM)
            upd = None
            for (ki, _), a in zip(tiles, weights[h]):
                term = _dot(vt_ref[0, ki, rows, :], a)
                upd = term if upd is None else upd + term
            acc_ref[rows, :] = acc_ref[rows, :] + upd

    rest = jnp.bitwise_and(qi, 3)
    for n in range(4):
        @pl.when(rest == n)
        def _():
            process([(qi, True)] + [(qi - d, False) for d in range(1, n + 1)])

    base = qi - 1 - rest

    def four(j, carry):
        process([(base - 4 * j - i, False) for i in range(4)])
        return carry

    lax.fori_loop(0, lax.shift_right_logical(qi, 2), four, 0)
    _store_query_tile(o_ref, qi, acc_ref[...].T.astype(BF16))


def _sb_attention(qt, k4, vt, tri):
    bsz, nq = qt.shape[0], qt.shape[1]
    return _attn_call(
        _sb_kernel, "stick_breaking_attention", bsz, nq,
        [qt, k4, vt, tri],
        [_seq_spec(qt.shape), _seq_spec(k4.shape), _seq_spec(vt.shape), _resident(tri.shape)],
        [pltpu.VMEM((HPM, 1, T), F32), pltpu.VMEM((GW, T), F32)])


def _nsa_kernel(qi, far_ref, bnd_ref, qt_ref, kc_ref, vct_ref, ks_ref, vst_ref, kw_ref, vwt_ref, g_ref,
                bias_ref, ovt_ref, o_ref,
                sc_ref, imp_ref, sel_ref, m_ref, acc_ref):
    q0 = qi * T
    n_cmp = kc_ref.shape[1]
    n_blk = ovt_ref.shape[0]
    n_sel = min(N_SEL, n_blk)
    w4 = HPM * T
    qt = qt_ref[0, qi]
    qcat = jnp.concatenate([qt[h * HEAD_DIM:(h + 1) * HEAD_DIM, :] for h in range(HPM)], axis=1)
    far_row = jnp.concatenate([jnp.full((1, T), far_ref[HPM + h], F32) for h in range(HPM)], axis=1)
    r, c = _causal_tile(w4)

    sc_ref[...] = _dot(kc_ref[0], qcat)
    delta = bias_ref[2 * T:2 * T + BAND_ROWS, :] - far_row

    @pl.when(qi == 0)
    def _():
        half = BAND_ROWS // 2
        sc_ref[0:half, :] = sc_ref[0:half, :] + delta[half:BAND_ROWS]

    @pl.when(qi > 0)
    def _():
        c0 = pl.multiple_of(qi * (T // CMP_STRIDE) - BAND_ROWS // 2, 8)
        sc_ref[pl.ds(c0, BAND_ROWS), :] = sc_ref[pl.ds(c0, BAND_ROWS), :] + delta

    ci = lax.broadcasted_iota(jnp.int32, (n_cmp, w4), 0)
    ti = q0 + jnp.bitwise_and(lax.broadcasted_iota(jnp.int32, (n_cmp, w4), 1), T - 1)
    valid_c = ti - CMP_STRIDE * ci - (CMP_LEN - 1) >= 0
    s = jnp.where(valid_c, sc_ref[...] + far_row, NEG)
    e = jnp.exp2(s - jnp.max(s, axis=0, keepdims=True))
    p = jnp.where(valid_c, e / jnp.sum(e, axis=0, keepdims=True), 0.0)
    o_cmp = _dot(vct_ref[0], p.astype(BF16))
    pc_sum = p[:, 0:T]
    for h in range(1, HPM):
        pc_sum = pc_sum + p[:, h * T:(h + 1) * T]

    hi, lo = _split2(pc_sum)
    imp = _dot(ovt_ref[...], hi) + _dot(ovt_ref[...], lo)
    ji = lax.broadcasted_iota(jnp.int32, (n_blk, T), 0)
    tq = q0 + lax.broadcasted_iota(jnp.int32, (n_blk, T), 1)
    forced = (ji == lax.shift_right_logical(tq, int(math.log2(SEL_LEN)))) | (ji == 0)
    imp = jnp.where(forced, FORCED_SCORE, imp)
    imp = jnp.where(ji * SEL_LEN <= tq, imp, -1.0)
    imp_ref[...] = imp

    def rank_body(jp, cnt):
        row = imp_ref[pl.ds(jp, 1), :]
        tie = jnp.where(row == imp, 1.0, 0.0) * jnp.where(ji > jp, 1.0, 0.0)
        return cnt + jnp.where(row > imp, 1.0, 0.0) + tie

    n_seen = jnp.minimum((qi + 1) * (T // SEL_LEN), n_blk)
    cnt = lax.fori_loop(0, n_seen, rank_body, jnp.zeros((n_blk, T), F32))
    sel = jnp.where(cnt < float(n_sel), 1.0, 0.0)
    sel_ref[...] = jnp.concatenate([sel] * HPM, axis=1)

    m_ref[...] = jnp.full(m_ref.shape, NEG, F32)
    acc_ref[...] = jnp.zeros(acc_ref.shape, F32)

    def bound_row(base):
        return jnp.concatenate([jnp.full((1, T), bnd_ref[base + 1 + h], F32) for h in range(HPM)], axis=1)

    def biased(s, mode, bound):
        if mode == "far":
            return s + (far_row if bound is None else far_row - bound)
        tile = bias_ref[T:2 * T, :] if mode == "near" else bias_ref[0:T, :]
        return s + tile if bound is None else s + tile - bound

    def update(state, ss, vts, bounded):
        alpha, ps = _softmax_probs(ss, m_ref, state, fixed=0.0 if bounded else None)
        _accumulate(acc_ref, slice(state * VROWS, (state + 1) * VROWS), alpha, vts, ps)

    def sel_mask(ki):
        per = T // SEL_LEN
        rows = [jnp.broadcast_to(sel_ref[pl.ds(ki * per + i, 1), :], (SEL_LEN, w4)) for i in range(per)]
        return jnp.concatenate(rows, axis=0) > 0.5

    def process_selected(tiles, bounded):
        bound = bound_row(BND_SEL) if bounded else None
        scores = [_dot(ks_ref[0, ki], qcat) for ki, _ in tiles]
        ss = []
        for (ki, mode), sc in zip(tiles, scores):
            mask = sel_mask(ki)
            if mode == "diag":
                mask = mask & (r <= c)
            ss.append(jnp.where(mask, biased(sc, mode, bound), NEG))
        update(0, ss, [_with_ones(vst_ref[0, ki]) for ki, _ in tiles], bounded)

    _run_bounded_or_online(bnd_ref[BND_SEL] > 0.5, qi, process_selected)

    def process_window(tiles, bounded):
        bound = bound_row(BND_WIN) if bounded else None
        scores = [_dot(kw_ref[0, ki], qcat) for ki, _, _ in tiles]
        ss = []
        for (ki, mode, mask), sc in zip(tiles, scores):
            s = biased(sc, mode, bound)
            ss.append(s if mask is None else jnp.where(mask, s, NEG))
        update(1, ss, [_with_ones(vwt_ref[0, ki]) for ki, _, _ in tiles], bounded)

    back = WINDOW // T

    def window_tiles(n):
        tiles = [(qi, "diag", r <= c)]
        for d in range(1, n + 1):
            mode = "near" if d == 1 else "far"
            tiles.append((qi - d, mode, (c < r) if d == back else None))
        return tiles

    win_bounded = bnd_ref[BND_WIN] > 0.5
    for n in range(back + 1):
        cond = (qi == n) if n < back else (qi >= back)

        @pl.when(cond & win_bounded)
        def _():
            process_window(window_tiles(n), True)

        @pl.when(cond & jnp.logical_not(win_bounded))
        def _():
            process_window(window_tiles(n), False)

    o_sel = acc_ref[0:HEAD_DIM, :] / acc_ref[HEAD_DIM:HEAD_DIM + 1, :]
    o_win = acc_ref[VROWS:VROWS + HEAD_DIM, :] / acc_ref[VROWS + HEAD_DIM:VROWS + HEAD_DIM + 1, :]
    gate = lambda br: jnp.concatenate(
        [g_ref[0, qi, HPM + br * HPM + h:HPM + br * HPM + h + 1, :] for h in range(HPM)], axis=1)
    o = gate(0) * o_cmp + gate(1) * o_sel + gate(2) * o_win
    ot = jnp.concatenate([o[:, h * T:(h + 1) * T] for h in range(HPM)], axis=0)
    _store_query_tile(o_ref, qi, ot.T.astype(BF16))


def _nsa_attention(far, bnd, qt, kc, vct, ks4, vst, kw4, vwt, st, bias, ovt):
    bsz, nq = qt.shape[0], qt.shape[1]
    n_cmp = kc.shape[1]
    n_blk = ovt.shape[0]
    w4 = HPM * T
    return _attn_call(
        _nsa_kernel, "native_sparse_attention", bsz, nq,
        [far, bnd, qt, kc, vct, ks4, vst, kw4, vwt, st, bias, ovt],
        [_smem(), _smem(), _seq_spec(qt.shape), _seq_spec(kc.shape), _seq_spec(vct.shape),
         _seq_spec(ks4.shape), _seq_spec(vst.shape), _seq_spec(kw4.shape), _seq_spec(vwt.shape),
         _seq_spec(st.shape), _resident(bias.shape), _whole(ovt.shape)],
        [pltpu.VMEM((n_cmp, w4), F32), pltpu.VMEM((n_blk, T), F32), pltpu.VMEM((n_blk, w4), F32),
         pltpu.VMEM((2, 1, w4), F32), pltpu.VMEM((2 * VROWS, w4), F32)])


def _post_kernel(x_ref, oa_ref, ob_ref, oc_ref, od_ref, wo_ref, gta_ref, g_ref, sc_ref, sh_ref,
                 gtm_ref, wup_ref, cw_ref, cb_ref, wdn_ref, y_ref, carry_ref):
    d_ff = wdn_ref.shape[1]
    mixed = (_dot(oa_ref[0], wo_ref[0, 0:GW]) + _dot(ob_ref[0], wo_ref[0, GW:2 * GW])
             + _dot(oc_ref[0], wo_ref[0, 2 * GW:3 * GW]) + _dot(od_ref[0], wo_ref[0, 3 * GW:4 * GW]))
    x1 = x_ref[0] + gta_ref[0] * mixed
    ms = jnp.mean(x1 * x1, axis=-1, keepdims=True)
    h = x1 * lax.rsqrt(ms + EPS) * g_ref[...]
    hb = (h * (1.0 + sc_ref[0]) + sh_ref[0]).astype(BF16)

    @pl.when(pl.program_id(1) == 0)
    def _():
        carry_ref[...] = jnp.zeros(carry_ref.shape, F32)

    def up(ch):
        gate = _dot(hb, wup_ref[0, :, ch * FF_CHUNK:(ch + 1) * FF_CHUNK])
        val = _dot(hb, wup_ref[0, :, d_ff + ch * FF_CHUNK:d_ff + (ch + 1) * FF_CHUNK])
        return gate, val

    row = lax.broadcasted_iota(jnp.int32, (TM_POST, FF_CHUNK), 0)
    y = jnp.zeros((TM_POST, x1.shape[1]), F32)
    n_chunks = d_ff // FF_CHUNK
    ahead = up(0)
    for ch in range(n_chunks):
        cols = slice(ch * FF_CHUNK, (ch + 1) * FF_CHUNK)
        gate, val = ahead
        if ch + 1 < n_chunks:
            ahead = up(ch + 1)
        prev = carry_ref[:, cols]
        g1 = jnp.where(row == 0, prev[7:8], pltpu.roll(gate, 1, 0))
        g2 = jnp.where(row == 0, prev[6:7], jnp.where(row == 1, prev[7:8], pltpu.roll(gate, 2, 0)))
        carry_ref[:, cols] = gate[TM_POST - 8:TM_POST]
        conv = (cw_ref[0, 0:1, cols] * g2 + cw_ref[0, 1:2, cols] * g1 + cw_ref[0, 2:3, cols] * gate
                + cb_ref[0, :, cols])
        act = conv * _sigmoid(conv) * val
        y = y + _dot(act.astype(BF16), wdn_ref[0, cols, :])
    y_ref[0] = x1 + gtm_ref[0] * y


def _post(layer, x, oa, ob, oc, od, wo, gta, g, sc, sh, gtm, wup, cw, cb, wdn):
    bsz, s, d = x.shape
    tok = lambda width: pl.BlockSpec((1, TM_POST, width), lambda b, i: (b, i, 0))
    vec = pl.BlockSpec((1, 1, d), lambda b, i: (b, 0, 0))
    of_layer = lambda a: pl.BlockSpec((1,) + a.shape[1:], lambda b, i: (layer,) + (0,) * (a.ndim - 1),
                                      pipeline_mode=pl.Buffered(1))
    return pl.pallas_call(
        _post_kernel,
        out_shape=jax.ShapeDtypeStruct((bsz, s, d), F32),
        grid=(bsz, s // TM_POST),
        in_specs=[tok(d), tok(GW), tok(GW), tok(GW), tok(GW), of_layer(wo), vec, _whole((1, d)), vec, vec,
                  vec, of_layer(wup), of_layer(cw), of_layer(cb), of_layer(wdn)],
        out_specs=tok(d),
        scratch_shapes=[pltpu.VMEM((8, wdn.shape[1]), F32)],
        compiler_params=pltpu.CompilerParams(
            dimension_semantics=("parallel", "arbitrary"), vmem_limit_bytes=VMEM_LIMIT),
        name="out_proj_mlp",
    )(x, oa, ob, oc, od, wo, gta, g, sc, sh, gtm, wup, cw, cb, wdn)


def _repack_w_in(w):
    off = {}
    pos = 0
    for name, size in (("a_q", GW), ("a_k", GW), ("a_v", GW), ("b_q", GW), ("b_k", GW), ("b_v", GW),
                       ("b_f", HPM), ("c_q", GW), ("c_k", GW), ("c_v", GW), ("d_q", GW),
                       ("d_kc", HEAD_DIM), ("d_vc", HEAD_DIM), ("d_ks", HEAD_DIM), ("d_vs", HEAD_DIM),
                       ("d_kw", HEAD_DIM), ("d_vw", HEAD_DIM), ("d_g", 3 * HPM)):
        off[name] = (pos, size)
        pos += size
    assert pos == w.shape[-1]
    col = lambda n: w[..., off[n][0]:off[n][0] + off[n][1]]
    order = ["a_q", "a_k", "a_v", "b_q", "b_k", "b_v", "c_q", "c_k", "c_v", "d_q",
             "d_ks", "d_vs", "d_kw", "d_vw", "d_kc", "d_vc", "b_f", "d_g"]
    parts = [col(n) for n in order] + [jnp.zeros(w.shape[:-1] + (128 - HPM - 3 * HPM,), w.dtype)]
    out = jnp.concatenate(parts, axis=-1).astype(BF16)
    assert out.shape[-1] == N_PROJ
    return out


def _compress_params(pe, phi_w):
    half = CMP_LEN // 2
    pe_rows = jnp.concatenate([pe[0], pe[1]], axis=1).reshape(2, half * 2 * HEAD_DIM)
    wk = phi_w[0].reshape(CMP_LEN, HEAD_DIM, HEAD_DIM)
    wv = phi_w[1].reshape(CMP_LEN, HEAD_DIM, HEAD_DIM)
    zero = jnp.zeros_like(wk)
    blk = jnp.concatenate([jnp.concatenate([wk, zero], axis=2), jnp.concatenate([zero, wv], axis=2)], axis=1)
    return pe_rows, blk.reshape(2, half * 2 * HEAD_DIM, 2 * HEAD_DIM).astype(BF16)


def _score_bounds(rel_bias, gq_a, gk_a, gq_b, gk_b, gq_d, gk_s, gk_w):
    amax = lambda g: jnp.max(jnp.abs(g))
    tb = rel_bias * LOG2E
    bmax, bmin = jnp.max(tb, axis=0), jnp.min(tb, axis=0)
    span = jnp.max(bmax - bmin)
    qk = lambda n, gq, gk, scale: n * amax(gq) * amax(gk) * (scale * LOG2E * BOUND_MARGIN)
    ok = lambda b, sp: (2.0 * b + sp <= MAX_LOG2_SPAN).astype(F32).reshape(1)
    b_fox = qk(HEAD_DIM, gq_b, gk_b, HEAD_DIM ** -0.5)
    b_a = qk(DIFF_QK_DIM, gq_a, gk_a, DIFF_QK_DIM ** -0.5)
    b_s = qk(HEAD_DIM, gq_d, gk_s, HEAD_DIM ** -0.5)
    b_w = qk(HEAD_DIM, gq_d, gk_w, HEAD_DIM ** -0.5)
    vec = jnp.concatenate([
        ok(b_fox, 0.0), b_fox.reshape(1),
        ok(b_a, span), b_a + bmax[:HPM],
        ok(b_s, span), b_s + bmax[HPM:],
        ok(b_w, span), b_w + bmax[HPM:]])
    return jnp.pad(vec, (0, BND_SIZE - vec.shape[0]))


def kernel(x, c, rel_bias, ada_w, ada_b, norm_mix_g, norm_ffn_g, w_in, w_out, diff_qnorm_g, diff_knorm_g, diff_lambda, diff_subln_g, fox_qnorm_g, fox_knorm_g, fox_b_f, nsa_qnorm_g, nsa_knorm_g, nsa_pe, nsa_phi_w, ffn_w_up, ffn_conv_w, ffn_conv_b, ffn_w_down):
    bsz, s, d = x.shape
    depth = ada_w.shape[0]
    assert s % T == 0 and WINDOW % T == 0 and T % SEL_LEN == 0 and d == 4 * GW
    assert T & (T - 1) == 0 and BAND_ROWS // 2 >= (MAX_DISTANCE + CMP_LEN) // CMP_STRIDE
    nk = s // T
    n_chunk = s // CMP_STRIDE
    n_blk = s // SEL_LEN
    d_ff = ffn_w_down.shape[1]
    assert d_ff % FF_CHUNK == 0

    bd32 = jnp.asarray(_block_ones(GW, DIFF_QK_DIM), BF16)
    bd64 = jnp.asarray(_block_ones(GW, HEAD_DIM), BF16)
    ti = np.arange(T)
    tri_incl = jnp.asarray((ti[None, :] <= ti[:, None]).astype(np.float32), BF16)
    tri_later = jnp.asarray((ti[None, :] > ti[:, None]).astype(np.float32), BF16)
    c_start = np.arange(n_chunk) * CMP_STRIDE
    s_start = np.arange(n_blk) * SEL_LEN
    ov = ((c_start[None, :] < s_start[:, None] + SEL_LEN) & (s_start[:, None] < c_start[None, :] + CMP_LEN)
          & (np.arange(n_chunk)[None, :] < n_chunk - 1))
    ovt = jnp.asarray(ov.astype(np.float32), BF16)

    bias_a, bias_d = _bias_tiles(rel_bias)
    far = rel_bias[NUM_BUCKETS - 1] * LOG2E
    mod = _adaln(c, ada_w, ada_b)

    w_re = _repack_w_in(w_in)
    wo_b, wup_b, wdn_b = w_out.astype(BF16), ffn_w_up.astype(BF16), ffn_w_down.astype(BF16)
    conv_w = ffn_conv_w.reshape(depth, CONV_WIDTH, d_ff)
    conv_b = ffn_conv_b.reshape(depth, 1, d_ff)

    for l in range(depth):
        sh_a, sc_a, gt_a, sh_m, sc_m, gt_m = [m.reshape(bsz, 1, d) for m in jnp.split(mod[l], 6, axis=-1)]
        scale_a = DIFF_QK_DIM ** -0.5 * LOG2E
        scale = HEAD_DIM ** -0.5 * LOG2E
        ones = jnp.ones((HEAD_DIM,), F32)
        gains = jnp.stack([
            jnp.tile(diff_qnorm_g[l] * scale_a, GW // DIFF_QK_DIM), jnp.tile(diff_knorm_g[l], GW // DIFF_QK_DIM),
            jnp.tile(fox_qnorm_g[l] * scale, HPM), jnp.tile(fox_knorm_g[l], HPM),
            jnp.tile(nsa_qnorm_g[l] * scale, HPM),
            jnp.concatenate([nsa_knorm_g[l, 1], ones, nsa_knorm_g[l, 2], ones]),
            jnp.ones((GW,), F32), jnp.ones((GW,), F32)])
        (a_qt, a_k, a_vt, b_qt, b_k, b_vt, c_qt, c_k, c_vt, d_qt,
         d_ks, d_kw, d_vst, d_vwt, d_cmp, smalls) = _inproj(
            l, x, norm_mix_g[l].reshape(1, d), sc_a, sh_a, w_re, gains, bd32, bd64)

        bf_row = jnp.pad(fox_b_f[l], (0, 128 - HPM)).reshape(1, 128)
        ck, st = _scalars(smalls, bf_row, tri_incl)

        pe_rows, w_blk = _compress_params(nsa_pe[l], nsa_phi_w[l])
        gk_pad = jnp.pad(nsa_knorm_g[l, 0], (0, 128 - HEAD_DIM)).reshape(1, 128)
        kc, vct = _compress(d_cmp, pe_rows, w_blk, gk_pad)

        bnd = _score_bounds(rel_bias, diff_qnorm_g[l], diff_knorm_g[l], fox_qnorm_g[l], fox_knorm_g[l],
                            nsa_qnorm_g[l], nsa_knorm_g[l, 1], nsa_knorm_g[l, 2])
        lam_init = 0.8 - 0.6 * math.exp(-0.3 * l)
        cst = jnp.full((1,), lam_init, F32)
        k4 = lambda a: a.reshape(bsz, nk, T, a.shape[-1])
        o_a = _diff_attention(far, cst, bnd, a_qt, k4(a_k), a_vt, bias_a, diff_lambda[l],
                              diff_subln_g[l].reshape(HEAD_DIM, 1))
        o_b = _fox_attention(bnd, b_qt, k4(b_k), b_vt, st, k4(ck))
        o_c = _sb_attention(c_qt, k4(c_k), c_vt, tri_later)
        o_d = _nsa_attention(far, bnd, d_qt, kc, vct, k4(d_ks), d_vst, k4(d_kw), d_vwt, st, bias_d, ovt)

        x = _post(l, x, o_a, o_b, o_c, o_d, wo_b, gt_a, norm_ffn_g[l].reshape(1, d),
                  sc_m, sh_m, gt_m, wup_b, conv_w, conv_b, wdn_b)
    return x
```

```python
import math

import numpy as np
import jax
import jax.numpy as jnp
from jax import lax
from jax.experimental import pallas as pl
from jax.experimental.pallas import tpu as pltpu

HEAD_DIM = 64
HPM = 4
GW = HPM * HEAD_DIM
DIFF_QK_DIM = HEAD_DIM // 2
NUM_BUCKETS = 32
MAX_EXACT = NUM_BUCKETS // 2
MAX_DISTANCE = 128
CMP_LEN = 32
CMP_STRIDE = 16
SEL_LEN = 64
N_SEL = 16
WINDOW = 512
FORCED_SCORE = 1.0e4
NEG = -1.0e30
EPS = 1e-6
CONV_WIDTH = 3
LOG2E = 1.4426950408889634

T = 256
TM = T
TM_POST = 256
FF_CHUNK = 256
ONES_ROWS = 16
VROWS = HEAD_DIM + ONES_ROWS
BAND_ROWS = 2 * T // CMP_STRIDE
VMEM_LIMIT = 56 * 1024 * 1024

BND_FOX, BND_DIFF, BND_SEL, BND_WIN, BND_SIZE = 0, 2, 7, 12, 24
BOUND_MARGIN = 1.02
MAX_LOG2_SPAN = 100.0

F32 = jnp.float32
BF16 = jnp.bfloat16


def _dot(a, b):
    return jnp.dot(a, b, preferred_element_type=F32)


def _split2(x):
    hi = x.astype(BF16)
    lo = (x - hi.astype(F32)).astype(BF16)
    return hi, lo


def _log_sigmoid(z):
    return jnp.minimum(z, 0.0) - jnp.log(1.0 + jnp.exp(-jnp.abs(z)))


def _sigmoid(z):
    return 1.0 / (1.0 + jnp.exp(-z))


def _whole(shape):
    nd = len(shape)
    return pl.BlockSpec(shape, lambda *_: (0,) * nd)


def _resident(shape):
    nd = len(shape)
    return pl.BlockSpec(shape, lambda *_: (0,) * nd, pipeline_mode=pl.Buffered(1))


def _smem():
    return pl.BlockSpec(memory_space=pltpu.SMEM)


def _t5_bucket_np(dist):
    n = np.maximum(dist, 0)
    ratio = math.log(MAX_DISTANCE / MAX_EXACT)
    out = None
    for dt in (np.float32, np.float64):
        large = MAX_EXACT + (np.log(np.maximum(n, 1).astype(dt) / dt(MAX_EXACT)) / dt(ratio)
                             * dt(NUM_BUCKETS - MAX_EXACT)).astype(np.int32)
        b = np.where(n < MAX_EXACT, n, np.minimum(large, NUM_BUCKETS - 1)).astype(np.int32)
        assert out is None or np.array_equal(out, b)
        out = b
    return out


def _bucket_tiles():
    r = np.arange(T)[:, None]
    c = np.arange(T)[None, :]
    diag = _t5_bucket_np(c - r)
    near = _t5_bucket_np(T + c - r)
    rb = np.arange(BAND_ROWS)[:, None]
    band = _t5_bucket_np(c - CMP_STRIDE * rb + (T - CMP_LEN + 1))
    return np.concatenate([diag, near, band], axis=0)


def _block_ones(n, group):
    i = np.arange(n)
    return (i[:, None] // group == i[None, :] // group).astype(np.float32)


def _bias_kernel(tbl_ref, bkt_ref, a_ref, d_ref):
    b = bkt_ref[...]
    for h in range(2 * HPM):
        acc = jnp.zeros(b.shape, F32)
        for k in range(NUM_BUCKETS):
            acc = jnp.where(b == k, tbl_ref[k, h] * LOG2E, acc)
        if h < HPM:
            a_ref[h] = acc[0:2 * T]
        else:
            d_ref[:, (h - HPM) * T:(h - HPM + 1) * T] = acc


def _bias_tiles(rel_bias):
    bkt = jnp.asarray(_bucket_tiles())
    rows = bkt.shape[0]
    return pl.pallas_call(
        _bias_kernel,
        out_shape=[jax.ShapeDtypeStruct((HPM, 2 * T, T), F32), jax.ShapeDtypeStruct((rows, HPM * T), F32)],
        in_specs=[_smem(), _whole(bkt.shape)],
        out_specs=[_whole((HPM, 2 * T, T)), _whole((rows, HPM * T))],
        name="bias_tiles",
    )(rel_bias, bkt)


def _adaln_kernel(c_ref, w_ref, b_ref, o_ref):
    c = c_ref[...]
    ca = c * _sigmoid(c)
    o_ref[0] = _dot(ca.astype(BF16), w_ref[0].astype(BF16)) + b_ref[0]


def _adaln(c, ada_w, ada_b):
    depth, d, n = ada_w.shape
    bsz = c.shape[0]
    rows = -(-bsz // 8) * 8
    cp = jnp.pad(c, ((0, rows - bsz), (0, 0)))
    tn = 1536
    out = pl.pallas_call(
        _adaln_kernel,
        out_shape=jax.ShapeDtypeStruct((depth, rows, n), F32),
        grid=(depth, n // tn),
        in_specs=[pl.BlockSpec((rows, d), lambda l, j: (0, 0)),
                  pl.BlockSpec((1, d, tn), lambda l, j: (l, 0, j)),
                  pl.BlockSpec((1, 1, tn), lambda l, j: (l, 0, j))],
        out_specs=pl.BlockSpec((1, rows, tn), lambda l, j: (l, 0, j)),
        name="adaln",
    )(cp, ada_w, ada_b.reshape(depth, 1, n))
    return out[:, :bsz]


N_PROJ = 11 * GW + 2 * 128


def _inproj_kernel(x_ref, g_ref, sc_ref, sh_ref, w_ref, gains_ref, bd32_ref, bd64_ref,
                   aq_ref, ak_ref, av_ref, bq_ref, bk_ref, bv_ref, cq_ref, ck_ref, cv_ref, dq_ref,
                   dks_ref, dkw_ref, dvs_ref, dvw_ref, cmp_ref, sm_ref):
    x = x_ref[0]
    ms = jnp.mean(x * x, axis=-1, keepdims=True)
    h = x * lax.rsqrt(ms + EPS) * g_ref[...]
    h = h * (1.0 + sc_ref[0]) + sh_ref[0]
    hb = h.astype(BF16)

    def proj(group, width=GW):
        off = group * GW
        return _dot(hb, w_ref[0, :, off:off + width])

    def segnorm(y, bd_ref, inv_n, gain_row):
        ss = _dot((y * y).astype(BF16), bd_ref[...])
        return y * lax.rsqrt(ss * inv_n + EPS) * gains_ref[gain_row:gain_row + 1, :]

    def put_t(ref, y):
        ref[0, 0] = y.T.astype(BF16)

    raw0 = proj(0)
    raw1 = proj(1)
    put_t(aq_ref, segnorm(raw0, bd32_ref, 1.0 / DIFF_QK_DIM, 0))
    put_t(av_ref, proj(2))
    raw3 = proj(3)
    ak_ref[0] = segnorm(raw1, bd32_ref, 1.0 / DIFF_QK_DIM, 1).astype(BF16)
    put_t(bv_ref, proj(5))
    raw4 = proj(4)
    put_t(bq_ref, segnorm(raw3, bd64_ref, 1.0 / HEAD_DIM, 2))
    put_t(cq_ref, proj(6) * (HEAD_DIM ** -0.5 * LOG2E))
    raw9 = proj(9)
    bk_ref[0] = segnorm(raw4, bd64_ref, 1.0 / HEAD_DIM, 3).astype(BF16)
    ck_ref[0] = proj(7).astype(BF16)
    y = proj(10)
    put_t(dq_ref, segnorm(raw9, bd64_ref, 1.0 / HEAD_DIM, 4))
    put_t(cv_ref, proj(8))
    cmp_ref[0] = proj(11, 128)
    sm_ref[0] = _dot(hb, w_ref[0, :, 11 * GW + 128:11 * GW + 256])

    yn = segnorm(y, bd64_ref, 1.0 / HEAD_DIM, 5)
    dks_ref[0] = yn[:, 0:HEAD_DIM].astype(BF16)
    dkw_ref[0] = yn[:, 2 * HEAD_DIM:3 * HEAD_DIM].astype(BF16)
    yt = y.T
    dvs_ref[0, 0] = yt[HEAD_DIM:2 * HEAD_DIM].astype(BF16)
    dvw_ref[0, 0] = yt[3 * HEAD_DIM:4 * HEAD_DIM].astype(BF16)


def _inproj(layer, x, g, sc, sh, w_re, gains, bd32, bd64):
    bsz, s, d = x.shape
    nk = s // T
    tok = lambda width, dt: jax.ShapeDtypeStruct((bsz, s, width), dt)
    tr = lambda rows: jax.ShapeDtypeStruct((bsz, nk, rows, T), BF16)
    tok_spec = lambda width: pl.BlockSpec((1, TM, width), lambda b, i: (b, i, 0))
    tr_spec = lambda rows: pl.BlockSpec((1, 1, rows, T), lambda b, i: (b, i, 0, 0))
    vec = pl.BlockSpec((1, 1, d), lambda b, i: (b, 0, 0))
    out_shape = [tr(GW), tok(GW, BF16), tr(GW)] * 3 + [tr(GW)] + [
        tok(HEAD_DIM, BF16), tok(HEAD_DIM, BF16), tr(HEAD_DIM), tr(HEAD_DIM),
        tok(128, F32), tok(128, F32)]
    out_specs = [tr_spec(GW), tok_spec(GW), tr_spec(GW)] * 3 + [tr_spec(GW)] + [
        tok_spec(HEAD_DIM), tok_spec(HEAD_DIM), tr_spec(HEAD_DIM), tr_spec(HEAD_DIM),
        tok_spec(128), tok_spec(128)]
    return pl.pallas_call(
        _inproj_kernel,
        out_shape=out_shape,
        grid=(bsz, s // TM),
        in_specs=[pl.BlockSpec((1, TM, d), lambda b, i: (b, i, 0)),
                  _whole((1, d)), vec, vec,
                  pl.BlockSpec((1,) + w_re.shape[1:], lambda b, i: (layer, 0, 0)),
                  _whole(gains.shape), _whole(bd32.shape), _whole(bd64.shape)],
        out_specs=out_specs,
        compiler_params=pltpu.CompilerParams(
            dimension_semantics=("parallel", "parallel"), vmem_limit_bytes=VMEM_LIMIT),
        name="inproj",
    )(x, g, sc, sh, w_re, gains, bd32, bd64)


def _scalars_kernel(sm_ref, bf_ref, tri_ref, ck_ref, st_ref):
    s = sm_ref.shape[1]
    carry = jnp.zeros((1, 128), F32)
    for blk in range(s // T):
        rows = slice(blk * T, (blk + 1) * T)
        lf = _log_sigmoid(sm_ref[0, rows, :] + bf_ref[...])
        h1 = lf.astype(BF16)
        r1 = lf - h1.astype(F32)
        h2 = r1.astype(BF16)
        h3 = (r1 - h2.astype(F32)).astype(BF16)
        tri = tri_ref[...]
        cb = _dot(tri, h1) + _dot(tri, h2) + _dot(tri, h3) + carry
        carry = cb[T - 1:T, :]
        cb2 = cb * LOG2E
        ck_ref[0, rows, :] = cb2
        col = lax.broadcasted_iota(jnp.int32, (T, 128), 1)
        comb = jnp.where(col < HPM, cb2, _sigmoid(sm_ref[0, rows, :]))
        st_ref[0, blk] = comb.T[0:16]


def _scalars(smalls, bf_row, tri):
    bsz, s, _ = smalls.shape
    return pl.pallas_call(
        _scalars_kernel,
        out_shape=[jax.ShapeDtypeStruct((bsz, s, 128), F32), jax.ShapeDtypeStruct((bsz, s // T, 16, T), F32)],
        grid=(bsz,),
        in_specs=[pl.BlockSpec((1, s, 128), lambda b: (b, 0, 0)), _whole((1, 128)), _whole((T, T))],
        out_specs=[pl.BlockSpec((1, s, 128), lambda b: (b, 0, 0)),
                   pl.BlockSpec((1, s // T, 16, T), lambda b: (b, 0, 0, 0))],
        compiler_params=pltpu.CompilerParams(dimension_semantics=("parallel",)),
        name="token_scalars",
    )(smalls, bf_row, tri)


def _compress_kernel(x_ref, pe_ref, w_ref, gk_ref, kc_ref, vct_ref):
    n = x_ref.shape[1] // CMP_STRIDE
    chunks = jnp.concatenate(
        [x_ref[0, pl.ds(i, n, stride=CMP_STRIDE), :] for i in range(CMP_STRIDE)], axis=1)
    ya = _dot((chunks + pe_ref[0:1]).astype(BF16), w_ref[0])
    yb = _dot((chunks + pe_ref[1:2]).astype(BF16), w_ref[1])
    y = ya + pltpu.roll(yb, n - 1, 0)
    lane = lax.broadcasted_iota(jnp.int32, y.shape, 1)
    ss = jnp.sum(jnp.where(lane < HEAD_DIM, y * y, 0.0), axis=-1, keepdims=True)
    kc = y * lax.rsqrt(ss * (1.0 / HEAD_DIM) + EPS) * gk_ref[...]
    kc_ref[0] = kc[:, 0:HEAD_DIM].astype(BF16)
    vct_ref[0] = y.T[HEAD_DIM:2 * HEAD_DIM].astype(BF16)


def _compress(cmp, pe_rows, w_blk, gk_pad):
    bsz, s, width = cmp.shape
    n = s // CMP_STRIDE
    return pl.pallas_call(
        _compress_kernel,
        out_shape=[jax.ShapeDtypeStruct((bsz, n, HEAD_DIM), BF16),
                   jax.ShapeDtypeStruct((bsz, HEAD_DIM, n), BF16)],
        grid=(bsz,),
        in_specs=[pl.BlockSpec((1, s, width), lambda b: (b, 0, 0)),
                  _whole(pe_rows.shape), _whole(w_blk.shape), _whole(gk_pad.shape)],
        out_specs=[pl.BlockSpec((1, n, HEAD_DIM), lambda b: (b, 0, 0)),
                   pl.BlockSpec((1, HEAD_DIM, n), lambda b: (b, 0, 0))],
        compiler_params=pltpu.CompilerParams(dimension_semantics=("parallel",)),
        name="nsa_compress",
    )(cmp, pe_rows, w_blk, gk_pad)


def _head_rows(qt, h):
    return qt[h * HEAD_DIM:(h + 1) * HEAD_DIM, :]


def _head_cols(kt, h):
    return kt[:, h * HEAD_DIM:(h + 1) * HEAD_DIM]


def _split_maps(qh):
    qf = qh.astype(F32)
    row = lax.broadcasted_iota(jnp.int32, qf.shape, 0)
    first = jnp.where(row < DIFF_QK_DIM, qf, 0.0).astype(BF16)
    second = jnp.where(row >= DIFF_QK_DIM, qf, 0.0).astype(BF16)
    return jnp.concatenate([first, second], axis=1)


def _with_ones(vt):
    return jnp.concatenate([vt, jnp.ones((ONES_ROWS, vt.shape[1]), BF16)], axis=0)


def _softmax_probs(ss, m_ref, idx, shift=None, fixed=None):
    if fixed is not None:
        already = isinstance(fixed, float) and fixed == 0.0
        return None, [jnp.exp2(s if already else s - fixed).astype(BF16) for s in ss]
    m_old = m_ref[idx]
    mx = jnp.max(ss[0], axis=0, keepdims=True)
    for s in ss[1:]:
        mx = jnp.maximum(mx, jnp.max(s, axis=0, keepdims=True))
    if shift is not None:
        mx = mx + shift
    m_new = jnp.maximum(m_old, mx)
    m_ref[idx] = m_new
    alpha = jnp.exp2(m_old - m_new)
    sub = m_new if shift is None else m_new - shift
    return alpha, [jnp.exp2(s - sub).astype(BF16) for s in ss]


def _accumulate(acc_ref, rows, alpha, vts, ps):
    upd = _dot(vts[0], ps[0])
    for vt, p in zip(vts[1:], ps[1:]):
        upd = upd + _dot(vt, p)
    acc_ref[rows, :] = (acc_ref[rows, :] if alpha is None else alpha * acc_ref[rows, :]) + upd


def _normalized(acc_ref, idx):
    base = idx * VROWS
    return acc_ref[base:base + HEAD_DIM, :] / acc_ref[base + HEAD_DIM:base + HEAD_DIM + 1, :]


def _causal_tile(width=T):
    r = lax.broadcasted_iota(jnp.int32, (T, width), 0)
    c = lax.broadcasted_iota(jnp.int32, (T, width), 1)
    if width != T:
        c = jnp.bitwise_and(c, T - 1)
    return r, c


def _for_tile_groups(qi, process):
    n_far = jnp.maximum(qi - 1, 0)
    n_pair = lax.shift_right_logical(n_far, 1)
    odd = jnp.bitwise_and(n_far, 1) == 1

    def pair(j, carry):
        process([(2 * j, "far"), (2 * j + 1, "far")])
        return carry

    lax.fori_loop(0, n_pair, pair, 0)

    @pl.when(qi == 0)
    def _():
        process([(qi, "diag")])

    @pl.when((qi >= 1) & jnp.logical_not(odd))
    def _():
        process([(qi - 1, "near"), (qi, "diag")])

    @pl.when((qi >= 1) & odd)
    def _():
        process([(qi - 2, "far"), (qi - 1, "near"), (qi, "diag")])


def _for_independent_tiles(qi, process):
    n_far = jnp.maximum(qi - 1, 0)
    n_group = lax.shift_right_logical(n_far, 2)
    rest = jnp.bitwise_and(n_far, 3)

    def many(j, carry):
        process([(4 * j + i, "far") for i in range(4)])
        return carry

    lax.fori_loop(0, n_group, many, 0)

    @pl.when(qi == 0)
    def _():
        process([(qi, "diag")])

    for n in range(4):
        @pl.when((qi >= 1) & (rest == n))
        def _():
            process([(4 * n_group + i, "far") for i in range(n)] + [(qi - 1, "near"), (qi, "diag")])


def _run_bounded_or_online(bounded_ok, qi, process):
    @pl.when(bounded_ok)
    def _():
        _for_independent_tiles(qi, lambda tiles: process(tiles, True))

    @pl.when(jnp.logical_not(bounded_ok))
    def _():
        _for_tile_groups(qi, lambda tiles: process(tiles, False))


def _attn_call(tile_kernel, name, bsz, nq, in_arrays, in_specs, scratch, out_width=GW):
    def kernel(*refs):
        def query_tile(qi, carry):
            tile_kernel(qi, *refs)
            return carry

        lax.fori_loop(0, nq, query_tile, 0)

    return pl.pallas_call(
        kernel,
        out_shape=jax.ShapeDtypeStruct((bsz, nq * T, out_width), BF16),
        grid=(bsz,),
        in_specs=in_specs,
        out_specs=pl.BlockSpec((1, nq * T, out_width), lambda b: (b, 0, 0)),
        scratch_shapes=scratch,
        compiler_params=pltpu.CompilerParams(
            dimension_semantics=("parallel",), vmem_limit_bytes=VMEM_LIMIT),
        name=name,
    )(*in_arrays)


def _store_query_tile(o_ref, qi, value):
    o_ref[0, pl.ds(pl.multiple_of(qi * T, T), T), :] = value


def _seq_spec(shape):
    nd = len(shape)
    return pl.BlockSpec((1,) + tuple(shape[1:]), lambda b: (b,) + (0,) * (nd - 1))


def _diff_kernel(qi, far_ref, cst_ref, bnd_ref, qt_ref, k_ref, vt_ref, bias_ref, lam_ref, subg_ref, o_ref,
                 m_ref, acc_ref):
    qmaps = [_split_maps(_head_rows(qt_ref[0, qi], h)) for h in range(HPM)]
    m_ref[...] = jnp.full(m_ref.shape, NEG, F32)
    acc_ref[...] = jnp.zeros(acc_ref.shape, F32)
    r, c = _causal_tile()

    def process(tiles, bounded):
        scores = [[_dot(_head_cols(k_ref[0, ki], h), qmaps[h]) for ki, _ in tiles] for h in range(HPM)]
        probs = []
        for idx in range(2 * HPM):
            h = idx // 2
            bound = bnd_ref[BND_DIFF + 1 + h] if bounded else None
            ss = []
            for (ki, mode), sc in zip(tiles, scores[h]):
                s = sc[:, (idx % 2) * T:(idx % 2 + 1) * T]
                if mode == "far":
                    s = s + ((far_ref[h] - bound) if bounded else far_ref[h])
                elif mode == "near":
                    s = s + bias_ref[h, T:2 * T, :]
                else:
                    s = jnp.where(r <= c, s + bias_ref[h, 0:T, :], NEG)
                if bounded and mode != "far":
                    s = s - bound
                ss.append(s)
            probs.append(_softmax_probs(ss, m_ref, idx, fixed=0.0 if bounded else None))
        for idx in range(2 * HPM):
            h = idx // 2
            vts = [_with_ones(vt_ref[0, ki, h * HEAD_DIM:(h + 1) * HEAD_DIM, :]) for ki, _ in tiles]
            alpha, ps = probs[idx]
            _accumulate(acc_ref, slice(idx * VROWS, (idx + 1) * VROWS), alpha, vts, ps)

    _run_bounded_or_online(bnd_ref[BND_DIFF] > 0.5, qi, process)

    lam_init = cst_ref[0]
    lv = lam_ref[...]
    lam = (jnp.exp(jnp.sum(lv[0:1] * lv[1:2], axis=-1, keepdims=True))
           - jnp.exp(jnp.sum(lv[2:3] * lv[3:4], axis=-1, keepdims=True)) + lam_init)
    outs = []
    for h in range(HPM):
        o = _normalized(acc_ref, 2 * h) - lam * _normalized(acc_ref, 2 * h + 1)
        ms = jnp.mean(o * o, axis=0, keepdims=True)
        outs.append(o * lax.rsqrt(ms + EPS) * subg_ref[...] * (1.0 - lam_init))
    _store_query_tile(o_ref, qi, jnp.concatenate(outs, axis=0).T.astype(BF16))


def _diff_attention(far, cst, bnd, qt, k4, vt, bias, lam, subg):
    bsz, nq = qt.shape[0], qt.shape[1]
    return _attn_call(
        _diff_kernel, "diff_attention", bsz, nq,
        [far, cst, bnd, qt, k4, vt, bias, lam, subg],
        [_smem(), _smem(), _smem(), _seq_spec(qt.shape), _seq_spec(k4.shape), _seq_spec(vt.shape),
         _resident(bias.shape), _whole(lam.shape), _whole(subg.shape)],
        [pltpu.VMEM((2 * HPM, 1, T), F32), pltpu.VMEM((2 * HPM * VROWS, T), F32)])


def _fox_kernel(qi, bnd_ref, qt_ref, k_ref, vt_ref, cq_ref, ck_ref, o_ref, m_ref, acc_ref):
    qh = [_head_rows(qt_ref[0, qi], h) for h in range(HPM)]
    m_ref[...] = jnp.full(m_ref.shape, NEG, F32)
    acc_ref[...] = jnp.zeros(acc_ref.shape, F32)
    r, c = _causal_tile()

    def process(tiles, bounded):
        scores = [[_dot(_head_cols(k_ref[0, ki], h), qh[h]) for ki, _ in tiles] for h in range(HPM)]
        probs = []
        for h in range(HPM):
            ss = []
            for (ki, mode), sc in zip(tiles, scores[h]):
                s = sc - ck_ref[0, ki][:, h:h + 1]
                if mode == "diag":
                    s = jnp.where(r <= c, s, NEG)
                ss.append(s)
            cq = cq_ref[0, qi, h:h + 1, :]
            fixed = (bnd_ref[BND_FOX + 1] - cq) if bounded else None
            probs.append(_softmax_probs(ss, m_ref, h, shift=cq, fixed=fixed))
        for h in range(HPM):
            vts = [_with_ones(vt_ref[0, ki, h * HEAD_DIM:(h + 1) * HEAD_DIM, :]) for ki, _ in tiles]
            alpha, ps = probs[h]
            _accumulate(acc_ref, slice(h * VROWS, (h + 1) * VROWS), alpha, vts, ps)

    _run_bounded_or_online(bnd_ref[BND_FOX] > 0.5, qi, process)
    outs = [_normalized(acc_ref, h) for h in range(HPM)]
    _store_query_tile(o_ref, qi, jnp.concatenate(outs, axis=0).T.astype(BF16))


def _fox_attention(bnd, qt, k4, vt, st, ck4):
    bsz, nq = qt.shape[0], qt.shape[1]
    return _attn_call(
        _fox_kernel, "forgetting_attention", bsz, nq,
        [bnd, qt, k4, vt, st, ck4],
        [_smem(), _seq_spec(qt.shape), _seq_spec(k4.shape), _seq_spec(vt.shape),
         _seq_spec(st.shape), _seq_spec(ck4.shape)],
        [pltpu.VMEM((HPM, 1, T), F32), pltpu.VMEM((HPM * VROWS, T), F32)])


def _sb_kernel(qi, qt_ref, k_ref, vt_ref, tri_ref, o_ref, run_ref, acc_ref):
    qh = [_head_rows(qt_ref[0, qi], h) for h in range(HPM)]
    run_ref[...] = jnp.zeros(run_ref.shape, F32)
    acc_ref[...] = jnp.zeros(acc_ref.shape, F32)
    r, c = _causal_tile()

    def process(tiles):
        scores = [[_dot(_head_cols(k_ref[0, ki], h), qh[h]) for ki, _ in tiles] for h in range(HPM)]
        weights = []
        for h in range(HPM):
            run = run_ref[h]
            ws = []
            for (ki, diag), z in zip(tiles, scores[h]):
                log_beta = jnp.minimum(z, 0.0) - jnp.log2(1.0 + jnp.exp2(-jnp.abs(z)))
                log_keep = log_beta - z
                if diag:
                    log_keep = jnp.where(r < c, log_keep, 0.0)
                keep_b = log_keep.astype(BF16)
                tail = _dot(tri_ref[...], keep_b) + run
                a = jnp.exp2(log_beta + tail)
                if diag:
                    a = jnp.where(r < c, a, 0.0)
                ws.append(a.astype(BF16))
                run = tail[0:1, :] + keep_b[0:1, :].astype(F32)
            run_ref[h] = run
            weights.append(ws)
        for h in range(HPM):
            rows = slice(h * HEAD_DIM, (h + 1) * HEAD_DIM)
            upd = None
            for (ki, _), a in zip(tiles, weights[h]):
                term = _dot(vt_ref[0, ki, rows, :], a)
                upd = term if upd is None else upd + term
            acc_ref[rows, :] = acc_ref[rows, :] + upd

    rest = jnp.bitwise_and(qi, 3)
    for n in range(4):
        @pl.when(rest == n)
        def _():
            process([(qi, True)] + [(qi - d, False) for d in range(1, n + 1)])

    base = qi - 1 - rest

    def four(j, carry):
        process([(base - 4 * j - i, False) for i in range(4)])
        return carry

    lax.fori_loop(0, lax.shift_right_logical(qi, 2), four, 0)
    _store_query_tile(o_ref, qi, acc_ref[...].T.astype(BF16))


def _sb_attention(qt, k4, vt, tri):
    bsz, nq = qt.shape[0], qt.shape[1]
    return _attn_call(
        _sb_kernel, "stick_breaking_attention", bsz, nq,
        [qt, k4, vt, tri],
        [_seq_spec(qt.shape), _seq_spec(k4.shape), _seq_spec(vt.shape), _resident(tri.shape)],
        [pltpu.VMEM((HPM, 1, T), F32), pltpu.VMEM((GW, T), F32)])


def _nsa_kernel(qi, far_ref, bnd_ref, qt_ref, kc_ref, vct_ref, ks_ref, vst_ref, kw_ref, vwt_ref, g_ref,
                bias_ref, ovt_ref, o_ref,
                sc_ref, key_ref, sel_ref, m_ref, acc_ref):
    q0 = qi * T
    n_cmp = kc_ref.shape[1]
    n_blk = ovt_ref.shape[0]
    n_sel = min(N_SEL, n_blk)
    w4 = HPM * T
    qt = qt_ref[0, qi]
    qcat = jnp.concatenate([qt[h * HEAD_DIM:(h + 1) * HEAD_DIM, :] for h in range(HPM)], axis=1)
    far_row = jnp.concatenate([jnp.full((1, T), far_ref[HPM + h], F32) for h in range(HPM)], axis=1)
    r, c = _causal_tile(w4)

    sc_ref[...] = _dot(kc_ref[0], qcat)
    delta = bias_ref[2 * T:2 * T + BAND_ROWS, :] - far_row

    @pl.when(qi == 0)
    def _():
        half = BAND_ROWS // 2
        sc_ref[0:half, :] = sc_ref[0:half, :] + delta[half:BAND_ROWS]

    @pl.when(qi > 0)
    def _():
        c0 = pl.multiple_of(qi * (T // CMP_STRIDE) - BAND_ROWS // 2, 8)
        sc_ref[pl.ds(c0, BAND_ROWS), :] = sc_ref[pl.ds(c0, BAND_ROWS), :] + delta

    ci = lax.broadcasted_iota(jnp.int32, (n_cmp, w4), 0)
    ti = q0 + jnp.bitwise_and(lax.broadcasted_iota(jnp.int32, (n_cmp, w4), 1), T - 1)
    valid_c = ti - CMP_STRIDE * ci - (CMP_LEN - 1) >= 0
    s = jnp.where(valid_c, sc_ref[...] + far_row, NEG)
    e = jnp.exp2(s - jnp.max(s, axis=0, keepdims=True))
    p = jnp.where(valid_c, e / jnp.sum(e, axis=0, keepdims=True), 0.0)
    o_cmp = _dot(vct_ref[0], p.astype(BF16))
    pc_sum = p[:, 0:T]
    for h in range(1, HPM):
        pc_sum = pc_sum + p[:, h * T:(h + 1) * T]

    hi, lo = _split2(pc_sum)
    imp = _dot(ovt_ref[...], hi) + _dot(ovt_ref[...], lo)
    ji = lax.broadcasted_iota(jnp.int32, (n_blk, T), 0)
    tq = q0 + lax.broadcasted_iota(jnp.int32, (n_blk, T), 1)
    forced = (ji == lax.shift_right_logical(tq, int(math.log2(SEL_LEN)))) | (ji == 0)
    imp = jnp.where(forced, FORCED_SCORE, imp)
    imp = jnp.where(ji * SEL_LEN <= tq, imp, -1.0)
    keys = lax.bitcast_convert_type(imp, jnp.int32)
    key_ref[...] = keys

    def rank_body(jp, cnt):
        row = key_ref[pl.ds(jp, 1), :]
        below = lax.shift_right_arithmetic(jp - ji, 31)
        return cnt + jnp.where(row > keys + below, 1, 0)

    n_seen = jnp.minimum((qi + 1) * (T // SEL_LEN), n_blk)
    cnt = lax.fori_loop(0, n_seen, rank_body, jnp.zeros((n_blk, T), jnp.int32))
    sel = jnp.where(cnt < n_sel, 1.0, 0.0)
    sel_ref[...] = jnp.concatenate([sel] * HPM, axis=1)

    m_ref[...] = jnp.full(m_ref.shape, NEG, F32)
    acc_ref[...] = jnp.zeros(acc_ref.shape, F32)

    def bound_row(base):
        return jnp.concatenate([jnp.full((1, T), bnd_ref[base + 1 + h], F32) for h in range(HPM)], axis=1)

    def biased(s, mode, bound):
        if mode == "far":
            return s + (far_row if bound is None else far_row - bound)
        tile = bias_ref[T:2 * T, :] if mode == "near" else bias_ref[0:T, :]
        return s + tile if bound is None else s + tile - bound

    def update(state, ss, vts, bounded):
        alpha, ps = _softmax_probs(ss, m_ref, state, fixed=0.0 if bounded else None)
        _accumulate(acc_ref, slice(state * VROWS, (state + 1) * VROWS), alpha, vts, ps)

    def sel_mask(ki):
        per = T // SEL_LEN
        rows = [jnp.broadcast_to(sel_ref[pl.ds(ki * per + i, 1), :], (SEL_LEN, w4)) for i in range(per)]
        return jnp.concatenate(rows, axis=0) > 0.5

    def process_selected(tiles, bounded):
        bound = bound_row(BND_SEL) if bounded else None
        scores = [_dot(ks_ref[0, ki], qcat) for ki, _ in tiles]
        ss = []
        for (ki, mode), sc in zip(tiles, scores):
            mask = sel_mask(ki)
            if mode == "diag":
                mask = mask & (r <= c)
            ss.append(jnp.where(mask, biased(sc, mode, bound), NEG))
        update(0, ss, [_with_ones(vst_ref[0, ki]) for ki, _ in tiles], bounded)

    _run_bounded_or_online(bnd_ref[BND_SEL] > 0.5, qi, process_selected)

    def process_window(tiles, bounded):
        bound = bound_row(BND_WIN) if bounded else None
        scores = [_dot(kw_ref[0, ki], qcat) for ki, _, _ in tiles]
        ss = []
        for (ki, mode, mask), sc in zip(tiles, scores):
            s = biased(sc, mode, bound)
            ss.append(s if mask is None else jnp.where(mask, s, NEG))
        update(1, ss, [_with_ones(vwt_ref[0, ki]) for ki, _, _ in tiles], bounded)

    back = WINDOW // T

    def window_tiles(n):
        tiles = [(qi, "diag", r <= c)]
        for d in range(1, n + 1):
            mode = "near" if d == 1 else "far"
            tiles.append((qi - d, mode, (c < r) if d == back else None))
        return tiles

    win_bounded = bnd_ref[BND_WIN] > 0.5
    for n in range(back + 1):
        cond = (qi == n) if n < back else (qi >= back)

        @pl.when(cond & win_bounded)
        def _():
            process_window(window_tiles(n), True)

        @pl.when(cond & jnp.logical_not(win_bounded))
        def _():
            process_window(window_tiles(n), False)

    o_sel = acc_ref[0:HEAD_DIM, :] / acc_ref[HEAD_DIM:HEAD_DIM + 1, :]
    o_win = acc_ref[VROWS:VROWS + HEAD_DIM, :] / acc_ref[VROWS + HEAD_DIM:VROWS + HEAD_DIM + 1, :]
    gate = lambda br: jnp.concatenate(
        [g_ref[0, qi, HPM + br * HPM + h:HPM + br * HPM + h + 1, :] for h in range(HPM)], axis=1)
    o = gate(0) * o_cmp + gate(1) * o_sel + gate(2) * o_win
    ot = jnp.concatenate([o[:, h * T:(h + 1) * T] for h in range(HPM)], axis=0)
    _store_query_tile(o_ref, qi, ot.T.astype(BF16))


def _nsa_attention(far, bnd, qt, kc, vct, ks4, vst, kw4, vwt, st, bias, ovt):
    bsz, nq = qt.shape[0], qt.shape[1]
    n_cmp = kc.shape[1]
    n_blk = ovt.shape[0]
    w4 = HPM * T
    return _attn_call(
        _nsa_kernel, "native_sparse_attention", bsz, nq,
        [far, bnd, qt, kc, vct, ks4, vst, kw4, vwt, st, bias, ovt],
        [_smem(), _smem(), _seq_spec(qt.shape), _seq_spec(kc.shape), _seq_spec(vct.shape),
         _seq_spec(ks4.shape), _seq_spec(vst.shape), _seq_spec(kw4.shape), _seq_spec(vwt.shape),
         _seq_spec(st.shape), _resident(bias.shape), _whole(ovt.shape)],
        [pltpu.VMEM((n_cmp, w4), F32), pltpu.VMEM((n_blk, T), jnp.int32), pltpu.VMEM((n_blk, w4), F32),
         pltpu.VMEM((2, 1, w4), F32), pltpu.VMEM((2 * VROWS, w4), F32)])


def _post_kernel(x_ref, oa_ref, ob_ref, oc_ref, od_ref, wo_ref, gta_ref, g_ref, sc_ref, sh_ref,
                 gtm_ref, wup_ref, cw_ref, cb_ref, wdn_ref, y_ref, carry_ref):
    d_ff = wdn_ref.shape[1]
    mixed = (_dot(oa_ref[0], wo_ref[0, 0:GW]) + _dot(ob_ref[0], wo_ref[0, GW:2 * GW])
             + _dot(oc_ref[0], wo_ref[0, 2 * GW:3 * GW]) + _dot(od_ref[0], wo_ref[0, 3 * GW:4 * GW]))
    x1 = x_ref[0] + gta_ref[0] * mixed
    ms = jnp.mean(x1 * x1, axis=-1, keepdims=True)
    h = x1 * lax.rsqrt(ms + EPS) * g_ref[...]
    hb = (h * (1.0 + sc_ref[0]) + sh_ref[0]).astype(BF16)

    @pl.when(pl.program_id(1) == 0)
    def _():
        carry_ref[...] = jnp.zeros(carry_ref.shape, F32)

    def up(ch):
        gate = _dot(hb, wup_ref[0, :, ch * FF_CHUNK:(ch + 1) * FF_CHUNK])
        val = _dot(hb, wup_ref[0, :, d_ff + ch * FF_CHUNK:d_ff + (ch + 1) * FF_CHUNK])
        return gate, val

    row = lax.broadcasted_iota(jnp.int32, (TM_POST, FF_CHUNK), 0)
    y = jnp.zeros((TM_POST, x1.shape[1]), F32)
    n_chunks = d_ff // FF_CHUNK
    ahead = up(0)
    for ch in range(n_chunks):
        cols = slice(ch * FF_CHUNK, (ch + 1) * FF_CHUNK)
        gate, val = ahead
        if ch + 1 < n_chunks:
            ahead = up(ch + 1)
        prev = carry_ref[:, cols]
        g1 = jnp.where(row == 0, prev[7:8], pltpu.roll(gate, 1, 0))
        g2 = jnp.where(row == 0, prev[6:7], jnp.where(row == 1, prev[7:8], pltpu.roll(gate, 2, 0)))
        carry_ref[:, cols] = gate[TM_POST - 8:TM_POST]
        conv = (cw_ref[0, 0:1, cols] * g2 + cw_ref[0, 1:2, cols] * g1 + cw_ref[0, 2:3, cols] * gate
                + cb_ref[0, :, cols])
        act = conv * _sigmoid(conv) * val
        y = y + _dot(act.astype(BF16), wdn_ref[0, cols, :])
    y_ref[0] = x1 + gtm_ref[0] * y


def _post(layer, x, oa, ob, oc, od, wo, gta, g, sc, sh, gtm, wup, cw, cb, wdn):
    bsz, s, d = x.shape
    tok = lambda width: pl.BlockSpec((1, TM_POST, width), lambda b, i: (b, i, 0))
    vec = pl.BlockSpec((1, 1, d), lambda b, i: (b, 0, 0))
    of_layer = lambda a: pl.BlockSpec((1,) + a.shape[1:], lambda b, i: (layer,) + (0,) * (a.ndim - 1),
                                      pipeline_mode=pl.Buffered(1))
    return pl.pallas_call(
        _post_kernel,
        out_shape=jax.ShapeDtypeStruct((bsz, s, d), F32),
        grid=(bsz, s // TM_POST),
        in_specs=[tok(d), tok(GW), tok(GW), tok(GW), tok(GW), of_layer(wo), vec, _whole((1, d)), vec, vec,
                  vec, of_layer(wup), of_layer(cw), of_layer(cb), of_layer(wdn)],
        out_specs=tok(d),
        scratch_shapes=[pltpu.VMEM((8, wdn.shape[1]), F32)],
        compiler_params=pltpu.CompilerParams(
            dimension_semantics=("parallel", "arbitrary"), vmem_limit_bytes=VMEM_LIMIT),
        name="out_proj_mlp",
    )(x, oa, ob, oc, od, wo, gta, g, sc, sh, gtm, wup, cw, cb, wdn)


def _repack_w_in(w):
    off = {}
    pos = 0
    for name, size in (("a_q", GW), ("a_k", GW), ("a_v", GW), ("b_q", GW), ("b_k", GW), ("b_v", GW),
                       ("b_f", HPM), ("c_q", GW), ("c_k", GW), ("c_v", GW), ("d_q", GW),
                       ("d_kc", HEAD_DIM), ("d_vc", HEAD_DIM), ("d_ks", HEAD_DIM), ("d_vs", HEAD_DIM),
                       ("d_kw", HEAD_DIM), ("d_vw", HEAD_DIM), ("d_g", 3 * HPM)):
        off[name] = (pos, size)
        pos += size
    assert pos == w.shape[-1]
    col = lambda n: w[..., off[n][0]:off[n][0] + off[n][1]]
    order = ["a_q", "a_k", "a_v", "b_q", "b_k", "b_v", "c_q", "c_k", "c_v", "d_q",
             "d_ks", "d_vs", "d_kw", "d_vw", "d_kc", "d_vc", "b_f", "d_g"]
    parts = [col(n) for n in order] + [jnp.zeros(w.shape[:-1] + (128 - HPM - 3 * HPM,), w.dtype)]
    out = jnp.concatenate(parts, axis=-1).astype(BF16)
    assert out.shape[-1] == N_PROJ
    return out


def _compress_params(pe, phi_w):
    half = CMP_LEN // 2
    pe_rows = jnp.concatenate([pe[0], pe[1]], axis=1).reshape(2, half * 2 * HEAD_DIM)
    wk = phi_w[0].reshape(CMP_LEN, HEAD_DIM, HEAD_DIM)
    wv = phi_w[1].reshape(CMP_LEN, HEAD_DIM, HEAD_DIM)
    zero = jnp.zeros_like(wk)
    blk = jnp.concatenate([jnp.concatenate([wk, zero], axis=2), jnp.concatenate([zero, wv], axis=2)], axis=1)
    return pe_rows, blk.reshape(2, half * 2 * HEAD_DIM, 2 * HEAD_DIM).astype(BF16)


def _score_bounds(rel_bias, gq_a, gk_a, gq_b, gk_b, gq_d, gk_s, gk_w):
    amax = lambda g: jnp.max(jnp.abs(g))
    tb = rel_bias * LOG2E
    bmax, bmin = jnp.max(tb, axis=0), jnp.min(tb, axis=0)
    span = jnp.max(bmax - bmin)
    qk = lambda n, gq, gk, scale: n * amax(gq) * amax(gk) * (scale * LOG2E * BOUND_MARGIN)
    ok = lambda b, sp: (2.0 * b + sp <= MAX_LOG2_SPAN).astype(F32).reshape(1)
    b_fox = qk(HEAD_DIM, gq_b, gk_b, HEAD_DIM ** -0.5)
    b_a = qk(DIFF_QK_DIM, gq_a, gk_a, DIFF_QK_DIM ** -0.5)
    b_s = qk(HEAD_DIM, gq_d, gk_s, HEAD_DIM ** -0.5)
    b_w = qk(HEAD_DIM, gq_d, gk_w, HEAD_DIM ** -0.5)
    vec = jnp.concatenate([
        ok(b_fox, 0.0), b_fox.reshape(1),
        ok(b_a, span), b_a + bmax[:HPM],
        ok(b_s, span), b_s + bmax[HPM:],
        ok(b_w, span), b_w + bmax[HPM:]])
    return jnp.pad(vec, (0, BND_SIZE - vec.shape[0]))


def kernel(x, c, rel_bias, ada_w, ada_b, norm_mix_g, norm_ffn_g, w_in, w_out, diff_qnorm_g, diff_knorm_g, diff_lambda, diff_subln_g, fox_qnorm_g, fox_knorm_g, fox_b_f, nsa_qnorm_g, nsa_knorm_g, nsa_pe, nsa_phi_w, ffn_w_up, ffn_conv_w, ffn_conv_b, ffn_w_down):
    bsz, s, d = x.shape
    depth = ada_w.shape[0]
    assert s % T == 0 and WINDOW % T == 0 and T % SEL_LEN == 0 and d == 4 * GW
    assert T & (T - 1) == 0 and BAND_ROWS // 2 >= (MAX_DISTANCE + CMP_LEN) // CMP_STRIDE
    nk = s // T
    n_chunk = s // CMP_STRIDE
    n_blk = s // SEL_LEN
    d_ff = ffn_w_down.shape[1]
    assert d_ff % FF_CHUNK == 0

    bd32 = jnp.asarray(_block_ones(GW, DIFF_QK_DIM), BF16)
    bd64 = jnp.asarray(_block_ones(GW, HEAD_DIM), BF16)
    ti = np.arange(T)
    tri_incl = jnp.asarray((ti[None, :] <= ti[:, None]).astype(np.float32), BF16)
    tri_later = jnp.asarray((ti[None, :] > ti[:, None]).astype(np.float32), BF16)
    c_start = np.arange(n_chunk) * CMP_STRIDE
    s_start = np.arange(n_blk) * SEL_LEN
    ov = ((c_start[None, :] < s_start[:, None] + SEL_LEN) & (s_start[:, None] < c_start[None, :] + CMP_LEN)
          & (np.arange(n_chunk)[None, :] < n_chunk - 1))
    ovt = jnp.asarray(ov.astype(np.float32), BF16)

    bias_a, bias_d = _bias_tiles(rel_bias)
    far = rel_bias[NUM_BUCKETS - 1] * LOG2E
    mod = _adaln(c, ada_w, ada_b)

    w_re = _repack_w_in(w_in)
    wo_b, wup_b, wdn_b = w_out.astype(BF16), ffn_w_up.astype(BF16), ffn_w_down.astype(BF16)
    conv_w = ffn_conv_w.reshape(depth, CONV_WIDTH, d_ff)
    conv_b = ffn_conv_b.reshape(depth, 1, d_ff)

    for l in range(depth):
        sh_a, sc_a, gt_a, sh_m, sc_m, gt_m = [m.reshape(bsz, 1, d) for m in jnp.split(mod[l], 6, axis=-1)]
        scale_a = DIFF_QK_DIM ** -0.5 * LOG2E
        scale = HEAD_DIM ** -0.5 * LOG2E
        ones = jnp.ones((HEAD_DIM,), F32)
        gains = jnp.stack([
            jnp.tile(diff_qnorm_g[l] * scale_a, GW // DIFF_QK_DIM), jnp.tile(diff_knorm_g[l], GW // DIFF_QK_DIM),
            jnp.tile(fox_qnorm_g[l] * scale, HPM), jnp.tile(fox_knorm_g[l], HPM),
            jnp.tile(nsa_qnorm_g[l] * scale, HPM),
            jnp.concatenate([nsa_knorm_g[l, 1], ones, nsa_knorm_g[l, 2], ones]),
            jnp.ones((GW,), F32), jnp.ones((GW,), F32)])
        (a_qt, a_k, a_vt, b_qt, b_k, b_vt, c_qt, c_k, c_vt, d_qt,
         d_ks, d_kw, d_vst, d_vwt, d_cmp, smalls) = _inproj(
            l, x, norm_mix_g[l].reshape(1, d), sc_a, sh_a, w_re, gains, bd32, bd64)

        bf_row = jnp.pad(fox_b_f[l], (0, 128 - HPM)).reshape(1, 128)
        ck, st = _scalars(smalls, bf_row, tri_incl)

        pe_rows, w_blk = _compress_params(nsa_pe[l], nsa_phi_w[l])
        gk_pad = jnp.pad(nsa_knorm_g[l, 0], (0, 128 - HEAD_DIM)).reshape(1, 128)
        kc, vct = _compress(d_cmp, pe_rows, w_blk, gk_pad)

        bnd = _score_bounds(rel_bias, diff_qnorm_g[l], diff_knorm_g[l], fox_qnorm_g[l], fox_knorm_g[l],
                            nsa_qnorm_g[l], nsa_knorm_g[l, 1], nsa_knorm_g[l, 2])
        lam_init = 0.8 - 0.6 * math.exp(-0.3 * l)
        cst = jnp.full((1,), lam_init, F32)
        k4 = lambda a: a.reshape(bsz, nk, T, a.shape[-1])
        o_a = _diff_attention(far, cst, bnd, a_qt, k4(a_k), a_vt, bias_a, diff_lambda[l],
                              diff_subln_g[l].reshape(HEAD_DIM, 1))
        o_b = _fox_attention(bnd, b_qt, k4(b_k), b_vt, st, k4(ck))
        o_c = _sb_attention(c_qt, k4(c_k), c_vt, tri_later)
        o_d = _nsa_attention(far, bnd, d_qt, kc, vct, k4(d_ks), d_vst, k4(d_kw), d_vwt, st, bias_d, ovt)

        x = _post(l, x, o_a, o_b, o_c, o_d, wo_b, gt_a, norm_ffn_g[l].reshape(1, d),
                  sc_m, sh_m, gt_m, wup_b, conv_w, conv_b, wdn_b)
    return x
```

```python
import math

import numpy as np
import jax
import jax.numpy as jnp
from jax import lax
from jax.experimental import pallas as pl
from jax.experimental.pallas import tpu as pltpu

HEAD_DIM = 64
HPM = 4
GW = HPM * HEAD_DIM
DIFF_QK_DIM = HEAD_DIM // 2
NUM_BUCKETS = 32
MAX_EXACT = NUM_BUCKETS // 2
MAX_DISTANCE = 128
CMP_LEN = 32
CMP_STRIDE = 16
SEL_LEN = 64
N_SEL = 16
WINDOW = 512
FORCED_SCORE = 1.0e4
NEG = -1.0e30
EPS = 1e-6
CONV_WIDTH = 3
LOG2E = 1.4426950408889634

T = 256
TM = T
TM_POST = 256
FF_CHUNK = 256
ONES_ROWS = 16
VROWS = HEAD_DIM + ONES_ROWS
BAND_ROWS = 2 * T // CMP_STRIDE
VMEM_LIMIT = 56 * 1024 * 1024

BND_FOX, BND_DIFF, BND_SEL, BND_WIN, BND_SIZE = 0, 2, 7, 12, 24
BOUND_MARGIN = 1.02
MAX_LOG2_SPAN = 100.0

F32 = jnp.float32
BF16 = jnp.bfloat16


def _dot(a, b):
    return jnp.dot(a, b, preferred_element_type=F32)


def _split2(x):
    hi = x.astype(BF16)
    lo = (x - hi.astype(F32)).astype(BF16)
    return hi, lo


def _log_sigmoid(z):
    return jnp.minimum(z, 0.0) - jnp.log(1.0 + jnp.exp(-jnp.abs(z)))


def _sigmoid(z):
    return 1.0 / (1.0 + jnp.exp(-z))


def _whole(shape):
    nd = len(shape)
    return pl.BlockSpec(shape, lambda *_: (0,) * nd)


def _resident(shape):
    nd = len(shape)
    return pl.BlockSpec(shape, lambda *_: (0,) * nd, pipeline_mode=pl.Buffered(1))


def _smem():
    return pl.BlockSpec(memory_space=pltpu.SMEM)


def _t5_bucket_np(dist):
    n = np.maximum(dist, 0)
    ratio = math.log(MAX_DISTANCE / MAX_EXACT)
    out = None
    for dt in (np.float32, np.float64):
        large = MAX_EXACT + (np.log(np.maximum(n, 1).astype(dt) / dt(MAX_EXACT)) / dt(ratio)
                             * dt(NUM_BUCKETS - MAX_EXACT)).astype(np.int32)
        b = np.where(n < MAX_EXACT, n, np.minimum(large, NUM_BUCKETS - 1)).astype(np.int32)
        assert out is None or np.array_equal(out, b)
        out = b
    return out


def _bucket_tiles():
    r = np.arange(T)[:, None]
    c = np.arange(T)[None, :]
    diag = _t5_bucket_np(c - r)
    near = _t5_bucket_np(T + c - r)
    rb = np.arange(BAND_ROWS)[:, None]
    band = _t5_bucket_np(c - CMP_STRIDE * rb + (T - CMP_LEN + 1))
    return np.concatenate([diag, near, band], axis=0)


def _block_ones(n, group):
    i = np.arange(n)
    return (i[:, None] // group == i[None, :] // group).astype(np.float32)


def _bias_kernel(tbl_ref, bkt_ref, a_ref, d_ref):
    b = bkt_ref[...]
    for h in range(2 * HPM):
        acc = jnp.zeros(b.shape, F32)
        for k in range(NUM_BUCKETS):
            acc = jnp.where(b == k, tbl_ref[k, h] * LOG2E, acc)
        if h < HPM:
            a_ref[h] = acc[0:2 * T]
        else:
            d_ref[:, (h - HPM) * T:(h - HPM + 1) * T] = acc


def _bias_tiles(rel_bias):
    bkt = jnp.asarray(_bucket_tiles())
    rows = bkt.shape[0]
    return pl.pallas_call(
        _bias_kernel,
        out_shape=[jax.ShapeDtypeStruct((HPM, 2 * T, T), F32), jax.ShapeDtypeStruct((rows, HPM * T), F32)],
        in_specs=[_smem(), _whole(bkt.shape)],
        out_specs=[_whole((HPM, 2 * T, T)), _whole((rows, HPM * T))],
        name="bias_tiles",
    )(rel_bias, bkt)


def _adaln_kernel(c_ref, w_ref, b_ref, o_ref):
    c = c_ref[...]
    ca = c * _sigmoid(c)
    o_ref[0] = _dot(ca.astype(BF16), w_ref[0].astype(BF16)) + b_ref[0]


def _adaln(c, ada_w, ada_b):
    depth, d, n = ada_w.shape
    bsz = c.shape[0]
    rows = -(-bsz // 8) * 8
    cp = jnp.pad(c, ((0, rows - bsz), (0, 0)))
    tn = 1536
    out = pl.pallas_call(
        _adaln_kernel,
        out_shape=jax.ShapeDtypeStruct((depth, rows, n), F32),
        grid=(depth, n // tn),
        in_specs=[pl.BlockSpec((rows, d), lambda l, j: (0, 0)),
                  pl.BlockSpec((1, d, tn), lambda l, j: (l, 0, j)),
                  pl.BlockSpec((1, 1, tn), lambda l, j: (l, 0, j))],
        out_specs=pl.BlockSpec((1, rows, tn), lambda l, j: (l, 0, j)),
        name="adaln",
    )(cp, ada_w, ada_b.reshape(depth, 1, n))
    return out[:, :bsz]


N_PROJ = 11 * GW + 2 * 128


def _inproj_kernel(x_ref, g_ref, sc_ref, sh_ref, w_ref, gains_ref, bd32_ref, bd64_ref,
                   aq_ref, ak_ref, av_ref, bq_ref, bk_ref, bv_ref, cq_ref, ck_ref, cv_ref, dq_ref,
                   dks_ref, dkw_ref, dvs_ref, dvw_ref, cmp_ref, sm_ref):
    x = x_ref[0]
    ms = jnp.mean(x * x, axis=-1, keepdims=True)
    h = x * lax.rsqrt(ms + EPS) * g_ref[...]
    h = h * (1.0 + sc_ref[0]) + sh_ref[0]
    hb = h.astype(BF16)

    def proj(group, width=GW):
        off = group * GW
        return _dot(hb, w_ref[0, :, off:off + width])

    def segnorm(y, bd_ref, inv_n, gain_row):
        ss = _dot((y * y).astype(BF16), bd_ref[...])
        return y * lax.rsqrt(ss * inv_n + EPS) * gains_ref[gain_row:gain_row + 1, :]

    def put_t(ref, y):
        ref[0, 0] = y.T.astype(BF16)

    raw0 = proj(0)
    raw1 = proj(1)
    put_t(aq_ref, segnorm(raw0, bd32_ref, 1.0 / DIFF_QK_DIM, 0))
    put_t(av_ref, proj(2))
    raw3 = proj(3)
    ak_ref[0] = segnorm(raw1, bd32_ref, 1.0 / DIFF_QK_DIM, 1).astype(BF16)
    put_t(bv_ref, proj(5))
    raw4 = proj(4)
    put_t(bq_ref, segnorm(raw3, bd64_ref, 1.0 / HEAD_DIM, 2))
    put_t(cq_ref, proj(6) * (HEAD_DIM ** -0.5 * LOG2E))
    raw9 = proj(9)
    bk_ref[0] = segnorm(raw4, bd64_ref, 1.0 / HEAD_DIM, 3).astype(BF16)
    ck_ref[0] = proj(7).astype(BF16)
    y = proj(10)
    put_t(dq_ref, segnorm(raw9, bd64_ref, 1.0 / HEAD_DIM, 4))
    put_t(cv_ref, proj(8))
    tail = proj(11)
    cmp_ref[0] = tail[:, 0:128]
    sm_ref[0] = tail[:, 128:256]

    yn = segnorm(y, bd64_ref, 1.0 / HEAD_DIM, 5)
    dks_ref[0] = yn[:, 0:HEAD_DIM].astype(BF16)
    dkw_ref[0] = yn[:, 2 * HEAD_DIM:3 * HEAD_DIM].astype(BF16)
    yt = y.T
    dvs_ref[0, 0] = yt[HEAD_DIM:2 * HEAD_DIM].astype(BF16)
    dvw_ref[0, 0] = yt[3 * HEAD_DIM:4 * HEAD_DIM].astype(BF16)


def _inproj(layer, x, g, sc, sh, w_re, gains, bd32, bd64):
    bsz, s, d = x.shape
    nk = s // T
    tok = lambda width, dt: jax.ShapeDtypeStruct((bsz, s, width), dt)
    tr = lambda rows: jax.ShapeDtypeStruct((bsz, nk, rows, T), BF16)
    tok_spec = lambda width: pl.BlockSpec((1, TM, width), lambda b, i: (b, i, 0))
    tr_spec = lambda rows: pl.BlockSpec((1, 1, rows, T), lambda b, i: (b, i, 0, 0))
    vec = pl.BlockSpec((1, 1, d), lambda b, i: (b, 0, 0))
    out_shape = [tr(GW), tok(GW, BF16), tr(GW)] * 3 + [tr(GW)] + [
        tok(HEAD_DIM, BF16), tok(HEAD_DIM, BF16), tr(HEAD_DIM), tr(HEAD_DIM),
        tok(128, F32), tok(128, F32)]
    out_specs = [tr_spec(GW), tok_spec(GW), tr_spec(GW)] * 3 + [tr_spec(GW)] + [
        tok_spec(HEAD_DIM), tok_spec(HEAD_DIM), tr_spec(HEAD_DIM), tr_spec(HEAD_DIM),
        tok_spec(128), tok_spec(128)]
    return pl.pallas_call(
        _inproj_kernel,
        out_shape=out_shape,
        grid=(bsz, s // TM),
        in_specs=[pl.BlockSpec((1, TM, d), lambda b, i: (b, i, 0)),
                  _whole((1, d)), vec, vec,
                  pl.BlockSpec((1,) + w_re.shape[1:], lambda b, i: (layer, 0, 0)),
                  _whole(gains.shape), _whole(bd32.shape), _whole(bd64.shape)],
        out_specs=out_specs,
        compiler_params=pltpu.CompilerParams(
            dimension_semantics=("parallel", "parallel"), vmem_limit_bytes=VMEM_LIMIT),
        name="inproj",
    )(x, g, sc, sh, w_re, gains, bd32, bd64)


def _scalars_kernel(sm_ref, bf_ref, tri_ref, ck_ref, st_ref):
    s = sm_ref.shape[1]
    carry = jnp.zeros((1, 128), F32)
    for blk in range(s // T):
        rows = slice(blk * T, (blk + 1) * T)
        lf = _log_sigmoid(sm_ref[0, rows, :] + bf_ref[...])
        h1 = lf.astype(BF16)
        r1 = lf - h1.astype(F32)
        h2 = r1.astype(BF16)
        h3 = (r1 - h2.astype(F32)).astype(BF16)
        tri = tri_ref[...]
        cb = _dot(tri, h1) + _dot(tri, h2) + _dot(tri, h3) + carry
        carry = cb[T - 1:T, :]
        cb2 = cb * LOG2E
        ck_ref[0, rows, :] = cb2
        col = lax.broadcasted_iota(jnp.int32, (T, 128), 1)
        comb = jnp.where(col < HPM, cb2, _sigmoid(sm_ref[0, rows, :]))
        st_ref[0, blk] = comb.T[0:16]


def _scalars(smalls, bf_row, tri):
    bsz, s, _ = smalls.shape
    return pl.pallas_call(
        _scalars_kernel,
        out_shape=[jax.ShapeDtypeStruct((bsz, s, 128), F32), jax.ShapeDtypeStruct((bsz, s // T, 16, T), F32)],
        grid=(bsz,),
        in_specs=[pl.BlockSpec((1, s, 128), lambda b: (b, 0, 0)), _whole((1, 128)), _whole((T, T))],
        out_specs=[pl.BlockSpec((1, s, 128), lambda b: (b, 0, 0)),
                   pl.BlockSpec((1, s // T, 16, T), lambda b: (b, 0, 0, 0))],
        compiler_params=pltpu.CompilerParams(dimension_semantics=("parallel",)),
        name="token_scalars",
    )(smalls, bf_row, tri)


def _compress_kernel(x_ref, pe_ref, w_ref, gk_ref, kc_ref, vct_ref):
    n = x_ref.shape[1] // CMP_STRIDE
    chunks = jnp.concatenate(
        [x_ref[0, pl.ds(i, n, stride=CMP_STRIDE), :] for i in range(CMP_STRIDE)], axis=1)
    ya = _dot((chunks + pe_ref[0:1]).astype(BF16), w_ref[0])
    yb = _dot((chunks + pe_ref[1:2]).astype(BF16), w_ref[1])
    y = ya + pltpu.roll(yb, n - 1, 0)
    lane = lax.broadcasted_iota(jnp.int32, y.shape, 1)
    ss = jnp.sum(jnp.where(lane < HEAD_DIM, y * y, 0.0), axis=-1, keepdims=True)
    kc = y * lax.rsqrt(ss * (1.0 / HEAD_DIM) + EPS) * gk_ref[...]
    kc_ref[0] = kc[:, 0:HEAD_DIM].astype(BF16)
    vct_ref[0] = y.T[HEAD_DIM:2 * HEAD_DIM].astype(BF16)


def _compress(cmp, pe_rows, w_blk, gk_pad):
    bsz, s, width = cmp.shape
    n = s // CMP_STRIDE
    return pl.pallas_call(
        _compress_kernel,
        out_shape=[jax.ShapeDtypeStruct((bsz, n, HEAD_DIM), BF16),
                   jax.ShapeDtypeStruct((bsz, HEAD_DIM, n), BF16)],
        grid=(bsz,),
        in_specs=[pl.BlockSpec((1, s, width), lambda b: (b, 0, 0)),
                  _whole(pe_rows.shape), _whole(w_blk.shape), _whole(gk_pad.shape)],
        out_specs=[pl.BlockSpec((1, n, HEAD_DIM), lambda b: (b, 0, 0)),
                   pl.BlockSpec((1, HEAD_DIM, n), lambda b: (b, 0, 0))],
        compiler_params=pltpu.CompilerParams(dimension_semantics=("parallel",)),
        name="nsa_compress",
    )(cmp, pe_rows, w_blk, gk_pad)


def _head_rows(qt, h):
    return qt[h * HEAD_DIM:(h + 1) * HEAD_DIM, :]


def _head_cols(kt, h):
    return kt[:, h * HEAD_DIM:(h + 1) * HEAD_DIM]


def _split_maps(qh):
    qf = qh.astype(F32)
    row = lax.broadcasted_iota(jnp.int32, qf.shape, 0)
    first = jnp.where(row < DIFF_QK_DIM, qf, 0.0).astype(BF16)
    second = jnp.where(row >= DIFF_QK_DIM, qf, 0.0).astype(BF16)
    return jnp.concatenate([first, second], axis=1)


def _with_ones(vt):
    return jnp.concatenate([vt, jnp.ones((ONES_ROWS, vt.shape[1]), BF16)], axis=0)


def _softmax_probs(ss, m_ref, idx, shift=None, fixed=None):
    if fixed is not None:
        already = isinstance(fixed, float) and fixed == 0.0
        return None, [jnp.exp2(s if already else s - fixed).astype(BF16) for s in ss]
    m_old = m_ref[idx]
    mx = jnp.max(ss[0], axis=0, keepdims=True)
    for s in ss[1:]:
        mx = jnp.maximum(mx, jnp.max(s, axis=0, keepdims=True))
    if shift is not None:
        mx = mx + shift
    m_new = jnp.maximum(m_old, mx)
    m_ref[idx] = m_new
    alpha = jnp.exp2(m_old - m_new)
    sub = m_new if shift is None else m_new - shift
    return alpha, [jnp.exp2(s - sub).astype(BF16) for s in ss]


def _accumulate(acc_ref, rows, alpha, vts, ps):
    upd = _dot(vts[0], ps[0])
    for vt, p in zip(vts[1:], ps[1:]):
        upd = upd + _dot(vt, p)
    acc_ref[rows, :] = (acc_ref[rows, :] if alpha is None else alpha * acc_ref[rows, :]) + upd


def _normalized(acc_ref, idx):
    base = idx * VROWS
    return acc_ref[base:base + HEAD_DIM, :] / acc_ref[base + HEAD_DIM:base + HEAD_DIM + 1, :]


def _causal_tile(width=T):
    r = lax.broadcasted_iota(jnp.int32, (T, width), 0)
    c = lax.broadcasted_iota(jnp.int32, (T, width), 1)
    if width != T:
        c = jnp.bitwise_and(c, T - 1)
    return r, c


def _for_tile_groups(qi, process):
    n_far = jnp.maximum(qi - 1, 0)
    n_pair = lax.shift_right_logical(n_far, 1)
    odd = jnp.bitwise_and(n_far, 1) == 1

    def pair(j, carry):
        process([(2 * j, "far"), (2 * j + 1, "far")])
        return carry

    lax.fori_loop(0, n_pair, pair, 0)

    @pl.when(qi == 0)
    def _():
        process([(qi, "diag")])

    @pl.when((qi >= 1) & jnp.logical_not(odd))
    def _():
        process([(qi - 1, "near"), (qi, "diag")])

    @pl.when((qi >= 1) & odd)
    def _():
        process([(qi - 2, "far"), (qi - 1, "near"), (qi, "diag")])


def _for_independent_tiles(qi, process):
    n_far = jnp.maximum(qi - 1, 0)
    n_group = lax.shift_right_logical(n_far, 2)
    rest = jnp.bitwise_and(n_far, 3)

    def many(j, carry):
        process([(4 * j + i, "far") for i in range(4)])
        return carry

    lax.fori_loop(0, n_group, many, 0)

    @pl.when(qi == 0)
    def _():
        process([(qi, "diag")])

    for n in range(4):
        @pl.when((qi >= 1) & (rest == n))
        def _():
            process([(4 * n_group + i, "far") for i in range(n)] + [(qi - 1, "near"), (qi, "diag")])


def _run_bounded_or_online(bounded_ok, qi, process):
    @pl.when(bounded_ok)
    def _():
        _for_independent_tiles(qi, lambda tiles: process(tiles, True))

    @pl.when(jnp.logical_not(bounded_ok))
    def _():
        _for_tile_groups(qi, lambda tiles: process(tiles, False))


def _attn_call(tile_kernel, name, bsz, nq, in_arrays, in_specs, scratch, out_width=GW):
    def kernel(*refs):
        def query_tile(qi, carry):
            tile_kernel(qi, *refs)
            return carry

        lax.fori_loop(0, nq, query_tile, 0)

    return pl.pallas_call(
        kernel,
        out_shape=jax.ShapeDtypeStruct((bsz, nq * T, out_width), BF16),
        grid=(bsz,),
        in_specs=in_specs,
        out_specs=pl.BlockSpec((1, nq * T, out_width), lambda b: (b, 0, 0)),
        scratch_shapes=scratch,
        compiler_params=pltpu.CompilerParams(
            dimension_semantics=("parallel",), vmem_limit_bytes=VMEM_LIMIT),
        name=name,
    )(*in_arrays)


def _store_query_tile(o_ref, qi, value):
    o_ref[0, pl.ds(pl.multiple_of(qi * T, T), T), :] = value


def _seq_spec(shape):
    nd = len(shape)
    return pl.BlockSpec((1,) + tuple(shape[1:]), lambda b: (b,) + (0,) * (nd - 1))


def _diff_kernel(qi, far_ref, cst_ref, bnd_ref, qt_ref, k_ref, vt_ref, bias_ref, lam_ref, subg_ref, o_ref,
                 m_ref, acc_ref):
    qmaps = [_split_maps(_head_rows(qt_ref[0, qi], h)) for h in range(HPM)]
    m_ref[...] = jnp.full(m_ref.shape, NEG, F32)
    acc_ref[...] = jnp.zeros(acc_ref.shape, F32)
    r, c = _causal_tile()

    def process(tiles, bounded):
        scores = [[_dot(_head_cols(k_ref[0, ki], h), qmaps[h]) for ki, _ in tiles] for h in range(HPM)]
        probs = []
        for idx in range(2 * HPM):
            h = idx // 2
            bound = bnd_ref[BND_DIFF + 1 + h] if bounded else None
            ss = []
            for (ki, mode), sc in zip(tiles, scores[h]):
                s = sc[:, (idx % 2) * T:(idx % 2 + 1) * T]
                if mode == "far":
                    s = s + ((far_ref[h] - bound) if bounded else far_ref[h])
                elif mode == "near":
                    s = s + bias_ref[h, T:2 * T, :]
                else:
                    s = jnp.where(r <= c, s + bias_ref[h, 0:T, :], NEG)
                if bounded and mode != "far":
                    s = s - bound
                ss.append(s)
            probs.append(_softmax_probs(ss, m_ref, idx, fixed=0.0 if bounded else None))
        for idx in range(2 * HPM):
            h = idx // 2
            vts = [_with_ones(vt_ref[0, ki, h * HEAD_DIM:(h + 1) * HEAD_DIM, :]) for ki, _ in tiles]
            alpha, ps = probs[idx]
            _accumulate(acc_ref, slice(idx * VROWS, (idx + 1) * VROWS), alpha, vts, ps)

    _run_bounded_or_online(bnd_ref[BND_DIFF] > 0.5, qi, process)

    lam_init = cst_ref[0]
    lv = lam_ref[...]
    lam = (jnp.exp(jnp.sum(lv[0:1] * lv[1:2], axis=-1, keepdims=True))
           - jnp.exp(jnp.sum(lv[2:3] * lv[3:4], axis=-1, keepdims=True)) + lam_init)
    outs = []
    for h in range(HPM):
        o = _normalized(acc_ref, 2 * h) - lam * _normalized(acc_ref, 2 * h + 1)
        ms = jnp.mean(o * o, axis=0, keepdims=True)
        outs.append(o * lax.rsqrt(ms + EPS) * subg_ref[...] * (1.0 - lam_init))
    _store_query_tile(o_ref, qi, jnp.concatenate(outs, axis=0).T.astype(BF16))


def _diff_attention(far, cst, bnd, qt, k4, vt, bias, lam, subg):
    bsz, nq = qt.shape[0], qt.shape[1]
    return _attn_call(
        _diff_kernel, "diff_attention", bsz, nq,
        [far, cst, bnd, qt, k4, vt, bias, lam, subg],
        [_smem(), _smem(), _smem(), _seq_spec(qt.shape), _seq_spec(k4.shape), _seq_spec(vt.shape),
         _resident(bias.shape), _whole(lam.shape), _whole(subg.shape)],
        [pltpu.VMEM((2 * HPM, 1, T), F32), pltpu.VMEM((2 * HPM * VROWS, T), F32)])


def _fox_kernel(qi, bnd_ref, qt_ref, k_ref, vt_ref, cq_ref, ck_ref, o_ref, m_ref, acc_ref):
    qh = [_head_rows(qt_ref[0, qi], h) for h in range(HPM)]
    m_ref[...] = jnp.full(m_ref.shape, NEG, F32)
    acc_ref[...] = jnp.zeros(acc_ref.shape, F32)
    r, c = _causal_tile()

    def process(tiles, bounded):
        scores = [[_dot(_head_cols(k_ref[0, ki], h), qh[h]) for ki, _ in tiles] for h in range(HPM)]
        probs = []
        for h in range(HPM):
            ss = []
            for (ki, mode), sc in zip(tiles, scores[h]):
                s = sc - ck_ref[0, ki][:, h:h + 1]
                if mode == "diag":
                    s = jnp.where(r <= c, s, NEG)
                ss.append(s)
            cq = cq_ref[0, qi, h:h + 1, :]
            fixed = (bnd_ref[BND_FOX + 1] - cq) if bounded else None
            probs.append(_softmax_probs(ss, m_ref, h, shift=cq, fixed=fixed))
        for h in range(HPM):
            vts = [_with_ones(vt_ref[0, ki, h * HEAD_DIM:(h + 1) * HEAD_DIM, :]) for ki, _ in tiles]
            alpha, ps = probs[h]
            _accumulate(acc_ref, slice(h * VROWS, (h + 1) * VROWS), alpha, vts, ps)

    _run_bounded_or_online(bnd_ref[BND_FOX] > 0.5, qi, process)
    outs = [_normalized(acc_ref, h) for h in range(HPM)]
    _store_query_tile(o_ref, qi, jnp.concatenate(outs, axis=0).T.astype(BF16))


def _fox_attention(bnd, qt, k4, vt, st, ck4):
    bsz, nq = qt.shape[0], qt.shape[1]
    return _attn_call(
        _fox_kernel, "forgetting_attention", bsz, nq,
        [bnd, qt, k4, vt, st, ck4],
        [_smem(), _seq_spec(qt.shape), _seq_spec(k4.shape), _seq_spec(vt.shape),
         _seq_spec(st.shape), _seq_spec(ck4.shape)],
        [pltpu.VMEM((HPM, 1, T), F32), pltpu.VMEM((HPM * VROWS, T), F32)])


def _sb_kernel(qi, qt_ref, k_ref, vt_ref, tri_ref, o_ref, run_ref, acc_ref):
    qh = [_head_rows(qt_ref[0, qi], h) for h in range(HPM)]
    run_ref[...] = jnp.zeros(run_ref.shape, F32)
    acc_ref[...] = jnp.zeros(acc_ref.shape, F32)
    r, c = _causal_tile()

    def process(tiles):
        scores = [[_dot(_head_cols(k_ref[0, ki], h), qh[h]) for ki, _ in tiles] for h in range(HPM)]
        weights = []
        for h in range(HPM):
            run = run_ref[h]
            ws = []
            for (ki, diag), z in zip(tiles, scores[h]):
                log_beta = jnp.minimum(z, 0.0) - jnp.log2(1.0 + jnp.exp2(-jnp.abs(z)))
                log_keep = log_beta - z
                if diag:
                    log_keep = jnp.where(r < c, log_keep, 0.0)
                keep_b = log_keep.astype(BF16)
                tail = _dot(tri_ref[...], keep_b) + run
                a = jnp.exp2(log_beta + tail)
                if diag:
                    a = jnp.where(r < c, a, 0.0)
                ws.append(a.astype(BF16))
                run = tail[0:1, :] + keep_b[0:1, :].astype(F32)
            run_ref[h] = run
            weights.append(ws)
        for h in range(HPM):
            rows = slice(h * HEAD_DIM, (h + 1) * HEAD_DIM)
            upd = None
            for (ki, _), a in zip(tiles, weights[h]):
                term = _dot(vt_ref[0, ki, rows, :], a)
                upd = term if upd is None else upd + term
            acc_ref[rows, :] = acc_ref[rows, :] + upd

    rest = jnp.bitwise_and(qi, 3)
    for n in range(4):
        @pl.when(rest == n)
        def _():
            process([(qi, True)] + [(qi - d, False) for d in range(1, n + 1)])

    base = qi - 1 - rest

    def four(j, carry):
        process([(base - 4 * j - i, False) for i in range(4)])
        return carry

    lax.fori_loop(0, lax.shift_right_logical(qi, 2), four, 0)
    _store_query_tile(o_ref, qi, acc_ref[...].T.astype(BF16))


def _sb_attention(qt, k4, vt, tri):
    bsz, nq = qt.shape[0], qt.shape[1]
    return _attn_call(
        _sb_kernel, "stick_breaking_attention", bsz, nq,
        [qt, k4, vt, tri],
        [_seq_spec(qt.shape), _seq_spec(k4.shape), _seq_spec(vt.shape), _resident(tri.shape)],
        [pltpu.VMEM((HPM, 1, T), F32), pltpu.VMEM((GW, T), F32)])


def _nsa_kernel(qi, far_ref, bnd_ref, qt_ref, kc_ref, vct_ref, ks_ref, vst_ref, kw_ref, vwt_ref, g_ref,
                bias_ref, ovt_ref, o_ref,
                sc_ref, key_ref, sel_ref, m_ref, acc_ref):
    q0 = qi * T
    n_cmp = kc_ref.shape[1]
    n_blk = ovt_ref.shape[0]
    n_sel = min(N_SEL, n_blk)
    w4 = HPM * T
    qt = qt_ref[0, qi]
    qcat = jnp.concatenate([qt[h * HEAD_DIM:(h + 1) * HEAD_DIM, :] for h in range(HPM)], axis=1)
    far_row = jnp.concatenate([jnp.full((1, T), far_ref[HPM + h], F32) for h in range(HPM)], axis=1)
    r, c = _causal_tile(w4)

    sc_ref[...] = _dot(kc_ref[0], qcat)
    delta = bias_ref[2 * T:2 * T + BAND_ROWS, :] - far_row

    @pl.when(qi == 0)
    def _():
        half = BAND_ROWS // 2
        sc_ref[0:half, :] = sc_ref[0:half, :] + delta[half:BAND_ROWS]

    @pl.when(qi > 0)
    def _():
        c0 = pl.multiple_of(qi * (T // CMP_STRIDE) - BAND_ROWS // 2, 8)
        sc_ref[pl.ds(c0, BAND_ROWS), :] = sc_ref[pl.ds(c0, BAND_ROWS), :] + delta

    ci = lax.broadcasted_iota(jnp.int32, (n_cmp, w4), 0)
    ti = q0 + jnp.bitwise_and(lax.broadcasted_iota(jnp.int32, (n_cmp, w4), 1), T - 1)
    valid_c = ti - CMP_STRIDE * ci - (CMP_LEN - 1) >= 0
    s = jnp.where(valid_c, sc_ref[...] + far_row, NEG)
    e = jnp.exp2(s - jnp.max(s, axis=0, keepdims=True))
    p = jnp.where(valid_c, e / jnp.sum(e, axis=0, keepdims=True), 0.0)
    o_cmp = _dot(vct_ref[0], p.astype(BF16))
    pc_sum = p[:, 0:T]
    for h in range(1, HPM):
        pc_sum = pc_sum + p[:, h * T:(h + 1) * T]

    hi, lo = _split2(pc_sum)
    imp = _dot(ovt_ref[...], hi) + _dot(ovt_ref[...], lo)
    ji = lax.broadcasted_iota(jnp.int32, (n_blk, T), 0)
    tq = q0 + lax.broadcasted_iota(jnp.int32, (n_blk, T), 1)
    forced = (ji == lax.shift_right_logical(tq, int(math.log2(SEL_LEN)))) | (ji == 0)
    imp = jnp.where(forced, FORCED_SCORE, imp)
    imp = jnp.where(ji * SEL_LEN <= tq, imp, -1.0)
    keys = lax.bitcast_convert_type(imp, jnp.int32)
    key_ref[...] = keys

    def rank_body(jp, cnt):
        row = key_ref[pl.ds(jp, 1), :]
        below = lax.shift_right_arithmetic(jp - ji, 31)
        return cnt + jnp.where(row > keys + below, 1, 0)

    n_seen = jnp.minimum((qi + 1) * (T // SEL_LEN), n_blk)
    cnt = lax.fori_loop(0, n_seen, rank_body, jnp.zeros((n_blk, T), jnp.int32))
    sel = jnp.where(cnt < n_sel, 1.0, 0.0)
    sel_ref[...] = jnp.concatenate([sel] * HPM, axis=1)

    m_ref[...] = jnp.full(m_ref.shape, NEG, F32)
    acc_ref[...] = jnp.zeros(acc_ref.shape, F32)

    def bound_row(base):
        return jnp.concatenate([jnp.full((1, T), bnd_ref[base + 1 + h], F32) for h in range(HPM)], axis=1)

    def biased(s, mode, bound):
        if mode == "far":
            return s + (far_row if bound is None else far_row - bound)
        tile = bias_ref[T:2 * T, :] if mode == "near" else bias_ref[0:T, :]
        return s + tile if bound is None else s + tile - bound

    def update(state, ss, vts, bounded):
        alpha, ps = _softmax_probs(ss, m_ref, state, fixed=0.0 if bounded else None)
        _accumulate(acc_ref, slice(state * VROWS, (state + 1) * VROWS), alpha, vts, ps)

    def sel_mask(ki):
        per = T // SEL_LEN
        rows = [jnp.broadcast_to(sel_ref[pl.ds(ki * per + i, 1), :], (SEL_LEN, w4)) for i in range(per)]
        return jnp.concatenate(rows, axis=0) > 0.5

    def process_selected(tiles, bounded):
        bound = bound_row(BND_SEL) if bounded else None
        scores = [_dot(ks_ref[0, ki], qcat) for ki, _ in tiles]
        ss = []
        for (ki, mode), sc in zip(tiles, scores):
            mask = sel_mask(ki)
            if mode == "diag":
                mask = mask & (r <= c)
            ss.append(jnp.where(mask, biased(sc, mode, bound), NEG))
        update(0, ss, [_with_ones(vst_ref[0, ki]) for ki, _ in tiles], bounded)

    _run_bounded_or_online(bnd_ref[BND_SEL] > 0.5, qi, process_selected)

    def process_window(tiles, bounded):
        bound = bound_row(BND_WIN) if bounded else None
        scores = [_dot(kw_ref[0, ki], qcat) for ki, _, _ in tiles]
        ss = []
        for (ki, mode, mask), sc in zip(tiles, scores):
            s = biased(sc, mode, bound)
            ss.append(s if mask is None else jnp.where(mask, s, NEG))
        update(1, ss, [_with_ones(vwt_ref[0, ki]) for ki, _, _ in tiles], bounded)

    back = WINDOW // T

    def window_tiles(n):
        tiles = [(qi, "diag", r <= c)]
        for d in range(1, n + 1):
            mode = "near" if d == 1 else "far"
            tiles.append((qi - d, mode, (c < r) if d == back else None))
        return tiles

    win_bounded = bnd_ref[BND_WIN] > 0.5
    for n in range(back + 1):
        cond = (qi == n) if n < back else (qi >= back)

        @pl.when(cond & win_bounded)
        def _():
            process_window(window_tiles(n), True)

        @pl.when(cond & jnp.logical_not(win_bounded))
        def _():
            process_window(window_tiles(n), False)

    o_sel = acc_ref[0:HEAD_DIM, :] / acc_ref[HEAD_DIM:HEAD_DIM + 1, :]
    o_win = acc_ref[VROWS:VROWS + HEAD_DIM, :] / acc_ref[VROWS + HEAD_DIM:VROWS + HEAD_DIM + 1, :]
    gate = lambda br: jnp.concatenate(
        [g_ref[0, qi, HPM + br * HPM + h:HPM + br * HPM + h + 1, :] for h in range(HPM)], axis=1)
    o = gate(0) * o_cmp + gate(1) * o_sel + gate(2) * o_win
    ot = jnp.concatenate([o[:, h * T:(h + 1) * T] for h in range(HPM)], axis=0)
    _store_query_tile(o_ref, qi, ot.T.astype(BF16))


def _nsa_attention(far, bnd, qt, kc, vct, ks4, vst, kw4, vwt, st, bias, ovt):
    bsz, nq = qt.shape[0], qt.shape[1]
    n_cmp = kc.shape[1]
    n_blk = ovt.shape[0]
    w4 = HPM * T
    return _attn_call(
        _nsa_kernel, "native_sparse_attention", bsz, nq,
        [far, bnd, qt, kc, vct, ks4, vst, kw4, vwt, st, bias, ovt],
        [_smem(), _smem(), _seq_spec(qt.shape), _seq_spec(kc.shape), _seq_spec(vct.shape),
         _seq_spec(ks4.shape), _seq_spec(vst.shape), _seq_spec(kw4.shape), _seq_spec(vwt.shape),
         _seq_spec(st.shape), _resident(bias.shape), _whole(ovt.shape)],
        [pltpu.VMEM((n_cmp, w4), F32), pltpu.VMEM((n_blk, T), jnp.int32), pltpu.VMEM((n_blk, w4), F32),
         pltpu.VMEM((2, 1, w4), F32), pltpu.VMEM((2 * VROWS, w4), F32)])


def _post_kernel(x_ref, oa_ref, ob_ref, oc_ref, od_ref, wo_ref, gta_ref, g_ref, sc_ref, sh_ref,
                 gtm_ref, wup_ref, cw_ref, cb_ref, wdn_ref, y_ref, carry_ref):
    d_ff = wdn_ref.shape[1]
    mixed = (_dot(oa_ref[0], wo_ref[0, 0:GW]) + _dot(ob_ref[0], wo_ref[0, GW:2 * GW])
             + _dot(oc_ref[0], wo_ref[0, 2 * GW:3 * GW]) + _dot(od_ref[0], wo_ref[0, 3 * GW:4 * GW]))
    x1 = x_ref[0] + gta_ref[0] * mixed
    ms = jnp.mean(x1 * x1, axis=-1, keepdims=True)
    h = x1 * lax.rsqrt(ms + EPS) * g_ref[...]
    hb = (h * (1.0 + sc_ref[0]) + sh_ref[0]).astype(BF16)

    @pl.when(pl.program_id(1) == 0)
    def _():
        carry_ref[...] = jnp.zeros(carry_ref.shape, F32)

    def up(ch):
        gate = _dot(hb, wup_ref[0, :, ch * FF_CHUNK:(ch + 1) * FF_CHUNK])
        val = _dot(hb, wup_ref[0, :, d_ff + ch * FF_CHUNK:d_ff + (ch + 1) * FF_CHUNK])
        return gate, val

    row = lax.broadcasted_iota(jnp.int32, (TM_POST, FF_CHUNK), 0)
    y = jnp.zeros((TM_POST, x1.shape[1]), F32)
    n_chunks = d_ff // FF_CHUNK
    ahead = up(0)
    for ch in range(n_chunks):
        cols = slice(ch * FF_CHUNK, (ch + 1) * FF_CHUNK)
        gate, val = ahead
        if ch + 1 < n_chunks:
            ahead = up(ch + 1)
        prev = carry_ref[:, cols]
        g1 = jnp.where(row == 0, prev[7:8], pltpu.roll(gate, 1, 0))
        g2 = jnp.where(row == 0, prev[6:7], jnp.where(row == 1, prev[7:8], pltpu.roll(gate, 2, 0)))
        carry_ref[:, cols] = gate[TM_POST - 8:TM_POST]
        conv = (cw_ref[0, 0:1, cols] * g2 + cw_ref[0, 1:2, cols] * g1 + cw_ref[0, 2:3, cols] * gate
                + cb_ref[0, :, cols])
        act = conv * _sigmoid(conv) * val
        y = y + _dot(act.astype(BF16), wdn_ref[0, cols, :])
    y_ref[0] = x1 + gtm_ref[0] * y


def _post(layer, x, oa, ob, oc, od, wo, gta, g, sc, sh, gtm, wup, cw, cb, wdn):
    bsz, s, d = x.shape
    tok = lambda width: pl.BlockSpec((1, TM_POST, width), lambda b, i: (b, i, 0))
    vec = pl.BlockSpec((1, 1, d), lambda b, i: (b, 0, 0))
    of_layer = lambda a: pl.BlockSpec((1,) + a.shape[1:], lambda b, i: (layer,) + (0,) * (a.ndim - 1),
                                      pipeline_mode=pl.Buffered(1))
    return pl.pallas_call(
        _post_kernel,
        out_shape=jax.ShapeDtypeStruct((bsz, s, d), F32),
        grid=(bsz, s // TM_POST),
        in_specs=[tok(d), tok(GW), tok(GW), tok(GW), tok(GW), of_layer(wo), vec, _whole((1, d)), vec, vec,
                  vec, of_layer(wup), of_layer(cw), of_layer(cb), of_layer(wdn)],
        out_specs=tok(d),
        scratch_shapes=[pltpu.VMEM((8, wdn.shape[1]), F32)],
        compiler_params=pltpu.CompilerParams(
            dimension_semantics=("parallel", "arbitrary"), vmem_limit_bytes=VMEM_LIMIT),
        name="out_proj_mlp",
    )(x, oa, ob, oc, od, wo, gta, g, sc, sh, gtm, wup, cw, cb, wdn)


def _repack_w_in(w):
    off = {}
    pos = 0
    for name, size in (("a_q", GW), ("a_k", GW), ("a_v", GW), ("b_q", GW), ("b_k", GW), ("b_v", GW),
                       ("b_f", HPM), ("c_q", GW), ("c_k", GW), ("c_v", GW), ("d_q", GW),
                       ("d_kc", HEAD_DIM), ("d_vc", HEAD_DIM), ("d_ks", HEAD_DIM), ("d_vs", HEAD_DIM),
                       ("d_kw", HEAD_DIM), ("d_vw", HEAD_DIM), ("d_g", 3 * HPM)):
        off[name] = (pos, size)
        pos += size
    assert pos == w.shape[-1]
    col = lambda n: w[..., off[n][0]:off[n][0] + off[n][1]]
    order = ["a_q", "a_k", "a_v", "b_q", "b_k", "b_v", "c_q", "c_k", "c_v", "d_q",
             "d_ks", "d_vs", "d_kw", "d_vw", "d_kc", "d_vc", "b_f", "d_g"]
    parts = [col(n) for n in order] + [jnp.zeros(w.shape[:-1] + (128 - HPM - 3 * HPM,), w.dtype)]
    out = jnp.concatenate(parts, axis=-1).astype(BF16)
    assert out.shape[-1] == N_PROJ
    return out


def _compress_params(pe, phi_w):
    half = CMP_LEN // 2
    pe_rows = jnp.concatenate([pe[0], pe[1]], axis=1).reshape(2, half * 2 * HEAD_DIM)
    wk = phi_w[0].reshape(CMP_LEN, HEAD_DIM, HEAD_DIM)
    wv = phi_w[1].reshape(CMP_LEN, HEAD_DIM, HEAD_DIM)
    zero = jnp.zeros_like(wk)
    blk = jnp.concatenate([jnp.concatenate([wk, zero], axis=2), jnp.concatenate([zero, wv], axis=2)], axis=1)
    return pe_rows, blk.reshape(2, half * 2 * HEAD_DIM, 2 * HEAD_DIM).astype(BF16)


def _score_bounds(rel_bias, gq_a, gk_a, gq_b, gk_b, gq_d, gk_s, gk_w):
    amax = lambda g: jnp.max(jnp.abs(g))
    tb = rel_bias * LOG2E
    bmax, bmin = jnp.max(tb, axis=0), jnp.min(tb, axis=0)
    span = jnp.max(bmax - bmin)
    qk = lambda n, gq, gk, scale: n * amax(gq) * amax(gk) * (scale * LOG2E * BOUND_MARGIN)
    ok = lambda b, sp: (2.0 * b + sp <= MAX_LOG2_SPAN).astype(F32).reshape(1)
    b_fox = qk(HEAD_DIM, gq_b, gk_b, HEAD_DIM ** -0.5)
    b_a = qk(DIFF_QK_DIM, gq_a, gk_a, DIFF_QK_DIM ** -0.5)
    b_s = qk(HEAD_DIM, gq_d, gk_s, HEAD_DIM ** -0.5)
    b_w = qk(HEAD_DIM, gq_d, gk_w, HEAD_DIM ** -0.5)
    vec = jnp.concatenate([
        ok(b_fox, 0.0), b_fox.reshape(1),
        ok(b_a, span), b_a + bmax[:HPM],
        ok(b_s, span), b_s + bmax[HPM:],
        ok(b_w, span), b_w + bmax[HPM:]])
    return jnp.pad(vec, (0, BND_SIZE - vec.shape[0]))


def kernel(x, c, rel_bias, ada_w, ada_b, norm_mix_g, norm_ffn_g, w_in, w_out, diff_qnorm_g, diff_knorm_g, diff_lambda, diff_subln_g, fox_qnorm_g, fox_knorm_g, fox_b_f, nsa_qnorm_g, nsa_knorm_g, nsa_pe, nsa_phi_w, ffn_w_up, ffn_conv_w, ffn_conv_b, ffn_w_down):
    bsz, s, d = x.shape
    depth = ada_w.shape[0]
    assert s % T == 0 and WINDOW % T == 0 and T % SEL_LEN == 0 and d == 4 * GW
    assert T & (T - 1) == 0 and BAND_ROWS // 2 >= (MAX_DISTANCE + CMP_LEN) // CMP_STRIDE
    nk = s // T
    n_chunk = s // CMP_STRIDE
    n_blk = s // SEL_LEN
    d_ff = ffn_w_down.shape[1]
    assert d_ff % FF_CHUNK == 0

    bd32 = jnp.asarray(_block_ones(GW, DIFF_QK_DIM), BF16)
    bd64 = jnp.asarray(_block_ones(GW, HEAD_DIM), BF16)
    ti = np.arange(T)
    tri_incl = jnp.asarray((ti[None, :] <= ti[:, None]).astype(np.float32), BF16)
    tri_later = jnp.asarray((ti[None, :] > ti[:, None]).astype(np.float32), BF16)
    c_start = np.arange(n_chunk) * CMP_STRIDE
    s_start = np.arange(n_blk) * SEL_LEN
    ov = ((c_start[None, :] < s_start[:, None] + SEL_LEN) & (s_start[:, None] < c_start[None, :] + CMP_LEN)
          & (np.arange(n_chunk)[None, :] < n_chunk - 1))
    ovt = jnp.asarray(ov.astype(np.float32), BF16)

    bias_a, bias_d = _bias_tiles(rel_bias)
    far = rel_bias[NUM_BUCKETS - 1] * LOG2E
    mod = _adaln(c, ada_w, ada_b)

    w_re = _repack_w_in(w_in)
    wo_b, wup_b, wdn_b = w_out.astype(BF16), ffn_w_up.astype(BF16), ffn_w_down.astype(BF16)
    conv_w = ffn_conv_w.reshape(depth, CONV_WIDTH, d_ff)
    conv_b = ffn_conv_b.reshape(depth, 1, d_ff)

    for l in range(depth):
        sh_a, sc_a, gt_a, sh_m, sc_m, gt_m = [m.reshape(bsz, 1, d) for m in jnp.split(mod[l], 6, axis=-1)]
        scale_a = DIFF_QK_DIM ** -0.5 * LOG2E
        scale = HEAD_DIM ** -0.5 * LOG2E
        ones = jnp.ones((HEAD_DIM,), F32)
        gains = jnp.stack([
            jnp.tile(diff_qnorm_g[l] * scale_a, GW // DIFF_QK_DIM), jnp.tile(diff_knorm_g[l], GW // DIFF_QK_DIM),
            jnp.tile(fox_qnorm_g[l] * scale, HPM), jnp.tile(fox_knorm_g[l], HPM),
            jnp.tile(nsa_qnorm_g[l] * scale, HPM),
            jnp.concatenate([nsa_knorm_g[l, 1], ones, nsa_knorm_g[l, 2], ones]),
            jnp.ones((GW,), F32), jnp.ones((GW,), F32)])
        (a_qt, a_k, a_vt, b_qt, b_k, b_vt, c_qt, c_k, c_vt, d_qt,
         d_ks, d_kw, d_vst, d_vwt, d_cmp, smalls) = _inproj(
            l, x, norm_mix_g[l].reshape(1, d), sc_a, sh_a, w_re, gains, bd32, bd64)

        bf_row = jnp.pad(fox_b_f[l], (0, 128 - HPM)).reshape(1, 128)
        ck, st = _scalars(smalls, bf_row, tri_incl)

        pe_rows, w_blk = _compress_params(nsa_pe[l], nsa_phi_w[l])
        gk_pad = jnp.pad(nsa_knorm_g[l, 0], (0, 128 - HEAD_DIM)).reshape(1, 128)
        kc, vct = _compress(d_cmp, pe_rows, w_blk, gk_pad)

        bnd = _score_bounds(rel_bias, diff_qnorm_g[l], diff_knorm_g[l], fox_qnorm_g[l], fox_knorm_g[l],
                            nsa_qnorm_g[l], nsa_knorm_g[l, 1], nsa_knorm_g[l, 2])
        lam_init = 0.8 - 0.6 * math.exp(-0.3 * l)
        cst = jnp.full((1,), lam_init, F32)
        k4 = lambda a: a.reshape(bsz, nk, T, a.shape[-1])
        o_a = _diff_attention(far, cst, bnd, a_qt, k4(a_k), a_vt, bias_a, diff_lambda[l],
                              diff_subln_g[l].reshape(HEAD_DIM, 1))
        o_b = _fox_attention(bnd, b_qt, k4(b_k), b_vt, st, k4(ck))
        o_c = _sb_attention(c_qt, k4(c_k), c_vt, tri_later)
        o_d = _nsa_attention(far, bnd, d_qt, kc, vct, k4(d_ks), d_vst, k4(d_kw), d_vwt, st, bias_d, ovt)

        x = _post(l, x, o_a, o_b, o_c, o_d, wo_b, gt_a, norm_ffn_g[l].reshape(1, d),
                  sc_m, sh_m, gt_m, wup_b, conv_w, conv_b, wdn_b)
    return x
```

```python
import math

import numpy as np
import jax
import jax.numpy as jnp
from jax import lax
from jax.experimental import pallas as pl
from jax.experimental.pallas import tpu as pltpu

HEAD_DIM = 64
HPM = 4
GW = HPM * HEAD_DIM
DIFF_QK_DIM = HEAD_DIM // 2
NUM_BUCKETS = 32
MAX_EXACT = NUM_BUCKETS // 2
MAX_DISTANCE = 128
CMP_LEN = 32
CMP_STRIDE = 16
SEL_LEN = 64
N_SEL = 16
WINDOW = 512
FORCED_SCORE = 1.0e4
NEG = -1.0e30
EPS = 1e-6
CONV_WIDTH = 3
LOG2E = 1.4426950408889634

T = 256
TM = 2 * T
TM_POST = 256
FF_CHUNK = 256
ONES_ROWS = 16
VROWS = HEAD_DIM + ONES_ROWS
BAND_ROWS = 2 * T // CMP_STRIDE
VMEM_LIMIT = 56 * 1024 * 1024

BND_FOX, BND_DIFF, BND_SEL, BND_WIN, BND_SIZE = 0, 2, 7, 12, 24
BOUND_MARGIN = 1.02
MAX_LOG2_SPAN = 100.0

F32 = jnp.float32
BF16 = jnp.bfloat16


def _dot(a, b):
    return jnp.dot(a, b, preferred_element_type=F32)


def _split2(x):
    hi = x.astype(BF16)
    lo = (x - hi.astype(F32)).astype(BF16)
    return hi, lo


def _log_sigmoid(z):
    return jnp.minimum(z, 0.0) - jnp.log(1.0 + jnp.exp(-jnp.abs(z)))


def _sigmoid(z):
    return 1.0 / (1.0 + jnp.exp(-z))


def _whole(shape):
    nd = len(shape)
    return pl.BlockSpec(shape, lambda *_: (0,) * nd)


def _resident(shape):
    nd = len(shape)
    return pl.BlockSpec(shape, lambda *_: (0,) * nd, pipeline_mode=pl.Buffered(1))


def _smem():
    return pl.BlockSpec(memory_space=pltpu.SMEM)


def _t5_bucket_np(dist):
    n = np.maximum(dist, 0)
    ratio = math.log(MAX_DISTANCE / MAX_EXACT)
    out = None
    for dt in (np.float32, np.float64):
        large = MAX_EXACT + (np.log(np.maximum(n, 1).astype(dt) / dt(MAX_EXACT)) / dt(ratio)
                             * dt(NUM_BUCKETS - MAX_EXACT)).astype(np.int32)
        b = np.where(n < MAX_EXACT, n, np.minimum(large, NUM_BUCKETS - 1)).astype(np.int32)
        assert out is None or np.array_equal(out, b)
        out = b
    return out


def _bucket_tiles():
    r = np.arange(T)[:, None]
    c = np.arange(T)[None, :]
    diag = _t5_bucket_np(c - r)
    near = _t5_bucket_np(T + c - r)
    rb = np.arange(BAND_ROWS)[:, None]
    band = _t5_bucket_np(c - CMP_STRIDE * rb + (T - CMP_LEN + 1))
    return np.concatenate([diag, near, band], axis=0)


def _block_ones(n, group):
    i = np.arange(n)
    return (i[:, None] // group == i[None, :] // group).astype(np.float32)


def _bias_kernel(tbl_ref, bkt_ref, a_ref, d_ref):
    b = bkt_ref[...]
    for h in range(2 * HPM):
        acc = jnp.zeros(b.shape, F32)
        for k in range(NUM_BUCKETS):
            acc = jnp.where(b == k, tbl_ref[k, h] * LOG2E, acc)
        if h < HPM:
            a_ref[h] = acc[0:2 * T]
        else:
            d_ref[:, (h - HPM) * T:(h - HPM + 1) * T] = acc


def _bias_tiles(rel_bias):
    bkt = jnp.asarray(_bucket_tiles())
    rows = bkt.shape[0]
    return pl.pallas_call(
        _bias_kernel,
        out_shape=[jax.ShapeDtypeStruct((HPM, 2 * T, T), F32), jax.ShapeDtypeStruct((rows, HPM * T), F32)],
        in_specs=[_smem(), _whole(bkt.shape)],
        out_specs=[_whole((HPM, 2 * T, T)), _whole((rows, HPM * T))],
        name="bias_tiles",
    )(rel_bias, bkt)


def _adaln_kernel(c_ref, w_ref, b_ref, o_ref):
    c = c_ref[...]
    ca = c * _sigmoid(c)
    o_ref[0] = _dot(ca.astype(BF16), w_ref[0].astype(BF16)) + b_ref[0]


def _adaln(c, ada_w, ada_b):
    depth, d, n = ada_w.shape
    bsz = c.shape[0]
    rows = -(-bsz // 8) * 8
    cp = jnp.pad(c, ((0, rows - bsz), (0, 0)))
    tn = 1536
    out = pl.pallas_call(
        _adaln_kernel,
        out_shape=jax.ShapeDtypeStruct((depth, rows, n), F32),
        grid=(depth, n // tn),
        in_specs=[pl.BlockSpec((rows, d), lambda l, j: (0, 0)),
                  pl.BlockSpec((1, d, tn), lambda l, j: (l, 0, j)),
                  pl.BlockSpec((1, 1, tn), lambda l, j: (l, 0, j))],
        out_specs=pl.BlockSpec((1, rows, tn), lambda l, j: (l, 0, j)),
        name="adaln",
    )(cp, ada_w, ada_b.reshape(depth, 1, n))
    return out[:, :bsz]


N_PROJ = 11 * GW + 2 * 128


def _inproj_kernel(x_ref, g_ref, sc_ref, sh_ref, w_ref, gains_ref, bd32_ref, bd64_ref,
                   aq_ref, ak_ref, av_ref, bq_ref, bk_ref, bv_ref, cq_ref, ck_ref, cv_ref, dq_ref,
                   dks_ref, dkw_ref, dvs_ref, dvw_ref, cmp_ref, sm_ref):
    x = x_ref[0]
    ms = jnp.mean(x * x, axis=-1, keepdims=True)
    h = x * lax.rsqrt(ms + EPS) * g_ref[...]
    h = h * (1.0 + sc_ref[0]) + sh_ref[0]
    hb = h.astype(BF16)

    def proj(group, width=GW):
        off = group * GW
        return _dot(hb, w_ref[0, :, off:off + width])

    def segnorm(y, bd_ref, inv_n, gain_row):
        ss = _dot((y * y).astype(BF16), bd_ref[...])
        return y * lax.rsqrt(ss * inv_n + EPS) * gains_ref[gain_row:gain_row + 1, :]

    def put_t(ref, y):
        for j in range(TM // T):
            ref[0, j] = y[j * T:(j + 1) * T].T.astype(BF16)

    raw0 = proj(0)
    raw1 = proj(1)
    put_t(aq_ref, segnorm(raw0, bd32_ref, 1.0 / DIFF_QK_DIM, 0))
    put_t(av_ref, proj(2))
    raw3 = proj(3)
    ak_ref[0] = segnorm(raw1, bd32_ref, 1.0 / DIFF_QK_DIM, 1).astype(BF16)
    put_t(bv_ref, proj(5))
    raw4 = proj(4)
    put_t(bq_ref, segnorm(raw3, bd64_ref, 1.0 / HEAD_DIM, 2))
    put_t(cq_ref, proj(6) * (HEAD_DIM ** -0.5 * LOG2E))
    raw9 = proj(9)
    bk_ref[0] = segnorm(raw4, bd64_ref, 1.0 / HEAD_DIM, 3).astype(BF16)
    ck_ref[0] = proj(7).astype(BF16)
    y = proj(10)
    put_t(dq_ref, segnorm(raw9, bd64_ref, 1.0 / HEAD_DIM, 4))
    put_t(cv_ref, proj(8))
    tail = proj(11)
    cmp_ref[0] = tail[:, 0:128]
    sm_ref[0] = tail[:, 128:256]

    yn = segnorm(y, bd64_ref, 1.0 / HEAD_DIM, 5)
    dks_ref[0] = yn[:, 0:HEAD_DIM].astype(BF16)
    dkw_ref[0] = yn[:, 2 * HEAD_DIM:3 * HEAD_DIM].astype(BF16)
    for j in range(TM // T):
        yt = y[j * T:(j + 1) * T].T
        dvs_ref[0, j] = yt[HEAD_DIM:2 * HEAD_DIM].astype(BF16)
        dvw_ref[0, j] = yt[3 * HEAD_DIM:4 * HEAD_DIM].astype(BF16)


def _inproj(layer, x, g, sc, sh, w_re, gains, bd32, bd64):
    bsz, s, d = x.shape
    nk = s // T
    tok = lambda width, dt: jax.ShapeDtypeStruct((bsz, s, width), dt)
    tr = lambda rows: jax.ShapeDtypeStruct((bsz, nk, rows, T), BF16)
    tok_spec = lambda width: pl.BlockSpec((1, TM, width), lambda b, i: (b, i, 0))
    tr_spec = lambda rows: pl.BlockSpec((1, TM // T, rows, T), lambda b, i: (b, i, 0, 0))
    vec = pl.BlockSpec((1, 1, d), lambda b, i: (b, 0, 0))
    out_shape = [tr(GW), tok(GW, BF16), tr(GW)] * 3 + [tr(GW)] + [
        tok(HEAD_DIM, BF16), tok(HEAD_DIM, BF16), tr(HEAD_DIM), tr(HEAD_DIM),
        tok(128, F32), tok(128, F32)]
    out_specs = [tr_spec(GW), tok_spec(GW), tr_spec(GW)] * 3 + [tr_spec(GW)] + [
        tok_spec(HEAD_DIM), tok_spec(HEAD_DIM), tr_spec(HEAD_DIM), tr_spec(HEAD_DIM),
        tok_spec(128), tok_spec(128)]
    return pl.pallas_call(
        _inproj_kernel,
        out_shape=out_shape,
        grid=(bsz, s // TM),
        in_specs=[pl.BlockSpec((1, TM, d), lambda b, i: (b, i, 0)),
                  _whole((1, d)), vec, vec,
                  pl.BlockSpec((1,) + w_re.shape[1:], lambda b, i: (layer, 0, 0)),
                  _whole(gains.shape), _whole(bd32.shape), _whole(bd64.shape)],
        out_specs=out_specs,
        compiler_params=pltpu.CompilerParams(
            dimension_semantics=("parallel", "parallel"), vmem_limit_bytes=VMEM_LIMIT),
        name="inproj",
    )(x, g, sc, sh, w_re, gains, bd32, bd64)


def _scalars_kernel(sm_ref, bf_ref, tri_ref, ck_ref, st_ref):
    s = sm_ref.shape[1]
    carry = jnp.zeros((1, 128), F32)
    for blk in range(s // T):
        rows = slice(blk * T, (blk + 1) * T)
        lf = _log_sigmoid(sm_ref[0, rows, :] + bf_ref[...])
        h1 = lf.astype(BF16)
        r1 = lf - h1.astype(F32)
        h2 = r1.astype(BF16)
        h3 = (r1 - h2.astype(F32)).astype(BF16)
        tri = tri_ref[...]
        cb = _dot(tri, h1) + _dot(tri, h2) + _dot(tri, h3) + carry
        carry = cb[T - 1:T, :]
        cb2 = cb * LOG2E
        ck_ref[0, rows, :] = cb2
        col = lax.broadcasted_iota(jnp.int32, (T, 128), 1)
        comb = jnp.where(col < HPM, cb2, _sigmoid(sm_ref[0, rows, :]))
        st_ref[0, blk] = comb.T[0:16]


def _scalars(smalls, bf_row, tri):
    bsz, s, _ = smalls.shape
    return pl.pallas_call(
        _scalars_kernel,
        out_shape=[jax.ShapeDtypeStruct((bsz, s, 128), F32), jax.ShapeDtypeStruct((bsz, s // T, 16, T), F32)],
        grid=(bsz,),
        in_specs=[pl.BlockSpec((1, s, 128), lambda b: (b, 0, 0)), _whole((1, 128)), _whole((T, T))],
        out_specs=[pl.BlockSpec((1, s, 128), lambda b: (b, 0, 0)),
                   pl.BlockSpec((1, s // T, 16, T), lambda b: (b, 0, 0, 0))],
        compiler_params=pltpu.CompilerParams(dimension_semantics=("parallel",)),
        name="token_scalars",
    )(smalls, bf_row, tri)


def _compress_kernel(x_ref, pe_ref, w_ref, gk_ref, kc_ref, vct_ref):
    n = x_ref.shape[1] // CMP_STRIDE
    chunks = jnp.concatenate(
        [x_ref[0, pl.ds(i, n, stride=CMP_STRIDE), :] for i in range(CMP_STRIDE)], axis=1)
    ya = _dot((chunks + pe_ref[0:1]).astype(BF16), w_ref[0])
    yb = _dot((chunks + pe_ref[1:2]).astype(BF16), w_ref[1])
    y = ya + pltpu.roll(yb, n - 1, 0)
    lane = lax.broadcasted_iota(jnp.int32, y.shape, 1)
    ss = jnp.sum(jnp.where(lane < HEAD_DIM, y * y, 0.0), axis=-1, keepdims=True)
    kc = y * lax.rsqrt(ss * (1.0 / HEAD_DIM) + EPS) * gk_ref[...]
    kc_ref[0] = kc[:, 0:HEAD_DIM].astype(BF16)
    vct_ref[0] = y.T[HEAD_DIM:2 * HEAD_DIM].astype(BF16)


def _compress(cmp, pe_rows, w_blk, gk_pad):
    bsz, s, width = cmp.shape
    n = s // CMP_STRIDE
    return pl.pallas_call(
        _compress_kernel,
        out_shape=[jax.ShapeDtypeStruct((bsz, n, HEAD_DIM), BF16),
                   jax.ShapeDtypeStruct((bsz, HEAD_DIM, n), BF16)],
        grid=(bsz,),
        in_specs=[pl.BlockSpec((1, s, width), lambda b: (b, 0, 0)),
                  _whole(pe_rows.shape), _whole(w_blk.shape), _whole(gk_pad.shape)],
        out_specs=[pl.BlockSpec((1, n, HEAD_DIM), lambda b: (b, 0, 0)),
                   pl.BlockSpec((1, HEAD_DIM, n), lambda b: (b, 0, 0))],
        compiler_params=pltpu.CompilerParams(dimension_semantics=("parallel",)),
        name="nsa_compress",
    )(cmp, pe_rows, w_blk, gk_pad)


def _head_rows(qt, h):
    return qt[h * HEAD_DIM:(h + 1) * HEAD_DIM, :]


def _head_cols(kt, h):
    return kt[:, h * HEAD_DIM:(h + 1) * HEAD_DIM]


def _split_maps(qh):
    qf = qh.astype(F32)
    row = lax.broadcasted_iota(jnp.int32, qf.shape, 0)
    first = jnp.where(row < DIFF_QK_DIM, qf, 0.0).astype(BF16)
    second = jnp.where(row >= DIFF_QK_DIM, qf, 0.0).astype(BF16)
    return jnp.concatenate([first, second], axis=1)


def _with_ones(vt):
    return jnp.concatenate([vt, jnp.ones((ONES_ROWS, vt.shape[1]), BF16)], axis=0)


def _softmax_probs(ss, m_ref, idx, shift=None, fixed=None):
    if fixed is not None:
        already = isinstance(fixed, float) and fixed == 0.0
        return None, [jnp.exp2(s if already else s - fixed).astype(BF16) for s in ss]
    m_old = m_ref[idx]
    mx = jnp.max(ss[0], axis=0, keepdims=True)
    for s in ss[1:]:
        mx = jnp.maximum(mx, jnp.max(s, axis=0, keepdims=True))
    if shift is not None:
        mx = mx + shift
    m_new = jnp.maximum(m_old, mx)
    m_ref[idx] = m_new
    alpha = jnp.exp2(m_old - m_new)
    sub = m_new if shift is None else m_new - shift
    return alpha, [jnp.exp2(s - sub).astype(BF16) for s in ss]


def _accumulate(acc_ref, rows, alpha, vts, ps):
    upd = _dot(vts[0], ps[0])
    for vt, p in zip(vts[1:], ps[1:]):
        upd = upd + _dot(vt, p)
    acc_ref[rows, :] = (acc_ref[rows, :] if alpha is None else alpha * acc_ref[rows, :]) + upd


def _normalized(acc_ref, idx):
    base = idx * VROWS
    return acc_ref[base:base + HEAD_DIM, :] / acc_ref[base + HEAD_DIM:base + HEAD_DIM + 1, :]


def _causal_tile(width=T):
    r = lax.broadcasted_iota(jnp.int32, (T, width), 0)
    c = lax.broadcasted_iota(jnp.int32, (T, width), 1)
    if width != T:
        c = jnp.bitwise_and(c, T - 1)
    return r, c


def _for_tile_groups(qi, process):
    n_far = jnp.maximum(qi - 1, 0)
    n_pair = lax.shift_right_logical(n_far, 1)
    odd = jnp.bitwise_and(n_far, 1) == 1

    def pair(j, carry):
        process([(2 * j, "far"), (2 * j + 1, "far")])
        return carry

    lax.fori_loop(0, n_pair, pair, 0)

    @pl.when(qi == 0)
    def _():
        process([(qi, "diag")])

    @pl.when((qi >= 1) & jnp.logical_not(odd))
    def _():
        process([(qi - 1, "near"), (qi, "diag")])

    @pl.when((qi >= 1) & odd)
    def _():
        process([(qi - 2, "far"), (qi - 1, "near"), (qi, "diag")])


def _for_independent_tiles(qi, process):
    n_far = jnp.maximum(qi - 1, 0)
    n_group = lax.shift_right_logical(n_far, 2)
    rest = jnp.bitwise_and(n_far, 3)

    def many(j, carry):
        process([(4 * j + i, "far") for i in range(4)])
        return carry

    lax.fori_loop(0, n_group, many, 0)

    @pl.when(qi == 0)
    def _():
        process([(qi, "diag")])

    for n in range(4):
        @pl.when((qi >= 1) & (rest == n))
        def _():
            process([(4 * n_group + i, "far") for i in range(n)] + [(qi - 1, "near"), (qi, "diag")])


def _run_bounded_or_online(bounded_ok, qi, process):
    @pl.when(bounded_ok)
    def _():
        _for_independent_tiles(qi, lambda tiles: process(tiles, True))

    @pl.when(jnp.logical_not(bounded_ok))
    def _():
        _for_tile_groups(qi, lambda tiles: process(tiles, False))


def _attn_call(tile_kernel, name, bsz, nq, in_arrays, in_specs, scratch, out_width=GW):
    def kernel(*refs):
        def query_tile(qi, carry):
            tile_kernel(qi, *refs)
            return carry

        lax.fori_loop(0, nq, query_tile, 0)

    return pl.pallas_call(
        kernel,
        out_shape=jax.ShapeDtypeStruct((bsz, nq * T, out_width), BF16),
        grid=(bsz,),
        in_specs=in_specs,
        out_specs=pl.BlockSpec((1, nq * T, out_width), lambda b: (b, 0, 0)),
        scratch_shapes=scratch,
        compiler_params=pltpu.CompilerParams(
            dimension_semantics=("parallel",), vmem_limit_bytes=VMEM_LIMIT),
        name=name,
    )(*in_arrays)


def _store_query_tile(o_ref, qi, value):
    o_ref[0, pl.ds(pl.multiple_of(qi * T, T), T), :] = value


def _seq_spec(shape):
    nd = len(shape)
    return pl.BlockSpec((1,) + tuple(shape[1:]), lambda b: (b,) + (0,) * (nd - 1))


def _diff_kernel(qi, far_ref, cst_ref, bnd_ref, qt_ref, k_ref, vt_ref, bias_ref, lam_ref, subg_ref, o_ref,
                 m_ref, acc_ref):
    qmaps = [_split_maps(_head_rows(qt_ref[0, qi], h)) for h in range(HPM)]
    m_ref[...] = jnp.full(m_ref.shape, NEG, F32)
    acc_ref[...] = jnp.zeros(acc_ref.shape, F32)
    r, c = _causal_tile()

    def process(tiles, bounded):
        scores = [[_dot(_head_cols(k_ref[0, ki], h), qmaps[h]) for ki, _ in tiles] for h in range(HPM)]
        probs = []
        for idx in range(2 * HPM):
            h = idx // 2
            bound = bnd_ref[BND_DIFF + 1 + h] if bounded else None
            ss = []
            for (ki, mode), sc in zip(tiles, scores[h]):
                s = sc[:, (idx % 2) * T:(idx % 2 + 1) * T]
                if mode == "far":
                    s = s + ((far_ref[h] - bound) if bounded else far_ref[h])
                elif mode == "near":
                    s = s + bias_ref[h, T:2 * T, :]
                else:
                    s = jnp.where(r <= c, s + bias_ref[h, 0:T, :], NEG)
                if bounded and mode != "far":
                    s = s - bound
                ss.append(s)
            probs.append(_softmax_probs(ss, m_ref, idx, fixed=0.0 if bounded else None))
        for idx in range(2 * HPM):
            h = idx // 2
            vts = [_with_ones(vt_ref[0, ki, h * HEAD_DIM:(h + 1) * HEAD_DIM, :]) for ki, _ in tiles]
            alpha, ps = probs[idx]
            _accumulate(acc_ref, slice(idx * VROWS, (idx + 1) * VROWS), alpha, vts, ps)

    _run_bounded_or_online(bnd_ref[BND_DIFF] > 0.5, qi, process)

    lam_init = cst_ref[0]
    lv = lam_ref[...]
    lam = (jnp.exp(jnp.sum(lv[0:1] * lv[1:2], axis=-1, keepdims=True))
           - jnp.exp(jnp.sum(lv[2:3] * lv[3:4], axis=-1, keepdims=True)) + lam_init)
    outs = []
    for h in range(HPM):
        o = _normalized(acc_ref, 2 * h) - lam * _normalized(acc_ref, 2 * h + 1)
        ms = jnp.mean(o * o, axis=0, keepdims=True)
        outs.append(o * lax.rsqrt(ms + EPS) * subg_ref[...] * (1.0 - lam_init))
    _store_query_tile(o_ref, qi, jnp.concatenate(outs, axis=0).T.astype(BF16))


def _diff_attention(far, cst, bnd, qt, k4, vt, bias, lam, subg):
    bsz, nq = qt.shape[0], qt.shape[1]
    return _attn_call(
        _diff_kernel, "diff_attention", bsz, nq,
        [far, cst, bnd, qt, k4, vt, bias, lam, subg],
        [_smem(), _smem(), _smem(), _seq_spec(qt.shape), _seq_spec(k4.shape), _seq_spec(vt.shape),
         _resident(bias.shape), _whole(lam.shape), _whole(subg.shape)],
        [pltpu.VMEM((2 * HPM, 1, T), F32), pltpu.VMEM((2 * HPM * VROWS, T), F32)])


def _fox_kernel(qi, bnd_ref, qt_ref, k_ref, vt_ref, cq_ref, ck_ref, o_ref, m_ref, acc_ref):
    qh = [_head_rows(qt_ref[0, qi], h) for h in range(HPM)]
    m_ref[...] = jnp.full(m_ref.shape, NEG, F32)
    acc_ref[...] = jnp.zeros(acc_ref.shape, F32)
    r, c = _causal_tile()

    def process(tiles, bounded):
        scores = [[_dot(_head_cols(k_ref[0, ki], h), qh[h]) for ki, _ in tiles] for h in range(HPM)]
        probs = []
        for h in range(HPM):
            ss = []
            for (ki, mode), sc in zip(tiles, scores[h]):
                s = sc - ck_ref[0, ki][:, h:h + 1]
                if mode == "diag":
                    s = jnp.where(r <= c, s, NEG)
                ss.append(s)
            cq = cq_ref[0, qi, h:h + 1, :]
            fixed = (bnd_ref[BND_FOX + 1] - cq) if bounded else None
            probs.append(_softmax_probs(ss, m_ref, h, shift=cq, fixed=fixed))
        for h in range(HPM):
            vts = [_with_ones(vt_ref[0, ki, h * HEAD_DIM:(h + 1) * HEAD_DIM, :]) for ki, _ in tiles]
            alpha, ps = probs[h]
            _accumulate(acc_ref, slice(h * VROWS, (h + 1) * VROWS), alpha, vts, ps)

    _run_bounded_or_online(bnd_ref[BND_FOX] > 0.5, qi, process)
    outs = [_normalized(acc_ref, h) for h in range(HPM)]
    _store_query_tile(o_ref, qi, jnp.concatenate(outs, axis=0).T.astype(BF16))


def _fox_attention(bnd, qt, k4, vt, st, ck4):
    bsz, nq = qt.shape[0], qt.shape[1]
    return _attn_call(
        _fox_kernel, "forgetting_attention", bsz, nq,
        [bnd, qt, k4, vt, st, ck4],
        [_smem(), _seq_spec(qt.shape), _seq_spec(k4.shape), _seq_spec(vt.shape),
         _seq_spec(st.shape), _seq_spec(ck4.shape)],
        [pltpu.VMEM((HPM, 1, T), F32), pltpu.VMEM((HPM * VROWS, T), F32)])


def _sb_kernel(qi, qt_ref, k_ref, vt_ref, tri_ref, o_ref, run_ref, acc_ref):
    qh = [_head_rows(qt_ref[0, qi], h) for h in range(HPM)]
    run_ref[...] = jnp.zeros(run_ref.shape, F32)
    acc_ref[...] = jnp.zeros(acc_ref.shape, F32)
    r, c = _causal_tile()

    def process(tiles):
        scores = [[_dot(_head_cols(k_ref[0, ki], h), qh[h]) for ki, _ in tiles] for h in range(HPM)]
        weights = []
        for h in range(HPM):
            run = run_ref[h]
            ws = []
            for (ki, diag), z in zip(tiles, scores[h]):
                log_beta = jnp.minimum(z, 0.0) - jnp.log2(1.0 + jnp.exp2(-jnp.abs(z)))
                log_keep = log_beta - z
                if diag:
                    log_keep = jnp.where(r < c, log_keep, 0.0)
                keep_b = log_keep.astype(BF16)
                tail = _dot(tri_ref[...], keep_b) + run
                a = jnp.exp2(log_beta + tail)
                if diag:
                    a = jnp.where(r < c, a, 0.0)
                ws.append(a.astype(BF16))
                run = tail[0:1, :] + keep_b[0:1, :].astype(F32)
            run_ref[h] = run
            weights.append(ws)
        for h in range(HPM):
            rows = slice(h * HEAD_DIM, (h + 1) * HEAD_DIM)
            upd = None
            for (ki, _), a in zip(tiles, weights[h]):
                term = _dot(vt_ref[0, ki, rows, :], a)
                upd = term if upd is None else upd + term
            acc_ref[rows, :] = acc_ref[rows, :] + upd

    rest = jnp.bitwise_and(qi, 3)
    for n in range(4):
        @pl.when(rest == n)
        def _():
            process([(qi, True)] + [(qi - d, False) for d in range(1, n + 1)])

    base = qi - 1 - rest

    def four(j, carry):
        process([(base - 4 * j - i, False) for i in range(4)])
        return carry

    lax.fori_loop(0, lax.shift_right_logical(qi, 2), four, 0)
    _store_query_tile(o_ref, qi, acc_ref[...].T.astype(BF16))


def _sb_attention(qt, k4, vt, tri):
    bsz, nq = qt.shape[0], qt.shape[1]
    return _attn_call(
        _sb_kernel, "stick_breaking_attention", bsz, nq,
        [qt, k4, vt, tri],
        [_seq_spec(qt.shape), _seq_spec(k4.shape), _seq_spec(vt.shape), _resident(tri.shape)],
        [pltpu.VMEM((HPM, 1, T), F32), pltpu.VMEM((GW, T), F32)])


def _nsa_kernel(qi, far_ref, bnd_ref, qt_ref, kc_ref, vct_ref, ks_ref, vst_ref, kw_ref, vwt_ref, g_ref,
                bias_ref, ovt_ref, o_ref,
                sc_ref, key_ref, sel_ref, m_ref, acc_ref):
    q0 = qi * T
    n_cmp = kc_ref.shape[1]
    n_blk = ovt_ref.shape[0]
    n_sel = min(N_SEL, n_blk)
    w4 = HPM * T
    qt = qt_ref[0, qi]
    qcat = jnp.concatenate([qt[h * HEAD_DIM:(h + 1) * HEAD_DIM, :] for h in range(HPM)], axis=1)
    far_row = jnp.concatenate([jnp.full((1, T), far_ref[HPM + h], F32) for h in range(HPM)], axis=1)
    r, c = _causal_tile(w4)

    sc_ref[...] = _dot(kc_ref[0], qcat)
    delta = bias_ref[2 * T:2 * T + BAND_ROWS, :] - far_row

    @pl.when(qi == 0)
    def _():
        half = BAND_ROWS // 2
        sc_ref[0:half, :] = sc_ref[0:half, :] + delta[half:BAND_ROWS]

    @pl.when(qi > 0)
    def _():
        c0 = pl.multiple_of(qi * (T // CMP_STRIDE) - BAND_ROWS // 2, 8)
        sc_ref[pl.ds(c0, BAND_ROWS), :] = sc_ref[pl.ds(c0, BAND_ROWS), :] + delta

    ci = lax.broadcasted_iota(jnp.int32, (n_cmp, w4), 0)
    ti = q0 + jnp.bitwise_and(lax.broadcasted_iota(jnp.int32, (n_cmp, w4), 1), T - 1)
    valid_c = ti - CMP_STRIDE * ci - (CMP_LEN - 1) >= 0
    s = jnp.where(valid_c, sc_ref[...] + far_row, NEG)
    e = jnp.exp2(s - jnp.max(s, axis=0, keepdims=True))
    p = jnp.where(valid_c, e / jnp.sum(e, axis=0, keepdims=True), 0.0)
    o_cmp = _dot(vct_ref[0], p.astype(BF16))
    pc_sum = p[:, 0:T]
    for h in range(1, HPM):
        pc_sum = pc_sum + p[:, h * T:(h + 1) * T]

    hi, lo = _split2(pc_sum)
    imp = _dot(ovt_ref[...], hi) + _dot(ovt_ref[...], lo)
    ji = lax.broadcasted_iota(jnp.int32, (n_blk, T), 0)
    tq = q0 + lax.broadcasted_iota(jnp.int32, (n_blk, T), 1)
    forced = (ji == lax.shift_right_logical(tq, int(math.log2(SEL_LEN)))) | (ji == 0)
    imp = jnp.where(forced, FORCED_SCORE, imp)
    imp = jnp.where(ji * SEL_LEN <= tq, imp, -1.0)
    keys = lax.bitcast_convert_type(imp, jnp.int32)
    key_ref[...] = keys

    def rank_body(jp, cnt):
        row = key_ref[pl.ds(jp, 1), :]
        below = lax.shift_right_arithmetic(jp - ji, 31)
        return cnt + jnp.where(row > keys + below, 1, 0)

    n_seen = jnp.minimum((qi + 1) * (T // SEL_LEN), n_blk)
    cnt = lax.fori_loop(0, n_seen, rank_body, jnp.zeros((n_blk, T), jnp.int32))
    sel = jnp.where(cnt < n_sel, 1.0, 0.0)
    sel_ref[...] = jnp.concatenate([sel] * HPM, axis=1)

    m_ref[...] = jnp.full(m_ref.shape, NEG, F32)
    acc_ref[...] = jnp.zeros(acc_ref.shape, F32)

    def bound_row(base):
        return jnp.concatenate([jnp.full((1, T), bnd_ref[base + 1 + h], F32) for h in range(HPM)], axis=1)

    def biased(s, mode, bound):
        if mode == "far":
            return s + (far_row if bound is None else far_row - bound)
        tile = bias_ref[T:2 * T, :] if mode == "near" else bias_ref[0:T, :]
        return s + tile if bound is None else s + tile - bound

    def update(state, ss, vts, bounded):
        alpha, ps = _softmax_probs(ss, m_ref, state, fixed=0.0 if bounded else None)
        _accumulate(acc_ref, slice(state * VROWS, (state + 1) * VROWS), alpha, vts, ps)

    def sel_mask(ki):
        per = T // SEL_LEN
        rows = [jnp.broadcast_to(sel_ref[pl.ds(ki * per + i, 1), :], (SEL_LEN, w4)) for i in range(per)]
        return jnp.concatenate(rows, axis=0) > 0.5

    def process_selected(tiles, bounded):
        bound = bound_row(BND_SEL) if bounded else None
        scores = [_dot(ks_ref[0, ki], qcat) for ki, _ in tiles]
        ss = []
        for (ki, mode), sc in zip(tiles, scores):
            mask = sel_mask(ki)
            if mode == "diag":
                mask = mask & (r <= c)
            ss.append(jnp.where(mask, biased(sc, mode, bound), NEG))
        update(0, ss, [_with_ones(vst_ref[0, ki]) for ki, _ in tiles], bounded)

    _run_bounded_or_online(bnd_ref[BND_SEL] > 0.5, qi, process_selected)

    def process_window(tiles, bounded):
        bound = bound_row(BND_WIN) if bounded else None
        scores = [_dot(kw_ref[0, ki], qcat) for ki, _, _ in tiles]
        ss = []
        for (ki, mode, mask), sc in zip(tiles, scores):
            s = biased(sc, mode, bound)
            ss.append(s if mask is None else jnp.where(mask, s, NEG))
        update(1, ss, [_with_ones(vwt_ref[0, ki]) for ki, _, _ in tiles], bounded)

    back = WINDOW // T

    def window_tiles(n):
        tiles = [(qi, "diag", r <= c)]
        for d in range(1, n + 1):
            mode = "near" if d == 1 else "far"
            tiles.append((qi - d, mode, (c < r) if d == back else None))
        return tiles

    win_bounded = bnd_ref[BND_WIN] > 0.5
    for n in range(back + 1):
        cond = (qi == n) if n < back else (qi >= back)

        @pl.when(cond & win_bounded)
        def _():
            process_window(window_tiles(n), True)

        @pl.when(cond & jnp.logical_not(win_bounded))
        def _():
            process_window(window_tiles(n), False)

    o_sel = acc_ref[0:HEAD_DIM, :] / acc_ref[HEAD_DIM:HEAD_DIM + 1, :]
    o_win = acc_ref[VROWS:VROWS + HEAD_DIM, :] / acc_ref[VROWS + HEAD_DIM:VROWS + HEAD_DIM + 1, :]
    gate = lambda br: jnp.concatenate(
        [g_ref[0, qi, HPM + br * HPM + h:HPM + br * HPM + h + 1, :] for h in range(HPM)], axis=1)
    o = gate(0) * o_cmp + gate(1) * o_sel + gate(2) * o_win
    ot = jnp.concatenate([o[:, h * T:(h + 1) * T] for h in range(HPM)], axis=0)
    _store_query_tile(o_ref, qi, ot.T.astype(BF16))


def _nsa_attention(far, bnd, qt, kc, vct, ks4, vst, kw4, vwt, st, bias, ovt):
    bsz, nq = qt.shape[0], qt.shape[1]
    n_cmp = kc.shape[1]
    n_blk = ovt.shape[0]
    w4 = HPM * T
    return _attn_call(
        _nsa_kernel, "native_sparse_attention", bsz, nq,
        [far, bnd, qt, kc, vct, ks4, vst, kw4, vwt, st, bias, ovt],
        [_smem(), _smem(), _seq_spec(qt.shape), _seq_spec(kc.shape), _seq_spec(vct.shape),
         _seq_spec(ks4.shape), _seq_spec(vst.shape), _seq_spec(kw4.shape), _seq_spec(vwt.shape),
         _seq_spec(st.shape), _resident(bias.shape), _whole(ovt.shape)],
        [pltpu.VMEM((n_cmp, w4), F32), pltpu.VMEM((n_blk, T), jnp.int32), pltpu.VMEM((n_blk, w4), F32),
         pltpu.VMEM((2, 1, w4), F32), pltpu.VMEM((2 * VROWS, w4), F32)])


def _post_kernel(x_ref, oa_ref, ob_ref, oc_ref, od_ref, wo_ref, gta_ref, g_ref, sc_ref, sh_ref,
                 gtm_ref, wup_ref, cw_ref, cb_ref, wdn_ref, y_ref, carry_ref):
    d_ff = wdn_ref.shape[1]
    mixed = (_dot(oa_ref[0], wo_ref[0, 0:GW]) + _dot(ob_ref[0], wo_ref[0, GW:2 * GW])
             + _dot(oc_ref[0], wo_ref[0, 2 * GW:3 * GW]) + _dot(od_ref[0], wo_ref[0, 3 * GW:4 * GW]))
    x1 = x_ref[0] + gta_ref[0] * mixed
    ms = jnp.mean(x1 * x1, axis=-1, keepdims=True)
    h = x1 * lax.rsqrt(ms + EPS) * g_ref[...]
    hb = (h * (1.0 + sc_ref[0]) + sh_ref[0]).astype(BF16)

    @pl.when(pl.program_id(1) == 0)
    def _():
        carry_ref[...] = jnp.zeros(carry_ref.shape, F32)

    def up(ch):
        gate = _dot(hb, wup_ref[0, :, ch * FF_CHUNK:(ch + 1) * FF_CHUNK])
        val = _dot(hb, wup_ref[0, :, d_ff + ch * FF_CHUNK:d_ff + (ch + 1) * FF_CHUNK])
        return gate, val

    row = lax.broadcasted_iota(jnp.int32, (TM_POST, FF_CHUNK), 0)
    y = jnp.zeros((TM_POST, x1.shape[1]), F32)
    n_chunks = d_ff // FF_CHUNK
    ahead = up(0)
    for ch in range(n_chunks):
        cols = slice(ch * FF_CHUNK, (ch + 1) * FF_CHUNK)
        gate, val = ahead
        if ch + 1 < n_chunks:
            ahead = up(ch + 1)
        prev = carry_ref[:, cols]
        g1 = jnp.where(row == 0, prev[7:8], pltpu.roll(gate, 1, 0))
        g2 = jnp.where(row == 0, prev[6:7], jnp.where(row == 1, prev[7:8], pltpu.roll(gate, 2, 0)))
        carry_ref[:, cols] = gate[TM_POST - 8:TM_POST]
        conv = (cw_ref[0, 0:1, cols] * g2 + cw_ref[0, 1:2, cols] * g1 + cw_ref[0, 2:3, cols] * gate
                + cb_ref[0, :, cols])
        act = conv * _sigmoid(conv) * val
        y = y + _dot(act.astype(BF16), wdn_ref[0, cols, :])
    y_ref[0] = x1 + gtm_ref[0] * y


def _post(layer, x, oa, ob, oc, od, wo, gta, g, sc, sh, gtm, wup, cw, cb, wdn):
    bsz, s, d = x.shape
    tok = lambda width: pl.BlockSpec((1, TM_POST, width), lambda b, i: (b, i, 0))
    vec = pl.BlockSpec((1, 1, d), lambda b, i: (b, 0, 0))
    of_layer = lambda a: pl.BlockSpec((1,) + a.shape[1:], lambda b, i: (layer,) + (0,) * (a.ndim - 1),
                                      pipeline_mode=pl.Buffered(1))
    return pl.pallas_call(
        _post_kernel,
        out_shape=jax.ShapeDtypeStruct((bsz, s, d), F32),
        grid=(bsz, s // TM_POST),
        in_specs=[tok(d), tok(GW), tok(GW), tok(GW), tok(GW), of_layer(wo), vec, _whole((1, d)), vec, vec,
                  vec, of_layer(wup), of_layer(cw), of_layer(cb), of_layer(wdn)],
        out_specs=tok(d),
        scratch_shapes=[pltpu.VMEM((8, wdn.shape[1]), F32)],
        compiler_params=pltpu.CompilerParams(
            dimension_semantics=("parallel", "arbitrary"), vmem_limit_bytes=VMEM_LIMIT),
        name="out_proj_mlp",
    )(x, oa, ob, oc, od, wo, gta, g, sc, sh, gtm, wup, cw, cb, wdn)


def _repack_w_in(w):
    off = {}
    pos = 0
    for name, size in (("a_q", GW), ("a_k", GW), ("a_v", GW), ("b_q", GW), ("b_k", GW), ("b_v", GW),
                       ("b_f", HPM), ("c_q", GW), ("c_k", GW), ("c_v", GW), ("d_q", GW),
                       ("d_kc", HEAD_DIM), ("d_vc", HEAD_DIM), ("d_ks", HEAD_DIM), ("d_vs", HEAD_DIM),
                       ("d_kw", HEAD_DIM), ("d_vw", HEAD_DIM), ("d_g", 3 * HPM)):
        off[name] = (pos, size)
        pos += size
    assert pos == w.shape[-1]
    col = lambda n: w[..., off[n][0]:off[n][0] + off[n][1]]
    order = ["a_q", "a_k", "a_v", "b_q", "b_k", "b_v", "c_q", "c_k", "c_v", "d_q",
             "d_ks", "d_vs", "d_kw", "d_vw", "d_kc", "d_vc", "b_f", "d_g"]
    parts = [col(n) for n in order] + [jnp.zeros(w.shape[:-1] + (128 - HPM - 3 * HPM,), w.dtype)]
    out = jnp.concatenate(parts, axis=-1).astype(BF16)
    assert out.shape[-1] == N_PROJ
    return out


def _compress_params(pe, phi_w):
    half = CMP_LEN // 2
    pe_rows = jnp.concatenate([pe[0], pe[1]], axis=1).reshape(2, half * 2 * HEAD_DIM)
    wk = phi_w[0].reshape(CMP_LEN, HEAD_DIM, HEAD_DIM)
    wv = phi_w[1].reshape(CMP_LEN, HEAD_DIM, HEAD_DIM)
    zero = jnp.zeros_like(wk)
    blk = jnp.concatenate([jnp.concatenate([wk, zero], axis=2), jnp.concatenate([zero, wv], axis=2)], axis=1)
    return pe_rows, blk.reshape(2, half * 2 * HEAD_DIM, 2 * HEAD_DIM).astype(BF16)


def _score_bounds(rel_bias, gq_a, gk_a, gq_b, gk_b, gq_d, gk_s, gk_w):
    amax = lambda g: jnp.max(jnp.abs(g))
    tb = rel_bias * LOG2E
    bmax, bmin = jnp.max(tb, axis=0), jnp.min(tb, axis=0)
    span = jnp.max(bmax - bmin)
    qk = lambda n, gq, gk, scale: n * amax(gq) * amax(gk) * (scale * LOG2E * BOUND_MARGIN)
    ok = lambda b, sp: (2.0 * b + sp <= MAX_LOG2_SPAN).astype(F32).reshape(1)
    b_fox = qk(HEAD_DIM, gq_b, gk_b, HEAD_DIM ** -0.5)
    b_a = qk(DIFF_QK_DIM, gq_a, gk_a, DIFF_QK_DIM ** -0.5)
    b_s = qk(HEAD_DIM, gq_d, gk_s, HEAD_DIM ** -0.5)
    b_w = qk(HEAD_DIM, gq_d, gk_w, HEAD_DIM ** -0.5)
    vec = jnp.concatenate([
        ok(b_fox, 0.0), b_fox.reshape(1),
        ok(b_a, span), b_a + bmax[:HPM],
        ok(b_s, span), b_s + bmax[HPM:],
        ok(b_w, span), b_w + bmax[HPM:]])
    return jnp.pad(vec, (0, BND_SIZE - vec.shape[0]))


def kernel(x, c, rel_bias, ada_w, ada_b, norm_mix_g, norm_ffn_g, w_in, w_out, diff_qnorm_g, diff_knorm_g, diff_lambda, diff_subln_g, fox_qnorm_g, fox_knorm_g, fox_b_f, nsa_qnorm_g, nsa_knorm_g, nsa_pe, nsa_phi_w, ffn_w_up, ffn_conv_w, ffn_conv_b, ffn_w_down):
    bsz, s, d = x.shape
    depth = ada_w.shape[0]
    assert s % TM == 0 and TM % T == 0 and WINDOW % T == 0 and T % SEL_LEN == 0 and d == 4 * GW
    assert T & (T - 1) == 0 and BAND_ROWS // 2 >= (MAX_DISTANCE + CMP_LEN) // CMP_STRIDE
    nk = s // T
    n_chunk = s // CMP_STRIDE
    n_blk = s // SEL_LEN
    d_ff = ffn_w_down.shape[1]
    assert d_ff % FF_CHUNK == 0

    bd32 = jnp.asarray(_block_ones(GW, DIFF_QK_DIM), BF16)
    bd64 = jnp.asarray(_block_ones(GW, HEAD_DIM), BF16)
    ti = np.arange(T)
    tri_incl = jnp.asarray((ti[None, :] <= ti[:, None]).astype(np.float32), BF16)
    tri_later = jnp.asarray((ti[None, :] > ti[:, None]).astype(np.float32), BF16)
    c_start = np.arange(n_chunk) * CMP_STRIDE
    s_start = np.arange(n_blk) * SEL_LEN
    ov = ((c_start[None, :] < s_start[:, None] + SEL_LEN) & (s_start[:, None] < c_start[None, :] + CMP_LEN)
          & (np.arange(n_chunk)[None, :] < n_chunk - 1))
    ovt = jnp.asarray(ov.astype(np.float32), BF16)

    bias_a, bias_d = _bias_tiles(rel_bias)
    far = rel_bias[NUM_BUCKETS - 1] * LOG2E
    mod = _adaln(c, ada_w, ada_b)

    w_re = _repack_w_in(w_in)
    wo_b, wup_b, wdn_b = w_out.astype(BF16), ffn_w_up.astype(BF16), ffn_w_down.astype(BF16)
    conv_w = ffn_conv_w.reshape(depth, CONV_WIDTH, d_ff)
    conv_b = ffn_conv_b.reshape(depth, 1, d_ff)

    for l in range(depth):
        sh_a, sc_a, gt_a, sh_m, sc_m, gt_m = [m.reshape(bsz, 1, d) for m in jnp.split(mod[l], 6, axis=-1)]
        scale_a = DIFF_QK_DIM ** -0.5 * LOG2E
        scale = HEAD_DIM ** -0.5 * LOG2E
        ones = jnp.ones((HEAD_DIM,), F32)
        gains = jnp.stack([
            jnp.tile(diff_qnorm_g[l] * scale_a, GW // DIFF_QK_DIM), jnp.tile(diff_knorm_g[l], GW // DIFF_QK_DIM),
            jnp.tile(fox_qnorm_g[l] * scale, HPM), jnp.tile(fox_knorm_g[l], HPM),
            jnp.tile(nsa_qnorm_g[l] * scale, HPM),
            jnp.concatenate([nsa_knorm_g[l, 1], ones, nsa_knorm_g[l, 2], ones]),
            jnp.ones((GW,), F32), jnp.ones((GW,), F32)])
        (a_qt, a_k, a_vt, b_qt, b_k, b_vt, c_qt, c_k, c_vt, d_qt,
         d_ks, d_kw, d_vst, d_vwt, d_cmp, smalls) = _inproj(
            l, x, norm_mix_g[l].reshape(1, d), sc_a, sh_a, w_re, gains, bd32, bd64)

        bf_row = jnp.pad(fox_b_f[l], (0, 128 - HPM)).reshape(1, 128)
        ck, st = _scalars(smalls, bf_row, tri_incl)

        pe_rows, w_blk = _compress_params(nsa_pe[l], nsa_phi_w[l])
        gk_pad = jnp.pad(nsa_knorm_g[l, 0], (0, 128 - HEAD_DIM)).reshape(1, 128)
        kc, vct = _compress(d_cmp, pe_rows, w_blk, gk_pad)

        bnd = _score_bounds(rel_bias, diff_qnorm_g[l], diff_knorm_g[l], fox_qnorm_g[l], fox_knorm_g[l],
                            nsa_qnorm_g[l], nsa_knorm_g[l, 1], nsa_knorm_g[l, 2])
        lam_init = 0.8 - 0.6 * math.exp(-0.3 * l)
        cst = jnp.full((1,), lam_init, F32)
        k4 = lambda a: a.reshape(bsz, nk, T, a.shape[-1])
        o_a = _diff_attention(far, cst, bnd, a_qt, k4(a_k), a_vt, bias_a, diff_lambda[l],
                              diff_subln_g[l].reshape(HEAD_DIM, 1))
        o_b = _fox_attention(bnd, b_qt, k4(b_k), b_vt, st, k4(ck))
        o_c = _sb_attention(c_qt, k4(c_k), c_vt, tri_later)
        o_d = _nsa_attention(far, bnd, d_qt, kc, vct, k4(d_ks), d_vst, k4(d_kw), d_vwt, st, bias_d, ovt)

        x = _post(l, x, o_a, o_b, o_c, o_d, wo_b, gt_a, norm_ffn_g[l].reshape(1, d),
                  sc_m, sh_m, gt_m, wup_b, conv_w, conv_b, wdn_b)
    return x
```

```python
import math

import numpy as np
import jax
import jax.numpy as jnp
from jax import lax
from jax.experimental import pallas as pl
from jax.experimental.pallas import tpu as pltpu

HEAD_DIM = 64
HPM = 4
GW = HPM * HEAD_DIM
DIFF_QK_DIM = HEAD_DIM // 2
NUM_BUCKETS = 32
MAX_EXACT = NUM_BUCKETS // 2
MAX_DISTANCE = 128
CMP_LEN = 32
CMP_STRIDE = 16
SEL_LEN = 64
N_SEL = 16
WINDOW = 512
FORCED_SCORE = 1.0e4
NEG = -1.0e30
EPS = 1e-6
CONV_WIDTH = 3
LOG2E = 1.4426950408889634

T = 256
TM = 4 * T
TM_POST = 256
FF_CHUNK = 256
ONES_ROWS = 16
VROWS = HEAD_DIM + ONES_ROWS
BAND_ROWS = 2 * T // CMP_STRIDE
VMEM_LIMIT = 56 * 1024 * 1024

BND_FOX, BND_DIFF, BND_SEL, BND_WIN, BND_SIZE = 0, 2, 7, 12, 24
BOUND_MARGIN = 1.02
MAX_LOG2_SPAN = 100.0

F32 = jnp.float32
BF16 = jnp.bfloat16


def _dot(a, b):
    return jnp.dot(a, b, preferred_element_type=F32)


def _split2(x):
    hi = x.astype(BF16)
    lo = (x - hi.astype(F32)).astype(BF16)
    return hi, lo


def _log_sigmoid(z):
    return jnp.minimum(z, 0.0) - jnp.log(1.0 + jnp.exp(-jnp.abs(z)))


def _sigmoid(z):
    return 1.0 / (1.0 + jnp.exp(-z))


def _whole(shape):
    nd = len(shape)
    return pl.BlockSpec(shape, lambda *_: (0,) * nd)


def _resident(shape):
    nd = len(shape)
    return pl.BlockSpec(shape, lambda *_: (0,) * nd, pipeline_mode=pl.Buffered(1))


def _smem():
    return pl.BlockSpec(memory_space=pltpu.SMEM)


def _t5_bucket_np(dist):
    n = np.maximum(dist, 0)
    ratio = math.log(MAX_DISTANCE / MAX_EXACT)
    out = None
    for dt in (np.float32, np.float64):
        large = MAX_EXACT + (np.log(np.maximum(n, 1).astype(dt) / dt(MAX_EXACT)) / dt(ratio)
                             * dt(NUM_BUCKETS - MAX_EXACT)).astype(np.int32)
        b = np.where(n < MAX_EXACT, n, np.minimum(large, NUM_BUCKETS - 1)).astype(np.int32)
        assert out is None or np.array_equal(out, b)
        out = b
    return out


def _bucket_tiles():
    r = np.arange(T)[:, None]
    c = np.arange(T)[None, :]
    diag = _t5_bucket_np(c - r)
    near = _t5_bucket_np(T + c - r)
    rb = np.arange(BAND_ROWS)[:, None]
    band = _t5_bucket_np(c - CMP_STRIDE * rb + (T - CMP_LEN + 1))
    return np.concatenate([diag, near, band], axis=0)


def _block_ones(n, group):
    i = np.arange(n)
    return (i[:, None] // group == i[None, :] // group).astype(np.float32)


def _bias_kernel(tbl_ref, bkt_ref, a_ref, d_ref):
    b = bkt_ref[...]
    for h in range(2 * HPM):
        acc = jnp.zeros(b.shape, F32)
        for k in range(NUM_BUCKETS):
            acc = jnp.where(b == k, tbl_ref[k, h] * LOG2E, acc)
        if h < HPM:
            a_ref[h] = acc[0:2 * T]
        else:
            d_ref[:, (h - HPM) * T:(h - HPM + 1) * T] = acc


def _bias_tiles(rel_bias):
    bkt = jnp.asarray(_bucket_tiles())
    rows = bkt.shape[0]
    return pl.pallas_call(
        _bias_kernel,
        out_shape=[jax.ShapeDtypeStruct((HPM, 2 * T, T), F32), jax.ShapeDtypeStruct((rows, HPM * T), F32)],
        in_specs=[_smem(), _whole(bkt.shape)],
        out_specs=[_whole((HPM, 2 * T, T)), _whole((rows, HPM * T))],
        name="bias_tiles",
    )(rel_bias, bkt)


def _adaln_kernel(c_ref, w_ref, b_ref, o_ref):
    c = c_ref[...]
    ca = c * _sigmoid(c)
    o_ref[0] = _dot(ca.astype(BF16), w_ref[0].astype(BF16)) + b_ref[0]


def _adaln(c, ada_w, ada_b):
    depth, d, n = ada_w.shape
    bsz = c.shape[0]
    rows = -(-bsz // 8) * 8
    cp = jnp.pad(c, ((0, rows - bsz), (0, 0)))
    tn = 1536
    out = pl.pallas_call(
        _adaln_kernel,
        out_shape=jax.ShapeDtypeStruct((depth, rows, n), F32),
        grid=(depth, n // tn),
        in_specs=[pl.BlockSpec((rows, d), lambda l, j: (0, 0)),
                  pl.BlockSpec((1, d, tn), lambda l, j: (l, 0, j)),
                  pl.BlockSpec((1, 1, tn), lambda l, j: (l, 0, j))],
        out_specs=pl.BlockSpec((1, rows, tn), lambda l, j: (l, 0, j)),
        name="adaln",
    )(cp, ada_w, ada_b.reshape(depth, 1, n))
    return out[:, :bsz]


N_PROJ = 11 * GW + 2 * 128


def _inproj_kernel(x_ref, g_ref, sc_ref, sh_ref, w_ref, gains_ref, bd32_ref, bd64_ref,
                   aq_ref, ak_ref, av_ref, bq_ref, bk_ref, bv_ref, cq_ref, ck_ref, cv_ref, dq_ref,
                   dks_ref, dkw_ref, dvs_ref, dvw_ref, cmp_ref, sm_ref):
    x = x_ref[0]
    ms = jnp.mean(x * x, axis=-1, keepdims=True)
    h = x * lax.rsqrt(ms + EPS) * g_ref[...]
    h = h * (1.0 + sc_ref[0]) + sh_ref[0]
    hb = h.astype(BF16)

    def proj(group, width=GW):
        off = group * GW
        return _dot(hb, w_ref[0, :, off:off + width])

    def segnorm(y, bd_ref, inv_n, gain_row):
        ss = _dot((y * y).astype(BF16), bd_ref[...])
        return y * lax.rsqrt(ss * inv_n + EPS) * gains_ref[gain_row:gain_row + 1, :]

    def put_t(ref, y):
        for j in range(TM // T):
            ref[0, j] = y[j * T:(j + 1) * T].T.astype(BF16)

    raw0 = proj(0)
    raw1 = proj(1)
    put_t(aq_ref, segnorm(raw0, bd32_ref, 1.0 / DIFF_QK_DIM, 0))
    put_t(av_ref, proj(2))
    raw3 = proj(3)
    ak_ref[0] = segnorm(raw1, bd32_ref, 1.0 / DIFF_QK_DIM, 1).astype(BF16)
    put_t(bv_ref, proj(5))
    raw4 = proj(4)
    put_t(bq_ref, segnorm(raw3, bd64_ref, 1.0 / HEAD_DIM, 2))
    put_t(cq_ref, proj(6) * (HEAD_DIM ** -0.5 * LOG2E))
    raw9 = proj(9)
    bk_ref[0] = segnorm(raw4, bd64_ref, 1.0 / HEAD_DIM, 3).astype(BF16)
    ck_ref[0] = proj(7).astype(BF16)
    y = proj(10)
    put_t(dq_ref, segnorm(raw9, bd64_ref, 1.0 / HEAD_DIM, 4))
    put_t(cv_ref, proj(8))
    tail = proj(11)
    cmp_ref[0] = tail[:, 0:128]
    sm_ref[0] = tail[:, 128:256]

    yn = segnorm(y, bd64_ref, 1.0 / HEAD_DIM, 5)
    dks_ref[0] = yn[:, 0:HEAD_DIM].astype(BF16)
    dkw_ref[0] = yn[:, 2 * HEAD_DIM:3 * HEAD_DIM].astype(BF16)
    for j in range(TM // T):
        yt = y[j * T:(j + 1) * T].T
        dvs_ref[0, j] = yt[HEAD_DIM:2 * HEAD_DIM].astype(BF16)
        dvw_ref[0, j] = yt[3 * HEAD_DIM:4 * HEAD_DIM].astype(BF16)


def _inproj(layer, x, g, sc, sh, w_re, gains, bd32, bd64):
    bsz, s, d = x.shape
    nk = s // T
    tok = lambda width, dt: jax.ShapeDtypeStruct((bsz, s, width), dt)
    tr = lambda rows: jax.ShapeDtypeStruct((bsz, nk, rows, T), BF16)
    tok_spec = lambda width: pl.BlockSpec((1, TM, width), lambda b, i: (b, i, 0))
    tr_spec = lambda rows: pl.BlockSpec((1, TM // T, rows, T), lambda b, i: (b, i, 0, 0))
    vec = pl.BlockSpec((1, 1, d), lambda b, i: (b, 0, 0))
    out_shape = [tr(GW), tok(GW, BF16), tr(GW)] * 3 + [tr(GW)] + [
        tok(HEAD_DIM, BF16), tok(HEAD_DIM, BF16), tr(HEAD_DIM), tr(HEAD_DIM),
        tok(128, F32), tok(128, F32)]
    out_specs = [tr_spec(GW), tok_spec(GW), tr_spec(GW)] * 3 + [tr_spec(GW)] + [
        tok_spec(HEAD_DIM), tok_spec(HEAD_DIM), tr_spec(HEAD_DIM), tr_spec(HEAD_DIM),
        tok_spec(128), tok_spec(128)]
    return pl.pallas_call(
        _inproj_kernel,
        out_shape=out_shape,
        grid=(bsz, s // TM),
        in_specs=[pl.BlockSpec((1, TM, d), lambda b, i: (b, i, 0)),
                  _whole((1, d)), vec, vec,
                  pl.BlockSpec((1,) + w_re.shape[1:], lambda b, i: (layer, 0, 0)),
                  _whole(gains.shape), _whole(bd32.shape), _whole(bd64.shape)],
        out_specs=out_specs,
        compiler_params=pltpu.CompilerParams(
            dimension_semantics=("parallel", "parallel"), vmem_limit_bytes=VMEM_LIMIT),
        name="inproj",
    )(x, g, sc, sh, w_re, gains, bd32, bd64)


def _scalars_kernel(sm_ref, bf_ref, tri_ref, ck_ref, st_ref):
    s = sm_ref.shape[1]
    carry = jnp.zeros((1, 128), F32)
    for blk in range(s // T):
        rows = slice(blk * T, (blk + 1) * T)
        lf = _log_sigmoid(sm_ref[0, rows, :] + bf_ref[...])
        h1 = lf.astype(BF16)
        r1 = lf - h1.astype(F32)
        h2 = r1.astype(BF16)
        h3 = (r1 - h2.astype(F32)).astype(BF16)
        tri = tri_ref[...]
        cb = _dot(tri, h1) + _dot(tri, h2) + _dot(tri, h3) + carry
        carry = cb[T - 1:T, :]
        cb2 = cb * LOG2E
        ck_ref[0, rows, :] = cb2
        col = lax.broadcasted_iota(jnp.int32, (T, 128), 1)
        comb = jnp.where(col < HPM, cb2, _sigmoid(sm_ref[0, rows, :]))
        st_ref[0, blk] = comb.T[0:16]


def _scalars(smalls, bf_row, tri):
    bsz, s, _ = smalls.shape
    return pl.pallas_call(
        _scalars_kernel,
        out_shape=[jax.ShapeDtypeStruct((bsz, s, 128), F32), jax.ShapeDtypeStruct((bsz, s // T, 16, T), F32)],
        grid=(bsz,),
        in_specs=[pl.BlockSpec((1, s, 128), lambda b: (b, 0, 0)), _whole((1, 128)), _whole((T, T))],
        out_specs=[pl.BlockSpec((1, s, 128), lambda b: (b, 0, 0)),
                   pl.BlockSpec((1, s // T, 16, T), lambda b: (b, 0, 0, 0))],
        compiler_params=pltpu.CompilerParams(dimension_semantics=("parallel",)),
        name="token_scalars",
    )(smalls, bf_row, tri)


def _compress_kernel(x_ref, pe_ref, w_ref, gk_ref, kc_ref, vct_ref):
    n = x_ref.shape[1] // CMP_STRIDE
    chunks = jnp.concatenate(
        [x_ref[0, pl.ds(i, n, stride=CMP_STRIDE), :] for i in range(CMP_STRIDE)], axis=1)
    ya = _dot((chunks + pe_ref[0:1]).astype(BF16), w_ref[0])
    yb = _dot((chunks + pe_ref[1:2]).astype(BF16), w_ref[1])
    y = ya + pltpu.roll(yb, n - 1, 0)
    lane = lax.broadcasted_iota(jnp.int32, y.shape, 1)
    ss = jnp.sum(jnp.where(lane < HEAD_DIM, y * y, 0.0), axis=-1, keepdims=True)
    kc = y * lax.rsqrt(ss * (1.0 / HEAD_DIM) + EPS) * gk_ref[...]
    kc_ref[0] = kc[:, 0:HEAD_DIM].astype(BF16)
    vct_ref[0] = y.T[HEAD_DIM:2 * HEAD_DIM].astype(BF16)


def _compress(cmp, pe_rows, w_blk, gk_pad):
    bsz, s, width = cmp.shape
    n = s // CMP_STRIDE
    return pl.pallas_call(
        _compress_kernel,
        out_shape=[jax.ShapeDtypeStruct((bsz, n, HEAD_DIM), BF16),
                   jax.ShapeDtypeStruct((bsz, HEAD_DIM, n), BF16)],
        grid=(bsz,),
        in_specs=[pl.BlockSpec((1, s, width), lambda b: (b, 0, 0)),
                  _whole(pe_rows.shape), _whole(w_blk.shape), _whole(gk_pad.shape)],
        out_specs=[pl.BlockSpec((1, n, HEAD_DIM), lambda b: (b, 0, 0)),
                   pl.BlockSpec((1, HEAD_DIM, n), lambda b: (b, 0, 0))],
        compiler_params=pltpu.CompilerParams(dimension_semantics=("parallel",)),
        name="nsa_compress",
    )(cmp, pe_rows, w_blk, gk_pad)


def _head_rows(qt, h):
    return qt[h * HEAD_DIM:(h + 1) * HEAD_DIM, :]


def _head_cols(kt, h):
    return kt[:, h * HEAD_DIM:(h + 1) * HEAD_DIM]


def _split_maps(qh):
    qf = qh.astype(F32)
    row = lax.broadcasted_iota(jnp.int32, qf.shape, 0)
    first = jnp.where(row < DIFF_QK_DIM, qf, 0.0).astype(BF16)
    second = jnp.where(row >= DIFF_QK_DIM, qf, 0.0).astype(BF16)
    return jnp.concatenate([first, second], axis=1)


def _with_ones(vt):
    return jnp.concatenate([vt, jnp.ones((ONES_ROWS, vt.shape[1]), BF16)], axis=0)


def _softmax_probs(ss, m_ref, idx, shift=None, fixed=None):
    if fixed is not None:
        already = isinstance(fixed, float) and fixed == 0.0
        return None, [jnp.exp2(s if already else s - fixed).astype(BF16) for s in ss]
    m_old = m_ref[idx]
    mx = jnp.max(ss[0], axis=0, keepdims=True)
    for s in ss[1:]:
        mx = jnp.maximum(mx, jnp.max(s, axis=0, keepdims=True))
    if shift is not None:
        mx = mx + shift
    m_new = jnp.maximum(m_old, mx)
    m_ref[idx] = m_new
    alpha = jnp.exp2(m_old - m_new)
    sub = m_new if shift is None else m_new - shift
    return alpha, [jnp.exp2(s - sub).astype(BF16) for s in ss]


def _accumulate(acc_ref, rows, alpha, vts, ps):
    upd = _dot(vts[0], ps[0])
    for vt, p in zip(vts[1:], ps[1:]):
        upd = upd + _dot(vt, p)
    acc_ref[rows, :] = (acc_ref[rows, :] if alpha is None else alpha * acc_ref[rows, :]) + upd


def _normalized(acc_ref, idx):
    base = idx * VROWS
    return acc_ref[base:base + HEAD_DIM, :] / acc_ref[base + HEAD_DIM:base + HEAD_DIM + 1, :]


def _causal_tile(width=T):
    r = lax.broadcasted_iota(jnp.int32, (T, width), 0)
    c = lax.broadcasted_iota(jnp.int32, (T, width), 1)
    if width != T:
        c = jnp.bitwise_and(c, T - 1)
    return r, c


def _for_tile_groups(qi, process):
    n_far = jnp.maximum(qi - 1, 0)
    n_pair = lax.shift_right_logical(n_far, 1)
    odd = jnp.bitwise_and(n_far, 1) == 1

    def pair(j, carry):
        process([(2 * j, "far"), (2 * j + 1, "far")])
        return carry

    lax.fori_loop(0, n_pair, pair, 0)

    @pl.when(qi == 0)
    def _():
        process([(qi, "diag")])

    @pl.when((qi >= 1) & jnp.logical_not(odd))
    def _():
        process([(qi - 1, "near"), (qi, "diag")])

    @pl.when((qi >= 1) & odd)
    def _():
        process([(qi - 2, "far"), (qi - 1, "near"), (qi, "diag")])


def _for_independent_tiles(qi, process):
    n_far = jnp.maximum(qi - 1, 0)
    n_group = lax.shift_right_logical(n_far, 2)
    rest = jnp.bitwise_and(n_far, 3)

    def many(j, carry):
        process([(4 * j + i, "far") for i in range(4)])
        return carry

    lax.fori_loop(0, n_group, many, 0)

    @pl.when(qi == 0)
    def _():
        process([(qi, "diag")])

    for n in range(4):
        @pl.when((qi >= 1) & (rest == n))
        def _():
            process([(4 * n_group + i, "far") for i in range(n)] + [(qi - 1, "near"), (qi, "diag")])


def _run_bounded_or_online(bounded_ok, qi, process):
    @pl.when(bounded_ok)
    def _():
        _for_independent_tiles(qi, lambda tiles: process(tiles, True))

    @pl.when(jnp.logical_not(bounded_ok))
    def _():
        _for_tile_groups(qi, lambda tiles: process(tiles, False))


def _attn_call(tile_kernel, name, bsz, nq, in_arrays, in_specs, scratch, out_width=GW):
    def kernel(*refs):
        def query_tile(qi, carry):
            tile_kernel(qi, *refs)
            return carry

        lax.fori_loop(0, nq, query_tile, 0)

    return pl.pallas_call(
        kernel,
        out_shape=jax.ShapeDtypeStruct((bsz, nq * T, out_width), BF16),
        grid=(bsz,),
        in_specs=in_specs,
        out_specs=pl.BlockSpec((1, nq * T, out_width), lambda b: (b, 0, 0)),
        scratch_shapes=scratch,
        compiler_params=pltpu.CompilerParams(
            dimension_semantics=("parallel",), vmem_limit_bytes=VMEM_LIMIT),
        name=name,
    )(*in_arrays)


def _store_query_tile(o_ref, qi, value):
    o_ref[0, pl.ds(pl.multiple_of(qi * T, T), T), :] = value


def _seq_spec(shape):
    nd = len(shape)
    return pl.BlockSpec((1,) + tuple(shape[1:]), lambda b: (b,) + (0,) * (nd - 1))


def _diff_kernel(qi, far_ref, cst_ref, bnd_ref, qt_ref, k_ref, vt_ref, bias_ref, lam_ref, subg_ref, o_ref,
                 m_ref, acc_ref):
    qmaps = [_split_maps(_head_rows(qt_ref[0, qi], h)) for h in range(HPM)]
    m_ref[...] = jnp.full(m_ref.shape, NEG, F32)
    acc_ref[...] = jnp.zeros(acc_ref.shape, F32)
    r, c = _causal_tile()

    def process(tiles, bounded):
        scores = [[_dot(_head_cols(k_ref[0, ki], h), qmaps[h]) for ki, _ in tiles] for h in range(HPM)]
        probs = []
        for idx in range(2 * HPM):
            h = idx // 2
            bound = bnd_ref[BND_DIFF + 1 + h] if bounded else None
            ss = []
            for (ki, mode), sc in zip(tiles, scores[h]):
                s = sc[:, (idx % 2) * T:(idx % 2 + 1) * T]
                if mode == "far":
                    s = s + ((far_ref[h] - bound) if bounded else far_ref[h])
                elif mode == "near":
                    s = s + bias_ref[h, T:2 * T, :]
                else:
                    s = jnp.where(r <= c, s + bias_ref[h, 0:T, :], NEG)
                if bounded and mode != "far":
                    s = s - bound
                ss.append(s)
            probs.append(_softmax_probs(ss, m_ref, idx, fixed=0.0 if bounded else None))
        for idx in range(2 * HPM):
            h = idx // 2
            vts = [_with_ones(vt_ref[0, ki, h * HEAD_DIM:(h + 1) * HEAD_DIM, :]) for ki, _ in tiles]
            alpha, ps = probs[idx]
            _accumulate(acc_ref, slice(idx * VROWS, (idx + 1) * VROWS), alpha, vts, ps)

    _run_bounded_or_online(bnd_ref[BND_DIFF] > 0.5, qi, process)

    lam_init = cst_ref[0]
    lv = lam_ref[...]
    lam = (jnp.exp(jnp.sum(lv[0:1] * lv[1:2], axis=-1, keepdims=True))
           - jnp.exp(jnp.sum(lv[2:3] * lv[3:4], axis=-1, keepdims=True)) + lam_init)
    outs = []
    for h in range(HPM):
        o = _normalized(acc_ref, 2 * h) - lam * _normalized(acc_ref, 2 * h + 1)
        ms = jnp.mean(o * o, axis=0, keepdims=True)
        outs.append(o * lax.rsqrt(ms + EPS) * subg_ref[...] * (1.0 - lam_init))
    _store_query_tile(o_ref, qi, jnp.concatenate(outs, axis=0).T.astype(BF16))


def _diff_attention(far, cst, bnd, qt, k4, vt, bias, lam, subg):
    bsz, nq = qt.shape[0], qt.shape[1]
    return _attn_call(
        _diff_kernel, "diff_attention", bsz, nq,
        [far, cst, bnd, qt, k4, vt, bias, lam, subg],
        [_smem(), _smem(), _smem(), _seq_spec(qt.shape), _seq_spec(k4.shape), _seq_spec(vt.shape),
         _resident(bias.shape), _whole(lam.shape), _whole(subg.shape)],
        [pltpu.VMEM((2 * HPM, 1, T), F32), pltpu.VMEM((2 * HPM * VROWS, T), F32)])


def _fox_kernel(qi, bnd_ref, qt_ref, k_ref, vt_ref, cq_ref, ck_ref, o_ref, m_ref, acc_ref):
    qh = [_head_rows(qt_ref[0, qi], h) for h in range(HPM)]
    m_ref[...] = jnp.full(m_ref.shape, NEG, F32)
    acc_ref[...] = jnp.zeros(acc_ref.shape, F32)
    r, c = _causal_tile()

    def process(tiles, bounded):
        scores = [[_dot(_head_cols(k_ref[0, ki], h), qh[h]) for ki, _ in tiles] for h in range(HPM)]
        probs = []
        for h in range(HPM):
            ss = []
            for (ki, mode), sc in zip(tiles, scores[h]):
                s = sc - ck_ref[0, ki][:, h:h + 1]
                if mode == "diag":
                    s = jnp.where(r <= c, s, NEG)
                ss.append(s)
            cq = cq_ref[0, qi, h:h + 1, :]
            fixed = (bnd_ref[BND_FOX + 1] - cq) if bounded else None
            probs.append(_softmax_probs(ss, m_ref, h, shift=cq, fixed=fixed))
        for h in range(HPM):
            vts = [_with_ones(vt_ref[0, ki, h * HEAD_DIM:(h + 1) * HEAD_DIM, :]) for ki, _ in tiles]
            alpha, ps = probs[h]
            _accumulate(acc_ref, slice(h * VROWS, (h + 1) * VROWS), alpha, vts, ps)

    _run_bounded_or_online(bnd_ref[BND_FOX] > 0.5, qi, process)
    outs = [_normalized(acc_ref, h) for h in range(HPM)]
    _store_query_tile(o_ref, qi, jnp.concatenate(outs, axis=0).T.astype(BF16))


def _fox_attention(bnd, qt, k4, vt, st, ck4):
    bsz, nq = qt.shape[0], qt.shape[1]
    return _attn_call(
        _fox_kernel, "forgetting_attention", bsz, nq,
        [bnd, qt, k4, vt, st, ck4],
        [_smem(), _seq_spec(qt.shape), _seq_spec(k4.shape), _seq_spec(vt.shape),
         _seq_spec(st.shape), _seq_spec(ck4.shape)],
        [pltpu.VMEM((HPM, 1, T), F32), pltpu.VMEM((HPM * VROWS, T), F32)])


def _sb_kernel(qi, qt_ref, k_ref, vt_ref, tri_ref, o_ref, run_ref, acc_ref):
    qh = [_head_rows(qt_ref[0, qi], h) for h in range(HPM)]
    run_ref[...] = jnp.zeros(run_ref.shape, F32)
    acc_ref[...] = jnp.zeros(acc_ref.shape, F32)
    r, c = _causal_tile()

    def process(tiles):
        scores = [[_dot(_head_cols(k_ref[0, ki], h), qh[h]) for ki, _ in tiles] for h in range(HPM)]
        weights = []
        for h in range(HPM):
            run = run_ref[h]
            ws = []
            for (ki, diag), z in zip(tiles, scores[h]):
                log_beta = jnp.minimum(z, 0.0) - jnp.log2(1.0 + jnp.exp2(-jnp.abs(z)))
                log_keep = log_beta - z
                if diag:
                    log_keep = jnp.where(r < c, log_keep, 0.0)
                keep_b = log_keep.astype(BF16)
                tail = _dot(tri_ref[...], keep_b) + run
                a = jnp.exp2(log_beta + tail)
                if diag:
                    a = jnp.where(r < c, a, 0.0)
                ws.append(a.astype(BF16))
                run = tail[0:1, :] + keep_b[0:1, :].astype(F32)
            run_ref[h] = run
            weights.append(ws)
        for h in range(HPM):
            rows = slice(h * HEAD_DIM, (h + 1) * HEAD_DIM)
            upd = None
            for (ki, _), a in zip(tiles, weights[h]):
                term = _dot(vt_ref[0, ki, rows, :], a)
                upd = term if upd is None else upd + term
            acc_ref[rows, :] = acc_ref[rows, :] + upd

    rest = jnp.bitwise_and(qi, 3)
    for n in range(4):
        @pl.when(rest == n)
        def _():
            process([(qi, True)] + [(qi - d, False) for d in range(1, n + 1)])

    base = qi - 1 - rest

    def four(j, carry):
        process([(base - 4 * j - i, False) for i in range(4)])
        return carry

    lax.fori_loop(0, lax.shift_right_logical(qi, 2), four, 0)
    _store_query_tile(o_ref, qi, acc_ref[...].T.astype(BF16))


def _sb_attention(qt, k4, vt, tri):
    bsz, nq = qt.shape[0], qt.shape[1]
    return _attn_call(
        _sb_kernel, "stick_breaking_attention", bsz, nq,
        [qt, k4, vt, tri],
        [_seq_spec(qt.shape), _seq_spec(k4.shape), _seq_spec(vt.shape), _resident(tri.shape)],
        [pltpu.VMEM((HPM, 1, T), F32), pltpu.VMEM((GW, T), F32)])


def _nsa_kernel(qi, far_ref, bnd_ref, qt_ref, kc_ref, vct_ref, ks_ref, vst_ref, kw_ref, vwt_ref, g_ref,
                bias_ref, ovt_ref, o_ref,
                sc_ref, key_ref, sel_ref, m_ref, acc_ref):
    q0 = qi * T
    n_cmp = kc_ref.shape[1]
    n_blk = ovt_ref.shape[0]
    n_sel = min(N_SEL, n_blk)
    w4 = HPM * T
    qt = qt_ref[0, qi]
    qcat = jnp.concatenate([qt[h * HEAD_DIM:(h + 1) * HEAD_DIM, :] for h in range(HPM)], axis=1)
    far_row = jnp.concatenate([jnp.full((1, T), far_ref[HPM + h], F32) for h in range(HPM)], axis=1)
    r, c = _causal_tile(w4)

    sc_ref[...] = _dot(kc_ref[0], qcat)
    delta = bias_ref[2 * T:2 * T + BAND_ROWS, :] - far_row

    @pl.when(qi == 0)
    def _():
        half = BAND_ROWS // 2
        sc_ref[0:half, :] = sc_ref[0:half, :] + delta[half:BAND_ROWS]

    @pl.when(qi > 0)
    def _():
        c0 = pl.multiple_of(qi * (T // CMP_STRIDE) - BAND_ROWS // 2, 8)
        sc_ref[pl.ds(c0, BAND_ROWS), :] = sc_ref[pl.ds(c0, BAND_ROWS), :] + delta

    ci = lax.broadcasted_iota(jnp.int32, (n_cmp, w4), 0)
    ti = q0 + jnp.bitwise_and(lax.broadcasted_iota(jnp.int32, (n_cmp, w4), 1), T - 1)
    valid_c = ti - CMP_STRIDE * ci - (CMP_LEN - 1) >= 0
    s = jnp.where(valid_c, sc_ref[...] + far_row, NEG)
    e = jnp.exp2(s - jnp.max(s, axis=0, keepdims=True))
    p = jnp.where(valid_c, e / jnp.sum(e, axis=0, keepdims=True), 0.0)
    o_cmp = _dot(vct_ref[0], p.astype(BF16))
    pc_sum = p[:, 0:T]
    for h in range(1, HPM):
        pc_sum = pc_sum + p[:, h * T:(h + 1) * T]

    hi, lo = _split2(pc_sum)
    imp = _dot(ovt_ref[...], hi) + _dot(ovt_ref[...], lo)
    ji = lax.broadcasted_iota(jnp.int32, (n_blk, T), 0)
    tq = q0 + lax.broadcasted_iota(jnp.int32, (n_blk, T), 1)
    forced = (ji == lax.shift_right_logical(tq, int(math.log2(SEL_LEN)))) | (ji == 0)
    imp = jnp.where(forced, FORCED_SCORE, imp)
    imp = jnp.where(ji * SEL_LEN <= tq, imp, -1.0)
    keys = lax.bitcast_convert_type(imp, jnp.int32)
    key_ref[...] = keys

    def rank_body(jp, cnt):
        row = key_ref[pl.ds(jp, 1), :]
        below = lax.shift_right_arithmetic(jp - ji, 31)
        return cnt + jnp.where(row > keys + below, 1, 0)

    n_seen = jnp.minimum((qi + 1) * (T // SEL_LEN), n_blk)
    cnt = lax.fori_loop(0, n_seen, rank_body, jnp.zeros((n_blk, T), jnp.int32))
    sel = jnp.where(cnt < n_sel, 1.0, 0.0)
    sel_ref[...] = jnp.concatenate([sel] * HPM, axis=1)

    m_ref[...] = jnp.full(m_ref.shape, NEG, F32)
    acc_ref[...] = jnp.zeros(acc_ref.shape, F32)

    def bound_row(base):
        return jnp.concatenate([jnp.full((1, T), bnd_ref[base + 1 + h], F32) for h in range(HPM)], axis=1)

    def biased(s, mode, bound):
        if mode == "far":
            return s + (far_row if bound is None else far_row - bound)
        tile = bias_ref[T:2 * T, :] if mode == "near" else bias_ref[0:T, :]
        return s + tile if bound is None else s + tile - bound

    def update(state, ss, vts, bounded):
        alpha, ps = _softmax_probs(ss, m_ref, state, fixed=0.0 if bounded else None)
        _accumulate(acc_ref, slice(state * VROWS, (state + 1) * VROWS), alpha, vts, ps)

    def sel_mask(ki):
        per = T // SEL_LEN
        rows = [jnp.broadcast_to(sel_ref[pl.ds(ki * per + i, 1), :], (SEL_LEN, w4)) for i in range(per)]
        return jnp.concatenate(rows, axis=0) > 0.5

    def process_selected(tiles, bounded):
        bound = bound_row(BND_SEL) if bounded else None
        scores = [_dot(ks_ref[0, ki], qcat) for ki, _ in tiles]
        ss = []
        for (ki, mode), sc in zip(tiles, scores):
            mask = sel_mask(ki)
            if mode == "diag":
                mask = mask & (r <= c)
            ss.append(jnp.where(mask, biased(sc, mode, bound), NEG))
        update(0, ss, [_with_ones(vst_ref[0, ki]) for ki, _ in tiles], bounded)

    _run_bounded_or_online(bnd_ref[BND_SEL] > 0.5, qi, process_selected)

    def process_window(tiles, bounded):
        bound = bound_row(BND_WIN) if bounded else None
        scores = [_dot(kw_ref[0, ki], qcat) for ki, _, _ in tiles]
        ss = []
        for (ki, mode, mask), sc in zip(tiles, scores):
            s = biased(sc, mode, bound)
            ss.append(s if mask is None else jnp.where(mask, s, NEG))
        update(1, ss, [_with_ones(vwt_ref[0, ki]) for ki, _, _ in tiles], bounded)

    back = WINDOW // T

    def window_tiles(n):
        tiles = [(qi, "diag", r <= c)]
        for d in range(1, n + 1):
            mode = "near" if d == 1 else "far"
            tiles.append((qi - d, mode, (c < r) if d == back else None))
        return tiles

    win_bounded = bnd_ref[BND_WIN] > 0.5
    for n in range(back + 1):
        cond = (qi == n) if n < back else (qi >= back)

        @pl.when(cond & win_bounded)
        def _():
            process_window(window_tiles(n), True)

        @pl.when(cond & jnp.logical_not(win_bounded))
        def _():
            process_window(window_tiles(n), False)

    o_sel = acc_ref[0:HEAD_DIM, :] / acc_ref[HEAD_DIM:HEAD_DIM + 1, :]
    o_win = acc_ref[VROWS:VROWS + HEAD_DIM, :] / acc_ref[VROWS + HEAD_DIM:VROWS + HEAD_DIM + 1, :]
    gate = lambda br: jnp.concatenate(
        [g_ref[0, qi, HPM + br * HPM + h:HPM + br * HPM + h + 1, :] for h in range(HPM)], axis=1)
    o = gate(0) * o_cmp + gate(1) * o_sel + gate(2) * o_win
    ot = jnp.concatenate([o[:, h * T:(h + 1) * T] for h in range(HPM)], axis=0)
    _store_query_tile(o_ref, qi, ot.T.astype(BF16))


def _nsa_attention(far, bnd, qt, kc, vct, ks4, vst, kw4, vwt, st, bias, ovt):
    bsz, nq = qt.shape[0], qt.shape[1]
    n_cmp = kc.shape[1]
    n_blk = ovt.shape[0]
    w4 = HPM * T
    return _attn_call(
        _nsa_kernel, "native_sparse_attention", bsz, nq,
        [far, bnd, qt, kc, vct, ks4, vst, kw4, vwt, st, bias, ovt],
        [_smem(), _smem(), _seq_spec(qt.shape), _seq_spec(kc.shape), _seq_spec(vct.shape),
         _seq_spec(ks4.shape), _seq_spec(vst.shape), _seq_spec(kw4.shape), _seq_spec(vwt.shape),
         _seq_spec(st.shape), _resident(bias.shape), _whole(ovt.shape)],
        [pltpu.VMEM((n_cmp, w4), F32), pltpu.VMEM((n_blk, T), jnp.int32), pltpu.VMEM((n_blk, w4), F32),
         pltpu.VMEM((2, 1, w4), F32), pltpu.VMEM((2 * VROWS, w4), F32)])


def _post_kernel(x_ref, oa_ref, ob_ref, oc_ref, od_ref, wo_ref, gta_ref, g_ref, sc_ref, sh_ref,
                 gtm_ref, wup_ref, cw_ref, cb_ref, wdn_ref, y_ref, carry_ref):
    d_ff = wdn_ref.shape[1]
    mixed = (_dot(oa_ref[0], wo_ref[0, 0:GW]) + _dot(ob_ref[0], wo_ref[0, GW:2 * GW])
             + _dot(oc_ref[0], wo_ref[0, 2 * GW:3 * GW]) + _dot(od_ref[0], wo_ref[0, 3 * GW:4 * GW]))
    x1 = x_ref[0] + gta_ref[0] * mixed
    ms = jnp.mean(x1 * x1, axis=-1, keepdims=True)
    h = x1 * lax.rsqrt(ms + EPS) * g_ref[...]
    hb = (h * (1.0 + sc_ref[0]) + sh_ref[0]).astype(BF16)

    @pl.when(pl.program_id(1) == 0)
    def _():
        carry_ref[...] = jnp.zeros(carry_ref.shape, F32)

    def up(ch):
        gate = _dot(hb, wup_ref[0, :, ch * FF_CHUNK:(ch + 1) * FF_CHUNK])
        val = _dot(hb, wup_ref[0, :, d_ff + ch * FF_CHUNK:d_ff + (ch + 1) * FF_CHUNK])
        return gate, val

    row = lax.broadcasted_iota(jnp.int32, (TM_POST, FF_CHUNK), 0)
    y = jnp.zeros((TM_POST, x1.shape[1]), F32)
    n_chunks = d_ff // FF_CHUNK
    ahead = up(0)
    for ch in range(n_chunks):
        cols = slice(ch * FF_CHUNK, (ch + 1) * FF_CHUNK)
        gate, val = ahead
        if ch + 1 < n_chunks:
            ahead = up(ch + 1)
        prev = carry_ref[:, cols]
        g1 = jnp.where(row == 0, prev[7:8], pltpu.roll(gate, 1, 0))
        g2 = jnp.where(row == 0, prev[6:7], jnp.where(row == 1, prev[7:8], pltpu.roll(gate, 2, 0)))
        carry_ref[:, cols] = gate[TM_POST - 8:TM_POST]
        conv = (cw_ref[0, 0:1, cols] * g2 + cw_ref[0, 1:2, cols] * g1 + cw_ref[0, 2:3, cols] * gate
                + cb_ref[0, :, cols])
        act = conv * _sigmoid(conv) * val
        y = y + _dot(act.astype(BF16), wdn_ref[0, cols, :])
    y_ref[0] = x1 + gtm_ref[0] * y


def _post(layer, x, oa, ob, oc, od, wo, gta, g, sc, sh, gtm, wup, cw, cb, wdn):
    bsz, s, d = x.shape
    tok = lambda width: pl.BlockSpec((1, TM_POST, width), lambda b, i: (b, i, 0))
    vec = pl.BlockSpec((1, 1, d), lambda b, i: (b, 0, 0))
    of_layer = lambda a: pl.BlockSpec((1,) + a.shape[1:], lambda b, i: (layer,) + (0,) * (a.ndim - 1),
                                      pipeline_mode=pl.Buffered(1))
    return pl.pallas_call(
        _post_kernel,
        out_shape=jax.ShapeDtypeStruct((bsz, s, d), F32),
        grid=(bsz, s // TM_POST),
        in_specs=[tok(d), tok(GW), tok(GW), tok(GW), tok(GW), of_layer(wo), vec, _whole((1, d)), vec, vec,
                  vec, of_layer(wup), of_layer(cw), of_layer(cb), of_layer(wdn)],
        out_specs=tok(d),
        scratch_shapes=[pltpu.VMEM((8, wdn.shape[1]), F32)],
        compiler_params=pltpu.CompilerParams(
            dimension_semantics=("parallel", "arbitrary"), vmem_limit_bytes=VMEM_LIMIT),
        name="out_proj_mlp",
    )(x, oa, ob, oc, od, wo, gta, g, sc, sh, gtm, wup, cw, cb, wdn)


def _repack_w_in(w):
    off = {}
    pos = 0
    for name, size in (("a_q", GW), ("a_k", GW), ("a_v", GW), ("b_q", GW), ("b_k", GW), ("b_v", GW),
                       ("b_f", HPM), ("c_q", GW), ("c_k", GW), ("c_v", GW), ("d_q", GW),
                       ("d_kc", HEAD_DIM), ("d_vc", HEAD_DIM), ("d_ks", HEAD_DIM), ("d_vs", HEAD_DIM),
                       ("d_kw", HEAD_DIM), ("d_vw", HEAD_DIM), ("d_g", 3 * HPM)):
        off[name] = (pos, size)
        pos += size
    assert pos == w.shape[-1]
    col = lambda n: w[..., off[n][0]:off[n][0] + off[n][1]]
    order = ["a_q", "a_k", "a_v", "b_q", "b_k", "b_v", "c_q", "c_k", "c_v", "d_q",
             "d_ks", "d_vs", "d_kw", "d_vw", "d_kc", "d_vc", "b_f", "d_g"]
    parts = [col(n) for n in order] + [jnp.zeros(w.shape[:-1] + (128 - HPM - 3 * HPM,), w.dtype)]
    out = jnp.concatenate(parts, axis=-1).astype(BF16)
    assert out.shape[-1] == N_PROJ
    return out


def _compress_params(pe, phi_w):
    half = CMP_LEN // 2
    pe_rows = jnp.concatenate([pe[0], pe[1]], axis=1).reshape(2, half * 2 * HEAD_DIM)
    wk = phi_w[0].reshape(CMP_LEN, HEAD_DIM, HEAD_DIM)
    wv = phi_w[1].reshape(CMP_LEN, HEAD_DIM, HEAD_DIM)
    zero = jnp.zeros_like(wk)
    blk = jnp.concatenate([jnp.concatenate([wk, zero], axis=2), jnp.concatenate([zero, wv], axis=2)], axis=1)
    return pe_rows, blk.reshape(2, half * 2 * HEAD_DIM, 2 * HEAD_DIM).astype(BF16)


def _score_bounds(rel_bias, gq_a, gk_a, gq_b, gk_b, gq_d, gk_s, gk_w):
    amax = lambda g: jnp.max(jnp.abs(g))
    tb = rel_bias * LOG2E
    bmax, bmin = jnp.max(tb, axis=0), jnp.min(tb, axis=0)
    span = jnp.max(bmax - bmin)
    qk = lambda n, gq, gk, scale: n * amax(gq) * amax(gk) * (scale * LOG2E * BOUND_MARGIN)
    ok = lambda b, sp: (2.0 * b + sp <= MAX_LOG2_SPAN).astype(F32).reshape(1)
    b_fox = qk(HEAD_DIM, gq_b, gk_b, HEAD_DIM ** -0.5)
    b_a = qk(DIFF_QK_DIM, gq_a, gk_a, DIFF_QK_DIM ** -0.5)
    b_s = qk(HEAD_DIM, gq_d, gk_s, HEAD_DIM ** -0.5)
    b_w = qk(HEAD_DIM, gq_d, gk_w, HEAD_DIM ** -0.5)
    vec = jnp.concatenate([
        ok(b_fox, 0.0), b_fox.reshape(1),
        ok(b_a, span), b_a + bmax[:HPM],
        ok(b_s, span), b_s + bmax[HPM:],
        ok(b_w, span), b_w + bmax[HPM:]])
    return jnp.pad(vec, (0, BND_SIZE - vec.shape[0]))


def kernel(x, c, rel_bias, ada_w, ada_b, norm_mix_g, norm_ffn_g, w_in, w_out, diff_qnorm_g, diff_knorm_g, diff_lambda, diff_subln_g, fox_qnorm_g, fox_knorm_g, fox_b_f, nsa_qnorm_g, nsa_knorm_g, nsa_pe, nsa_phi_w, ffn_w_up, ffn_conv_w, ffn_conv_b, ffn_w_down):
    bsz, s, d = x.shape
    depth = ada_w.shape[0]
    assert s % TM == 0 and TM % T == 0 and WINDOW % T == 0 and T % SEL_LEN == 0 and d == 4 * GW
    assert T & (T - 1) == 0 and BAND_ROWS // 2 >= (MAX_DISTANCE + CMP_LEN) // CMP_STRIDE
    nk = s // T
    n_chunk = s // CMP_STRIDE
    n_blk = s // SEL_LEN
    d_ff = ffn_w_down.shape[1]
    assert d_ff % FF_CHUNK == 0

    bd32 = jnp.asarray(_block_ones(GW, DIFF_QK_DIM), BF16)
    bd64 = jnp.asarray(_block_ones(GW, HEAD_DIM), BF16)
    ti = np.arange(T)
    tri_incl = jnp.asarray((ti[None, :] <= ti[:, None]).astype(np.float32), BF16)
    tri_later = jnp.asarray((ti[None, :] > ti[:, None]).astype(np.float32), BF16)
    c_start = np.arange(n_chunk) * CMP_STRIDE
    s_start = np.arange(n_blk) * SEL_LEN
    ov = ((c_start[None, :] < s_start[:, None] + SEL_LEN) & (s_start[:, None] < c_start[None, :] + CMP_LEN)
          & (np.arange(n_chunk)[None, :] < n_chunk - 1))
    ovt = jnp.asarray(ov.astype(np.float32), BF16)

    bias_a, bias_d = _bias_tiles(rel_bias)
    far = rel_bias[NUM_BUCKETS - 1] * LOG2E
    mod = _adaln(c, ada_w, ada_b)

    w_re = _repack_w_in(w_in)
    wo_b, wup_b, wdn_b = w_out.astype(BF16), ffn_w_up.astype(BF16), ffn_w_down.astype(BF16)
    conv_w = ffn_conv_w.reshape(depth, CONV_WIDTH, d_ff)
    conv_b = ffn_conv_b.reshape(depth, 1, d_ff)

    for l in range(depth):
        sh_a, sc_a, gt_a, sh_m, sc_m, gt_m = [m.reshape(bsz, 1, d) for m in jnp.split(mod[l], 6, axis=-1)]
        scale_a = DIFF_QK_DIM ** -0.5 * LOG2E
        scale = HEAD_DIM ** -0.5 * LOG2E
        ones = jnp.ones((HEAD_DIM,), F32)
        gains = jnp.stack([
            jnp.tile(diff_qnorm_g[l] * scale_a, GW // DIFF_QK_DIM), jnp.tile(diff_knorm_g[l], GW // DIFF_QK_DIM),
            jnp.tile(fox_qnorm_g[l] * scale, HPM), jnp.tile(fox_knorm_g[l], HPM),
            jnp.tile(nsa_qnorm_g[l] * scale, HPM),
            jnp.concatenate([nsa_knorm_g[l, 1], ones, nsa_knorm_g[l, 2], ones]),
            jnp.ones((GW,), F32), jnp.ones((GW,), F32)])
        (a_qt, a_k, a_vt, b_qt, b_k, b_vt, c_qt, c_k, c_vt, d_qt,
         d_ks, d_kw, d_vst, d_vwt, d_cmp, smalls) = _inproj(
            l, x, norm_mix_g[l].reshape(1, d), sc_a, sh_a, w_re, gains, bd32, bd64)

        bf_row = jnp.pad(fox_b_f[l], (0, 128 - HPM)).reshape(1, 128)
        ck, st = _scalars(smalls, bf_row, tri_incl)

        pe_rows, w_blk = _compress_params(nsa_pe[l], nsa_phi_w[l])
        gk_pad = jnp.pad(nsa_knorm_g[l, 0], (0, 128 - HEAD_DIM)).reshape(1, 128)
        kc, vct = _compress(d_cmp, pe_rows, w_blk, gk_pad)

        bnd = _score_bounds(rel_bias, diff_qnorm_g[l], diff_knorm_g[l], fox_qnorm_g[l], fox_knorm_g[l],
                            nsa_qnorm_g[l], nsa_knorm_g[l, 1], nsa_knorm_g[l, 2])
        lam_init = 0.8 - 0.6 * math.exp(-0.3 * l)
        cst = jnp.full((1,), lam_init, F32)
        k4 = lambda a: a.reshape(bsz, nk, T, a.shape[-1])
        o_a = _diff_attention(far, cst, bnd, a_qt, k4(a_k), a_vt, bias_a, diff_lambda[l],
                              diff_subln_g[l].reshape(HEAD_DIM, 1))
        o_b = _fox_attention(bnd, b_qt, k4(b_k), b_vt, st, k4(ck))
        o_c = _sb_attention(c_qt, k4(c_k), c_vt, tri_later)
        o_d = _nsa_attention(far, bnd, d_qt, kc, vct, k4(d_ks), d_vst, k4(d_kw), d_vwt, st, bias_d, ovt)

        x = _post(l, x, o_a, o_b, o_c, o_d, wo_b, gt_a, norm_ffn_g[l].reshape(1, d),
                  sc_m, sh_m, gt_m, wup_b, conv_w, conv_b, wdn_b)
    return x
```

```python
import math

import numpy as np
import jax
import jax.numpy as jnp
from jax import lax
from jax.experimental import pallas as pl
from jax.experimental.pallas import tpu as pltpu

HEAD_DIM = 64
HPM = 4
GW = HPM * HEAD_DIM
DIFF_QK_DIM = HEAD_DIM // 2
NUM_BUCKETS = 32
MAX_EXACT = NUM_BUCKETS // 2
MAX_DISTANCE = 128
CMP_LEN = 32
CMP_STRIDE = 16
SEL_LEN = 64
N_SEL = 16
WINDOW = 512
FORCED_SCORE = 1.0e4
NEG = -1.0e30
EPS = 1e-6
CONV_WIDTH = 3
LOG2E = 1.4426950408889634

T = 256
TM = 4 * T
TM_POST = 256
FF_CHUNK = 256
ONES_ROWS = 16
VROWS = HEAD_DIM + ONES_ROWS
BAND_ROWS = 2 * T // CMP_STRIDE
VMEM_LIMIT = 56 * 1024 * 1024

BND_FOX, BND_DIFF, BND_SEL, BND_WIN, BND_SIZE = 0, 2, 7, 12, 24
BOUND_MARGIN = 1.02
MAX_LOG2_SPAN = 100.0

F32 = jnp.float32
BF16 = jnp.bfloat16


def _dot(a, b):
    return jnp.dot(a, b, preferred_element_type=F32)


def _split2(x):
    hi = x.astype(BF16)
    lo = (x - hi.astype(F32)).astype(BF16)
    return hi, lo


def _log_sigmoid(z):
    return jnp.minimum(z, 0.0) - jnp.log(1.0 + jnp.exp(-jnp.abs(z)))


def _sigmoid(z):
    return 1.0 / (1.0 + jnp.exp(-z))


def _whole(shape):
    nd = len(shape)
    return pl.BlockSpec(shape, lambda *_: (0,) * nd)


def _resident(shape):
    nd = len(shape)
    return pl.BlockSpec(shape, lambda *_: (0,) * nd, pipeline_mode=pl.Buffered(1))


def _smem():
    return pl.BlockSpec(memory_space=pltpu.SMEM)


def _t5_bucket_np(dist):
    n = np.maximum(dist, 0)
    ratio = math.log(MAX_DISTANCE / MAX_EXACT)
    out = None
    for dt in (np.float32, np.float64):
        large = MAX_EXACT + (np.log(np.maximum(n, 1).astype(dt) / dt(MAX_EXACT)) / dt(ratio)
                             * dt(NUM_BUCKETS - MAX_EXACT)).astype(np.int32)
        b = np.where(n < MAX_EXACT, n, np.minimum(large, NUM_BUCKETS - 1)).astype(np.int32)
        assert out is None or np.array_equal(out, b)
        out = b
    return out


def _bucket_tiles():
    r = np.arange(T)[:, None]
    c = np.arange(T)[None, :]
    diag = _t5_bucket_np(c - r)
    near = _t5_bucket_np(T + c - r)
    rb = np.arange(BAND_ROWS)[:, None]
    band = _t5_bucket_np(c - CMP_STRIDE * rb + (T - CMP_LEN + 1))
    return np.concatenate([diag, near, band], axis=0)


def _block_ones(n, group):
    i = np.arange(n)
    return (i[:, None] // group == i[None, :] // group).astype(np.float32)


def _bias_kernel(tbl_ref, bkt_ref, a_ref, d_ref):
    b = bkt_ref[...]
    for h in range(2 * HPM):
        acc = jnp.zeros(b.shape, F32)
        for k in range(NUM_BUCKETS):
            acc = jnp.where(b == k, tbl_ref[k, h] * LOG2E, acc)
        if h < HPM:
            a_ref[h] = acc[0:2 * T]
        else:
            d_ref[:, (h - HPM) * T:(h - HPM + 1) * T] = acc


def _bias_tiles(rel_bias):
    bkt = jnp.asarray(_bucket_tiles())
    rows = bkt.shape[0]
    return pl.pallas_call(
        _bias_kernel,
        out_shape=[jax.ShapeDtypeStruct((HPM, 2 * T, T), F32), jax.ShapeDtypeStruct((rows, HPM * T), F32)],
        in_specs=[_smem(), _whole(bkt.shape)],
        out_specs=[_whole((HPM, 2 * T, T)), _whole((rows, HPM * T))],
        name="bias_tiles",
    )(rel_bias, bkt)


def _adaln_kernel(c_ref, w_ref, b_ref, o_ref):
    c = c_ref[...]
    ca = c * _sigmoid(c)
    o_ref[0] = _dot(ca.astype(BF16), w_ref[0].astype(BF16)) + b_ref[0]


def _adaln(c, ada_w, ada_b):
    depth, d, n = ada_w.shape
    bsz = c.shape[0]
    rows = -(-bsz // 8) * 8
    cp = jnp.pad(c, ((0, rows - bsz), (0, 0)))
    tn = 1536
    out = pl.pallas_call(
        _adaln_kernel,
        out_shape=jax.ShapeDtypeStruct((depth, rows, n), F32),
        grid=(depth, n // tn),
        in_specs=[pl.BlockSpec((rows, d), lambda l, j: (0, 0)),
                  pl.BlockSpec((1, d, tn), lambda l, j: (l, 0, j)),
                  pl.BlockSpec((1, 1, tn), lambda l, j: (l, 0, j))],
        out_specs=pl.BlockSpec((1, rows, tn), lambda l, j: (l, 0, j)),
        name="adaln",
    )(cp, ada_w, ada_b.reshape(depth, 1, n))
    return out[:, :bsz]


N_PROJ = 11 * GW + 2 * 128


def _inproj_kernel(x_ref, g_ref, sc_ref, sh_ref, w_ref, gains_ref, bd32_ref, bd64_ref,
                   aq_ref, ak_ref, av_ref, bq_ref, bk_ref, bv_ref, cq_ref, ck_ref, cv_ref, dq_ref,
                   dks_ref, dkw_ref, dvs_ref, dvw_ref, cmp_ref, sm_ref):
    x = x_ref[0]
    ms = jnp.mean(x * x, axis=-1, keepdims=True)
    h = x * lax.rsqrt(ms + EPS) * g_ref[...]
    h = h * (1.0 + sc_ref[0]) + sh_ref[0]
    hb = h.astype(BF16)

    def proj(group, width=GW):
        off = group * GW
        return _dot(hb, w_ref[0, :, off:off + width])

    def segnorm(y, bd_ref, inv_n, gain_row):
        ss = _dot((y * y).astype(BF16), bd_ref[...])
        return y * lax.rsqrt(ss * inv_n + EPS) * gains_ref[gain_row:gain_row + 1, :]

    def put_t(ref, y):
        for j in range(TM // T):
            ref[0, j] = y[j * T:(j + 1) * T].T.astype(BF16)

    raw0 = proj(0)
    raw1 = proj(1)
    put_t(aq_ref, segnorm(raw0, bd32_ref, 1.0 / DIFF_QK_DIM, 0))
    put_t(av_ref, proj(2))
    raw3 = proj(3)
    ak_ref[0] = segnorm(raw1, bd32_ref, 1.0 / DIFF_QK_DIM, 1).astype(BF16)
    put_t(bv_ref, proj(5))
    raw4 = proj(4)
    put_t(bq_ref, segnorm(raw3, bd64_ref, 1.0 / HEAD_DIM, 2))
    put_t(cq_ref, proj(6) * (HEAD_DIM ** -0.5 * LOG2E))
    raw9 = proj(9)
    bk_ref[0] = segnorm(raw4, bd64_ref, 1.0 / HEAD_DIM, 3).astype(BF16)
    ck_ref[0] = proj(7).astype(BF16)
    y = proj(10)
    put_t(dq_ref, segnorm(raw9, bd64_ref, 1.0 / HEAD_DIM, 4))
    put_t(cv_ref, proj(8))
    tail = proj(11)
    cmp_ref[0] = tail[:, 0:128]
    sm_ref[0] = tail[:, 128:256]

    yn = segnorm(y, bd64_ref, 1.0 / HEAD_DIM, 5)
    dks_ref[0] = yn[:, 0:HEAD_DIM].astype(BF16)
    dkw_ref[0] = yn[:, 2 * HEAD_DIM:3 * HEAD_DIM].astype(BF16)
    for j in range(TM // T):
        yt = y[j * T:(j + 1) * T].T
        dvs_ref[0, j] = yt[HEAD_DIM:2 * HEAD_DIM].astype(BF16)
        dvw_ref[0, j] = yt[3 * HEAD_DIM:4 * HEAD_DIM].astype(BF16)


def _inproj(layer, x, g, sc, sh, w_re, gains, bd32, bd64):
    bsz, s, d = x.shape
    nk = s // T
    tok = lambda width, dt: jax.ShapeDtypeStruct((bsz, s, width), dt)
    tr = lambda rows: jax.ShapeDtypeStruct((bsz, nk, rows, T), BF16)
    tok_spec = lambda width: pl.BlockSpec((1, TM, width), lambda b, i: (b, i, 0))
    tr_spec = lambda rows: pl.BlockSpec((1, TM // T, rows, T), lambda b, i: (b, i, 0, 0))
    vec = pl.BlockSpec((1, 1, d), lambda b, i: (b, 0, 0))
    out_shape = [tr(GW), tok(GW, BF16), tr(GW)] * 3 + [tr(GW)] + [
        tok(HEAD_DIM, BF16), tok(HEAD_DIM, BF16), tr(HEAD_DIM), tr(HEAD_DIM),
        tok(128, F32), tok(128, F32)]
    out_specs = [tr_spec(GW), tok_spec(GW), tr_spec(GW)] * 3 + [tr_spec(GW)] + [
        tok_spec(HEAD_DIM), tok_spec(HEAD_DIM), tr_spec(HEAD_DIM), tr_spec(HEAD_DIM),
        tok_spec(128), tok_spec(128)]
    return pl.pallas_call(
        _inproj_kernel,
        out_shape=out_shape,
        grid=(bsz, s // TM),
        in_specs=[pl.BlockSpec((1, TM, d), lambda b, i: (b, i, 0)),
                  _whole((1, d)), vec, vec,
                  pl.BlockSpec((1,) + w_re.shape[1:], lambda b, i: (layer, 0, 0)),
                  _whole(gains.shape), _whole(bd32.shape), _whole(bd64.shape)],
        out_specs=out_specs,
        compiler_params=pltpu.CompilerParams(
            dimension_semantics=("parallel", "parallel"), vmem_limit_bytes=VMEM_LIMIT),
        name="inproj",
    )(x, g, sc, sh, w_re, gains, bd32, bd64)


def _scalars_kernel(sm_ref, bf_ref, tri_ref, ck_ref, st_ref):
    s = sm_ref.shape[1]
    carry = jnp.zeros((1, 128), F32)
    for blk in range(s // T):
        rows = slice(blk * T, (blk + 1) * T)
        lf = _log_sigmoid(sm_ref[0, rows, :] + bf_ref[...])
        h1 = lf.astype(BF16)
        r1 = lf - h1.astype(F32)
        h2 = r1.astype(BF16)
        h3 = (r1 - h2.astype(F32)).astype(BF16)
        tri = tri_ref[...]
        cb = _dot(tri, h1) + _dot(tri, h2) + _dot(tri, h3) + carry
        carry = cb[T - 1:T, :]
        cb2 = cb * LOG2E
        ck_ref[0, rows, :] = cb2
        col = lax.broadcasted_iota(jnp.int32, (T, 128), 1)
        comb = jnp.where(col < HPM, cb2, _sigmoid(sm_ref[0, rows, :]))
        st_ref[0, blk] = comb.T[0:16]


def _scalars(smalls, bf_row, tri):
    bsz, s, _ = smalls.shape
    return pl.pallas_call(
        _scalars_kernel,
        out_shape=[jax.ShapeDtypeStruct((bsz, s, 128), F32), jax.ShapeDtypeStruct((bsz, s // T, 16, T), F32)],
        grid=(bsz,),
        in_specs=[pl.BlockSpec((1, s, 128), lambda b: (b, 0, 0)), _whole((1, 128)), _whole((T, T))],
        out_specs=[pl.BlockSpec((1, s, 128), lambda b: (b, 0, 0)),
                   pl.BlockSpec((1, s // T, 16, T), lambda b: (b, 0, 0, 0))],
        compiler_params=pltpu.CompilerParams(dimension_semantics=("parallel",)),
        name="token_scalars",
    )(smalls, bf_row, tri)


def _compress_kernel(x_ref, pe_ref, w_ref, gk_ref, kc_ref, vct_ref):
    n = x_ref.shape[1] // CMP_STRIDE
    chunks = jnp.concatenate(
        [x_ref[0, pl.ds(i, n, stride=CMP_STRIDE), :] for i in range(CMP_STRIDE)], axis=1)
    ya = _dot((chunks + pe_ref[0:1]).astype(BF16), w_ref[0])
    yb = _dot((chunks + pe_ref[1:2]).astype(BF16), w_ref[1])
    y = ya + pltpu.roll(yb, n - 1, 0)
    lane = lax.broadcasted_iota(jnp.int32, y.shape, 1)
    ss = jnp.sum(jnp.where(lane < HEAD_DIM, y * y, 0.0), axis=-1, keepdims=True)
    kc = y * lax.rsqrt(ss * (1.0 / HEAD_DIM) + EPS) * gk_ref[...]
    kc_ref[0] = kc[:, 0:HEAD_DIM].astype(BF16)
    vct_ref[0] = y.T[HEAD_DIM:2 * HEAD_DIM].astype(BF16)


def _compress(cmp, pe_rows, w_blk, gk_pad):
    bsz, s, width = cmp.shape
    n = s // CMP_STRIDE
    return pl.pallas_call(
        _compress_kernel,
        out_shape=[jax.ShapeDtypeStruct((bsz, n, HEAD_DIM), BF16),
                   jax.ShapeDtypeStruct((bsz, HEAD_DIM, n), BF16)],
        grid=(bsz,),
        in_specs=[pl.BlockSpec((1, s, width), lambda b: (b, 0, 0)),
                  _whole(pe_rows.shape), _whole(w_blk.shape), _whole(gk_pad.shape)],
        out_specs=[pl.BlockSpec((1, n, HEAD_DIM), lambda b: (b, 0, 0)),
                   pl.BlockSpec((1, HEAD_DIM, n), lambda b: (b, 0, 0))],
        compiler_params=pltpu.CompilerParams(dimension_semantics=("parallel",)),
        name="nsa_compress",
    )(cmp, pe_rows, w_blk, gk_pad)


def _head_rows(qt, h):
    return qt[h * HEAD_DIM:(h + 1) * HEAD_DIM, :]


def _head_cols(kt, h):
    return kt[:, h * HEAD_DIM:(h + 1) * HEAD_DIM]


def _split_maps(qh):
    qf = qh.astype(F32)
    row = lax.broadcasted_iota(jnp.int32, qf.shape, 0)
    first = jnp.where(row < DIFF_QK_DIM, qf, 0.0).astype(BF16)
    second = jnp.where(row >= DIFF_QK_DIM, qf, 0.0).astype(BF16)
    return jnp.concatenate([first, second], axis=1)


def _with_ones(vt):
    return jnp.concatenate([vt, jnp.ones((ONES_ROWS, vt.shape[1]), BF16)], axis=0)


def _softmax_probs(ss, m_ref, idx, shift=None, fixed=None):
    if fixed is not None:
        already = isinstance(fixed, float) and fixed == 0.0
        return None, [jnp.exp2(s if already else s - fixed).astype(BF16) for s in ss]
    m_old = m_ref[idx]
    mx = jnp.max(ss[0], axis=0, keepdims=True)
    for s in ss[1:]:
        mx = jnp.maximum(mx, jnp.max(s, axis=0, keepdims=True))
    if shift is not None:
        mx = mx + shift
    m_new = jnp.maximum(m_old, mx)
    m_ref[idx] = m_new
    alpha = jnp.exp2(m_old - m_new)
    sub = m_new if shift is None else m_new - shift
    return alpha, [jnp.exp2(s - sub).astype(BF16) for s in ss]


def _accumulate(acc_ref, rows, alpha, vts, ps):
    upd = _dot(vts[0], ps[0])
    for vt, p in zip(vts[1:], ps[1:]):
        upd = upd + _dot(vt, p)
    acc_ref[rows, :] = (acc_ref[rows, :] if alpha is None else alpha * acc_ref[rows, :]) + upd


def _normalized(acc_ref, idx):
    base = idx * VROWS
    return acc_ref[base:base + HEAD_DIM, :] / acc_ref[base + HEAD_DIM:base + HEAD_DIM + 1, :]


def _causal_tile(width=T):
    r = lax.broadcasted_iota(jnp.int32, (T, width), 0)
    c = lax.broadcasted_iota(jnp.int32, (T, width), 1)
    if width != T:
        c = jnp.bitwise_and(c, T - 1)
    return r, c


def _for_tile_groups(qi, process):
    n_far = jnp.maximum(qi - 1, 0)
    n_pair = lax.shift_right_logical(n_far, 1)
    odd = jnp.bitwise_and(n_far, 1) == 1

    def pair(j, carry):
        process([(2 * j, "far"), (2 * j + 1, "far")])
        return carry

    lax.fori_loop(0, n_pair, pair, 0)

    @pl.when(qi == 0)
    def _():
        process([(qi, "diag")])

    @pl.when((qi >= 1) & jnp.logical_not(odd))
    def _():
        process([(qi - 1, "near"), (qi, "diag")])

    @pl.when((qi >= 1) & odd)
    def _():
        process([(qi - 2, "far"), (qi - 1, "near"), (qi, "diag")])


def _for_independent_tiles(qi, process):
    n_far = jnp.maximum(qi - 1, 0)
    n_group = lax.shift_right_logical(n_far, 2)
    rest = jnp.bitwise_and(n_far, 3)

    def many(j, carry):
        process([(4 * j + i, "far") for i in range(4)])
        return carry

    lax.fori_loop(0, n_group, many, 0)

    @pl.when(qi == 0)
    def _():
        process([(qi, "diag")])

    for n in range(4):
        @pl.when((qi >= 1) & (rest == n))
        def _():
            process([(4 * n_group + i, "far") for i in range(n)] + [(qi - 1, "near"), (qi, "diag")])


def _run_bounded_or_online(bounded_ok, qi, process):
    @pl.when(bounded_ok)
    def _():
        _for_independent_tiles(qi, lambda tiles: process(tiles, True))

    @pl.when(jnp.logical_not(bounded_ok))
    def _():
        _for_tile_groups(qi, lambda tiles: process(tiles, False))


def _attn_call(tile_kernel, name, bsz, nq, in_arrays, in_specs, scratch, out_width=GW):
    def kernel(*refs):
        def query_tile(qi, carry):
            tile_kernel(qi, *refs)
            return carry

        lax.fori_loop(0, nq, query_tile, 0)

    return pl.pallas_call(
        kernel,
        out_shape=jax.ShapeDtypeStruct((bsz, nq * T, out_width), BF16),
        grid=(bsz,),
        in_specs=in_specs,
        out_specs=pl.BlockSpec((1, nq * T, out_width), lambda b: (b, 0, 0)),
        scratch_shapes=scratch,
        compiler_params=pltpu.CompilerParams(
            dimension_semantics=("parallel",), vmem_limit_bytes=VMEM_LIMIT),
        name=name,
    )(*in_arrays)


def _store_query_tile(o_ref, qi, value):
    o_ref[0, pl.ds(pl.multiple_of(qi * T, T), T), :] = value


def _seq_spec(shape):
    nd = len(shape)
    return pl.BlockSpec((1,) + tuple(shape[1:]), lambda b: (b,) + (0,) * (nd - 1))


def _diff_kernel(qi, far_ref, cst_ref, bnd_ref, qt_ref, k_ref, vt_ref, bias_ref, lam_ref, subg_ref, o_ref,
                 m_ref, acc_ref):
    qmaps = [_split_maps(_head_rows(qt_ref[0, qi], h)) for h in range(HPM)]
    m_ref[...] = jnp.full(m_ref.shape, NEG, F32)
    acc_ref[...] = jnp.zeros(acc_ref.shape, F32)
    r, c = _causal_tile()

    def process(tiles, bounded):
        scores = [[_dot(_head_cols(k_ref[0, ki], h), qmaps[h]) for ki, _ in tiles] for h in range(HPM)]
        probs = []
        for idx in range(2 * HPM):
            h = idx // 2
            bound = bnd_ref[BND_DIFF + 1 + h] if bounded else None
            ss = []
            for (ki, mode), sc in zip(tiles, scores[h]):
                s = sc[:, (idx % 2) * T:(idx % 2 + 1) * T]
                if mode == "far":
                    s = s + ((far_ref[h] - bound) if bounded else far_ref[h])
                elif mode == "near":
                    s = s + bias_ref[h, T:2 * T, :]
                else:
                    s = jnp.where(r <= c, s + bias_ref[h, 0:T, :], NEG)
                if bounded and mode != "far":
                    s = s - bound
                ss.append(s)
            probs.append(_softmax_probs(ss, m_ref, idx, fixed=0.0 if bounded else None))
        for idx in range(2 * HPM):
            h = idx // 2
            vts = [_with_ones(vt_ref[0, ki, h * HEAD_DIM:(h + 1) * HEAD_DIM, :]) for ki, _ in tiles]
            alpha, ps = probs[idx]
            _accumulate(acc_ref, slice(idx * VROWS, (idx + 1) * VROWS), alpha, vts, ps)

    _run_bounded_or_online(bnd_ref[BND_DIFF] > 0.5, qi, process)

    lam_init = cst_ref[0]
    lv = lam_ref[...]
    lam = (jnp.exp(jnp.sum(lv[0:1] * lv[1:2], axis=-1, keepdims=True))
           - jnp.exp(jnp.sum(lv[2:3] * lv[3:4], axis=-1, keepdims=True)) + lam_init)
    outs = []
    for h in range(HPM):
        o = _normalized(acc_ref, 2 * h) - lam * _normalized(acc_ref, 2 * h + 1)
        ms = jnp.mean(o * o, axis=0, keepdims=True)
        outs.append(o * lax.rsqrt(ms + EPS) * subg_ref[...] * (1.0 - lam_init))
    _store_query_tile(o_ref, qi, jnp.concatenate(outs, axis=0).T.astype(BF16))


def _diff_attention(far, cst, bnd, qt, k4, vt, bias, lam, subg):
    bsz, nq = qt.shape[0], qt.shape[1]
    return _attn_call(
        _diff_kernel, "diff_attention", bsz, nq,
        [far, cst, bnd, qt, k4, vt, bias, lam, subg],
        [_smem(), _smem(), _smem(), _seq_spec(qt.shape), _seq_spec(k4.shape), _seq_spec(vt.shape),
         _resident(bias.shape), _whole(lam.shape), _whole(subg.shape)],
        [pltpu.VMEM((2 * HPM, 1, T), F32), pltpu.VMEM((2 * HPM * VROWS, T), F32)])


def _fox_kernel(qi, bnd_ref, qt_ref, k_ref, vt_ref, cq_ref, ck_ref, o_ref, m_ref, acc_ref):
    m_ref[...] = jnp.full(m_ref.shape, NEG, F32)
    acc_ref[...] = jnp.zeros(acc_ref.shape, F32)
    r, c = _causal_tile()

    def process(tiles, bounded):
        scores = [[_dot(_head_cols(k_ref[0, ki], h), _head_rows(qt_ref[0, qi], h)) for ki, _ in tiles]
                  for h in range(HPM)]
        probs = []
        for h in range(HPM):
            ss = []
            for (ki, mode), sc in zip(tiles, scores[h]):
                s = sc - ck_ref[0, ki][:, h:h + 1]
                if mode == "diag":
                    s = jnp.where(r <= c, s, NEG)
                ss.append(s)
            cq = cq_ref[0, qi, h:h + 1, :]
            fixed = (bnd_ref[BND_FOX + 1] - cq) if bounded else None
            probs.append(_softmax_probs(ss, m_ref, h, shift=cq, fixed=fixed))
        for h in range(HPM):
            vts = [_with_ones(vt_ref[0, ki, h * HEAD_DIM:(h + 1) * HEAD_DIM, :]) for ki, _ in tiles]
            alpha, ps = probs[h]
            _accumulate(acc_ref, slice(h * VROWS, (h + 1) * VROWS), alpha, vts, ps)

    _run_bounded_or_online(bnd_ref[BND_FOX] > 0.5, qi, process)
    outs = [_normalized(acc_ref, h) for h in range(HPM)]
    _store_query_tile(o_ref, qi, jnp.concatenate(outs, axis=0).T.astype(BF16))


def _fox_attention(bnd, qt, k4, vt, st, ck4):
    bsz, nq = qt.shape[0], qt.shape[1]
    return _attn_call(
        _fox_kernel, "forgetting_attention", bsz, nq,
        [bnd, qt, k4, vt, st, ck4],
        [_smem(), _seq_spec(qt.shape), _seq_spec(k4.shape), _seq_spec(vt.shape),
         _seq_spec(st.shape), _seq_spec(ck4.shape)],
        [pltpu.VMEM((HPM, 1, T), F32), pltpu.VMEM((HPM * VROWS, T), F32)])


def _sb_kernel(qi, qt_ref, k_ref, vt_ref, tri_ref, o_ref, run_ref, acc_ref):
    run_ref[...] = jnp.zeros(run_ref.shape, F32)
    acc_ref[...] = jnp.zeros(acc_ref.shape, F32)
    r, c = _causal_tile()

    def process(tiles):
        scores = [[_dot(_head_cols(k_ref[0, ki], h), _head_rows(qt_ref[0, qi], h)) for ki, _ in tiles]
                  for h in range(HPM)]
        weights = []
        for h in range(HPM):
            run = run_ref[h]
            ws = []
            for (ki, diag), z in zip(tiles, scores[h]):
                log_beta = jnp.minimum(z, 0.0) - jnp.log2(1.0 + jnp.exp2(-jnp.abs(z)))
                log_keep = log_beta - z
                if diag:
                    log_keep = jnp.where(r < c, log_keep, 0.0)
                keep_b = log_keep.astype(BF16)
                tail = _dot(tri_ref[...], keep_b) + run
                a = jnp.exp2(log_beta + tail)
                if diag:
                    a = jnp.where(r < c, a, 0.0)
                ws.append(a.astype(BF16))
                run = tail[0:1, :] + keep_b[0:1, :].astype(F32)
            run_ref[h] = run
            weights.append(ws)
        for h in range(HPM):
            rows = slice(h * HEAD_DIM, (h + 1) * HEAD_DIM)
            upd = None
            for (ki, _), a in zip(tiles, weights[h]):
                term = _dot(vt_ref[0, ki, rows, :], a)
                upd = term if upd is None else upd + term
            acc_ref[rows, :] = acc_ref[rows, :] + upd

    rest = jnp.bitwise_and(qi, 3)
    for n in range(4):
        @pl.when(rest == n)
        def _():
            process([(qi, True)] + [(qi - d, False) for d in range(1, n + 1)])

    base = qi - 1 - rest

    def four(j, carry):
        process([(base - 4 * j - i, False) for i in range(4)])
        return carry

    lax.fori_loop(0, lax.shift_right_logical(qi, 2), four, 0)
    _store_query_tile(o_ref, qi, acc_ref[...].T.astype(BF16))


def _sb_attention(qt, k4, vt, tri):
    bsz, nq = qt.shape[0], qt.shape[1]
    return _attn_call(
        _sb_kernel, "stick_breaking_attention", bsz, nq,
        [qt, k4, vt, tri],
        [_seq_spec(qt.shape), _seq_spec(k4.shape), _seq_spec(vt.shape), _resident(tri.shape)],
        [pltpu.VMEM((HPM, 1, T), F32), pltpu.VMEM((GW, T), F32)])


def _nsa_kernel(qi, far_ref, bnd_ref, qt_ref, kc_ref, vct_ref, ks_ref, vst_ref, kw_ref, vwt_ref, g_ref,
                bias_ref, ovt_ref, o_ref,
                sc_ref, key_ref, sel_ref, m_ref, acc_ref):
    q0 = qi * T
    n_cmp = kc_ref.shape[1]
    n_blk = ovt_ref.shape[0]
    n_sel = min(N_SEL, n_blk)
    w4 = HPM * T
    qt = qt_ref[0, qi]
    qcat = jnp.concatenate([qt[h * HEAD_DIM:(h + 1) * HEAD_DIM, :] for h in range(HPM)], axis=1)
    far_row = jnp.concatenate([jnp.full((1, T), far_ref[HPM + h], F32) for h in range(HPM)], axis=1)
    r, c = _causal_tile(w4)

    sc_ref[...] = _dot(kc_ref[0], qcat)
    delta = bias_ref[2 * T:2 * T + BAND_ROWS, :] - far_row

    @pl.when(qi == 0)
    def _():
        half = BAND_ROWS // 2
        sc_ref[0:half, :] = sc_ref[0:half, :] + delta[half:BAND_ROWS]

    @pl.when(qi > 0)
    def _():
        c0 = pl.multiple_of(qi * (T // CMP_STRIDE) - BAND_ROWS // 2, 8)
        sc_ref[pl.ds(c0, BAND_ROWS), :] = sc_ref[pl.ds(c0, BAND_ROWS), :] + delta

    ci = lax.broadcasted_iota(jnp.int32, (n_cmp, w4), 0)
    ti = q0 + jnp.bitwise_and(lax.broadcasted_iota(jnp.int32, (n_cmp, w4), 1), T - 1)
    valid_c = ti - CMP_STRIDE * ci - (CMP_LEN - 1) >= 0
    s = jnp.where(valid_c, sc_ref[...] + far_row, NEG)
    e = jnp.exp2(s - jnp.max(s, axis=0, keepdims=True))
    p = jnp.where(valid_c, e / jnp.sum(e, axis=0, keepdims=True), 0.0)
    o_cmp = _dot(vct_ref[0], p.astype(BF16))
    pc_sum = p[:, 0:T]
    for h in range(1, HPM):
        pc_sum = pc_sum + p[:, h * T:(h + 1) * T]

    hi, lo = _split2(pc_sum)
    imp = _dot(ovt_ref[...], hi) + _dot(ovt_ref[...], lo)
    ji = lax.broadcasted_iota(jnp.int32, (n_blk, T), 0)
    tq = q0 + lax.broadcasted_iota(jnp.int32, (n_blk, T), 1)
    forced = (ji == lax.shift_right_logical(tq, int(math.log2(SEL_LEN)))) | (ji == 0)
    imp = jnp.where(forced, FORCED_SCORE, imp)
    imp = jnp.where(ji * SEL_LEN <= tq, imp, -1.0)
    keys = lax.bitcast_convert_type(imp, jnp.int32)
    key_ref[...] = keys

    def rank_body(jp, cnt):
        row = key_ref[pl.ds(jp, 1), :]
        below = lax.shift_right_arithmetic(jp - ji, 31)
        return cnt + jnp.where(row > keys + below, 1, 0)

    n_seen = jnp.minimum((qi + 1) * (T // SEL_LEN), n_blk)
    cnt = lax.fori_loop(0, n_seen, rank_body, jnp.zeros((n_blk, T), jnp.int32))
    sel = jnp.where(cnt < n_sel, 1.0, 0.0)
    sel_ref[...] = jnp.concatenate([sel] * HPM, axis=1)

    m_ref[...] = jnp.full(m_ref.shape, NEG, F32)
    acc_ref[...] = jnp.zeros(acc_ref.shape, F32)

    def bound_row(base):
        return jnp.concatenate([jnp.full((1, T), bnd_ref[base + 1 + h], F32) for h in range(HPM)], axis=1)

    def biased(s, mode, bound):
        if mode == "far":
            return s + (far_row if bound is None else far_row - bound)
        tile = bias_ref[T:2 * T, :] if mode == "near" else bias_ref[0:T, :]
        return s + tile if bound is None else s + tile - bound

    def update(state, ss, vts, bounded):
        alpha, ps = _softmax_probs(ss, m_ref, state, fixed=0.0 if bounded else None)
        _accumulate(acc_ref, slice(state * VROWS, (state + 1) * VROWS), alpha, vts, ps)

    def sel_mask(ki):
        per = T // SEL_LEN
        rows = [jnp.broadcast_to(sel_ref[pl.ds(ki * per + i, 1), :], (SEL_LEN, w4)) for i in range(per)]
        return jnp.concatenate(rows, axis=0) > 0.5

    def process_selected(tiles, bounded):
        bound = bound_row(BND_SEL) if bounded else None
        scores = [_dot(ks_ref[0, ki], qcat) for ki, _ in tiles]
        ss = []
        for (ki, mode), sc in zip(tiles, scores):
            mask = sel_mask(ki)
            if mode == "diag":
                mask = mask & (r <= c)
            ss.append(jnp.where(mask, biased(sc, mode, bound), NEG))
        update(0, ss, [_with_ones(vst_ref[0, ki]) for ki, _ in tiles], bounded)

    _run_bounded_or_online(bnd_ref[BND_SEL] > 0.5, qi, process_selected)

    def process_window(tiles, bounded):
        bound = bound_row(BND_WIN) if bounded else None
        scores = [_dot(kw_ref[0, ki], qcat) for ki, _, _ in tiles]
        ss = []
        for (ki, mode, mask), sc in zip(tiles, scores):
            s = biased(sc, mode, bound)
            ss.append(s if mask is None else jnp.where(mask, s, NEG))
        update(1, ss, [_with_ones(vwt_ref[0, ki]) for ki, _, _ in tiles], bounded)

    back = WINDOW // T

    def window_tiles(n):
        tiles = [(qi, "diag", r <= c)]
        for d in range(1, n + 1):
            mode = "near" if d == 1 else "far"
            tiles.append((qi - d, mode, (c < r) if d == back else None))
        return tiles

    win_bounded = bnd_ref[BND_WIN] > 0.5
    for n in range(back + 1):
        cond = (qi == n) if n < back else (qi >= back)

        @pl.when(cond & win_bounded)
        def _():
            process_window(window_tiles(n), True)

        @pl.when(cond & jnp.logical_not(win_bounded))
        def _():
            process_window(window_tiles(n), False)

    o_sel = acc_ref[0:HEAD_DIM, :] / acc_ref[HEAD_DIM:HEAD_DIM + 1, :]
    o_win = acc_ref[VROWS:VROWS + HEAD_DIM, :] / acc_ref[VROWS + HEAD_DIM:VROWS + HEAD_DIM + 1, :]
    gate = lambda br: jnp.concatenate(
        [g_ref[0, qi, HPM + br * HPM + h:HPM + br * HPM + h + 1, :] for h in range(HPM)], axis=1)
    o = gate(0) * o_cmp + gate(1) * o_sel + gate(2) * o_win
    ot = jnp.concatenate([o[:, h * T:(h + 1) * T] for h in range(HPM)], axis=0)
    _store_query_tile(o_ref, qi, ot.T.astype(BF16))


def _nsa_attention(far, bnd, qt, kc, vct, ks4, vst, kw4, vwt, st, bias, ovt):
    bsz, nq = qt.shape[0], qt.shape[1]
    n_cmp = kc.shape[1]
    n_blk = ovt.shape[0]
    w4 = HPM * T
    return _attn_call(
        _nsa_kernel, "native_sparse_attention", bsz, nq,
        [far, bnd, qt, kc, vct, ks4, vst, kw4, vwt, st, bias, ovt],
        [_smem(), _smem(), _seq_spec(qt.shape), _seq_spec(kc.shape), _seq_spec(vct.shape),
         _seq_spec(ks4.shape), _seq_spec(vst.shape), _seq_spec(kw4.shape), _seq_spec(vwt.shape),
         _seq_spec(st.shape), _resident(bias.shape), _whole(ovt.shape)],
        [pltpu.VMEM((n_cmp, w4), F32), pltpu.VMEM((n_blk, T), jnp.int32), pltpu.VMEM((n_blk, w4), F32),
         pltpu.VMEM((2, 1, w4), F32), pltpu.VMEM((2 * VROWS, w4), F32)])


def _post_kernel(x_ref, oa_ref, ob_ref, oc_ref, od_ref, wo_ref, gta_ref, g_ref, sc_ref, sh_ref,
                 gtm_ref, wup_ref, cw_ref, cb_ref, wdn_ref, y_ref, carry_ref):
    d_ff = wdn_ref.shape[1]
    mixed = (_dot(oa_ref[0], wo_ref[0, 0:GW]) + _dot(ob_ref[0], wo_ref[0, GW:2 * GW])
             + _dot(oc_ref[0], wo_ref[0, 2 * GW:3 * GW]) + _dot(od_ref[0], wo_ref[0, 3 * GW:4 * GW]))
    x1 = x_ref[0] + gta_ref[0] * mixed
    ms = jnp.mean(x1 * x1, axis=-1, keepdims=True)
    h = x1 * lax.rsqrt(ms + EPS) * g_ref[...]
    hb = (h * (1.0 + sc_ref[0]) + sh_ref[0]).astype(BF16)

    @pl.when(pl.program_id(1) == 0)
    def _():
        carry_ref[...] = jnp.zeros(carry_ref.shape, F32)

    def up(ch):
        gate = _dot(hb, wup_ref[0, :, ch * FF_CHUNK:(ch + 1) * FF_CHUNK])
        val = _dot(hb, wup_ref[0, :, d_ff + ch * FF_CHUNK:d_ff + (ch + 1) * FF_CHUNK])
        return gate, val

    row = lax.broadcasted_iota(jnp.int32, (TM_POST, FF_CHUNK), 0)
    y = jnp.zeros((TM_POST, x1.shape[1]), F32)
    n_chunks = d_ff // FF_CHUNK
    ahead = up(0)
    for ch in range(n_chunks):
        cols = slice(ch * FF_CHUNK, (ch + 1) * FF_CHUNK)
        gate, val = ahead
        if ch + 1 < n_chunks:
            ahead = up(ch + 1)
        prev = carry_ref[:, cols]
        g1 = jnp.where(row == 0, prev[7:8], pltpu.roll(gate, 1, 0))
        g2 = jnp.where(row == 0, prev[6:7], jnp.where(row == 1, prev[7:8], pltpu.roll(gate, 2, 0)))
        carry_ref[:, cols] = gate[TM_POST - 8:TM_POST]
        conv = (cw_ref[0, 0:1, cols] * g2 + cw_ref[0, 1:2, cols] * g1 + cw_ref[0, 2:3, cols] * gate
                + cb_ref[0, :, cols])
        act = conv * _sigmoid(conv) * val
        y = y + _dot(act.astype(BF16), wdn_ref[0, cols, :])
    y_ref[0] = x1 + gtm_ref[0] * y


def _post(layer, x, oa, ob, oc, od, wo, gta, g, sc, sh, gtm, wup, cw, cb, wdn):
    bsz, s, d = x.shape
    tok = lambda width: pl.BlockSpec((1, TM_POST, width), lambda b, i: (b, i, 0))
    vec = pl.BlockSpec((1, 1, d), lambda b, i: (b, 0, 0))
    of_layer = lambda a: pl.BlockSpec((1,) + a.shape[1:], lambda b, i: (layer,) + (0,) * (a.ndim - 1),
                                      pipeline_mode=pl.Buffered(1))
    return pl.pallas_call(
        _post_kernel,
        out_shape=jax.ShapeDtypeStruct((bsz, s, d), F32),
        grid=(bsz, s // TM_POST),
        in_specs=[tok(d), tok(GW), tok(GW), tok(GW), tok(GW), of_layer(wo), vec, _whole((1, d)), vec, vec,
                  vec, of_layer(wup), of_layer(cw), of_layer(cb), of_layer(wdn)],
        out_specs=tok(d),
        scratch_shapes=[pltpu.VMEM((8, wdn.shape[1]), F32)],
        compiler_params=pltpu.CompilerParams(
            dimension_semantics=("parallel", "arbitrary"), vmem_limit_bytes=VMEM_LIMIT),
        name="out_proj_mlp",
    )(x, oa, ob, oc, od, wo, gta, g, sc, sh, gtm, wup, cw, cb, wdn)


def _repack_w_in(w):
    off = {}
    pos = 0
    for name, size in (("a_q", GW), ("a_k", GW), ("a_v", GW), ("b_q", GW), ("b_k", GW), ("b_v", GW),
                       ("b_f", HPM), ("c_q", GW), ("c_k", GW), ("c_v", GW), ("d_q", GW),
                       ("d_kc", HEAD_DIM), ("d_vc", HEAD_DIM), ("d_ks", HEAD_DIM), ("d_vs", HEAD_DIM),
                       ("d_kw", HEAD_DIM), ("d_vw", HEAD_DIM), ("d_g", 3 * HPM)):
        off[name] = (pos, size)
        pos += size
    assert pos == w.shape[-1]
    col = lambda n: w[..., off[n][0]:off[n][0] + off[n][1]]
    order = ["a_q", "a_k", "a_v", "b_q", "b_k", "b_v", "c_q", "c_k", "c_v", "d_q",
             "d_ks", "d_vs", "d_kw", "d_vw", "d_kc", "d_vc", "b_f", "d_g"]
    parts = [col(n) for n in order] + [jnp.zeros(w.shape[:-1] + (128 - HPM - 3 * HPM,), w.dtype)]
    out = jnp.concatenate(parts, axis=-1).astype(BF16)
    assert out.shape[-1] == N_PROJ
    return out


def _compress_params(pe, phi_w):
    half = CMP_LEN // 2
    pe_rows = jnp.concatenate([pe[0], pe[1]], axis=1).reshape(2, half * 2 * HEAD_DIM)
    wk = phi_w[0].reshape(CMP_LEN, HEAD_DIM, HEAD_DIM)
    wv = phi_w[1].reshape(CMP_LEN, HEAD_DIM, HEAD_DIM)
    zero = jnp.zeros_like(wk)
    blk = jnp.concatenate([jnp.concatenate([wk, zero], axis=2), jnp.concatenate([zero, wv], axis=2)], axis=1)
    return pe_rows, blk.reshape(2, half * 2 * HEAD_DIM, 2 * HEAD_DIM).astype(BF16)


def _score_bounds(rel_bias, gq_a, gk_a, gq_b, gk_b, gq_d, gk_s, gk_w):
    amax = lambda g: jnp.max(jnp.abs(g))
    tb = rel_bias * LOG2E
    bmax, bmin = jnp.max(tb, axis=0), jnp.min(tb, axis=0)
    span = jnp.max(bmax - bmin)
    qk = lambda n, gq, gk, scale: n * amax(gq) * amax(gk) * (scale * LOG2E * BOUND_MARGIN)
    ok = lambda b, sp: (2.0 * b + sp <= MAX_LOG2_SPAN).astype(F32).reshape(1)
    b_fox = qk(HEAD_DIM, gq_b, gk_b, HEAD_DIM ** -0.5)
    b_a = qk(DIFF_QK_DIM, gq_a, gk_a, DIFF_QK_DIM ** -0.5)
    b_s = qk(HEAD_DIM, gq_d, gk_s, HEAD_DIM ** -0.5)
    b_w = qk(HEAD_DIM, gq_d, gk_w, HEAD_DIM ** -0.5)
    vec = jnp.concatenate([
        ok(b_fox, 0.0), b_fox.reshape(1),
        ok(b_a, span), b_a + bmax[:HPM],
        ok(b_s, span), b_s + bmax[HPM:],
        ok(b_w, span), b_w + bmax[HPM:]])
    return jnp.pad(vec, (0, BND_SIZE - vec.shape[0]))


def kernel(x, c, rel_bias, ada_w, ada_b, norm_mix_g, norm_ffn_g, w_in, w_out, diff_qnorm_g, diff_knorm_g, diff_lambda, diff_subln_g, fox_qnorm_g, fox_knorm_g, fox_b_f, nsa_qnorm_g, nsa_knorm_g, nsa_pe, nsa_phi_w, ffn_w_up, ffn_conv_w, ffn_conv_b, ffn_w_down):
    bsz, s, d = x.shape
    depth = ada_w.shape[0]
    assert s % TM == 0 and TM % T == 0 and WINDOW % T == 0 and T % SEL_LEN == 0 and d == 4 * GW
    assert T & (T - 1) == 0 and BAND_ROWS // 2 >= (MAX_DISTANCE + CMP_LEN) // CMP_STRIDE
    nk = s // T
    n_chunk = s // CMP_STRIDE
    n_blk = s // SEL_LEN
    d_ff = ffn_w_down.shape[1]
    assert d_ff % FF_CHUNK == 0

    bd32 = jnp.asarray(_block_ones(GW, DIFF_QK_DIM), BF16)
    bd64 = jnp.asarray(_block_ones(GW, HEAD_DIM), BF16)
    ti = np.arange(T)
    tri_incl = jnp.asarray((ti[None, :] <= ti[:, None]).astype(np.float32), BF16)
    tri_later = jnp.asarray((ti[None, :] > ti[:, None]).astype(np.float32), BF16)
    c_start = np.arange(n_chunk) * CMP_STRIDE
    s_start = np.arange(n_blk) * SEL_LEN
    ov = ((c_start[None, :] < s_start[:, None] + SEL_LEN) & (s_start[:, None] < c_start[None, :] + CMP_LEN)
          & (np.arange(n_chunk)[None, :] < n_chunk - 1))
    ovt = jnp.asarray(ov.astype(np.float32), BF16)

    bias_a, bias_d = _bias_tiles(rel_bias)
    far = rel_bias[NUM_BUCKETS - 1] * LOG2E
    mod = _adaln(c, ada_w, ada_b)

    w_re = _repack_w_in(w_in)
    wo_b, wup_b, wdn_b = w_out.astype(BF16), ffn_w_up.astype(BF16), ffn_w_down.astype(BF16)
    conv_w = ffn_conv_w.reshape(depth, CONV_WIDTH, d_ff)
    conv_b = ffn_conv_b.reshape(depth, 1, d_ff)

    for l in range(depth):
        sh_a, sc_a, gt_a, sh_m, sc_m, gt_m = [m.reshape(bsz, 1, d) for m in jnp.split(mod[l], 6, axis=-1)]
        scale_a = DIFF_QK_DIM ** -0.5 * LOG2E
        scale = HEAD_DIM ** -0.5 * LOG2E
        ones = jnp.ones((HEAD_DIM,), F32)
        gains = jnp.stack([
            jnp.tile(diff_qnorm_g[l] * scale_a, GW // DIFF_QK_DIM), jnp.tile(diff_knorm_g[l], GW // DIFF_QK_DIM),
            jnp.tile(fox_qnorm_g[l] * scale, HPM), jnp.tile(fox_knorm_g[l], HPM),
            jnp.tile(nsa_qnorm_g[l] * scale, HPM),
            jnp.concatenate([nsa_knorm_g[l, 1], ones, nsa_knorm_g[l, 2], ones]),
            jnp.ones((GW,), F32), jnp.ones((GW,), F32)])
        (a_qt, a_k, a_vt, b_qt, b_k, b_vt, c_qt, c_k, c_vt, d_qt,
         d_ks, d_kw, d_vst, d_vwt, d_cmp, smalls) = _inproj(
            l, x, norm_mix_g[l].reshape(1, d), sc_a, sh_a, w_re, gains, bd32, bd64)

        bf_row = jnp.pad(fox_b_f[l], (0, 128 - HPM)).reshape(1, 128)
        ck, st = _scalars(smalls, bf_row, tri_incl)

        pe_rows, w_blk = _compress_params(nsa_pe[l], nsa_phi_w[l])
        gk_pad = jnp.pad(nsa_knorm_g[l, 0], (0, 128 - HEAD_DIM)).reshape(1, 128)
        kc, vct = _compress(d_cmp, pe_rows, w_blk, gk_pad)

        bnd = _score_bounds(rel_bias, diff_qnorm_g[l], diff_knorm_g[l], fox_qnorm_g[l], fox_knorm_g[l],
                            nsa_qnorm_g[l], nsa_knorm_g[l, 1], nsa_knorm_g[l, 2])
        lam_init = 0.8 - 0.6 * math.exp(-0.3 * l)
        cst = jnp.full((1,), lam_init, F32)
        k4 = lambda a: a.reshape(bsz, nk, T, a.shape[-1])
        o_a = _diff_attention(far, cst, bnd, a_qt, k4(a_k), a_vt, bias_a, diff_lambda[l],
                              diff_subln_g[l].reshape(HEAD_DIM, 1))
        o_b = _fox_attention(bnd, b_qt, k4(b_k), b_vt, st, k4(ck))
        o_c = _sb_attention(c_qt, k4(c_k), c_vt, tri_later)
        o_d = _nsa_attention(far, bnd, d_qt, kc, vct, k4(d_ks), d_vst, k4(d_kw), d_vwt, st, bias_d, ovt)

        x = _post(l, x, o_a, o_b, o_c, o_d, wo_b, gt_a, norm_ffn_g[l].reshape(1, d),
                  sc_m, sh_m, gt_m, wup_b, conv_w, conv_b, wdn_b)
    return x
```
